```python
import math
import jax, jax.numpy as jnp
from jax import lax
import numpy as np

D_MODEL = 1024
BATCH = 8
SEQ = 2048
DEPTH = 2

GRID_W = 64
CTX_LEN = 256
N_MIXERS = 2
M_HEADS = 4
M_DV = D_MODEL // M_HEADS
M_DK = M_DV // 2
M_CHUNK = 128
M_PROJ = 2 * M_HEADS * M_DK + 2 * M_HEADS * M_DV + 4 * M_HEADS
F_GROUPS = 8
F_GROUP_DIM = D_MODEL // F_GROUPS
FFN_DIM = 2816
CONV_W = 3
EPS = 1e-6

kernel_name = "hybrid_mlstm_fourier_dit_block"


def rmsnorm(x, g):
    x32 = x.astype(jnp.float32)
    y = x32 * lax.rsqrt(jnp.mean(x32 * x32, axis=-1, keepdims=True) + EPS)
    return (y * g.astype(jnp.float32)).astype(x.dtype)


def modulate(h, shift, scale):
    return h * (1 + scale[..., None, :]) + shift[..., None, :]


def _ctx_stream_needed(layer):
    return any(j % N_MIXERS == 0 for j in range(layer + 1, DEPTH))


def mlstm_project(h, w_in, b_in):
    B, T, _ = h.shape
    hk, hv = M_HEADS * M_DK, M_HEADS * M_DV
    p = h @ w_in + b_in
    q, k, v, o, gates = jnp.split(p, [hk, 2 * hk, 2 * hk + hv, 2 * hk + 2 * hv], axis=-1)

    def heads(t, d):
        return t.reshape(B, T, M_HEADS, d).transpose(0, 2, 1, 3).astype(jnp.float32)

    q = heads(q, M_DK)
    k = heads(k, M_DK) * (M_DK ** -0.5)
    v = heads(v, M_DV)
    g = gates.astype(jnp.float32).reshape(B, T, 4, M_HEADS).transpose(2, 0, 3, 1)
    fwd = (g[0], jax.nn.log_sigmoid(g[1]))
    bwd = (g[2], jax.nn.log_sigmoid(g[3]))
    return q, k, v, o, fwd, bwd


def mlstm_zero_state(B):
    return (jnp.zeros((B, M_HEADS, M_DK, M_DV), jnp.float32),
            jnp.zeros((B, M_HEADS, M_DK), jnp.float32),
            jnp.zeros((B, M_HEADS), jnp.float32))


def mlstm_chunk_scan(q, k, v, logi, logf, state0, with_outputs):
    B, H, T, _ = q.shape
    L = M_CHUNK
    nc = T // L

    def chunks(t):
        return jnp.moveaxis(t.reshape((B, H, nc, L) + t.shape[3:]), 2, 0)

    tri = jnp.tril(jnp.ones((L, L), dtype=bool))

    def step(carry, inp):
        C, n, m = carry
        qc, kc, vc, li, lf = inp
        b = jnp.cumsum(lf, axis=-1)
        bL = b[..., -1]
        g_s = bL[..., None] - b + li
        m_new = jnp.maximum(bL + m, jnp.max(g_s, axis=-1))
        a = jnp.exp(bL + m - m_new)
        ws = jnp.exp(g_s - m_new[..., None])
        C_new = a[..., None, None] * C + jnp.einsum('bhs,bhsd,bhse->bhde', ws, kc, vc)
        n_new = a[..., None] * n + jnp.einsum('bhs,bhsd->bhd', ws, kc)
        if not with_outputs:
            return (C_new, n_new, m_new), None
        dmat = jnp.where(tri, b[..., :, None] - b[..., None, :] + li[..., None, :], -jnp.inf)
        inter = b + m[..., None]
        m_t = jnp.maximum(inter, jnp.max(dmat, axis=-1))
        w_inter = jnp.exp(inter - m_t)
        s = jnp.einsum('bhtd,bhsd->bhts', qc, kc) * jnp.exp(dmat - m_t[..., None])
        num = w_inter[..., None] * jnp.einsum('bhtd,bhde->bhte', qc, C) + jnp.einsum('bhts,bhse->bhte', s, vc)
        den = w_inter * jnp.einsum('bhtd,bhd->bht', qc, n) + jnp.sum(s, axis=-1)
        h = num / jnp.maximum(jnp.abs(den), jnp.exp(-m_t))[..., None]
        return (C_new, n_new, m_new), h

    state, hs = lax.scan(step, state0, (chunks(q), chunks(k), chunks(v), chunks(logi), chunks(logf)))
    if with_outputs:
        hs = jnp.moveaxis(hs, 0, 2).reshape(B, H, T, M_DV)
    return hs, state


def mlstm_direction(q, k, v, gates, state0, reverse, with_outputs):
    logi, logf = gates
    if reverse:
        q, k, v = q[:, :, ::-1], k[:, :, ::-1], v[:, :, ::-1]
        logi, logf = logi[..., ::-1], logf[..., ::-1]
    h, state = mlstm_chunk_scan(q, k, v, logi, logf, state0, with_outputs)
    if reverse and with_outputs:
        h = h[:, :, ::-1]
    return h, state


def mlstm_readout(h, o, norm_g, w_out):
    B, H, T, _ = h.shape
    hn = h * lax.rsqrt(jnp.mean(h * h, axis=-1, keepdims=True) + EPS)
    hn = hn * norm_g.astype(jnp.float32).reshape(H, 1, M_DV)
    hn = hn.transpose(0, 2, 1, 3).reshape(B, T, H * M_DV)
    y = (hn * jax.nn.sigmoid(o.astype(jnp.float32))).astype(o.dtype)
    return y @ w_out


def mlstm_mixer(hx, hc, w_in, b_in, norm_g, w_out, ctx_outputs):
    B = hx.shape[0]
    s0 = mlstm_zero_state(B)
    qc, kc, vc, oc, gfc, gbc = mlstm_project(hc, w_in, b_in)
    hcf, st_f = mlstm_direction(qc, kc, vc, gfc, s0, False, ctx_outputs)
    hcb, st_b = mlstm_direction(qc, kc, vc, gbc, s0, True, ctx_outputs)
    qx, kx, vx, ox, gfx, gbx = mlstm_project(hx, w_in, b_in)
    hxf, _ = mlstm_direction(qx, kx, vx, gfx, st_f, False, True)
    hxb, _ = mlstm_direction(qx, kx, vx, gbx, st_b, True, True)
    yx = mlstm_readout(hxf + hxb, ox, norm_g, w_out)
    yc = mlstm_readout(hcf + hcb, oc, norm_g, w_out) if ctx_outputs else None
    return yx, yc


def fourier_mixer(h, w_out, b_out):
    B, T, D = h.shape
    hg = h.astype(jnp.float32).reshape(B, T, F_GROUPS, F_GROUP_DIM)
    y = jnp.real(jnp.fft.fft2(hg, axes=(1, 3), norm="ortho")).reshape(B, T, D).astype(h.dtype)
    return y @ w_out + b_out


def dwconv3(g, w, b, axis):
    n = g.shape[axis]
    pad = [(0, 0)] * g.ndim
    pad[axis] = (1, 1)
    gp = jnp.pad(g, pad)
    return (lax.slice_in_dim(gp, 0, n, axis=axis) * w[0]
            + lax.slice_in_dim(gp, 1, n + 1, axis=axis) * w[1]
            + lax.slice_in_dim(gp, 2, n + 2, axis=axis) * w[2] + b)


def conv_ffn(h, up_w, conv_w, conv_b, down_w, on_grid):
    u, g = jnp.split(h @ up_w, 2, axis=-1)
    if on_grid:
        B, T, F = g.shape
        rows = T // GRID_W
        g = dwconv3(g.reshape(B, rows, GRID_W, F), conv_w, conv_b, axis=2).reshape(B, T, F)
    else:
        g = dwconv3(g, conv_w, conv_b, axis=1)
    return (jax.nn.silu(g) * u) @ down_w


def setup_inputs(seed: int = 0) -> dict:
    key = jax.random.key(seed)
    ks = jax.random.split(key, 24)
    D, F = D_MODEL, FFN_DIM
    NA = (DEPTH + N_MIXERS - 1) // N_MIXERS
    NB = DEPTH // N_MIXERS
    nrm = jax.random.normal
    x = nrm(ks[0], (BATCH, SEQ, D), jnp.float32)
    c = nrm(ks[1], (BATCH, D), jnp.float32)
    ctx = nrm(ks[2], (BATCH, CTX_LEN, D), jnp.float32)
    c_ctx = nrm(ks[3], (D,), jnp.float32)
    ada_w = nrm(ks[4], (DEPTH, D, 6 * D), jnp.float32) * D ** -0.5
    ada_b = 0.02 * nrm(ks[5], (DEPTH, 6 * D), jnp.float32)
    pre_mix_g = 1.0 + 0.05 * nrm(ks[6], (DEPTH, D), jnp.float32)
    post_mix_g = 1.0 + 0.05 * nrm(ks[7], (DEPTH, D), jnp.float32)
    pre_ffn_g = 1.0 + 0.05 * nrm(ks[8], (DEPTH, D), jnp.float32)
    post_ffn_g = 1.0 + 0.05 * nrm(ks[9], (DEPTH, D), jnp.float32)
    ffn_up_w = nrm(ks[10], (DEPTH, D, 2 * F), jnp.float32) * D ** -0.5
    ffn_conv_w = nrm(ks[11], (DEPTH, CONV_W, F), jnp.float32) * CONV_W ** -0.5
    ffn_conv_b = 0.02 * nrm(ks[12], (DEPTH, F), jnp.float32)
    ffn_down_w = nrm(ks[13], (DEPTH, F, D), jnp.float32) * F ** -0.5
    m_in_w = nrm(ks[14], (NA, D, M_PROJ), jnp.float32) * D ** -0.5
    lin_b = 0.02 * nrm(ks[15], (NA, M_PROJ - 4 * M_HEADS), jnp.float32)
    f_base = jnp.linspace(3.0, 6.0, M_HEADS, dtype=jnp.float32)
    gn = 0.1 * nrm(ks[16], (NA, 4, M_HEADS), jnp.float32)
    gate_b = gn + jnp.stack([jnp.zeros_like(f_base), f_base, jnp.zeros_like(f_base), f_base])[None]
    m_in_b = jnp.concatenate([lin_b, gate_b.reshape(NA, 4 * M_HEADS)], axis=-1)
    m_norm_g = 1.0 + 0.05 * nrm(ks[17], (NA, M_HEADS * M_DV), jnp.float32)
    m_out_w = nrm(ks[18], (NA, M_HEADS * M_DV, D), jnp.float32) * (M_HEADS * M_DV) ** -0.5
    f_out_w = nrm(ks[19], (NB, D, D), jnp.float32) * D ** -0.5
    f_out_b = 0.02 * nrm(ks[20], (NB, D), jnp.float32)
    return {"x": x, "c": c, "ctx": ctx, "c_ctx": c_ctx,
            "ada_w": ada_w, "ada_b": ada_b,
            "pre_mix_g": pre_mix_g, "post_mix_g": post_mix_g,
            "pre_ffn_g": pre_ffn_g, "post_ffn_g": post_ffn_g,
            "ffn_up_w": ffn_up_w, "ffn_conv_w": ffn_conv_w, "ffn_conv_b": ffn_conv_b,
            "ffn_down_w": ffn_down_w,
            "m_in_w": m_in_w, "m_in_b": m_in_b, "m_norm_g": m_norm_g, "m_out_w": m_out_w,
            "f_out_w": f_out_w, "f_out_b": f_out_b}


def reference(x, c, ctx, c_ctx, ada_w, ada_b, pre_mix_g, post_mix_g, pre_ffn_g, post_ffn_g,
              ffn_up_w, ffn_conv_w, ffn_conv_b, ffn_down_w,
              m_in_w, m_in_b, m_norm_g, m_out_w, f_out_w, f_out_b):
    cx = jax.nn.silu(c)
    cc = jax.nn.silu(c_ctx)
    h_ctx = ctx
    for i in range(DEPTH):
        is_mlstm = (i % N_MIXERS) == 0
        j = i // N_MIXERS
        ctx_out = _ctx_stream_needed(i)
        sh1, sc1, g1, sh2, sc2, g2 = jnp.split(cx @ ada_w[i] + ada_b[i], 6, axis=-1)
        if is_mlstm or ctx_out:
            csh1, csc1, cg1, csh2, csc2, cg2 = jnp.split(cc @ ada_w[i] + ada_b[i], 6, axis=-1)
        hx = modulate(rmsnorm(x, pre_mix_g[i]), sh1, sc1)
        if is_mlstm:
            hc = modulate(rmsnorm(h_ctx, pre_mix_g[i]), csh1, csc1)
            yx, yc = mlstm_mixer(hx, hc, m_in_w[j], m_in_b[j], m_norm_g[j], m_out_w[j], ctx_out)
        else:
            yx = fourier_mixer(hx, f_out_w[j], f_out_b[j])
            if ctx_out:
                hc = modulate(rmsnorm(h_ctx, pre_mix_g[i]), csh1, csc1)
                yc = fourier_mixer(hc, f_out_w[j], f_out_b[j])
        x = x + g1[..., None, :] * rmsnorm(yx, post_mix_g[i])
        if ctx_out:
            h_ctx = h_ctx + cg1[..., None, :] * rmsnorm(yc, post_mix_g[i])
        hx = modulate(rmsnorm(x, pre_ffn_g[i]), sh2, sc2)
        yx = conv_ffn(hx, ffn_up_w[i], ffn_conv_w[i], ffn_conv_b[i], ffn_down_w[i], True)
        x = x + g2[..., None, :] * rmsnorm(yx, post_ffn_g[i])
        if ctx_out:
            hc = modulate(rmsnorm(h_ctx, pre_ffn_g[i]), csh2, csc2)
            yc = conv_ffn(hc, ffn_up_w[i], ffn_conv_w[i], ffn_conv_b[i], ffn_down_w[i], False)
            h_ctx = h_ctx + cg2[..., None, :] * rmsnorm(yc, post_ffn_g[i])
    return x
```

```python
import functools

import numpy as np
import jax
import jax.numpy as jnp
from jax import lax
from jax.experimental import pallas as pl
from jax.experimental.pallas import tpu as pltpu

F32 = jnp.float32
BF16 = jnp.bfloat16

M_HEADS = 4
M_CHUNK = 128
F_GROUPS = 8
GRID_W = 64
EPS = 1e-6
N_MOD = 6
COND_ROWS = 16

VMEM_LIMIT_BYTES = 56 * 1024 * 1024
LANES = 128


def _params(*sem):
    return pltpu.CompilerParams(dimension_semantics=sem, vmem_limit_bytes=VMEM_LIMIT_BYTES)


def _rms(x, g):
    return x * lax.rsqrt(jnp.mean(x * x, axis=-1, keepdims=True) + EPS) * g


def _bdot(a, b):
    return jnp.dot(a, b, preferred_element_type=F32)


def _ada_kernel(c_ref, w_ref, b_ref, o_ref):
    c = c_ref[...]
    s = c * jax.nn.sigmoid(c)
    o_ref[0] = jnp.dot(s, w_ref[0], precision=lax.Precision.HIGHEST,
                       preferred_element_type=F32) + b_ref[0]


def _ada(cond, ada_w, ada_b):
    depth, d, n = ada_w.shape
    tn = n // 4
    return pl.pallas_call(
        _ada_kernel,
        grid=(depth, n // tn),
        in_specs=[pl.BlockSpec((COND_ROWS, d), lambda i, j: (0, 0)),
                  pl.BlockSpec((1, d, tn), lambda i, j: (i, 0, j)),
                  pl.BlockSpec((1, 1, tn), lambda i, j: (i, 0, j))],
        out_specs=pl.BlockSpec((1, COND_ROWS, tn), lambda i, j: (i, 0, j)),
        out_shape=jax.ShapeDtypeStruct((depth, COND_ROWS, n), F32),
        compiler_params=_params("arbitrary", "arbitrary"),
        name="ada",
    )(cond, ada_w, ada_b.reshape(depth, 1, n))


def _inproj_kernel(dk, x_ref, sh_ref, sc_ref, g_ref, wq, wk, wv, wo, wg, bq, bk, bv, bo, bg,
                   q_out, k_out, v_out, o_out, gate_out):
    h = _rms(x_ref[0], g_ref[...]) * (1.0 + sc_ref[0]) + sh_ref[0]
    hb = h.astype(BF16)
    q_out[0] = (_bdot(hb, wq[...]) + bq[...]).astype(BF16)
    k_out[0] = ((_bdot(hb, wk[...]) + bk[...]) * (dk ** -0.5)).astype(BF16)
    v_out[0] = (_bdot(hb, wv[...]) + bv[...]).astype(BF16)
    o_out[0] = jax.nn.sigmoid(_bdot(hb, wo[...]) + bo[...]).astype(BF16)
    gate_out[0] = _bdot(hb, wg[...]) + bg[...]


def _inproj(x, sh, sc, g, ws, bs, tm):
    bsz, t, d = x.shape
    hk = ws[0].shape[1]
    hv = ws[2].shape[1]
    dk = hk // M_HEADS
    per_batch = sh.shape[0] > 1
    mod_spec = pl.BlockSpec((1, 1, d), (lambda b, i: (b, 0, 0)) if per_batch else (lambda b, i: (0, 0, 0)))
    full = lambda a: pl.BlockSpec(a.shape, lambda b, i: (0, 0))
    tok = lambda n: pl.BlockSpec((1, tm, n), lambda b, i: (b, i, 0))
    return pl.pallas_call(
        functools.partial(_inproj_kernel, dk),
        grid=(bsz, t // tm),
        in_specs=[tok(d), mod_spec, mod_spec, full(g)] + [full(w) for w in ws] + [full(b) for b in bs],
        out_specs=[tok(hk), tok(hk), tok(hv), tok(hv), tok(LANES)],
        out_shape=[jax.ShapeDtypeStruct((bsz, t, hk), BF16), jax.ShapeDtypeStruct((bsz, t, hk), BF16),
                   jax.ShapeDtypeStruct((bsz, t, hv), BF16), jax.ShapeDtypeStruct((bsz, t, hv), BF16),
                   jax.ShapeDtypeStruct((bsz, t, LANES), F32)],
        compiler_params=_params("parallel", "parallel"),
        name="mlstm_inproj",
    )(x, sh, sc, g, *ws, *bs)


def _lane_scan(x, op, fill, reverse):
    n = x.shape[-1]
    lane = lax.broadcasted_iota(jnp.int32, x.shape, x.ndim - 1)
    d = 1
    while d < n:
        if reverse:
            shifted = jnp.where(lane < n - d, pltpu.roll(x, n - d, x.ndim - 1), fill)
        else:
            shifted = jnp.where(lane >= d, pltpu.roll(x, d, x.ndim - 1), fill)
        x = op(x, shifted)
        d *= 2
    return x


def _log_sigmoid(x):
    return jnp.minimum(x, 0.0) - jnp.log1p(jnp.exp(-jnp.abs(x)))


def _gate_kernel(gf_ref, gb_ref, of_ref, ob_ref, m_s):
    L = gf_ref.shape[-1]

    @pl.when(pl.program_id(0) == 0)
    def _():
        m_s[...] = jnp.zeros(m_s.shape, F32)

    for d, (g_ref, o_ref) in enumerate(((gf_ref, of_ref), (gb_ref, ob_ref))):
        edge = L - 1 if d == 0 else 0
        li = g_ref[0]
        lf = _log_sigmoid(g_ref[1])
        b = _lane_scan(lf, jnp.add, 0.0, reverse=(d == 1))
        r1 = li - b
        cm = _lane_scan(r1, jnp.maximum, -jnp.inf, reverse=(d == 1))
        m = m_s[d]
        mx = jnp.maximum(m, cm[:, edge:edge + 1])
        m_s[d] = b[:, edge:edge + 1] + mx
        mc = jnp.maximum(m, cm)
        o_ref[0] = r1
        o_ref[1] = mc
        o_ref[2] = jnp.exp(m - mc)
        o_ref[3] = jnp.exp(-(b + mc))
        o_ref[4] = jnp.exp(r1 - mx)
        o_ref[5] = jnp.broadcast_to(jnp.exp(m - mx), r1.shape)


def _gate_prep(gates_c, gates_x):
    bsz, tc = gates_c.shape[:2]
    t = gates_x.shape[1]
    n = tc + t
    r = bsz * M_HEADS
    nc = n // M_CHUNK

    def rowform(g, first_kind):
        g = g[:, :, :4 * M_HEADS].reshape(bsz, n, 4, M_HEADS)[:, :, first_kind:first_kind + 2]
        return g.transpose(2, 0, 3, 1).reshape(2, r, n)

    gf = rowform(jnp.concatenate([gates_c, gates_x], axis=1), 0)
    gb = rowform(jnp.concatenate([gates_x, gates_c], axis=1), 2)
    outf, outb = pl.pallas_call(
        _gate_kernel,
        grid=(nc,),
        in_specs=[pl.BlockSpec((2, r, M_CHUNK), lambda j: (0, 0, j)),
                  pl.BlockSpec((2, r, M_CHUNK), lambda j: (0, 0, nc - 1 - j))],
        out_specs=[pl.BlockSpec((6, r, M_CHUNK), lambda j: (0, 0, j)),
                   pl.BlockSpec((6, r, M_CHUNK), lambda j: (0, 0, nc - 1 - j))],
        out_shape=[jax.ShapeDtypeStruct((6, r, n), F32)] * 2,
        scratch_shapes=[pltpu.VMEM((2, r, 1), F32)],
        compiler_params=_params("arbitrary"),
        name="mlstm_gates",
    )(gf, gb)
    outf = outf.reshape(6, bsz, M_HEADS, n)
    outb = outb.reshape(6, bsz, M_HEADS, n)

    def pack(f, b):
        cols = jnp.stack([f[1], f[2], f[3], f[4], b[1], b[2], b[3], b[4]], axis=-1)
        rows = jnp.stack([f[0], b[0], f[5], b[5]], axis=2)
        return cols, rows

    return pack(outf[..., tc:], outb[..., :t]), pack(outf[..., :tc], outb[..., t:])


def _scan_kernel(nsteps, with_outputs, has_init, *refs):
    (qf, kf, vf, cf, rf, qb, kb, vb, cb, rb), refs = refs[:10], refs[10:]
    if has_init:
        (c0, n0), refs = refs[:2], refs[2:]
    if with_outputs:
        h_out, c_s, n_s = refs
    else:
        c_s, n_s = refs
    s = pl.program_id(1)
    L = M_CHUNK
    dk = qf.shape[2] // M_HEADS
    dv = vf.shape[2] // M_HEADS

    @pl.when(s == 0)
    def _():
        if has_init:
            c_s[...] = c0[...]
            n_s[...] = n0[...]
        else:
            c_s[...] = jnp.zeros(c_s.shape, F32)
            n_s[...] = jnp.zeros(n_s.shape, F32)

    row = lax.broadcasted_iota(jnp.int32, (L, L), 0)
    col = lax.broadcasted_iota(jnp.int32, (L, L), 1)
    streams = ((qf, kf, vf, cf, rf), (qb, kb, vb, cb, rb))
    results = []
    for d in range(2):
        q_ref, k_ref, v_ref, col_ref, row_ref = streams[d]
        mask = (col <= row) if d == 0 else (col >= row)
        for h in range(M_HEADS):
            k = k_ref[0, :, h * dk:(h + 1) * dk]
            v = v_ref[0, :, h * dv:(h + 1) * dv]
            cols = col_ref[0, h]
            ws = cols[:, 4 * d + 3:4 * d + 4]
            a = row_ref[0, h, 2 + d:3 + d, 0:1]
            c_old = c_s[0, d, h]
            n_old = n_s[0, d, h]
            kw = ws * k.astype(F32)
            c_s[0, d, h] = a * c_old + lax.dot_general(
                kw.astype(BF16), v, (((0,), (0,)), ((), ())), preferred_element_type=F32)
            n_s[0, d, h] = a * n_old + jnp.sum(kw, axis=0, keepdims=True)
            if not with_outputs:
                continue
            q = q_ref[0, :, h * dk:(h + 1) * dk]
            mc = cols[:, 4 * d:4 * d + 1]
            wi = cols[:, 4 * d + 1:4 * d + 2]
            fl = cols[:, 4 * d + 2:4 * d + 3]
            r1r = row_ref[0, h, d:d + 1, :]
            p = jnp.exp(jnp.where(mask, r1r - mc, -jnp.inf))
            sp = lax.dot_general(q, k, (((1,), (1,)), ((), ())), preferred_element_type=F32) * p
            num = wi * _bdot(q, c_old.astype(BF16)) + _bdot(sp.astype(BF16), v)
            den = (wi * jnp.sum(q.astype(F32) * n_old, axis=-1, keepdims=True)
                   + jnp.sum(sp, axis=-1, keepdims=True))
            results.append((d, h, num / jnp.maximum(jnp.abs(den), fl)))

    if with_outputs:
        def emit(d, accumulate):
            chunk = s if d == 0 else nsteps - 1 - s
            rows = pl.ds(pl.multiple_of(chunk * L, L), L)
            for dd, h, val in results:
                if dd != d:
                    continue
                if accumulate:
                    h_out[0, rows, h * dv:(h + 1) * dv] += val
                else:
                    h_out[0, rows, h * dv:(h + 1) * dv] = val

        fwd_first = 2 * s <= nsteps - 1
        bwd_first = 2 * s < nsteps - 1
        pl.when(fwd_first)(lambda: emit(0, False))
        pl.when(jnp.logical_not(fwd_first))(lambda: emit(0, True))
        pl.when(bwd_first)(lambda: emit(1, False))
        pl.when(jnp.logical_not(bwd_first))(lambda: emit(1, True))


def _scan(q, k, v, cols, rows, init, with_outputs):
    bsz, t, hk = q.shape
    hv = v.shape[2]
    dk, dv = hk // M_HEADS, hv // M_HEADS
    nc = t // M_CHUNK
    L = M_CHUNK

    def specs(cidx):
        return [pl.BlockSpec((1, L, hk), lambda b, s: (b, cidx(s), 0)),
                pl.BlockSpec((1, L, hk), lambda b, s: (b, cidx(s), 0)),
                pl.BlockSpec((1, L, hv), lambda b, s: (b, cidx(s), 0)),
                pl.BlockSpec((1, M_HEADS, L, 8), lambda b, s: (b, 0, cidx(s), 0)),
                pl.BlockSpec((1, M_HEADS, 4, L), lambda b, s: (b, 0, 0, cidx(s)))]

    args = [q, k, v, cols, rows] * 2
    in_specs = specs(lambda s: s) + specs(lambda s: nc - 1 - s)
    state_shapes = [(bsz, 2, M_HEADS, dk, dv), (bsz, 2, M_HEADS, 1, dk)]
    state_specs = [pl.BlockSpec((1,) + shp[1:], lambda b, s: (b, 0, 0, 0, 0)) for shp in state_shapes]
    if init is not None:
        args += list(init)
        in_specs += state_specs
    if with_outputs:
        out_specs = pl.BlockSpec((1, t, hv), lambda b, s: (b, 0, 0))
        out_shape = jax.ShapeDtypeStruct((bsz, t, hv), F32)
        scratch = [pltpu.VMEM((1,) + shp[1:], F32) for shp in state_shapes]
    else:
        out_specs = state_specs
        out_shape = [jax.ShapeDtypeStruct(shp, F32) for shp in state_shapes]
        scratch = []
    return pl.pallas_call(
        functools.partial(_scan_kernel, nc, with_outputs, init is not None),
        grid=(bsz, nc),
        in_specs=in_specs,
        out_specs=out_specs,
        out_shape=out_shape,
        scratch_shapes=scratch,
        compiler_params=_params("parallel", "arbitrary"),
        name="mlstm_scan" if with_outputs else "mlstm_ctx_state",
    )(*args)


def _readout_kernel(h_ref, o_ref, x_ref, ng_ref, w_ref, pg_ref, gate_ref, out_ref):
    h = h_ref[0]
    dv = h.shape[1] // M_HEADS
    parts = []
    for hd in range(M_HEADS):
        hh = h[:, hd * dv:(hd + 1) * dv]
        parts.append(hh * lax.rsqrt(jnp.mean(hh * hh, axis=-1, keepdims=True) + EPS))
    hn = jnp.concatenate(parts, axis=-1) * ng_ref[...]
    y = (hn * o_ref[0].astype(F32)).astype(BF16)
    z = _bdot(y, w_ref[...])
    out_ref[0] = x_ref[0] + gate_ref[0] * _rms(z, pg_ref[...])


def _readout(h, sig_o, x, norm_g, w_out, post_g, gate, tm):
    bsz, t, d = x.shape
    hv = h.shape[2]
    tok = lambda n: pl.BlockSpec((1, tm, n), lambda b, i: (b, i, 0))
    full = lambda a: pl.BlockSpec(a.shape, lambda b, i: (0, 0))
    return pl.pallas_call(
        _readout_kernel,
        grid=(bsz, t // tm),
        in_specs=[tok(hv), tok(hv), tok(d), full(norm_g), full(w_out), full(post_g),
                  pl.BlockSpec((1, 1, d), lambda b, i: (b, 0, 0))],
        out_specs=tok(d),
        out_shape=jax.ShapeDtypeStruct((bsz, t, d), F32),
        compiler_params=_params("parallel", "parallel"),
        name="mlstm_readout",
    )(h, sig_o, x, norm_g, w_out, post_g, gate)


def _ffn_kernel(fc, x_ref, sh_ref, sc_ref, gate_ref, pre_ref, post_ref, wu_ref, wg_ref, cw_ref, cb_ref,
                wd_ref, out_ref, act_ref):
    x = x_ref[0]
    tm = x.shape[0]
    f = wu_ref.shape[1]
    hb = (_rms(x, pre_ref[...]) * (1.0 + sc_ref[0]) + sh_ref[0]).astype(BF16)
    pos = lax.broadcasted_iota(jnp.int32, (tm, fc), 0) % GRID_W
    has_prev = pos != 0
    has_next = pos != GRID_W - 1
    for j in range(f // fc):
        cs = slice(j * fc, (j + 1) * fc)
        u = _bdot(hb, wu_ref[:, cs])
        g = _bdot(hb, wg_ref[:, cs])
        g_prev = jnp.where(has_prev, pltpu.roll(g, 1, 0), 0.0)
        g_next = jnp.where(has_next, pltpu.roll(g, tm - 1, 0), 0.0)
        gc = g_prev * cw_ref[0:1, cs] + g * cw_ref[1:2, cs] + g_next * cw_ref[2:3, cs] + cb_ref[:, cs]
        act_ref[:, cs] = (gc * jax.nn.sigmoid(gc) * u).astype(BF16)
    y = _bdot(act_ref[...], wd_ref[...])
    out_ref[0] = x + gate_ref[0] * _rms(y, post_ref[...])


def _ffn(x, sh, sc, gate, pre_g, post_g, wu, wg, cw, cb, wd, tm, fc):
    bsz, t, d = x.shape
    f = wu.shape[1]
    tok = pl.BlockSpec((1, tm, d), lambda b, i: (b, i, 0))
    mod = pl.BlockSpec((1, 1, d), lambda b, i: (b, 0, 0))
    full = lambda a: pl.BlockSpec(a.shape, lambda b, i: (0, 0))
    return pl.pallas_call(
        functools.partial(_ffn_kernel, fc),
        grid=(bsz, t // tm),
        in_specs=[tok, mod, mod, mod, full(pre_g), full(post_g), full(wu), full(wg), full(cw), full(cb),
                  full(wd)],
        out_specs=tok,
        out_shape=jax.ShapeDtypeStruct((bsz, t, d), F32),
        scratch_shapes=[pltpu.VMEM((tm, f), BF16)],
        compiler_params=_params("parallel", "parallel"),
        name="conv_ffn",
    )(x, sh, sc, gate, pre_g, post_g, wu, wg, cw, cb, wd)


def _fourier_channel_kernel(x_ref, sh_ref, sc_ref, pre_ref, cs_ref, ab_ref):
    hb = (_rms(x_ref[0], pre_ref[...]) * (1.0 + sc_ref[0]) + sh_ref[0]).astype(BF16)
    gd = cs_ref.shape[0]
    for g in range(hb.shape[1] // gd):
        r = _bdot(hb[:, g * gd:(g + 1) * gd], cs_ref[...])
        ab_ref[0, 0, :, g * gd:(g + 1) * gd] = r[:, :gd].astype(BF16)
        ab_ref[0, 1, :, g * gd:(g + 1) * gd] = r[:, gd:].astype(BF16)


def _fourier_channel(x, sh, sc, pre_g, cs, tm):
    bsz, t, d = x.shape
    mod = pl.BlockSpec((1, 1, d), lambda b, i: (b, 0, 0))
    full = lambda a: pl.BlockSpec(a.shape, lambda b, i: (0, 0))
    return pl.pallas_call(
        _fourier_channel_kernel,
        grid=(bsz, t // tm),
        in_specs=[pl.BlockSpec((1, tm, d), lambda b, i: (b, i, 0)), mod, mod, full(pre_g), full(cs)],
        out_specs=pl.BlockSpec((1, 2, tm, d), lambda b, i: (b, 0, i, 0)),
        out_shape=jax.ShapeDtypeStruct((bsz, 2, t, d), BF16),
        compiler_params=_params("parallel", "parallel"),
        name="fourier_channel_dft",
    )(x, sh, sc, pre_g, cs)


def _fourier_token_kernel(scale, dft_ref, ab_ref, w_ref, bias_ref, post_ref, gate_ref, x_ref, out_ref):
    y = _bdot(dft_ref[...], ab_ref[0]) * scale
    z = _bdot(y.astype(BF16), w_ref[...]) + bias_ref[...]
    out_ref[0] = x_ref[0] + gate_ref[0] * _rms(z, post_ref[...])


def _fourier_token(dft, ab, w, bias, post_g, gate, x, scale, tm):
    bsz, t, d = x.shape
    full = lambda a: pl.BlockSpec(a.shape, lambda b, i: (0, 0))
    tok = pl.BlockSpec((1, tm, d), lambda b, i: (b, i, 0))
    return pl.pallas_call(
        functools.partial(_fourier_token_kernel, scale),
        grid=(bsz, t // tm),
        in_specs=[pl.BlockSpec((tm, 2 * t), lambda b, i: (i, 0)),
                  pl.BlockSpec((1, 2 * t, d), lambda b, i: (b, 0, 0)),
                  full(w), full(bias), full(post_g), pl.BlockSpec((1, 1, d), lambda b, i: (b, 0, 0)), tok],
        out_specs=tok,
        out_shape=jax.ShapeDtypeStruct((bsz, t, d), F32),
        compiler_params=_params("parallel", "arbitrary"),
        name="fourier_token_dft",
    )(dft, ab, w, bias, post_g, gate, x)


def _dft_tables(t, gd):
    def cs(n):
        idx = np.arange(n, dtype=np.int64)
        ang = 2.0 * np.pi * ((idx[:, None] * idx[None, :]) % n).astype(np.float64) / n
        return np.cos(ang), np.sin(ang)
    cg, sg = cs(gd)
    ct, st = cs(t)
    chan = np.concatenate([cg, sg], axis=1).astype(np.float32)
    tok = np.concatenate([ct, -st], axis=1).astype(np.float32)
    return jnp.asarray(chan).astype(BF16), jnp.asarray(tok).astype(BF16)


def _row(v):
    return v.reshape(1, -1)


def kernel(x, c, ctx, c_ctx, ada_w, ada_b, pre_mix_g, post_mix_g, pre_ffn_g, post_ffn_g, ffn_up_w, ffn_conv_w,
           ffn_conv_b, ffn_down_w, m_in_w, m_in_b, m_norm_g, m_out_w, f_out_w, f_out_b):
    bsz, t, d = x.shape
    tc = ctx.shape[1]
    f = ffn_down_w.shape[1]
    assert ada_w.shape[0] == 2 and m_in_w.shape[0] == 1 and f_out_w.shape[0] == 1
    assert bsz + 1 <= COND_ROWS and t % M_CHUNK == 0 and tc % M_CHUNK == 0 and t % GRID_W == 0
    hv = m_out_w.shape[1]
    hk = (m_in_w.shape[2] - 2 * hv - 4 * M_HEADS) // 2
    tm = min(512, t)
    fc = 256
    assert f % fc == 0 and t % tm == 0 and tm % GRID_W == 0

    cond = jnp.zeros((COND_ROWS, d), F32).at[:bsz].set(c).at[bsz].set(c_ctx)
    mod = _ada(cond, ada_w, ada_b)
    lat = mod[:, :bsz].reshape(2, bsz, N_MOD, 1, d)
    cmod = mod[0, bsz].reshape(N_MOD, 1, 1, d)

    def ffn(xin, i):
        wu = ffn_up_w[i, :, :f].astype(BF16)
        wg = ffn_up_w[i, :, f:].astype(BF16)
        return _ffn(xin, lat[i, :, 3], lat[i, :, 4], lat[i, :, 5], _row(pre_ffn_g[i]), _row(post_ffn_g[i]),
                    wu, wg, ffn_conv_w[i], _row(ffn_conv_b[i]), ffn_down_w[i].astype(BF16), tm, fc)

    w_in = m_in_w[0]
    b_in = m_in_b[0]
    segs = [0, hk, 2 * hk, 2 * hk + hv, 2 * hk + 2 * hv]
    ws = [w_in[:, segs[j]:segs[j + 1]].astype(BF16) for j in range(4)]
    bs = [_row(b_in[segs[j]:segs[j + 1]]) for j in range(4)]
    ngate = 4 * M_HEADS
    ws.append(jnp.pad(w_in[:, segs[4]:], ((0, 0), (0, LANES - ngate))).astype(BF16))
    bs.append(_row(jnp.pad(b_in[segs[4]:], (0, LANES - ngate))))
    pre0 = _row(pre_mix_g[0])

    qc, kc, vc, _, gc = _inproj(ctx, cmod[0], cmod[1], pre0, ws, bs, min(tm, tc))
    qx, kx, vx, ox, gx = _inproj(x, lat[0, :, 0], lat[0, :, 1], pre0, ws, bs, tm)
    (cols_x, rows_x), (cols_c, rows_c) = _gate_prep(gc, gx)
    state = _scan(qc, kc, vc, cols_c, rows_c, None, False)
    h = _scan(qx, kx, vx, cols_x, rows_x, state, True)
    x = _readout(h, ox, x, _row(m_norm_g[0]), m_out_w[0].astype(BF16), _row(post_mix_g[0]), lat[0, :, 2], tm)
    x = ffn(x, 0)

    gd = d // F_GROUPS
    chan, tok = _dft_tables(t, gd)
    ab = _fourier_channel(x, lat[1, :, 0], lat[1, :, 1], _row(pre_mix_g[1]), chan, tm)
    x = _fourier_token(tok, ab.reshape(bsz, 2 * t, d), f_out_w[0].astype(BF16), _row(f_out_b[0]),
                       _row(post_mix_g[1]), lat[1, :, 2], x, float(1.0 / np.sqrt(t * gd)), tm)
    x = ffn(x, 1)
    return x
```

```python
import functools

import numpy as np
import jax
import jax.numpy as jnp
from jax import lax
from jax.experimental import pallas as pl
from jax.experimental.pallas import tpu as pltpu

F32 = jnp.float32
BF16 = jnp.bfloat16

M_HEADS = 4
M_CHUNK = 128
F_GROUPS = 8
GRID_W = 64
EPS = 1e-6
N_MOD = 6
COND_ROWS = 16

VMEM_LIMIT_BYTES = 56 * 1024 * 1024
LANES = 128


def _params(*sem):
    return pltpu.CompilerParams(dimension_semantics=sem, vmem_limit_bytes=VMEM_LIMIT_BYTES)


def _rms(x, g):
    return x * lax.rsqrt(jnp.mean(x * x, axis=-1, keepdims=True) + EPS) * g


def _bdot(a, b):
    return jnp.dot(a, b, preferred_element_type=F32)


def _ada_kernel(c_ref, w_ref, b_ref, o_ref):
    c = c_ref[...]
    s = c * jax.nn.sigmoid(c)
    o_ref[0] = jnp.dot(s, w_ref[0], precision=lax.Precision.HIGHEST,
                       preferred_element_type=F32) + b_ref[0]


def _ada(cond, ada_w, ada_b):
    depth, d, n = ada_w.shape
    tn = n // 4
    return pl.pallas_call(
        _ada_kernel,
        grid=(depth, n // tn),
        in_specs=[pl.BlockSpec((COND_ROWS, d), lambda i, j: (0, 0)),
                  pl.BlockSpec((1, d, tn), lambda i, j: (i, 0, j)),
                  pl.BlockSpec((1, 1, tn), lambda i, j: (i, 0, j))],
        out_specs=pl.BlockSpec((1, COND_ROWS, tn), lambda i, j: (i, 0, j)),
        out_shape=jax.ShapeDtypeStruct((depth, COND_ROWS, n), F32),
        compiler_params=_params("arbitrary", "arbitrary"),
        name="ada",
    )(cond, ada_w, ada_b.reshape(depth, 1, n))


def _inproj_kernel(dk, x_ref, sh_ref, sc_ref, g_ref, wq, wk, wv, wo, wg, bq, bk, bv, bo, bg,
                   q_out, kt_out, v_out, o_out, gate_out):
    h = _rms(x_ref[0], g_ref[...]) * (1.0 + sc_ref[0]) + sh_ref[0]
    hb = h.astype(BF16)
    q_out[0] = (_bdot(hb, wq[...]) + bq[...]).astype(BF16)
    kt_out[0] = jnp.transpose((_bdot(hb, wk[...]) + bk[...]) * (dk ** -0.5)).astype(BF16)
    v_out[0] = (_bdot(hb, wv[...]) + bv[...]).astype(BF16)
    o_out[0] = jax.nn.sigmoid(_bdot(hb, wo[...]) + bo[...]).astype(BF16)
    gate_out[0] = jnp.transpose(_bdot(hb, wg[...]) + bg[...])[:gate_out.shape[1]]


def _inproj(x, sh, sc, g, ws, bs, tm):
    bsz, t, d = x.shape
    hk = ws[0].shape[1]
    hv = ws[2].shape[1]
    dk = hk // M_HEADS
    per_batch = sh.shape[0] > 1
    mod_spec = pl.BlockSpec((1, 1, d), (lambda b, i: (b, 0, 0)) if per_batch else (lambda b, i: (0, 0, 0)))
    full = lambda a: pl.BlockSpec(a.shape, lambda b, i: (0, 0))
    tok = lambda n: pl.BlockSpec((1, tm, n), lambda b, i: (b, i, 0))
    tok_t = lambda n: pl.BlockSpec((1, n, tm), lambda b, i: (b, 0, i))
    return pl.pallas_call(
        functools.partial(_inproj_kernel, dk),
        grid=(bsz, t // tm),
        in_specs=[tok(d), mod_spec, mod_spec, full(g)] + [full(w) for w in ws] + [full(b) for b in bs],
        out_specs=[tok(hk), tok_t(hk), tok(hv), tok(hv), tok_t(4 * M_HEADS)],
        out_shape=[jax.ShapeDtypeStruct((bsz, t, hk), BF16), jax.ShapeDtypeStruct((bsz, hk, t), BF16),
                   jax.ShapeDtypeStruct((bsz, t, hv), BF16), jax.ShapeDtypeStruct((bsz, t, hv), BF16),
                   jax.ShapeDtypeStruct((bsz, 4 * M_HEADS, t), F32)],
        compiler_params=_params("parallel", "parallel"),
        name="mlstm_inproj",
    )(x, sh, sc, g, *ws, *bs)


def _lane_scan(x, op, fill, reverse):
    n = x.shape[-1]
    lane = lax.broadcasted_iota(jnp.int32, x.shape, x.ndim - 1)
    d = 1
    while d < n:
        if reverse:
            shifted = jnp.where(lane < n - d, pltpu.roll(x, n - d, x.ndim - 1), fill)
        else:
            shifted = jnp.where(lane >= d, pltpu.roll(x, d, x.ndim - 1), fill)
        x = op(x, shifted)
        d *= 2
    return x


def _log_sigmoid(x):
    return jnp.minimum(x, 0.0) - jnp.log1p(jnp.exp(-jnp.abs(x)))


def _gate_kernel(gf_ref, gb_ref, of_ref, ob_ref, m_s):
    L = gf_ref.shape[-1]

    @pl.when(pl.program_id(0) == 0)
    def _():
        m_s[...] = jnp.zeros(m_s.shape, F32)

    for d, (g_ref, o_ref) in enumerate(((gf_ref, of_ref), (gb_ref, ob_ref))):
        edge = L - 1 if d == 0 else 0
        li = g_ref[0]
        lf = _log_sigmoid(g_ref[1])
        b = _lane_scan(lf, jnp.add, 0.0, reverse=(d == 1))
        r1 = li - b
        cm = _lane_scan(r1, jnp.maximum, -jnp.inf, reverse=(d == 1))
        m = m_s[d]
        mx = jnp.maximum(m, cm[:, edge:edge + 1])
        m_s[d] = b[:, edge:edge + 1] + mx
        mc = jnp.maximum(m, cm)
        o_ref[0] = r1
        o_ref[1] = mc
        o_ref[2] = jnp.broadcast_to(m, r1.shape)
        o_ref[3] = jnp.exp(-(b + mc))
        o_ref[4] = jnp.exp(r1 - mx)
        o_ref[5] = jnp.broadcast_to(jnp.exp(m - mx), r1.shape)


def _gate_prep(gates_c, gates_x):
    bsz, _, tc = gates_c.shape
    t = gates_x.shape[2]
    n = tc + t
    r = bsz * M_HEADS
    nc = n // M_CHUNK

    def rowform(g, first_kind):
        g = g.reshape(bsz, 4, M_HEADS, n)[:, first_kind:first_kind + 2]
        return g.transpose(1, 0, 2, 3).reshape(2, r, n)

    gf = rowform(jnp.concatenate([gates_c, gates_x], axis=2), 0)
    gb = rowform(jnp.concatenate([gates_x, gates_c], axis=2), 2)
    outf, outb = pl.pallas_call(
        _gate_kernel,
        grid=(nc,),
        in_specs=[pl.BlockSpec((2, r, M_CHUNK), lambda j: (0, 0, j)),
                  pl.BlockSpec((2, r, M_CHUNK), lambda j: (0, 0, nc - 1 - j))],
        out_specs=[pl.BlockSpec((6, r, M_CHUNK), lambda j: (0, 0, j)),
                   pl.BlockSpec((6, r, M_CHUNK), lambda j: (0, 0, nc - 1 - j))],
        out_shape=[jax.ShapeDtypeStruct((6, r, n), F32)] * 2,
        scratch_shapes=[pltpu.VMEM((2, r, 1), F32)],
        compiler_params=_params("arbitrary"),
        name="mlstm_gates",
    )(gf, gb)
    outf = outf.reshape(6, bsz, M_HEADS, n)
    outb = outb.reshape(6, bsz, M_HEADS, n)

    def pack(f, b):
        cols = jnp.stack([f[1], f[3], b[1], b[3]], axis=-1)
        rows = jnp.stack([f[0], f[4], f[5], f[2], b[0], b[4], b[5], b[2]], axis=2)
        return cols, rows

    return pack(outf[..., tc:], outb[..., :t]), pack(outf[..., :tc], outb[..., t:])


def _scan_kernel(nsteps, with_outputs, *refs):
    if with_outputs:
        (qf, ktf, vf, cf, rf, qb, ktb, vb, cb, rb, c0, h_out, c_s) = refs
        streams = ((qf, ktf, vf, cf, rf), (qb, ktb, vb, cb, rb))
    else:
        (ktf, vf, rf, ktb, vb, rb, c_s) = refs
        streams = ((None, ktf, vf, None, rf), (None, ktb, vb, None, rb))
    s = pl.program_id(1)
    L = M_CHUNK
    dk = ktf.shape[1] // M_HEADS
    dv = vf.shape[2] // M_HEADS

    @pl.when(s == 0)
    def _():
        if with_outputs:
            c_s[...] = c0[...]
        else:
            c_s[...] = jnp.zeros(c_s.shape, F32)

    row = lax.broadcasted_iota(jnp.int32, (L, L), 0)
    col = lax.broadcasted_iota(jnp.int32, (L, L), 1)
    ones = jnp.ones((L, LANES), BF16)
    results = []
    for d in range(2):
        q_ref, kt_ref, v_ref, col_ref, row_ref = streams[d]
        mask = (col <= row) if d == 0 else (col >= row)
        for h in range(M_HEADS):
            kt = kt_ref[0, h * dk:(h + 1) * dk, :]
            v1 = jnp.concatenate([v_ref[0, :, h * dv:(h + 1) * dv], ones], axis=1)
            rows = row_ref[0, h, 4 * d:4 * d + 4, :]
            c_old = c_s[0, d, h]
            kw = (kt.astype(F32) * rows[1:2]).astype(BF16)
            c_s[0, d, h] = rows[2:3, 0:1] * c_old + _bdot(kw, v1)
            if not with_outputs:
                continue
            q = q_ref[0, :, h * dk:(h + 1) * dk]
            cols = col_ref[0, h]
            mc = cols[:, 2 * d:2 * d + 1]
            fl = cols[:, 2 * d + 1:2 * d + 2]
            p = jnp.exp(jnp.where(mask, rows[0:1] - mc, -jnp.inf))
            wi = jnp.exp(rows[3:4] - mc)
            lhs = jnp.concatenate([_bdot(q, kt) * p, q.astype(F32) * wi], axis=1).astype(BF16)
            rhs = jnp.concatenate([v1, c_old.astype(BF16)], axis=0)
            out = _bdot(lhs, rhs)
            inv = 1.0 / jnp.maximum(jnp.abs(out[:, dv:]), fl)
            results.append((d, h, out[:, :dv] * jnp.concatenate([inv] * (dv // LANES), axis=1)))

    if with_outputs:
        def emit(d, accumulate):
            chunk = s if d == 0 else nsteps - 1 - s
            rows = pl.ds(pl.multiple_of(chunk * L, L), L)
            for dd, h, val in results:
                if dd != d:
                    continue
                if accumulate:
                    h_out[0, rows, h * dv:(h + 1) * dv] += val
                else:
                    h_out[0, rows, h * dv:(h + 1) * dv] = val

        fwd_first = 2 * s <= nsteps - 1
        bwd_first = 2 * s < nsteps - 1
        pl.when(fwd_first)(lambda: emit(0, False))
        pl.when(jnp.logical_not(fwd_first))(lambda: emit(0, True))
        pl.when(bwd_first)(lambda: emit(1, False))
        pl.when(jnp.logical_not(bwd_first))(lambda: emit(1, True))


def _scan(q, kt, v, cols, rows, init):
    bsz, hk, t = kt.shape
    hv = v.shape[2]
    dk, dv = hk // M_HEADS, hv // M_HEADS
    assert dk == M_CHUNK == LANES and dv % LANES == 0
    nc = t // M_CHUNK
    L = M_CHUNK
    with_outputs = init is not None

    def specs(cidx):
        sp = [pl.BlockSpec((1, hk, L), lambda b, s: (b, 0, cidx(s))),
              pl.BlockSpec((1, L, hv), lambda b, s: (b, cidx(s), 0)),
              pl.BlockSpec((1, M_HEADS, 8, L), lambda b, s: (b, 0, 0, cidx(s)))]
        if with_outputs:
            sp = ([pl.BlockSpec((1, L, hk), lambda b, s: (b, cidx(s), 0))] + sp[:2]
                  + [pl.BlockSpec((1, M_HEADS, L, 4), lambda b, s: (b, 0, cidx(s), 0))] + sp[2:])
        return sp

    args = [q, kt, v, cols, rows] if with_outputs else [kt, v, rows]
    args = args * 2
    in_specs = specs(lambda s: s) + specs(lambda s: nc - 1 - s)
    state_shape = (bsz, 2, M_HEADS, dk, dv + LANES)
    state_spec = pl.BlockSpec((1,) + state_shape[1:], lambda b, s: (b, 0, 0, 0, 0))
    if with_outputs:
        args.append(init)
        in_specs.append(state_spec)
        out_specs = pl.BlockSpec((1, t, hv), lambda b, s: (b, 0, 0))
        out_shape = jax.ShapeDtypeStruct((bsz, t, hv), F32)
        scratch = [pltpu.VMEM((1,) + state_shape[1:], F32)]
    else:
        out_specs = state_spec
        out_shape = jax.ShapeDtypeStruct(state_shape, F32)
        scratch = []
    return pl.pallas_call(
        functools.partial(_scan_kernel, nc, with_outputs),
        grid=(bsz, nc),
        in_specs=in_specs,
        out_specs=out_specs,
        out_shape=out_shape,
        scratch_shapes=scratch,
        compiler_params=_params("parallel", "arbitrary"),
        name="mlstm_scan" if with_outputs else "mlstm_ctx_state",
    )(*args)


def _readout_kernel(h_ref, o_ref, x_ref, ng_ref, w_ref, pg_ref, gate_ref, out_ref):
    h = h_ref[0]
    dv = h.shape[1] // M_HEADS
    parts = []
    for hd in range(M_HEADS):
        hh = h[:, hd * dv:(hd + 1) * dv]
        parts.append(hh * lax.rsqrt(jnp.mean(hh * hh, axis=-1, keepdims=True) + EPS))
    hn = jnp.concatenate(parts, axis=-1) * ng_ref[...]
    y = (hn * o_ref[0].astype(F32)).astype(BF16)
    z = _bdot(y, w_ref[...])
    out_ref[0] = x_ref[0] + gate_ref[0] * _rms(z, pg_ref[...])


def _readout(h, sig_o, x, norm_g, w_out, post_g, gate, tm):
    bsz, t, d = x.shape
    hv = h.shape[2]
    tok = lambda n: pl.BlockSpec((1, tm, n), lambda b, i: (b, i, 0))
    full = lambda a: pl.BlockSpec(a.shape, lambda b, i: (0, 0))
    return pl.pallas_call(
        _readout_kernel,
        grid=(bsz, t // tm),
        in_specs=[tok(hv), tok(hv), tok(d), full(norm_g), full(w_out), full(post_g),
                  pl.BlockSpec((1, 1, d), lambda b, i: (b, 0, 0))],
        out_specs=tok(d),
        out_shape=jax.ShapeDtypeStruct((bsz, t, d), F32),
        compiler_params=_params("parallel", "parallel"),
        name="mlstm_readout",
    )(h, sig_o, x, norm_g, w_out, post_g, gate)


def _ffn_kernel(fc, x_ref, sh_ref, sc_ref, gate_ref, pre_ref, post_ref, wu_ref, wg_ref, cw_ref, cb_ref,
                wd_ref, out_ref, act_ref):
    x = x_ref[0]
    tm = x.shape[0]
    f = wu_ref.shape[1]
    hb = (_rms(x, pre_ref[...]) * (1.0 + sc_ref[0]) + sh_ref[0]).astype(BF16)
    pos = lax.broadcasted_iota(jnp.int32, (tm, fc), 0) % GRID_W
    has_prev = pos != 0
    has_next = pos != GRID_W - 1
    for j in range(f // fc):
        cs = slice(j * fc, (j + 1) * fc)
        u = _bdot(hb, wu_ref[:, cs])
        g = _bdot(hb, wg_ref[:, cs])
        g_prev = jnp.where(has_prev, pltpu.roll(g, 1, 0), 0.0)
        g_next = jnp.where(has_next, pltpu.roll(g, tm - 1, 0), 0.0)
        gc = g_prev * cw_ref[0:1, cs] + g * cw_ref[1:2, cs] + g_next * cw_ref[2:3, cs] + cb_ref[:, cs]
        act_ref[:, cs] = (gc * jax.nn.sigmoid(gc) * u).astype(BF16)
    y = _bdot(act_ref[...], wd_ref[...])
    out_ref[0] = x + gate_ref[0] * _rms(y, post_ref[...])


def _ffn(x, sh, sc, gate, pre_g, post_g, wu, wg, cw, cb, wd, tm, fc):
    bsz, t, d = x.shape
    f = wu.shape[1]
    tok = pl.BlockSpec((1, tm, d), lambda b, i: (b, i, 0))
    mod = pl.BlockSpec((1, 1, d), lambda b, i: (b, 0, 0))
    full = lambda a: pl.BlockSpec(a.shape, lambda b, i: (0, 0))
    return pl.pallas_call(
        functools.partial(_ffn_kernel, fc),
        grid=(bsz, t // tm),
        in_specs=[tok, mod, mod, mod, full(pre_g), full(post_g), full(wu), full(wg), full(cw), full(cb),
                  full(wd)],
        out_specs=tok,
        out_shape=jax.ShapeDtypeStruct((bsz, t, d), F32),
        scratch_shapes=[pltpu.VMEM((tm, f), BF16)],
        compiler_params=_params("parallel", "parallel"),
        name="conv_ffn",
    )(x, sh, sc, gate, pre_g, post_g, wu, wg, cw, cb, wd)


def _fourier_channel_kernel(x_ref, sh_ref, sc_ref, pre_ref, cs_ref, ab_ref):
    hb = (_rms(x_ref[0], pre_ref[...]) * (1.0 + sc_ref[0]) + sh_ref[0]).astype(BF16)
    gd = cs_ref.shape[0]
    for g in range(hb.shape[1] // gd):
        r = _bdot(hb[:, g * gd:(g + 1) * gd], cs_ref[...])
        ab_ref[0, 0, :, g * gd:(g + 1) * gd] = r[:, :gd].astype(BF16)
        ab_ref[0, 1, :, g * gd:(g + 1) * gd] = r[:, gd:].astype(BF16)


def _fourier_channel(x, sh, sc, pre_g, cs, tm):
    bsz, t, d = x.shape
    mod = pl.BlockSpec((1, 1, d), lambda b, i: (b, 0, 0))
    full = lambda a: pl.BlockSpec(a.shape, lambda b, i: (0, 0))
    return pl.pallas_call(
        _fourier_channel_kernel,
        grid=(bsz, t // tm),
        in_specs=[pl.BlockSpec((1, tm, d), lambda b, i: (b, i, 0)), mod, mod, full(pre_g), full(cs)],
        out_specs=pl.BlockSpec((1, 2, tm, d), lambda b, i: (b, 0, i, 0)),
        out_shape=jax.ShapeDtypeStruct((bsz, 2, t, d), BF16),
        compiler_params=_params("parallel", "parallel"),
        name="fourier_channel_dft",
    )(x, sh, sc, pre_g, cs)


def _fourier_token_kernel(scale, dft_ref, ab_ref, w_ref, bias_ref, post_ref, gate_ref, x_ref, out_ref):
    y = _bdot(dft_ref[...], ab_ref[0]) * scale
    z = _bdot(y.astype(BF16), w_ref[...]) + bias_ref[...]
    out_ref[0] = x_ref[0] + gate_ref[0] * _rms(z, post_ref[...])


def _fourier_token(dft, ab, w, bias, post_g, gate, x, scale, tm):
    bsz, t, d = x.shape
    full = lambda a: pl.BlockSpec(a.shape, lambda b, i: (0, 0))
    tok = pl.BlockSpec((1, tm, d), lambda b, i: (b, i, 0))
    return pl.pallas_call(
        functools.partial(_fourier_token_kernel, scale),
        grid=(bsz, t // tm),
        in_specs=[pl.BlockSpec((tm, 2 * t), lambda b, i: (i, 0)),
                  pl.BlockSpec((1, 2 * t, d), lambda b, i: (b, 0, 0)),
                  full(w), full(bias), full(post_g), pl.BlockSpec((1, 1, d), lambda b, i: (b, 0, 0)), tok],
        out_specs=tok,
        out_shape=jax.ShapeDtypeStruct((bsz, t, d), F32),
        compiler_params=_params("parallel", "arbitrary"),
        name="fourier_token_dft",
    )(dft, ab, w, bias, post_g, gate, x)


def _dft_tables(t, gd):
    def cs(n):
        idx = np.arange(n, dtype=np.int64)
        ang = 2.0 * np.pi * ((idx[:, None] * idx[None, :]) % n).astype(np.float64) / n
        return np.cos(ang), np.sin(ang)
    cg, sg = cs(gd)
    ct, st = cs(t)
    chan = np.concatenate([cg, sg], axis=1).astype(np.float32)
    tok = np.concatenate([ct, -st], axis=1).astype(np.float32)
    return jnp.asarray(chan).astype(BF16), jnp.asarray(tok).astype(BF16)


def _row(v):
    return v.reshape(1, -1)


def kernel(x, c, ctx, c_ctx, ada_w, ada_b, pre_mix_g, post_mix_g, pre_ffn_g, post_ffn_g, ffn_up_w, ffn_conv_w,
           ffn_conv_b, ffn_down_w, m_in_w, m_in_b, m_norm_g, m_out_w, f_out_w, f_out_b):
    bsz, t, d = x.shape
    tc = ctx.shape[1]
    f = ffn_down_w.shape[1]
    assert ada_w.shape[0] == 2 and m_in_w.shape[0] == 1 and f_out_w.shape[0] == 1
    assert bsz + 1 <= COND_ROWS and t % M_CHUNK == 0 and tc % M_CHUNK == 0 and t % GRID_W == 0
    hv = m_out_w.shape[1]
    hk = (m_in_w.shape[2] - 2 * hv - 4 * M_HEADS) // 2
    tm = min(512, t)
    fc = 256
    assert f % fc == 0 and t % tm == 0 and tm % GRID_W == 0

    cond = jnp.zeros((COND_ROWS, d), F32).at[:bsz].set(c).at[bsz].set(c_ctx)
    mod = _ada(cond, ada_w, ada_b)
    lat = mod[:, :bsz].reshape(2, bsz, N_MOD, 1, d)
    cmod = mod[0, bsz].reshape(N_MOD, 1, 1, d)

    def ffn(xin, i):
        wu = ffn_up_w[i, :, :f].astype(BF16)
        wg = ffn_up_w[i, :, f:].astype(BF16)
        return _ffn(xin, lat[i, :, 3], lat[i, :, 4], lat[i, :, 5], _row(pre_ffn_g[i]), _row(post_ffn_g[i]),
                    wu, wg, ffn_conv_w[i], _row(ffn_conv_b[i]), ffn_down_w[i].astype(BF16), tm, fc)

    w_in = m_in_w[0]
    b_in = m_in_b[0]
    segs = [0, hk, 2 * hk, 2 * hk + hv, 2 * hk + 2 * hv]
    ws = [w_in[:, segs[j]:segs[j + 1]].astype(BF16) for j in range(4)]
    bs = [_row(b_in[segs[j]:segs[j + 1]]) for j in range(4)]
    ngate = 4 * M_HEADS
    ws.append(jnp.pad(w_in[:, segs[4]:], ((0, 0), (0, LANES - ngate))).astype(BF16))
    bs.append(_row(jnp.pad(b_in[segs[4]:], (0, LANES - ngate))))
    pre0 = _row(pre_mix_g[0])

    _, kc, vc, _, gc = _inproj(ctx, cmod[0], cmod[1], pre0, ws, bs, min(tm, tc))
    qx, kx, vx, ox, gx = _inproj(x, lat[0, :, 0], lat[0, :, 1], pre0, ws, bs, tm)
    (cols_x, rows_x), (_, rows_c) = _gate_prep(gc, gx)
    state = _scan(None, kc, vc, None, rows_c, None)
    h = _scan(qx, kx, vx, cols_x, rows_x, state)
    x = _readout(h, ox, x, _row(m_norm_g[0]), m_out_w[0].astype(BF16), _row(post_mix_g[0]), lat[0, :, 2], tm)
    x = ffn(x, 0)

    gd = d // F_GROUPS
    chan, tok = _dft_tables(t, gd)
    ab = _fourier_channel(x, lat[1, :, 0], lat[1, :, 1], _row(pre_mix_g[1]), chan, tm)
    x = _fourier_token(tok, ab.reshape(bsz, 2 * t, d), f_out_w[0].astype(BF16), _row(f_out_b[0]),
                       _row(post_mix_g[1]), lat[1, :, 2], x, float(1.0 / np.sqrt(t * gd)), tm)
    x = ffn(x, 1)
    return x
```

```python
import functools

import numpy as np
import jax
import jax.numpy as jnp
from jax import lax
from jax.experimental import pallas as pl
from jax.experimental.pallas import tpu as pltpu

F32 = jnp.float32
BF16 = jnp.bfloat16

M_HEADS = 4
M_CHUNK = 128
F_GROUPS = 8
GRID_W = 64
EPS = 1e-6
N_MOD = 6
COND_ROWS = 16

VMEM_LIMIT_BYTES = 56 * 1024 * 1024
LANES = 128


def _params(*sem):
    return pltpu.CompilerParams(dimension_semantics=sem, vmem_limit_bytes=VMEM_LIMIT_BYTES)


def _rms(x, g):
    return x * lax.rsqrt(jnp.mean(x * x, axis=-1, keepdims=True) + EPS) * g


def _bdot(a, b):
    return jnp.dot(a, b, preferred_element_type=F32)


def _ada_kernel(c_ref, w_ref, b_ref, o_ref):
    c = c_ref[...]
    s = c * jax.nn.sigmoid(c)
    o_ref[0] = jnp.dot(s, w_ref[0], precision=lax.Precision.HIGHEST,
                       preferred_element_type=F32) + b_ref[0]


def _ada(cond, ada_w, ada_b):
    depth, d, n = ada_w.shape
    tn = n // 4
    return pl.pallas_call(
        _ada_kernel,
        grid=(depth, n // tn),
        in_specs=[pl.BlockSpec((COND_ROWS, d), lambda i, j: (0, 0)),
                  pl.BlockSpec((1, d, tn), lambda i, j: (i, 0, j)),
                  pl.BlockSpec((1, 1, tn), lambda i, j: (i, 0, j))],
        out_specs=pl.BlockSpec((1, COND_ROWS, tn), lambda i, j: (i, 0, j)),
        out_shape=jax.ShapeDtypeStruct((depth, COND_ROWS, n), F32),
        compiler_params=_params("arbitrary", "arbitrary"),
        name="ada",
    )(cond, ada_w, ada_b.reshape(depth, 1, n))


def _inproj_kernel(dk, x_ref, sh_ref, sc_ref, g_ref, wq, wk, wv, wo, wg, bq, bk, bv, bo, bg,
                   q_out, kt_out, v_out, o_out, gate_out):
    h = _rms(x_ref[0], g_ref[...]) * (1.0 + sc_ref[0]) + sh_ref[0]
    hb = h.astype(BF16)
    q_out[0] = (_bdot(hb, wq[...]) + bq[...]).astype(BF16)
    kt_out[0] = jnp.transpose((_bdot(hb, wk[...]) + bk[...]) * (dk ** -0.5)).astype(BF16)
    v_out[0] = (_bdot(hb, wv[...]) + bv[...]).astype(BF16)
    o_out[0] = jax.nn.sigmoid(_bdot(hb, wo[...]) + bo[...]).astype(BF16)
    gate_out[0] = jnp.transpose(_bdot(hb, wg[...]) + bg[...])[:gate_out.shape[1]]


def _inproj(x, sh, sc, g, ws, bs, tm):
    bsz, t, d = x.shape
    hk = ws[0].shape[1]
    hv = ws[2].shape[1]
    dk = hk // M_HEADS
    per_batch = sh.shape[0] > 1
    mod_spec = pl.BlockSpec((1, 1, d), (lambda b, i: (b, 0, 0)) if per_batch else (lambda b, i: (0, 0, 0)))
    full = lambda a: pl.BlockSpec(a.shape, lambda b, i: (0, 0))
    tok = lambda n: pl.BlockSpec((1, tm, n), lambda b, i: (b, i, 0))
    tok_t = lambda n: pl.BlockSpec((1, n, tm), lambda b, i: (b, 0, i))
    return pl.pallas_call(
        functools.partial(_inproj_kernel, dk),
        grid=(bsz, t // tm),
        in_specs=[tok(d), mod_spec, mod_spec, full(g)] + [full(w) for w in ws] + [full(b) for b in bs],
        out_specs=[tok(hk), tok_t(hk), tok(hv), tok(hv), tok_t(4 * M_HEADS)],
        out_shape=[jax.ShapeDtypeStruct((bsz, t, hk), BF16), jax.ShapeDtypeStruct((bsz, hk, t), BF16),
                   jax.ShapeDtypeStruct((bsz, t, hv), BF16), jax.ShapeDtypeStruct((bsz, t, hv), BF16),
                   jax.ShapeDtypeStruct((bsz, 4 * M_HEADS, t), F32)],
        compiler_params=_params("parallel", "parallel"),
        name="mlstm_inproj",
    )(x, sh, sc, g, *ws, *bs)


def _lane_scan(x, op, fill, reverse):
    n = x.shape[-1]
    lane = lax.broadcasted_iota(jnp.int32, x.shape, x.ndim - 1)
    d = 1
    while d < n:
        if reverse:
            shifted = jnp.where(lane < n - d, pltpu.roll(x, n - d, x.ndim - 1), fill)
        else:
            shifted = jnp.where(lane >= d, pltpu.roll(x, d, x.ndim - 1), fill)
        x = op(x, shifted)
        d *= 2
    return x


def _log_sigmoid(x):
    return jnp.minimum(x, 0.0) - jnp.log1p(jnp.exp(-jnp.abs(x)))


def _gate_kernel(gf_ref, gb_ref, of_ref, ob_ref, m_s):
    L = gf_ref.shape[-1]

    @pl.when(pl.program_id(0) == 0)
    def _():
        m_s[...] = jnp.zeros(m_s.shape, F32)

    for d, (g_ref, o_ref) in enumerate(((gf_ref, of_ref), (gb_ref, ob_ref))):
        edge = L - 1 if d == 0 else 0
        li = g_ref[0]
        lf = _log_sigmoid(g_ref[1])
        b = _lane_scan(lf, jnp.add, 0.0, reverse=(d == 1))
        r1 = li - b
        cm = _lane_scan(r1, jnp.maximum, -jnp.inf, reverse=(d == 1))
        m = m_s[d]
        mx = jnp.maximum(m, cm[:, edge:edge + 1])
        m_s[d] = b[:, edge:edge + 1] + mx
        mc = jnp.maximum(m, cm)
        o_ref[0] = r1
        o_ref[1] = mc
        o_ref[2] = jnp.broadcast_to(m, r1.shape)
        o_ref[3] = jnp.exp(-(b + mc))
        o_ref[4] = jnp.exp(r1 - mx)
        o_ref[5] = jnp.broadcast_to(jnp.exp(m - mx), r1.shape)


def _gate_prep(gates_c, gates_x):
    bsz, _, tc = gates_c.shape
    t = gates_x.shape[2]
    n = tc + t
    r = bsz * M_HEADS
    nc = n // M_CHUNK

    def rowform(g, first_kind):
        g = g.reshape(bsz, 4, M_HEADS, n)[:, first_kind:first_kind + 2]
        return g.transpose(1, 0, 2, 3).reshape(2, r, n)

    gf = rowform(jnp.concatenate([gates_c, gates_x], axis=2), 0)
    gb = rowform(jnp.concatenate([gates_x, gates_c], axis=2), 2)
    outf, outb = pl.pallas_call(
        _gate_kernel,
        grid=(nc,),
        in_specs=[pl.BlockSpec((2, r, M_CHUNK), lambda j: (0, 0, j)),
                  pl.BlockSpec((2, r, M_CHUNK), lambda j: (0, 0, nc - 1 - j))],
        out_specs=[pl.BlockSpec((6, r, M_CHUNK), lambda j: (0, 0, j)),
                   pl.BlockSpec((6, r, M_CHUNK), lambda j: (0, 0, nc - 1 - j))],
        out_shape=[jax.ShapeDtypeStruct((6, r, n), F32)] * 2,
        scratch_shapes=[pltpu.VMEM((2, r, 1), F32)],
        compiler_params=_params("arbitrary"),
        name="mlstm_gates",
    )(gf, gb)
    outf = outf.reshape(6, bsz, M_HEADS, n)
    outb = outb.reshape(6, bsz, M_HEADS, n)

    def pack(f, b):
        cols = jnp.stack([f[1], f[3], b[1], b[3]], axis=-1)
        rows = jnp.stack([f[0], f[4], f[5], f[2], b[0], b[4], b[5], b[2]], axis=2)
        return cols, rows

    return pack(outf[..., tc:], outb[..., :t]), pack(outf[..., :tc], outb[..., t:])


def _scan_kernel(nsteps, with_outputs, *refs):
    if with_outputs:
        (qf, ktf, vf, cf, rf, qb, ktb, vb, cb, rb, c0, h_out, c_s) = refs
        streams = ((qf, ktf, vf, cf, rf), (qb, ktb, vb, cb, rb))
    else:
        (ktf, vf, rf, ktb, vb, rb, c_s) = refs
        streams = ((None, ktf, vf, None, rf), (None, ktb, vb, None, rb))
    s = pl.program_id(1)
    L = M_CHUNK
    dk = ktf.shape[1] // M_HEADS
    dv = vf.shape[2] // M_HEADS

    @pl.when(s == 0)
    def _():
        if with_outputs:
            c_s[...] = c0[...]
            h_out[...] = jnp.zeros(h_out.shape, F32)
        else:
            c_s[...] = jnp.zeros(c_s.shape, F32)

    row = lax.broadcasted_iota(jnp.int32, (L, L), 0)
    col = lax.broadcasted_iota(jnp.int32, (L, L), 1)
    ones = jnp.ones((L, LANES), BF16)
    tok = [pl.ds(pl.multiple_of(c * L, L), L) for c in (s, nsteps - 1 - s)]
    for d in range(2):
        q_ref, kt_ref, v_ref, col_ref, row_ref = streams[d]
        mask = (col <= row) if d == 0 else (col >= row)
        for h in range(M_HEADS):
            kt = kt_ref[0, h * dk:(h + 1) * dk, :]
            v1 = jnp.concatenate([v_ref[0, :, h * dv:(h + 1) * dv], ones], axis=1)
            rows = row_ref[0, h, 4 * d:4 * d + 4, :]
            c_old = c_s[0, d, h]
            kw = (kt.astype(F32) * rows[1:2]).astype(BF16)
            c_s[0, d, h] = rows[2:3, 0:1] * c_old + _bdot(kw, v1)
            if not with_outputs:
                continue
            q = q_ref[0, :, h * dk:(h + 1) * dk]
            cols = col_ref[0, h]
            mc = cols[:, 2 * d:2 * d + 1]
            fl = cols[:, 2 * d + 1:2 * d + 2]
            p = jnp.exp(jnp.where(mask, rows[0:1] - mc, -jnp.inf))
            wi = jnp.exp(rows[3:4] - mc)
            lhs = jnp.concatenate([_bdot(q, kt) * p, q.astype(F32) * wi], axis=1).astype(BF16)
            rhs = jnp.concatenate([v1, c_old.astype(BF16)], axis=0)
            out = _bdot(lhs, rhs)
            inv = 1.0 / jnp.maximum(jnp.abs(out[:, dv:]), fl)
            h_out[0, tok[d], h * dv:(h + 1) * dv] += out[:, :dv] * jnp.concatenate([inv] * (dv // LANES), axis=1)


def _scan(q, kt, v, cols, rows, init):
    bsz, hk, t = kt.shape
    hv = v.shape[2]
    dk, dv = hk // M_HEADS, hv // M_HEADS
    assert dk == M_CHUNK == LANES and dv % LANES == 0
    nc = t // M_CHUNK
    L = M_CHUNK
    with_outputs = init is not None

    def specs(cidx):
        sp = [pl.BlockSpec((1, hk, L), lambda b, s: (b, 0, cidx(s))),
              pl.BlockSpec((1, L, hv), lambda b, s: (b, cidx(s), 0)),
              pl.BlockSpec((1, M_HEADS, 8, L), lambda b, s: (b, 0, 0, cidx(s)))]
        if with_outputs:
            sp = ([pl.BlockSpec((1, L, hk), lambda b, s: (b, cidx(s), 0))] + sp[:2]
                  + [pl.BlockSpec((1, M_HEADS, L, 4), lambda b, s: (b, 0, cidx(s), 0))] + sp[2:])
        return sp

    args = [q, kt, v, cols, rows] if with_outputs else [kt, v, rows]
    args = args * 2
    in_specs = specs(lambda s: s) + specs(lambda s: nc - 1 - s)
    state_shape = (bsz, 2, M_HEADS, dk, dv + LANES)
    state_spec = pl.BlockSpec((1,) + state_shape[1:], lambda b, s: (b, 0, 0, 0, 0))
    if with_outputs:
        args.append(init)
        in_specs.append(state_spec)
        out_specs = pl.BlockSpec((1, t, hv), lambda b, s: (b, 0, 0))
        out_shape = jax.ShapeDtypeStruct((bsz, t, hv), F32)
        scratch = [pltpu.VMEM((1,) + state_shape[1:], F32)]
    else:
        out_specs = state_spec
        out_shape = jax.ShapeDtypeStruct(state_shape, F32)
        scratch = []
    return pl.pallas_call(
        functools.partial(_scan_kernel, nc, with_outputs),
        grid=(bsz, nc),
        in_specs=in_specs,
        out_specs=out_specs,
        out_shape=out_shape,
        scratch_shapes=scratch,
        compiler_params=_params("parallel", "arbitrary"),
        name="mlstm_scan" if with_outputs else "mlstm_ctx_state",
    )(*args)


def _readout_kernel(h_ref, o_ref, x_ref, ng_ref, w_ref, pg_ref, gate_ref, out_ref):
    h = h_ref[0]
    dv = h.shape[1] // M_HEADS
    parts = []
    for hd in range(M_HEADS):
        hh = h[:, hd * dv:(hd + 1) * dv]
        parts.append(hh * lax.rsqrt(jnp.mean(hh * hh, axis=-1, keepdims=True) + EPS))
    hn = jnp.concatenate(parts, axis=-1) * ng_ref[...]
    y = (hn * o_ref[0].astype(F32)).astype(BF16)
    z = _bdot(y, w_ref[...])
    out_ref[0] = x_ref[0] + gate_ref[0] * _rms(z, pg_ref[...])


def _readout(h, sig_o, x, norm_g, w_out, post_g, gate, tm):
    bsz, t, d = x.shape
    hv = h.shape[2]
    tok = lambda n: pl.BlockSpec((1, tm, n), lambda b, i: (b, i, 0))
    full = lambda a: pl.BlockSpec(a.shape, lambda b, i: (0, 0))
    return pl.pallas_call(
        _readout_kernel,
        grid=(bsz, t // tm),
        in_specs=[tok(hv), tok(hv), tok(d), full(norm_g), full(w_out), full(post_g),
                  pl.BlockSpec((1, 1, d), lambda b, i: (b, 0, 0))],
        out_specs=tok(d),
        out_shape=jax.ShapeDtypeStruct((bsz, t, d), F32),
        compiler_params=_params("parallel", "parallel"),
        name="mlstm_readout",
    )(h, sig_o, x, norm_g, w_out, post_g, gate)


def _ffn_kernel(fc, x_ref, sh_ref, sc_ref, gate_ref, pre_ref, post_ref, wu_ref, wg_ref, cw_ref, cb_ref,
                wd_ref, out_ref, act_ref):
    x = x_ref[0]
    tm = x.shape[0]
    f = wu_ref.shape[1]
    hb = (_rms(x, pre_ref[...]) * (1.0 + sc_ref[0]) + sh_ref[0]).astype(BF16)
    pos = lax.broadcasted_iota(jnp.int32, (tm, fc), 0) % GRID_W
    has_prev = pos != 0
    has_next = pos != GRID_W - 1
    for j in range(f // fc):
        cs = slice(j * fc, (j + 1) * fc)
        u = _bdot(hb, wu_ref[:, cs])
        g = _bdot(hb, wg_ref[:, cs])
        g_prev = jnp.where(has_prev, pltpu.roll(g, 1, 0), 0.0)
        g_next = jnp.where(has_next, pltpu.roll(g, tm - 1, 0), 0.0)
        gc = g_prev * cw_ref[0:1, cs] + g * cw_ref[1:2, cs] + g_next * cw_ref[2:3, cs] + cb_ref[:, cs]
        act_ref[:, cs] = (gc * jax.nn.sigmoid(gc) * u).astype(BF16)
    y = _bdot(act_ref[...], wd_ref[...])
    out_ref[0] = x + gate_ref[0] * _rms(y, post_ref[...])


def _ffn(x, sh, sc, gate, pre_g, post_g, wu, wg, cw, cb, wd, tm, fc):
    bsz, t, d = x.shape
    f = wu.shape[1]
    tok = pl.BlockSpec((1, tm, d), lambda b, i: (b, i, 0))
    mod = pl.BlockSpec((1, 1, d), lambda b, i: (b, 0, 0))
    full = lambda a: pl.BlockSpec(a.shape, lambda b, i: (0, 0))
    return pl.pallas_call(
        functools.partial(_ffn_kernel, fc),
        grid=(bsz, t // tm),
        in_specs=[tok, mod, mod, mod, full(pre_g), full(post_g), full(wu), full(wg), full(cw), full(cb),
                  full(wd)],
        out_specs=tok,
        out_shape=jax.ShapeDtypeStruct((bsz, t, d), F32),
        scratch_shapes=[pltpu.VMEM((tm, f), BF16)],
        compiler_params=_params("parallel", "parallel"),
        name="conv_ffn",
    )(x, sh, sc, gate, pre_g, post_g, wu, wg, cw, cb, wd)


RADIX = 4


def _fourier_channel_kernel(x_ref, sh_ref, sc_ref, pre_ref, cs_ref, g_ref):
    gd = cs_ref.shape[0]
    hb = [(_rms(x_ref[0, q], pre_ref[...]) * (1.0 + sc_ref[0]) + sh_ref[0]).astype(BF16) for q in range(RADIX)]
    for g in range(hb[0].shape[1] // gd):
        lanes = slice(g * gd, (g + 1) * gd)
        r = [_bdot(h[:, lanes], cs_ref[...]) for h in hb]
        a = [v[:, :gd] for v in r]
        b = [v[:, gd:] for v in r]
        a02p, a02m, a13p, a13m = a[0] + a[2], a[0] - a[2], a[1] + a[3], a[1] - a[3]
        b02p, b02m, b13p, b13m = b[0] + b[2], b[0] - b[2], b[1] + b[3], b[1] - b[3]
        re_im = ((a02p + a13p, -(b02p + b13p)),
                 (a02m - b13m, -(b02m + a13m)),
                 (a02p - a13p, b13p - b02p),
                 (a02m + b13m, a13m - b02m))
        for k, (re, im) in enumerate(re_im):
            g_ref[0, k, 0, :, lanes] = re.astype(BF16)
            g_ref[0, k, 1, :, lanes] = im.astype(BF16)


def _fourier_channel(x, sh, sc, pre_g, cs, tm):
    bsz, t, d = x.shape
    tq = t // RADIX
    mod = pl.BlockSpec((1, 1, d), lambda b, i: (b, 0, 0))
    full = lambda a: pl.BlockSpec(a.shape, lambda b, i: (0, 0))
    return pl.pallas_call(
        _fourier_channel_kernel,
        grid=(bsz, tq // tm),
        in_specs=[pl.BlockSpec((1, RADIX, tm, d), lambda b, i: (b, 0, i, 0)), mod, mod, full(pre_g), full(cs)],
        out_specs=pl.BlockSpec((1, RADIX, 2, tm, d), lambda b, i: (b, 0, 0, i, 0)),
        out_shape=jax.ShapeDtypeStruct((bsz, RADIX, 2, tq, d), BF16),
        compiler_params=_params("parallel", "parallel"),
        name="fourier_channel_dft",
    )(x.reshape(bsz, RADIX, tq, d), sh, sc, pre_g, cs)


def _fourier_token_kernel(scale, dft_ref, g_ref, w_ref, bias_ref, post_ref, gate_ref, x_ref, out_ref):
    d = w_ref.shape[0]
    for r in range(RADIX):
        y = _bdot(dft_ref[r], g_ref[0, r]) * scale
        z = _bdot(y.astype(BF16), w_ref[...]) + bias_ref[...]
        out_ref[0, :, r * d:(r + 1) * d] = x_ref[0, :, r * d:(r + 1) * d] + gate_ref[0] * _rms(z, post_ref[...])


def _fourier_token(dft, g, w, bias, post_g, gate, x, scale, tm):
    bsz, t, d = x.shape
    tq = t // RADIX
    full = lambda a: pl.BlockSpec(a.shape, lambda b, i: (0, 0))
    tok = pl.BlockSpec((1, tm, RADIX * d), lambda b, i: (b, i, 0))
    out = pl.pallas_call(
        functools.partial(_fourier_token_kernel, scale),
        grid=(bsz, tq // tm),
        in_specs=[pl.BlockSpec((RADIX, tm, 2 * tq), lambda b, i: (0, i, 0)),
                  pl.BlockSpec((1, RADIX, 2 * tq, d), lambda b, i: (b, 0, 0, 0)),
                  full(w), full(bias), full(post_g), pl.BlockSpec((1, 1, d), lambda b, i: (b, 0, 0)), tok],
        out_specs=tok,
        out_shape=jax.ShapeDtypeStruct((bsz, tq, RADIX * d), F32),
        compiler_params=_params("parallel", "arbitrary"),
        name="fourier_token_dft",
    )(dft, g.reshape(bsz, RADIX, 2 * tq, d), w, bias, post_g, gate, x.reshape(bsz, tq, RADIX * d))
    return out.reshape(bsz, t, d)


def _dft_tables(t, gd):
    idx = np.arange(gd, dtype=np.int64)
    ang = 2.0 * np.pi * ((idx[:, None] * idx[None, :]) % gd).astype(np.float64) / gd
    chan = np.concatenate([np.cos(ang), np.sin(ang)], axis=1).astype(np.float32)
    tq = t // RADIX
    k = RADIX * np.arange(tq, dtype=np.int64)[None, :, None] + np.arange(RADIX, dtype=np.int64)[:, None, None]
    ang = 2.0 * np.pi * ((k * np.arange(tq, dtype=np.int64)[None, None, :]) % t).astype(np.float64) / t
    tok = np.concatenate([np.cos(ang), np.sin(ang)], axis=2).astype(np.float32)
    return jnp.asarray(chan).astype(BF16), jnp.asarray(tok).astype(BF16)


def _row(v):
    return v.reshape(1, -1)


def kernel(x, c, ctx, c_ctx, ada_w, ada_b, pre_mix_g, post_mix_g, pre_ffn_g, post_ffn_g, ffn_up_w, ffn_conv_w,
           ffn_conv_b, ffn_down_w, m_in_w, m_in_b, m_norm_g, m_out_w, f_out_w, f_out_b):
    bsz, t, d = x.shape
    tc = ctx.shape[1]
    f = ffn_down_w.shape[1]
    assert ada_w.shape[0] == 2 and m_in_w.shape[0] == 1 and f_out_w.shape[0] == 1
    assert bsz + 1 <= COND_ROWS and t % M_CHUNK == 0 and tc % M_CHUNK == 0 and t % GRID_W == 0
    hv = m_out_w.shape[1]
    hk = (m_in_w.shape[2] - 2 * hv - 4 * M_HEADS) // 2
    tm = min(512, t)
    fc = 256
    assert f % fc == 0 and t % tm == 0 and tm % GRID_W == 0

    cond = jnp.zeros((COND_ROWS, d), F32).at[:bsz].set(c).at[bsz].set(c_ctx)
    mod = _ada(cond, ada_w, ada_b)
    lat = mod[:, :bsz].reshape(2, bsz, N_MOD, 1, d)
    cmod = mod[0, bsz].reshape(N_MOD, 1, 1, d)

    def ffn(xin, i):
        wu = ffn_up_w[i, :, :f].astype(BF16)
        wg = ffn_up_w[i, :, f:].astype(BF16)
        return _ffn(xin, lat[i, :, 3], lat[i, :, 4], lat[i, :, 5], _row(pre_ffn_g[i]), _row(post_ffn_g[i]),
                    wu, wg, ffn_conv_w[i], _row(ffn_conv_b[i]), ffn_down_w[i].astype(BF16), tm, fc)

    w_in = m_in_w[0]
    b_in = m_in_b[0]
    segs = [0, hk, 2 * hk, 2 * hk + hv, 2 * hk + 2 * hv]
    ws = [w_in[:, segs[j]:segs[j + 1]].astype(BF16) for j in range(4)]
    bs = [_row(b_in[segs[j]:segs[j + 1]]) for j in range(4)]
    ngate = 4 * M_HEADS
    ws.append(jnp.pad(w_in[:, segs[4]:], ((0, 0), (0, LANES - ngate))).astype(BF16))
    bs.append(_row(jnp.pad(b_in[segs[4]:], (0, LANES - ngate))))
    pre0 = _row(pre_mix_g[0])

    _, kc, vc, _, gc = _inproj(ctx, cmod[0], cmod[1], pre0, ws, bs, min(tm, tc))
    qx, kx, vx, ox, gx = _inproj(x, lat[0, :, 0], lat[0, :, 1], pre0, ws, bs, tm)
    (cols_x, rows_x), (_, rows_c) = _gate_prep(gc, gx)
    state = _scan(None, kc, vc, None, rows_c, None)
    h = _scan(qx, kx, vx, cols_x, rows_x, state)
    x = _readout(h, ox, x, _row(m_norm_g[0]), m_out_w[0].astype(BF16), _row(post_mix_g[0]), lat[0, :, 2], tm)
    x = ffn(x, 0)

    gd = d // F_GROUPS
    tq = t // RADIX
    tm_chan, tm_tok = min(128, tq), min(256, tq)
    assert t % RADIX == 0 and tq % tm_chan == 0 and tq % tm_tok == 0
    chan, tok = _dft_tables(t, gd)
    g = _fourier_channel(x, lat[1, :, 0], lat[1, :, 1], _row(pre_mix_g[1]), chan, tm_chan)
    x = _fourier_token(tok, g, f_out_w[0].astype(BF16), _row(f_out_b[0]),
                       _row(post_mix_g[1]), lat[1, :, 2], x, float(1.0 / np.sqrt(t * gd)), tm_tok)
    x = ffn(x, 1)
    return x
```

```python
import functools

import numpy as np
import jax
import jax.numpy as jnp
from jax import lax
from jax.experimental import pallas as pl
from jax.experimental.pallas import tpu as pltpu

F32 = jnp.float32
BF16 = jnp.bfloat16

M_HEADS = 4
M_CHUNK = 128
F_GROUPS = 8
GRID_W = 64
EPS = 1e-6
N_MOD = 6
COND_ROWS = 16

VMEM_LIMIT_BYTES = 56 * 1024 * 1024
LANES = 128


def _params(*sem):
    return pltpu.CompilerParams(dimension_semantics=sem, vmem_limit_bytes=VMEM_LIMIT_BYTES)


def _rms(x, g):
    return x * lax.rsqrt(jnp.mean(x * x, axis=-1, keepdims=True) + EPS) * g


def _bdot(a, b):
    return jnp.dot(a, b, preferred_element_type=F32)


def _ada_kernel(c_ref, w_ref, b_ref, o_ref):
    c = c_ref[...]
    s = c * jax.nn.sigmoid(c)
    o_ref[0] = _bdot(s.astype(BF16), w_ref[0].astype(BF16)) + b_ref[0]


def _ada(cond, ada_w, ada_b):
    depth, d, n = ada_w.shape
    tn = n // 4
    return pl.pallas_call(
        _ada_kernel,
        grid=(depth, n // tn),
        in_specs=[pl.BlockSpec((COND_ROWS, d), lambda i, j: (0, 0)),
                  pl.BlockSpec((1, d, tn), lambda i, j: (i, 0, j)),
                  pl.BlockSpec((1, 1, tn), lambda i, j: (i, 0, j))],
        out_specs=pl.BlockSpec((1, COND_ROWS, tn), lambda i, j: (i, 0, j)),
        out_shape=jax.ShapeDtypeStruct((depth, COND_ROWS, n), F32),
        compiler_params=_params("arbitrary", "arbitrary"),
        name="ada",
    )(cond, ada_w, ada_b.reshape(depth, 1, n))


def _inproj_kernel(x_ref, sh_ref, sc_ref, g_ref, w_ref, b_ref, q_out, kt_out, v_out, o_out, gate_out):
    hk, hv = q_out.shape[2], v_out.shape[2]
    dk = hk // M_HEADS
    h = _rms(x_ref[0], g_ref[...]) * (1.0 + sc_ref[0]) + sh_ref[0]
    hb = h.astype(BF16)

    def proj(lo, hi):
        return _bdot(hb, w_ref[:, lo:hi]) + b_ref[:, lo:hi]

    q_out[0] = proj(0, hk).astype(BF16)
    kt_out[0] = jnp.transpose(proj(hk, 2 * hk) * (dk ** -0.5)).astype(BF16)
    v_out[0] = proj(2 * hk, 2 * hk + hv).astype(BF16)
    o_out[0] = jax.nn.sigmoid(proj(2 * hk + hv, 2 * hk + 2 * hv)).astype(BF16)
    gate_out[0] = jnp.transpose(proj(2 * hk + 2 * hv, w_ref.shape[1]))[:gate_out.shape[1]]


def _inproj(x, sh, sc, g, w, bias, hk, hv, tm):
    bsz, t, d = x.shape
    per_batch = sh.shape[0] > 1
    mod_spec = pl.BlockSpec((1, 1, d), (lambda b, i: (b, 0, 0)) if per_batch else (lambda b, i: (0, 0, 0)))
    full = lambda a: pl.BlockSpec(a.shape, lambda b, i: (0, 0))
    tok = lambda n: pl.BlockSpec((1, tm, n), lambda b, i: (b, i, 0))
    tok_t = lambda n: pl.BlockSpec((1, n, tm), lambda b, i: (b, 0, i))
    return pl.pallas_call(
        _inproj_kernel,
        grid=(bsz, t // tm),
        in_specs=[tok(d), mod_spec, mod_spec, full(g), full(w), full(bias)],
        out_specs=[tok(hk), tok_t(hk), tok(hv), tok(hv), tok_t(4 * M_HEADS)],
        out_shape=[jax.ShapeDtypeStruct((bsz, t, hk), BF16), jax.ShapeDtypeStruct((bsz, hk, t), BF16),
                   jax.ShapeDtypeStruct((bsz, t, hv), BF16), jax.ShapeDtypeStruct((bsz, t, hv), BF16),
                   jax.ShapeDtypeStruct((bsz, 4 * M_HEADS, t), F32)],
        compiler_params=_params("parallel", "parallel"),
        name="mlstm_inproj",
    )(x, sh, sc, g, w, bias)


def _lane_scan(x, op, fill, reverse):
    n = x.shape[-1]
    lane = lax.broadcasted_iota(jnp.int32, x.shape, x.ndim - 1)
    d = 1
    while d < n:
        if reverse:
            shifted = jnp.where(lane < n - d, pltpu.roll(x, n - d, x.ndim - 1), fill)
        else:
            shifted = jnp.where(lane >= d, pltpu.roll(x, d, x.ndim - 1), fill)
        x = op(x, shifted)
        d *= 2
    return x


def _log_sigmoid(x):
    return jnp.minimum(x, 0.0) - jnp.log1p(jnp.exp(-jnp.abs(x)))


def _gate_kernel(gf_ref, gb_ref, of_ref, ob_ref, m_s):
    L = gf_ref.shape[-1]

    @pl.when(pl.program_id(0) == 0)
    def _():
        m_s[...] = jnp.zeros(m_s.shape, F32)

    for d, (g_ref, o_ref) in enumerate(((gf_ref, of_ref), (gb_ref, ob_ref))):
        edge = L - 1 if d == 0 else 0
        li = g_ref[:, 0]
        lf = _log_sigmoid(g_ref[:, 1])
        b = _lane_scan(lf, jnp.add, 0.0, reverse=(d == 1))
        r1 = li - b
        cm = _lane_scan(r1, jnp.maximum, -jnp.inf, reverse=(d == 1))
        m = m_s[d]
        mx = jnp.maximum(m, cm[..., edge:edge + 1])
        m_s[d] = b[..., edge:edge + 1] + mx
        mc = jnp.maximum(m, cm)
        o_ref[0] = r1
        o_ref[1] = mc
        o_ref[2] = jnp.broadcast_to(m, r1.shape)
        o_ref[3] = jnp.exp(-(b + mc))
        o_ref[4] = jnp.exp(r1 - mx)
        o_ref[5] = jnp.broadcast_to(jnp.exp(m - mx), r1.shape)


def _gate_prep(gates_c, gates_x):
    bsz = gates_c.shape[0]
    n = gates_c.shape[2] + gates_x.shape[2]
    nc = n // M_CHUNK
    gf = jnp.concatenate([gates_c, gates_x], axis=2).reshape(bsz, 4, M_HEADS, n)
    gb = jnp.concatenate([gates_x, gates_c], axis=2).reshape(bsz, 4, M_HEADS, n)
    return pl.pallas_call(
        _gate_kernel,
        grid=(nc,),
        in_specs=[pl.BlockSpec((bsz, 2, M_HEADS, M_CHUNK), lambda j: (0, 0, 0, j)),
                  pl.BlockSpec((bsz, 2, M_HEADS, M_CHUNK), lambda j: (0, 1, 0, nc - 1 - j))],
        out_specs=[pl.BlockSpec((6, bsz, M_HEADS, M_CHUNK), lambda j: (0, 0, 0, j)),
                   pl.BlockSpec((6, bsz, M_HEADS, M_CHUNK), lambda j: (0, 0, 0, nc - 1 - j))],
        out_shape=[jax.ShapeDtypeStruct((6, bsz, M_HEADS, n), F32)] * 2,
        scratch_shapes=[pltpu.VMEM((2, bsz, M_HEADS, 1), F32)],
        compiler_params=_params("arbitrary"),
        name="mlstm_gates",
    )(gf, gb)


def _scan_kernel(nsteps, with_outputs, *refs):
    if with_outputs:
        (qf, ktf, vf, rf, qb, ktb, vb, rb, c0, h_out, c_s) = refs
        streams = ((qf, ktf, vf, rf), (qb, ktb, vb, rb))
    else:
        (ktf, vf, rf, ktb, vb, rb, c_s) = refs
        streams = ((None, ktf, vf, rf), (None, ktb, vb, rb))
    s = pl.program_id(1)
    L = M_CHUNK
    dk = ktf.shape[1] // M_HEADS
    dv = vf.shape[2] // M_HEADS

    @pl.when(s == 0)
    def _():
        if with_outputs:
            c_s[...] = c0[...]
            h_out[...] = jnp.zeros(h_out.shape, F32)
        else:
            c_s[...] = jnp.zeros(c_s.shape, F32)

    row = lax.broadcasted_iota(jnp.int32, (L, L), 0)
    col = lax.broadcasted_iota(jnp.int32, (L, L), 1)
    ones = jnp.ones((L, LANES), BF16)
    tok = [pl.ds(pl.multiple_of(c * L, L), L) for c in (s, nsteps - 1 - s)]
    if with_outputs:
        stack = [r_ref[j, 0] for r_ref in (rf, rb) for j in (1, 3)]
        stack.append(jnp.zeros((L - 4 * M_HEADS, L), F32))
        cols = jnp.transpose(jnp.concatenate(stack, axis=0))
    for d in range(2):
        q_ref, kt_ref, v_ref, row_ref = streams[d]
        mask = (col <= row) if d == 0 else (col >= row)
        for h in range(M_HEADS):
            kt = kt_ref[0, h * dk:(h + 1) * dk, :]
            v1 = jnp.concatenate([v_ref[0, :, h * dv:(h + 1) * dv], ones], axis=1)
            c_old = c_s[0, d, h]
            kw = (kt.astype(F32) * row_ref[4, 0, h:h + 1, :]).astype(BF16)
            c_s[0, d, h] = row_ref[5, 0, h:h + 1, 0:1] * c_old + _bdot(kw, v1)
            if not with_outputs:
                continue
            q = q_ref[0, :, h * dk:(h + 1) * dk]
            j = 2 * M_HEADS * d + h
            mc = cols[:, j:j + 1]
            fl = cols[:, j + M_HEADS:j + M_HEADS + 1]
            p = jnp.exp(jnp.where(mask, row_ref[0, 0, h:h + 1, :] - mc, -jnp.inf))
            wi = jnp.exp(row_ref[2, 0, h:h + 1, :] - mc)
            lhs = jnp.concatenate([_bdot(q, kt) * p, q.astype(F32) * wi], axis=1).astype(BF16)
            rhs = jnp.concatenate([v1, c_old.astype(BF16)], axis=0)
            out = _bdot(lhs, rhs)
            inv = 1.0 / jnp.maximum(jnp.abs(out[:, dv:]), fl)
            h_out[0, tok[d], h * dv:(h + 1) * dv] += out[:, :dv] * jnp.concatenate([inv] * (dv // LANES), axis=1)


def _scan(q, kt, v, gates_f, gates_b, off_f, off_b, init):
    bsz, hk, t = kt.shape
    hv = v.shape[2]
    dk, dv = hk // M_HEADS, hv // M_HEADS
    assert dk == M_CHUNK == LANES and dv % LANES == 0
    nc = t // M_CHUNK
    L = M_CHUNK
    with_outputs = init is not None

    def specs(cidx, off):
        sp = [pl.BlockSpec((1, hk, L), lambda b, s: (b, 0, cidx(s))),
              pl.BlockSpec((1, L, hv), lambda b, s: (b, cidx(s), 0)),
              pl.BlockSpec((6, 1, M_HEADS, L), lambda b, s: (0, b, 0, off + cidx(s)))]
        if with_outputs:
            sp = [pl.BlockSpec((1, L, hk), lambda b, s: (b, cidx(s), 0))] + sp
        return sp

    lead = [q] if with_outputs else []
    args = lead + [kt, v, gates_f] + lead + [kt, v, gates_b]
    in_specs = specs(lambda s: s, off_f) + specs(lambda s: nc - 1 - s, off_b)
    state_shape = (bsz, 2, M_HEADS, dk, dv + LANES)
    state_spec = pl.BlockSpec((1,) + state_shape[1:], lambda b, s: (b, 0, 0, 0, 0))
    if with_outputs:
        args.append(init)
        in_specs.append(state_spec)
        out_specs = pl.BlockSpec((1, t, hv), lambda b, s: (b, 0, 0))
        out_shape = jax.ShapeDtypeStruct((bsz, t, hv), F32)
        scratch = [pltpu.VMEM((1,) + state_shape[1:], F32)]
    else:
        out_specs = state_spec
        out_shape = jax.ShapeDtypeStruct(state_shape, F32)
        scratch = []
    return pl.pallas_call(
        functools.partial(_scan_kernel, nc, with_outputs),
        grid=(bsz, nc),
        in_specs=in_specs,
        out_specs=out_specs,
        out_shape=out_shape,
        scratch_shapes=scratch,
        compiler_params=_params("parallel", "arbitrary"),
        name="mlstm_scan" if with_outputs else "mlstm_ctx_state",
    )(*args)


def _readout_kernel(h_ref, o_ref, x_ref, ng_ref, w_ref, pg_ref, gate_ref, out_ref):
    h = h_ref[0]
    dv = h.shape[1] // M_HEADS
    parts = []
    for hd in range(M_HEADS):
        hh = h[:, hd * dv:(hd + 1) * dv]
        parts.append(hh * lax.rsqrt(jnp.mean(hh * hh, axis=-1, keepdims=True) + EPS))
    hn = jnp.concatenate(parts, axis=-1) * ng_ref[...]
    y = (hn * o_ref[0].astype(F32)).astype(BF16)
    z = _bdot(y, w_ref[...])
    out_ref[0] = x_ref[0] + gate_ref[0] * _rms(z, pg_ref[...])


def _readout(h, sig_o, x, norm_g, w_out, post_g, gate, tm):
    bsz, t, d = x.shape
    hv = h.shape[2]
    tok = lambda n: pl.BlockSpec((1, tm, n), lambda b, i: (b, i, 0))
    full = lambda a: pl.BlockSpec(a.shape, lambda b, i: (0, 0))
    return pl.pallas_call(
        _readout_kernel,
        grid=(bsz, t // tm),
        in_specs=[tok(hv), tok(hv), tok(d), full(norm_g), full(w_out), full(post_g),
                  pl.BlockSpec((1, 1, d), lambda b, i: (b, 0, 0))],
        out_specs=tok(d),
        out_shape=jax.ShapeDtypeStruct((bsz, t, d), F32),
        compiler_params=_params("parallel", "parallel"),
        name="mlstm_readout",
    )(h, sig_o, x, norm_g, w_out, post_g, gate)


def _ffn_kernel(fc, x_ref, sh_ref, sc_ref, gate_ref, pre_ref, post_ref, wu_ref, wg_ref, cw_ref, cb_ref,
                wd_ref, out_ref, act_ref):
    x = x_ref[0]
    tm = x.shape[0]
    f = wu_ref.shape[1]
    hb = (_rms(x, pre_ref[...]) * (1.0 + sc_ref[0]) + sh_ref[0]).astype(BF16)
    pos = lax.broadcasted_iota(jnp.int32, (tm, fc), 0) % GRID_W
    has_prev = pos != 0
    has_next = pos != GRID_W - 1
    for j in range(f // fc):
        cs = slice(j * fc, (j + 1) * fc)
        u = _bdot(hb, wu_ref[:, cs])
        g = _bdot(hb, wg_ref[:, cs])
        g_prev = jnp.where(has_prev, pltpu.roll(g, 1, 0), 0.0)
        g_next = jnp.where(has_next, pltpu.roll(g, tm - 1, 0), 0.0)
        gc = g_prev * cw_ref[0:1, cs] + g * cw_ref[1:2, cs] + g_next * cw_ref[2:3, cs] + cb_ref[:, cs]
        act_ref[:, cs] = (gc * jax.nn.sigmoid(gc) * u).astype(BF16)
    y = _bdot(act_ref[...], wd_ref[...])
    out_ref[0] = x + gate_ref[0] * _rms(y, post_ref[...])


def _ffn(x, sh, sc, gate, layer, pre_g, post_g, w_up, cw, cb, w_down, tm, fc):
    bsz, t, d = x.shape
    f = w_down.shape[1]
    tok = pl.BlockSpec((1, tm, d), lambda b, i: (b, i, 0))
    mod = pl.BlockSpec((1, 1, d), lambda b, i: (b, 0, 0))

    def lay(rows, cols, col_block=0):
        return pl.BlockSpec((None, rows, cols), lambda b, i: (layer, 0, col_block))

    return pl.pallas_call(
        functools.partial(_ffn_kernel, fc),
        grid=(bsz, t // tm),
        in_specs=[tok, mod, mod, mod, lay(1, d), lay(1, d), lay(d, f, 0), lay(d, f, 1), lay(cw.shape[1], f),
                  lay(1, f), lay(f, d)],
        out_specs=tok,
        out_shape=jax.ShapeDtypeStruct((bsz, t, d), F32),
        scratch_shapes=[pltpu.VMEM((tm, f), BF16)],
        compiler_params=_params("parallel", "parallel"),
        name="conv_ffn",
    )(x, sh, sc, gate, pre_g, post_g, w_up, w_up, cw, cb, w_down)


RADIX = 4


def _fourier_channel_kernel(x_ref, sh_ref, sc_ref, pre_ref, cs_ref, g_ref):
    gd = cs_ref.shape[0]
    hb = [(_rms(x_ref[0, q], pre_ref[...]) * (1.0 + sc_ref[0]) + sh_ref[0]).astype(BF16) for q in range(RADIX)]
    for g in range(hb[0].shape[1] // gd):
        lanes = slice(g * gd, (g + 1) * gd)
        r = [_bdot(h[:, lanes], cs_ref[...]) for h in hb]
        a = [v[:, :gd] for v in r]
        b = [v[:, gd:] for v in r]
        a02p, a02m, a13p, a13m = a[0] + a[2], a[0] - a[2], a[1] + a[3], a[1] - a[3]
        b02p, b02m, b13p, b13m = b[0] + b[2], b[0] - b[2], b[1] + b[3], b[1] - b[3]
        re_im = ((a02p + a13p, -(b02p + b13p)),
                 (a02m - b13m, -(b02m + a13m)),
                 (a02p - a13p, b13p - b02p),
                 (a02m + b13m, a13m - b02m))
        for k, (re, im) in enumerate(re_im):
            g_ref[0, k, 0, :, lanes] = re.astype(BF16)
            g_ref[0, k, 1, :, lanes] = im.astype(BF16)


def _fourier_channel(x, sh, sc, pre_g, cs, tm):
    bsz, t, d = x.shape
    tq = t // RADIX
    mod = pl.BlockSpec((1, 1, d), lambda b, i: (b, 0, 0))
    full = lambda a: pl.BlockSpec(a.shape, lambda b, i: (0, 0))
    return pl.pallas_call(
        _fourier_channel_kernel,
        grid=(bsz, tq // tm),
        in_specs=[pl.BlockSpec((1, RADIX, tm, d), lambda b, i: (b, 0, i, 0)), mod, mod, full(pre_g), full(cs)],
        out_specs=pl.BlockSpec((1, RADIX, 2, tm, d), lambda b, i: (b, 0, 0, i, 0)),
        out_shape=jax.ShapeDtypeStruct((bsz, RADIX, 2, tq, d), BF16),
        compiler_params=_params("parallel", "parallel"),
        name="fourier_channel_dft",
    )(x.reshape(bsz, RADIX, tq, d), sh, sc, pre_g, cs)


def _fourier_token_kernel(scale, dft_ref, g_ref, w_ref, bias_ref, post_ref, gate_ref, x_ref, out_ref,
                          x_s, out_s):
    tm = dft_ref.shape[1]
    nblk = x_s.shape[0]
    for c in range(nblk):
        x_s[c] = x_ref[0, :, c * LANES:(c + 1) * LANES]
    for r in range(RADIX):
        rows = pl.ds(r, tm, stride=RADIX)
        y = _bdot(dft_ref[r], g_ref[0, r]) * scale
        z = _bdot(y.astype(BF16), w_ref[...]) + bias_ref[...]
        xr = jnp.concatenate([x_s[c, rows, :] for c in range(nblk)], axis=1)
        res = xr + gate_ref[0] * _rms(z, post_ref[...])
        for c in range(nblk):
            out_s[c, rows, :] = res[:, c * LANES:(c + 1) * LANES]
    for c in range(nblk):
        out_ref[0, :, c * LANES:(c + 1) * LANES] = out_s[c]


def _fourier_token(dft, g, w, bias, post_g, gate, x, scale, tm):
    bsz, t, d = x.shape
    tq = t // RADIX
    full = lambda a: pl.BlockSpec(a.shape, lambda b, i: (0, 0))
    tok = pl.BlockSpec((1, RADIX * tm, d), lambda b, i: (b, i, 0))
    return pl.pallas_call(
        functools.partial(_fourier_token_kernel, scale),
        grid=(bsz, tq // tm),
        in_specs=[pl.BlockSpec((RADIX, tm, 2 * tq), lambda b, i: (0, i, 0)),
                  pl.BlockSpec((1, RADIX, 2 * tq, d), lambda b, i: (b, 0, 0, 0)),
                  full(w), full(bias), full(post_g), pl.BlockSpec((1, 1, d), lambda b, i: (b, 0, 0)), tok],
        out_specs=tok,
        out_shape=jax.ShapeDtypeStruct((bsz, t, d), F32),
        scratch_shapes=[pltpu.VMEM((d // LANES, RADIX * tm, LANES), F32)] * 2,
        compiler_params=_params("parallel", "arbitrary"),
        name="fourier_token_dft",
    )(dft, g.reshape(bsz, RADIX, 2 * tq, d), w, bias, post_g, gate, x)


def _dft_tables(t, gd):
    idx = np.arange(gd, dtype=np.int64)
    ang = 2.0 * np.pi * ((idx[:, None] * idx[None, :]) % gd).astype(np.float64) / gd
    chan = np.concatenate([np.cos(ang), np.sin(ang)], axis=1).astype(np.float32)
    tq = t // RADIX
    k = RADIX * np.arange(tq, dtype=np.int64)[None, :, None] + np.arange(RADIX, dtype=np.int64)[:, None, None]
    ang = 2.0 * np.pi * ((k * np.arange(tq, dtype=np.int64)[None, None, :]) % t).astype(np.float64) / t
    tok = np.concatenate([np.cos(ang), np.sin(ang)], axis=2).astype(np.float32)
    return jnp.asarray(chan).astype(BF16), jnp.asarray(tok).astype(BF16)


def _row(v):
    return v.reshape(1, -1)


def kernel(x, c, ctx, c_ctx, ada_w, ada_b, pre_mix_g, post_mix_g, pre_ffn_g, post_ffn_g, ffn_up_w, ffn_conv_w,
           ffn_conv_b, ffn_down_w, m_in_w, m_in_b, m_norm_g, m_out_w, f_out_w, f_out_b):
    bsz, t, d = x.shape
    tc = ctx.shape[1]
    f = ffn_down_w.shape[1]
    assert ada_w.shape[0] == 2 and m_in_w.shape[0] == 1 and f_out_w.shape[0] == 1
    assert bsz + 1 <= COND_ROWS and t % M_CHUNK == 0 and tc % M_CHUNK == 0 and t % GRID_W == 0
    hv = m_out_w.shape[1]
    hk = (m_in_w.shape[2] - 2 * hv - 4 * M_HEADS) // 2
    tm = min(512, t)
    fc = 256
    assert f % fc == 0 and t % tm == 0 and tm % GRID_W == 0

    cond = jnp.zeros((COND_ROWS, d), F32).at[:bsz].set(c).at[bsz].set(c_ctx)
    mod = _ada(cond, ada_w, ada_b)
    lat = mod[:, :bsz].reshape(2, bsz, N_MOD, 1, d)
    cmod = mod[0, bsz].reshape(N_MOD, 1, 1, d)

    w_up = ffn_up_w.astype(BF16)
    w_down = ffn_down_w.astype(BF16)
    ffn_rows = [a.reshape(a.shape[0], 1, a.shape[1]) for a in (pre_ffn_g, post_ffn_g, ffn_conv_b)]

    def ffn(xin, i):
        return _ffn(xin, lat[i, :, 3], lat[i, :, 4], lat[i, :, 5], i, ffn_rows[0], ffn_rows[1],
                    w_up, ffn_conv_w, ffn_rows[2], w_down, tm, fc)

    lane_pad = (-m_in_w.shape[2]) % LANES
    w_in = jnp.pad(m_in_w[0].astype(BF16), ((0, 0), (0, lane_pad)))
    b_in = _row(jnp.pad(m_in_b[0], (0, lane_pad)))
    pre0 = _row(pre_mix_g[0])

    _, kc, vc, _, gc = _inproj(ctx, cmod[0], cmod[1], pre0, w_in, b_in, hk, hv, min(tm, tc))
    qx, kx, vx, ox, gx = _inproj(x, lat[0, :, 0], lat[0, :, 1], pre0, w_in, b_in, hk, hv, tm)
    gates_f, gates_b = _gate_prep(gc, gx)
    state = _scan(None, kc, vc, gates_f, gates_b, 0, t // M_CHUNK, None)
    h = _scan(qx, kx, vx, gates_f, gates_b, tc // M_CHUNK, 0, state)
    x = _readout(h, ox, x, _row(m_norm_g[0]), m_out_w[0].astype(BF16), _row(post_mix_g[0]), lat[0, :, 2], tm)
    x = ffn(x, 0)

    gd = d // F_GROUPS
    tq = t // RADIX
    tm_chan, tm_tok = min(128, tq), min(256, tq)
    assert t % RADIX == 0 and tq % tm_chan == 0 and tq % tm_tok == 0
    chan, tok = _dft_tables(t, gd)
    g = _fourier_channel(x, lat[1, :, 0], lat[1, :, 1], _row(pre_mix_g[1]), chan, tm_chan)
    x = _fourier_token(tok, g, f_out_w[0].astype(BF16), _row(f_out_b[0]),
                       _row(post_mix_g[1]), lat[1, :, 2], x, float(1.0 / np.sqrt(t * gd)), tm_tok)
    x = ffn(x, 1)
    return x
```

```python
import functools

import numpy as np
import jax
import jax.numpy as jnp
from jax import lax
from jax.experimental import pallas as pl
from jax.experimental.pallas import tpu as pltpu

F32 = jnp.float32
BF16 = jnp.bfloat16

M_HEADS = 4
M_CHUNK = 128
F_GROUPS = 8
GRID_W = 64
EPS = 1e-6
N_MOD = 6
COND_ROWS = 16

VMEM_LIMIT_BYTES = 56 * 1024 * 1024
LANES = 128


def _params(*sem):
    return pltpu.CompilerParams(dimension_semantics=sem, vmem_limit_bytes=VMEM_LIMIT_BYTES)


def _rms(x, g):
    return x * lax.rsqrt(jnp.mean(x * x, axis=-1, keepdims=True) + EPS) * g


def _bdot(a, b):
    return jnp.dot(a, b, preferred_element_type=F32)


def _ada_kernel(c_ref, w_ref, b_ref, o_ref):
    c = c_ref[...]
    s = c * jax.nn.sigmoid(c)
    o_ref[0] = _bdot(s.astype(BF16), w_ref[0].astype(BF16)) + b_ref[0]


def _ada(cond, ada_w, ada_b):
    depth, d, n = ada_w.shape
    tn = n // 4
    return pl.pallas_call(
        _ada_kernel,
        grid=(depth, n // tn),
        in_specs=[pl.BlockSpec((COND_ROWS, d), lambda i, j: (0, 0)),
                  pl.BlockSpec((1, d, tn), lambda i, j: (i, 0, j)),
                  pl.BlockSpec((1, 1, tn), lambda i, j: (i, 0, j))],
        out_specs=pl.BlockSpec((1, COND_ROWS, tn), lambda i, j: (i, 0, j)),
        out_shape=jax.ShapeDtypeStruct((depth, COND_ROWS, n), F32),
        compiler_params=_params("arbitrary", "arbitrary"),
        name="ada",
    )(cond, ada_w, ada_b.reshape(depth, 1, n))


def _inproj_kernel(x_ref, sh_ref, sc_ref, g_ref, w_ref, b_ref, q_out, kt_out, v_out, o_out, gate_out):
    hk, hv = q_out.shape[2], v_out.shape[2]
    dk = hk // M_HEADS
    h = _rms(x_ref[0], g_ref[...]) * (1.0 + sc_ref[0]) + sh_ref[0]
    hb = h.astype(BF16)

    def proj(lo, hi):
        return _bdot(hb, w_ref[:, lo:hi]) + b_ref[:, lo:hi]

    q_out[0] = proj(0, hk).astype(BF16)
    kt_out[0] = jnp.transpose(proj(hk, 2 * hk) * (dk ** -0.5)).astype(BF16)
    v_out[0] = proj(2 * hk, 2 * hk + hv).astype(BF16)
    o_out[0] = jax.nn.sigmoid(proj(2 * hk + hv, 2 * hk + 2 * hv)).astype(BF16)
    gate_out[0] = jnp.transpose(proj(2 * hk + 2 * hv, w_ref.shape[1]))[:gate_out.shape[1]]


def _inproj(x, sh, sc, g, w, bias, hk, hv, tm):
    bsz, t, d = x.shape
    per_batch = sh.shape[0] > 1
    mod_spec = pl.BlockSpec((1, 1, d), (lambda b, i: (b, 0, 0)) if per_batch else (lambda b, i: (0, 0, 0)))
    full = lambda a: pl.BlockSpec(a.shape, lambda b, i: (0, 0))
    tok = lambda n: pl.BlockSpec((1, tm, n), lambda b, i: (b, i, 0))
    tok_t = lambda n: pl.BlockSpec((1, n, tm), lambda b, i: (b, 0, i))
    return pl.pallas_call(
        _inproj_kernel,
        grid=(bsz, t // tm),
        in_specs=[tok(d), mod_spec, mod_spec, full(g), full(w), full(bias)],
        out_specs=[tok(hk), tok_t(hk), tok(hv), tok(hv), tok_t(4 * M_HEADS)],
        out_shape=[jax.ShapeDtypeStruct((bsz, t, hk), BF16), jax.ShapeDtypeStruct((bsz, hk, t), BF16),
                   jax.ShapeDtypeStruct((bsz, t, hv), BF16), jax.ShapeDtypeStruct((bsz, t, hv), BF16),
                   jax.ShapeDtypeStruct((bsz, 4 * M_HEADS, t), F32)],
        compiler_params=_params("parallel", "parallel"),
        name="mlstm_inproj",
    )(x, sh, sc, g, w, bias)


def _lane_scan(x, op, fill, reverse):
    n = x.shape[-1]
    lane = lax.broadcasted_iota(jnp.int32, x.shape, x.ndim - 1)
    d = 1
    while d < n:
        if reverse:
            shifted = jnp.where(lane < n - d, pltpu.roll(x, n - d, x.ndim - 1), fill)
        else:
            shifted = jnp.where(lane >= d, pltpu.roll(x, d, x.ndim - 1), fill)
        x = op(x, shifted)
        d *= 2
    return x


def _log_sigmoid(x):
    return jnp.minimum(x, 0.0) - jnp.log1p(jnp.exp(-jnp.abs(x)))


def _gate_kernel(gf_ref, gb_ref, of_ref, ob_ref, m_s):
    L = gf_ref.shape[-1]

    @pl.when(pl.program_id(0) == 0)
    def _():
        m_s[...] = jnp.zeros(m_s.shape, F32)

    for d, (g_ref, o_ref) in enumerate(((gf_ref, of_ref), (gb_ref, ob_ref))):
        edge = L - 1 if d == 0 else 0
        li = g_ref[:, 0]
        lf = _log_sigmoid(g_ref[:, 1])
        b = _lane_scan(lf, jnp.add, 0.0, reverse=(d == 1))
        r1 = li - b
        cm = _lane_scan(r1, jnp.maximum, -jnp.inf, reverse=(d == 1))
        m = m_s[d]
        mx = jnp.maximum(m, cm[..., edge:edge + 1])
        m_s[d] = b[..., edge:edge + 1] + mx
        mc = jnp.maximum(m, cm)
        o_ref[0] = r1
        o_ref[1] = mc
        o_ref[2] = jnp.broadcast_to(m, r1.shape)
        o_ref[3] = jnp.exp(-(b + mc))
        o_ref[4] = jnp.exp(r1 - mx)
        o_ref[5] = jnp.broadcast_to(jnp.exp(m - mx), r1.shape)


def _gate_prep(gates_c, gates_x):
    bsz = gates_c.shape[0]
    n = gates_c.shape[2] + gates_x.shape[2]
    nc = n // M_CHUNK
    gf = jnp.concatenate([gates_c, gates_x], axis=2).reshape(bsz, 4, M_HEADS, n)
    gb = jnp.concatenate([gates_x, gates_c], axis=2).reshape(bsz, 4, M_HEADS, n)
    return pl.pallas_call(
        _gate_kernel,
        grid=(nc,),
        in_specs=[pl.BlockSpec((bsz, 2, M_HEADS, M_CHUNK), lambda j: (0, 0, 0, j)),
                  pl.BlockSpec((bsz, 2, M_HEADS, M_CHUNK), lambda j: (0, 1, 0, nc - 1 - j))],
        out_specs=[pl.BlockSpec((6, bsz, M_HEADS, M_CHUNK), lambda j: (0, 0, 0, j)),
                   pl.BlockSpec((6, bsz, M_HEADS, M_CHUNK), lambda j: (0, 0, 0, nc - 1 - j))],
        out_shape=[jax.ShapeDtypeStruct((6, bsz, M_HEADS, n), F32)] * 2,
        scratch_shapes=[pltpu.VMEM((2, bsz, M_HEADS, 1), F32)],
        compiler_params=_params("arbitrary"),
        name="mlstm_gates",
    )(gf, gb)


def _scan_kernel(nchunks, cps, with_outputs, *refs):
    if with_outputs:
        (qf, ktf, vf, rf, qb, ktb, vb, rb, c0, h_out, c_s) = refs
        streams = ((qf, ktf, vf, rf), (qb, ktb, vb, rb))
    else:
        (ktf, vf, rf, ktb, vb, rb, c_s) = refs
        streams = ((None, ktf, vf, rf), (None, ktb, vb, rb))
    s = pl.program_id(1)
    L = M_CHUNK
    dk = ktf.shape[1] // M_HEADS
    dv = vf.shape[2] // M_HEADS

    @pl.when(s == 0)
    def _():
        if with_outputs:
            c_s[...] = c0[...]
            h_out[...] = jnp.zeros(h_out.shape, F32)
        else:
            c_s[...] = jnp.zeros(c_s.shape, F32)

    row = lax.broadcasted_iota(jnp.int32, (L, L), 0)
    col = lax.broadcasted_iota(jnp.int32, (L, L), 1)
    for cc in range(cps):
        sub = (cc, cps - 1 - cc)
        blk = [slice(i * L, (i + 1) * L) for i in sub]
        chunk = (s * cps + cc, nchunks - 1 - (s * cps + cc))
        tok = [pl.ds(pl.multiple_of(c * L, L), L) for c in chunk]
        if with_outputs:
            stack = [r_ref[j, 0, :, blk[d]] for d, r_ref in enumerate((rf, rb)) for j in (1, 3)]
            stack.append(jnp.zeros((L - 4 * M_HEADS, L), F32))
            cols = jnp.transpose(jnp.concatenate(stack, axis=0))
        for d in range(2):
            q_ref, kt_ref, v_ref, row_ref = streams[d]
            mask = (col <= row) if d == 0 else (col >= row)
            for h in range(M_HEADS):
                kt = kt_ref[0, h * dk:(h + 1) * dk, blk[d]]
                v = v_ref[0, blk[d], h * dv:(h + 1) * dv]
                c_old = c_s[0, d, h]
                kw = kt.astype(F32) * row_ref[4, 0, h:h + 1, blk[d]]
                n_upd = jnp.broadcast_to(jnp.sum(kw, axis=1, keepdims=True), (dk, LANES))
                upd = jnp.concatenate([_bdot(kw.astype(BF16), v), n_upd], axis=1)
                c_s[0, d, h] = row_ref[5, 0, h:h + 1, sub[d] * L:sub[d] * L + 1] * c_old + upd
                if not with_outputs:
                    continue
                q = q_ref[0, blk[d], h * dk:(h + 1) * dk]
                cb = c_old.astype(BF16)
                j = 2 * M_HEADS * d + h
                mc = cols[:, j:j + 1]
                fl = cols[:, j + M_HEADS:j + M_HEADS + 1]
                p = jnp.exp(jnp.where(mask, row_ref[0, 0, h:h + 1, blk[d]] - mc, -jnp.inf))
                wi = jnp.exp(row_ref[2, 0, h:h + 1, blk[d]] - mc)
                sq = _bdot(q, jnp.concatenate([kt, cb[:, dv:]], axis=1))
                sp = sq[:, :L] * p
                lhs = jnp.concatenate([sp, q.astype(F32) * wi], axis=1).astype(BF16)
                num = _bdot(lhs, jnp.concatenate([v, cb[:, :dv]], axis=0))
                den = wi * sq[:, L:] + jnp.sum(sp, axis=1, keepdims=True)
                inv = 1.0 / jnp.maximum(jnp.abs(den), fl)
                h_out[0, tok[d], h * dv:(h + 1) * dv] += num * jnp.concatenate([inv] * (dv // LANES), axis=1)


def _scan(q, kt, v, gates_f, gates_b, off_f, off_b, init):
    bsz, hk, t = kt.shape
    hv = v.shape[2]
    dk, dv = hk // M_HEADS, hv // M_HEADS
    assert dk == M_CHUNK == LANES and dv % LANES == 0
    nc = t // M_CHUNK
    cps = 2 if (nc % 2 == 0 and off_f % 2 == 0 and off_b % 2 == 0) else 1
    nsteps = nc // cps
    L = cps * M_CHUNK
    with_outputs = init is not None

    def specs(cidx, off):
        sp = [pl.BlockSpec((1, hk, L), lambda b, s: (b, 0, cidx(s))),
              pl.BlockSpec((1, L, hv), lambda b, s: (b, cidx(s), 0)),
              pl.BlockSpec((6, 1, M_HEADS, L), lambda b, s: (0, b, 0, off // cps + cidx(s)))]
        if with_outputs:
            sp = [pl.BlockSpec((1, L, hk), lambda b, s: (b, cidx(s), 0))] + sp
        return sp

    lead = [q] if with_outputs else []
    args = lead + [kt, v, gates_f] + lead + [kt, v, gates_b]
    in_specs = specs(lambda s: s, off_f) + specs(lambda s: nsteps - 1 - s, off_b)
    state_shape = (bsz, 2, M_HEADS, dk, dv + LANES)
    state_spec = pl.BlockSpec((1,) + state_shape[1:], lambda b, s: (b, 0, 0, 0, 0))
    if with_outputs:
        args.append(init)
        in_specs.append(state_spec)
        out_specs = pl.BlockSpec((1, t, hv), lambda b, s: (b, 0, 0))
        out_shape = jax.ShapeDtypeStruct((bsz, t, hv), F32)
        scratch = [pltpu.VMEM((1,) + state_shape[1:], F32)]
    else:
        out_specs = state_spec
        out_shape = jax.ShapeDtypeStruct(state_shape, F32)
        scratch = []
    return pl.pallas_call(
        functools.partial(_scan_kernel, nc, cps, with_outputs),
        grid=(bsz, nsteps),
        in_specs=in_specs,
        out_specs=out_specs,
        out_shape=out_shape,
        scratch_shapes=scratch,
        compiler_params=_params("parallel", "arbitrary"),
        name="mlstm_scan" if with_outputs else "mlstm_ctx_state",
    )(*args)


def _readout_kernel(h_ref, o_ref, x_ref, ng_ref, w_ref, pg_ref, gate_ref, out_ref):
    h = h_ref[0]
    dv = h.shape[1] // M_HEADS
    parts = []
    for hd in range(M_HEADS):
        hh = h[:, hd * dv:(hd + 1) * dv]
        parts.append(hh * lax.rsqrt(jnp.mean(hh * hh, axis=-1, keepdims=True) + EPS))
    hn = jnp.concatenate(parts, axis=-1) * ng_ref[...]
    y = (hn * o_ref[0].astype(F32)).astype(BF16)
    z = _bdot(y, w_ref[...])
    out_ref[0] = x_ref[0] + gate_ref[0] * _rms(z, pg_ref[...])


def _readout(h, sig_o, x, norm_g, w_out, post_g, gate, tm):
    bsz, t, d = x.shape
    hv = h.shape[2]
    tok = lambda n: pl.BlockSpec((1, tm, n), lambda b, i: (b, i, 0))
    full = lambda a: pl.BlockSpec(a.shape, lambda b, i: (0, 0))
    return pl.pallas_call(
        _readout_kernel,
        grid=(bsz, t // tm),
        in_specs=[tok(hv), tok(hv), tok(d), full(norm_g), full(w_out), full(post_g),
                  pl.BlockSpec((1, 1, d), lambda b, i: (b, 0, 0))],
        out_specs=tok(d),
        out_shape=jax.ShapeDtypeStruct((bsz, t, d), F32),
        compiler_params=_params("parallel", "parallel"),
        name="mlstm_readout",
    )(h, sig_o, x, norm_g, w_out, post_g, gate)


def _ffn_kernel(fc, x_ref, sh_ref, sc_ref, gate_ref, pre_ref, post_ref, wu_ref, wg_ref, cw_ref, cb_ref,
                wd_ref, out_ref, act_ref):
    x = x_ref[0]
    tm = x.shape[0]
    f = wu_ref.shape[1]
    hb = (_rms(x, pre_ref[...]) * (1.0 + sc_ref[0]) + sh_ref[0]).astype(BF16)
    pos = lax.broadcasted_iota(jnp.int32, (tm, fc), 0) % GRID_W
    has_prev = pos != 0
    has_next = pos != GRID_W - 1
    for j in range(f // fc):
        cs = slice(j * fc, (j + 1) * fc)
        u = _bdot(hb, wu_ref[:, cs])
        g = _bdot(hb, wg_ref[:, cs])
        g_prev = jnp.where(has_prev, pltpu.roll(g, 1, 0), 0.0)
        g_next = jnp.where(has_next, pltpu.roll(g, tm - 1, 0), 0.0)
        gc = g_prev * cw_ref[0:1, cs] + g * cw_ref[1:2, cs] + g_next * cw_ref[2:3, cs] + cb_ref[:, cs]
        act_ref[:, cs] = (gc * jax.nn.sigmoid(gc) * u).astype(BF16)
    y = _bdot(act_ref[...], wd_ref[...])
    out_ref[0] = x + gate_ref[0] * _rms(y, post_ref[...])


def _ffn(x, sh, sc, gate, layer, pre_g, post_g, w_up, cw, cb, w_down, tm, fc):
    bsz, t, d = x.shape
    f = w_down.shape[1]
    tok = pl.BlockSpec((1, tm, d), lambda b, i: (b, i, 0))
    mod = pl.BlockSpec((1, 1, d), lambda b, i: (b, 0, 0))

    def lay(rows, cols, col_block=0):
        return pl.BlockSpec((None, rows, cols), lambda b, i: (layer, 0, col_block),
                            pipeline_mode=pl.Buffered(1))

    return pl.pallas_call(
        functools.partial(_ffn_kernel, fc),
        grid=(bsz, t // tm),
        in_specs=[tok, mod, mod, mod, lay(1, d), lay(1, d), lay(d, f, 0), lay(d, f, 1), lay(cw.shape[1], f),
                  lay(1, f), lay(f, d)],
        out_specs=tok,
        out_shape=jax.ShapeDtypeStruct((bsz, t, d), F32),
        scratch_shapes=[pltpu.VMEM((tm, f), BF16)],
        compiler_params=_params("parallel", "parallel"),
        name="conv_ffn",
    )(x, sh, sc, gate, pre_g, post_g, w_up, w_up, cw, cb, w_down)


RADIX = 4


def _fourier_channel_kernel(x_ref, sh_ref, sc_ref, pre_ref, cs_ref, g_ref):
    gd = cs_ref.shape[0]
    hb = [(_rms(x_ref[0, q], pre_ref[...]) * (1.0 + sc_ref[0]) + sh_ref[0]).astype(BF16) for q in range(RADIX)]
    for g in range(hb[0].shape[1] // gd):
        lanes = slice(g * gd, (g + 1) * gd)
        r = [_bdot(h[:, lanes], cs_ref[...]) for h in hb]
        a = [v[:, :gd] for v in r]
        b = [v[:, gd:] for v in r]
        a02p, a02m, a13p, a13m = a[0] + a[2], a[0] - a[2], a[1] + a[3], a[1] - a[3]
        b02p, b02m, b13p, b13m = b[0] + b[2], b[0] - b[2], b[1] + b[3], b[1] - b[3]
        re_im = ((a02p + a13p, -(b02p + b13p)),
                 (a02m - b13m, -(b02m + a13m)),
                 (a02p - a13p, b13p - b02p),
                 (a02m + b13m, a13m - b02m))
        for k, (re, im) in enumerate(re_im):
            g_ref[0, k, 0, :, lanes] = re.astype(BF16)
            g_ref[0, k, 1, :, lanes] = im.astype(BF16)


def _fourier_channel(x, sh, sc, pre_g, cs, tm):
    bsz, t, d = x.shape
    tq = t // RADIX
    mod = pl.BlockSpec((1, 1, d), lambda b, i: (b, 0, 0))
    full = lambda a: pl.BlockSpec(a.shape, lambda b, i: (0, 0))
    return pl.pallas_call(
        _fourier_channel_kernel,
        grid=(bsz, tq // tm),
        in_specs=[pl.BlockSpec((1, RADIX, tm, d), lambda b, i: (b, 0, i, 0)), mod, mod, full(pre_g), full(cs)],
        out_specs=pl.BlockSpec((1, RADIX, 2, tm, d), lambda b, i: (b, 0, 0, i, 0)),
        out_shape=jax.ShapeDtypeStruct((bsz, RADIX, 2, tq, d), BF16),
        compiler_params=_params("parallel", "parallel"),
        name="fourier_channel_dft",
    )(x.reshape(bsz, RADIX, tq, d), sh, sc, pre_g, cs)


def _fourier_token_kernel(scale, dft_ref, g_ref, w_ref, bias_ref, post_ref, gate_ref, x_ref, out_ref,
                          x_s, out_s):
    tm = dft_ref.shape[1]
    nblk = x_s.shape[0]
    for c in range(nblk):
        x_s[c] = x_ref[0, :, c * LANES:(c + 1) * LANES]
    for r in range(RADIX):
        rows = pl.ds(r, tm, stride=RADIX)
        y = _bdot(dft_ref[r], g_ref[0, r]) * scale
        z = _bdot(y.astype(BF16), w_ref[...]) + bias_ref[...]
        xr = jnp.concatenate([x_s[c, rows, :] for c in range(nblk)], axis=1)
        res = xr + gate_ref[0] * _rms(z, post_ref[...])
        for c in range(nblk):
            out_s[c, rows, :] = res[:, c * LANES:(c + 1) * LANES]
    for c in range(nblk):
        out_ref[0, :, c * LANES:(c + 1) * LANES] = out_s[c]


def _fourier_token(dft, g, w, bias, post_g, gate, x, scale, tm):
    bsz, t, d = x.shape
    tq = t // RADIX
    full = lambda a: pl.BlockSpec(a.shape, lambda b, i: (0, 0))
    tok = pl.BlockSpec((1, RADIX * tm, d), lambda b, i: (b, i, 0))
    return pl.pallas_call(
        functools.partial(_fourier_token_kernel, scale),
        grid=(bsz, tq // tm),
        in_specs=[pl.BlockSpec((RADIX, tm, 2 * tq), lambda b, i: (0, i, 0)),
                  pl.BlockSpec((1, RADIX, 2 * tq, d), lambda b, i: (b, 0, 0, 0)),
                  full(w), full(bias), full(post_g), pl.BlockSpec((1, 1, d), lambda b, i: (b, 0, 0)), tok],
        out_specs=tok,
        out_shape=jax.ShapeDtypeStruct((bsz, t, d), F32),
        scratch_shapes=[pltpu.VMEM((d // LANES, RADIX * tm, LANES), F32)] * 2,
        compiler_params=_params("parallel", "arbitrary"),
        name="fourier_token_dft",
    )(dft, g.reshape(bsz, RADIX, 2 * tq, d), w, bias, post_g, gate, x)


def _dft_tables(t, gd):
    idx = np.arange(gd, dtype=np.int64)
    ang = 2.0 * np.pi * ((idx[:, None] * idx[None, :]) % gd).astype(np.float64) / gd
    chan = np.concatenate([np.cos(ang), np.sin(ang)], axis=1).astype(np.float32)
    tq = t // RADIX
    k = RADIX * np.arange(tq, dtype=np.int64)[None, :, None] + np.arange(RADIX, dtype=np.int64)[:, None, None]
    ang = 2.0 * np.pi * ((k * np.arange(tq, dtype=np.int64)[None, None, :]) % t).astype(np.float64) / t
    tok = np.concatenate([np.cos(ang), np.sin(ang)], axis=2).astype(np.float32)
    return jnp.asarray(chan).astype(BF16), jnp.asarray(tok).astype(BF16)


def _row(v):
    return v.reshape(1, -1)


def kernel(x, c, ctx, c_ctx, ada_w, ada_b, pre_mix_g, post_mix_g, pre_ffn_g, post_ffn_g, ffn_up_w, ffn_conv_w,
           ffn_conv_b, ffn_down_w, m_in_w, m_in_b, m_norm_g, m_out_w, f_out_w, f_out_b):
    bsz, t, d = x.shape
    tc = ctx.shape[1]
    f = ffn_down_w.shape[1]
    assert ada_w.shape[0] == 2 and m_in_w.shape[0] == 1 and f_out_w.shape[0] == 1
    assert bsz + 1 <= COND_ROWS and t % M_CHUNK == 0 and tc % M_CHUNK == 0 and t % GRID_W == 0
    hv = m_out_w.shape[1]
    hk = (m_in_w.shape[2] - 2 * hv - 4 * M_HEADS) // 2
    tm = min(512, t)
    fc = 256
    assert f % fc == 0 and t % tm == 0 and tm % GRID_W == 0

    cond = jnp.zeros((COND_ROWS, d), F32).at[:bsz].set(c).at[bsz].set(c_ctx)
    mod = _ada(cond, ada_w, ada_b)
    lat = mod[:, :bsz].reshape(2, bsz, N_MOD, 1, d)
    cmod = mod[0, bsz].reshape(N_MOD, 1, 1, d)

    w_up = ffn_up_w.astype(BF16)
    w_down = ffn_down_w.astype(BF16)
    ffn_rows = [a.reshape(a.shape[0], 1, a.shape[1]) for a in (pre_ffn_g, post_ffn_g, ffn_conv_b)]

    def ffn(xin, i):
        return _ffn(xin, lat[i, :, 3], lat[i, :, 4], lat[i, :, 5], i, ffn_rows[0], ffn_rows[1],
                    w_up, ffn_conv_w, ffn_rows[2], w_down, tm, fc)

    lane_pad = (-m_in_w.shape[2]) % LANES
    w_in = jnp.pad(m_in_w[0].astype(BF16), ((0, 0), (0, lane_pad)))
    b_in = _row(jnp.pad(m_in_b[0], (0, lane_pad)))
    pre0 = _row(pre_mix_g[0])

    _, kc, vc, _, gc = _inproj(ctx, cmod[0], cmod[1], pre0, w_in, b_in, hk, hv, min(tm, tc))
    qx, kx, vx, ox, gx = _inproj(x, lat[0, :, 0], lat[0, :, 1], pre0, w_in, b_in, hk, hv, tm)
    gates_f, gates_b = _gate_prep(gc, gx)
    state = _scan(None, kc, vc, gates_f, gates_b, 0, t // M_CHUNK, None)
    h = _scan(qx, kx, vx, gates_f, gates_b, tc // M_CHUNK, 0, state)
    x = _readout(h, ox, x, _row(m_norm_g[0]), m_out_w[0].astype(BF16), _row(post_mix_g[0]), lat[0, :, 2], tm)
    x = ffn(x, 0)

    gd = d // F_GROUPS
    tq = t // RADIX
    tm_chan, tm_tok = min(128, tq), min(256, tq)
    assert t % RADIX == 0 and tq % tm_chan == 0 and tq % tm_tok == 0
    chan, tok = _dft_tables(t, gd)
    g = _fourier_channel(x, lat[1, :, 0], lat[1, :, 1], _row(pre_mix_g[1]), chan, tm_chan)
    x = _fourier_token(tok, g, f_out_w[0].astype(BF16), _row(f_out_b[0]),
                       _row(post_mix_g[1]), lat[1, :, 2], x, float(1.0 / np.sqrt(t * gd)), tm_tok)
    x = ffn(x, 1)
    return x
```

```python
import functools

import numpy as np
import jax
import jax.numpy as jnp
from jax import lax
from jax.experimental import pallas as pl
from jax.experimental.pallas import tpu as pltpu

F32 = jnp.float32
BF16 = jnp.bfloat16

M_HEADS = 4
M_CHUNK = 128
F_GROUPS = 8
GRID_W = 64
EPS = 1e-6
N_MOD = 6
COND_ROWS = 16

VMEM_LIMIT_BYTES = 56 * 1024 * 1024
LANES = 128


def _params(*sem):
    return pltpu.CompilerParams(dimension_semantics=sem, vmem_limit_bytes=VMEM_LIMIT_BYTES)


def _rms(x, g):
    return x * lax.rsqrt(jnp.mean(x * x, axis=-1, keepdims=True) + EPS) * g


def _bdot(a, b):
    return jnp.dot(a, b, preferred_element_type=F32)


def _ada_kernel(c_ref, w_ref, b_ref, o_ref):
    c = c_ref[...]
    s = c * jax.nn.sigmoid(c)
    o_ref[0] = _bdot(s.astype(BF16), w_ref[0].astype(BF16)) + b_ref[0]


def _ada(cond, ada_w, ada_b):
    depth, d, n = ada_w.shape
    tn = n // 4
    return pl.pallas_call(
        _ada_kernel,
        grid=(depth, n // tn),
        in_specs=[pl.BlockSpec((COND_ROWS, d), lambda i, j: (0, 0)),
                  pl.BlockSpec((1, d, tn), lambda i, j: (i, 0, j)),
                  pl.BlockSpec((1, 1, tn), lambda i, j: (i, 0, j))],
        out_specs=pl.BlockSpec((1, COND_ROWS, tn), lambda i, j: (i, 0, j)),
        out_shape=jax.ShapeDtypeStruct((depth, COND_ROWS, n), F32),
        compiler_params=_params("arbitrary", "arbitrary"),
        name="ada",
    )(cond, ada_w, ada_b.reshape(depth, 1, n))


def _inproj_kernel(x_ref, sh_ref, sc_ref, g_ref, w_ref, b_ref, q_out, kt_out, v_out, o_out, gate_out):
    hk, hv = q_out.shape[2], v_out.shape[2]
    dk = hk // M_HEADS
    h = _rms(x_ref[0], g_ref[...]) * (1.0 + sc_ref[0]) + sh_ref[0]
    hb = h.astype(BF16)

    def proj(lo, hi):
        return _bdot(hb, w_ref[:, lo:hi]) + b_ref[:, lo:hi]

    q_out[0] = proj(0, hk).astype(BF16)
    kt_out[0] = jnp.transpose(proj(hk, 2 * hk) * (dk ** -0.5)).astype(BF16)
    v_out[0] = proj(2 * hk, 2 * hk + hv).astype(BF16)
    o_out[0] = jax.nn.sigmoid(proj(2 * hk + hv, 2 * hk + 2 * hv)).astype(BF16)
    gate_out[0] = jnp.transpose(proj(2 * hk + 2 * hv, w_ref.shape[1]))[:gate_out.shape[1]]


def _inproj(x, sh, sc, g, w, bias, hk, hv, tm):
    bsz, t, d = x.shape
    per_batch = sh.shape[0] > 1
    mod_spec = pl.BlockSpec((1, 1, d), (lambda b, i: (b, 0, 0)) if per_batch else (lambda b, i: (0, 0, 0)))
    full = lambda a: pl.BlockSpec(a.shape, lambda b, i: (0, 0))
    tok = lambda n: pl.BlockSpec((1, tm, n), lambda b, i: (b, i, 0))
    tok_t = lambda n: pl.BlockSpec((1, n, tm), lambda b, i: (b, 0, i))
    return pl.pallas_call(
        _inproj_kernel,
        grid=(bsz, t // tm),
        in_specs=[tok(d), mod_spec, mod_spec, full(g), full(w), full(bias)],
        out_specs=[tok(hk), tok_t(hk), tok(hv), tok(hv), tok_t(4 * M_HEADS)],
        out_shape=[jax.ShapeDtypeStruct((bsz, t, hk), BF16), jax.ShapeDtypeStruct((bsz, hk, t), BF16),
                   jax.ShapeDtypeStruct((bsz, t, hv), BF16), jax.ShapeDtypeStruct((bsz, t, hv), BF16),
                   jax.ShapeDtypeStruct((bsz, 4 * M_HEADS, t), F32)],
        compiler_params=_params("parallel", "parallel"),
        name="mlstm_inproj",
    )(x, sh, sc, g, w, bias)


def _lane_scan(x, op, fill, reverse):
    n = x.shape[-1]
    lane = lax.broadcasted_iota(jnp.int32, x.shape, x.ndim - 1)
    d = 1
    while d < n:
        if reverse:
            shifted = jnp.where(lane < n - d, pltpu.roll(x, n - d, x.ndim - 1), fill)
        else:
            shifted = jnp.where(lane >= d, pltpu.roll(x, d, x.ndim - 1), fill)
        x = op(x, shifted)
        d *= 2
    return x


def _log_sigmoid(x):
    return jnp.minimum(x, 0.0) - jnp.log1p(jnp.exp(-jnp.abs(x)))


def _gate_kernel(gf_ref, gb_ref, of_ref, ob_ref, m_s):
    L = gf_ref.shape[-1]

    @pl.when(pl.program_id(0) == 0)
    def _():
        m_s[...] = jnp.zeros(m_s.shape, F32)

    for d, (g_ref, o_ref) in enumerate(((gf_ref, of_ref), (gb_ref, ob_ref))):
        edge = L - 1 if d == 0 else 0
        li = g_ref[:, 0]
        lf = _log_sigmoid(g_ref[:, 1])
        b = _lane_scan(lf, jnp.add, 0.0, reverse=(d == 1))
        r1 = li - b
        cm = _lane_scan(r1, jnp.maximum, -jnp.inf, reverse=(d == 1))
        m = m_s[d]
        mx = jnp.maximum(m, cm[..., edge:edge + 1])
        m_s[d] = b[..., edge:edge + 1] + mx
        mc = jnp.maximum(m, cm)
        o_ref[0] = r1
        o_ref[1] = mc
        o_ref[2] = jnp.broadcast_to(m, r1.shape)
        o_ref[3] = jnp.exp(-(b + mc))
        o_ref[4] = jnp.exp(r1 - mx)
        o_ref[5] = jnp.broadcast_to(jnp.exp(m - mx), r1.shape)


def _gate_prep(gates_c, gates_x):
    bsz = gates_c.shape[0]
    n = gates_c.shape[2] + gates_x.shape[2]
    nc = n // M_CHUNK
    gf = jnp.concatenate([gates_c, gates_x], axis=2).reshape(bsz, 4, M_HEADS, n)
    gb = jnp.concatenate([gates_x, gates_c], axis=2).reshape(bsz, 4, M_HEADS, n)
    return pl.pallas_call(
        _gate_kernel,
        grid=(nc,),
        in_specs=[pl.BlockSpec((bsz, 2, M_HEADS, M_CHUNK), lambda j: (0, 0, 0, j)),
                  pl.BlockSpec((bsz, 2, M_HEADS, M_CHUNK), lambda j: (0, 1, 0, nc - 1 - j))],
        out_specs=[pl.BlockSpec((6, bsz, M_HEADS, M_CHUNK), lambda j: (0, 0, 0, j)),
                   pl.BlockSpec((6, bsz, M_HEADS, M_CHUNK), lambda j: (0, 0, 0, nc - 1 - j))],
        out_shape=[jax.ShapeDtypeStruct((6, bsz, M_HEADS, n), F32)] * 2,
        scratch_shapes=[pltpu.VMEM((2, bsz, M_HEADS, 1), F32)],
        compiler_params=_params("arbitrary"),
        name="mlstm_gates",
    )(gf, gb)


def _scan_kernel(nchunks, cps, with_outputs, *refs):
    if with_outputs:
        (qf, ktf, vf, rf, qb, ktb, vb, rb, c0, h_out, c_s) = refs
        streams = ((qf, ktf, vf, rf), (qb, ktb, vb, rb))
    else:
        (ktf, vf, rf, ktb, vb, rb, c_s) = refs
        streams = ((None, ktf, vf, rf), (None, ktb, vb, rb))
    s = pl.program_id(1)
    L = M_CHUNK
    dk = ktf.shape[1] // M_HEADS
    dv = vf.shape[2] // M_HEADS

    @pl.when(s == 0)
    def _():
        if with_outputs:
            c_s[...] = c0[...]
            h_out[...] = jnp.zeros(h_out.shape, F32)
        else:
            c_s[...] = jnp.zeros(c_s.shape, F32)

    row = lax.broadcasted_iota(jnp.int32, (L, L), 0)
    col = lax.broadcasted_iota(jnp.int32, (L, L), 1)
    for cc in range(cps):
        sub = (cc, cps - 1 - cc)
        blk = [slice(i * L, (i + 1) * L) for i in sub]
        chunk = (s * cps + cc, nchunks - 1 - (s * cps + cc))
        tok = [pl.ds(pl.multiple_of(c * L, L), L) for c in chunk]
        if with_outputs:
            stack = [r_ref[j, 0, :, blk[d]] for d, r_ref in enumerate((rf, rb)) for j in (1, 3)]
            stack.append(jnp.zeros((L - 4 * M_HEADS, L), F32))
            cols = jnp.transpose(jnp.concatenate(stack, axis=0))
        for d in range(2):
            q_ref, kt_ref, v_ref, row_ref = streams[d]
            mask = (col <= row) if d == 0 else (col >= row)
            for h in range(M_HEADS):
                kt = kt_ref[0, h * dk:(h + 1) * dk, blk[d]]
                v = v_ref[0, blk[d], h * dv:(h + 1) * dv]
                c_old = c_s[0, d, h]
                kw = kt.astype(F32) * row_ref[4, 0, h:h + 1, blk[d]]
                n_upd = jnp.broadcast_to(jnp.sum(kw, axis=1, keepdims=True), (dk, LANES))
                upd = jnp.concatenate([_bdot(kw.astype(BF16), v), n_upd], axis=1)
                c_s[0, d, h] = row_ref[5, 0, h:h + 1, sub[d] * L:sub[d] * L + 1] * c_old + upd
                if not with_outputs:
                    continue
                q = q_ref[0, blk[d], h * dk:(h + 1) * dk]
                cb = c_old.astype(BF16)
                j = 2 * M_HEADS * d + h
                mc = cols[:, j:j + 1]
                fl = cols[:, j + M_HEADS:j + M_HEADS + 1]
                p = jnp.exp(jnp.where(mask, row_ref[0, 0, h:h + 1, blk[d]] - mc, -jnp.inf))
                wi = jnp.exp(row_ref[2, 0, h:h + 1, blk[d]] - mc)
                sq = _bdot(q, jnp.concatenate([kt, cb[:, dv:]], axis=1))
                sp = sq[:, :L] * p
                lhs = jnp.concatenate([sp, q.astype(F32) * wi], axis=1).astype(BF16)
                num = _bdot(lhs, jnp.concatenate([v, cb[:, :dv]], axis=0))
                den = wi * sq[:, L:] + jnp.sum(sp, axis=1, keepdims=True)
                inv = 1.0 / jnp.maximum(jnp.abs(den), fl)
                h_out[0, tok[d], h * dv:(h + 1) * dv] += num * jnp.concatenate([inv] * (dv // LANES), axis=1)


def _scan(q, kt, v, gates_f, gates_b, off_f, off_b, init):
    bsz, hk, t = kt.shape
    hv = v.shape[2]
    dk, dv = hk // M_HEADS, hv // M_HEADS
    assert dk == M_CHUNK == LANES and dv % LANES == 0
    nc = t // M_CHUNK
    cps = 2 if (nc % 2 == 0 and off_f % 2 == 0 and off_b % 2 == 0) else 1
    nsteps = nc // cps
    L = cps * M_CHUNK
    with_outputs = init is not None

    def specs(cidx, off):
        sp = [pl.BlockSpec((1, hk, L), lambda b, s: (b, 0, cidx(s))),
              pl.BlockSpec((1, L, hv), lambda b, s: (b, cidx(s), 0)),
              pl.BlockSpec((6, 1, M_HEADS, L), lambda b, s: (0, b, 0, off // cps + cidx(s)))]
        if with_outputs:
            sp = [pl.BlockSpec((1, L, hk), lambda b, s: (b, cidx(s), 0))] + sp
        return sp

    lead = [q] if with_outputs else []
    args = lead + [kt, v, gates_f] + lead + [kt, v, gates_b]
    in_specs = specs(lambda s: s, off_f) + specs(lambda s: nsteps - 1 - s, off_b)
    state_shape = (bsz, 2, M_HEADS, dk, dv + LANES)
    state_spec = pl.BlockSpec((1,) + state_shape[1:], lambda b, s: (b, 0, 0, 0, 0))
    if with_outputs:
        args.append(init)
        in_specs.append(state_spec)
        out_specs = pl.BlockSpec((1, t, hv), lambda b, s: (b, 0, 0))
        out_shape = jax.ShapeDtypeStruct((bsz, t, hv), F32)
        scratch = [pltpu.VMEM((1,) + state_shape[1:], F32)]
    else:
        out_specs = state_spec
        out_shape = jax.ShapeDtypeStruct(state_shape, F32)
        scratch = []
    return pl.pallas_call(
        functools.partial(_scan_kernel, nc, cps, with_outputs),
        grid=(bsz, nsteps),
        in_specs=in_specs,
        out_specs=out_specs,
        out_shape=out_shape,
        scratch_shapes=scratch,
        compiler_params=_params("parallel", "arbitrary"),
        name="mlstm_scan" if with_outputs else "mlstm_ctx_state",
    )(*args)


def _readout_core(h, sig_o, x, gate, norm_g, w_ref, post_g):
    dv = h.shape[1] // M_HEADS
    parts = []
    for hd in range(M_HEADS):
        hh = h[:, hd * dv:(hd + 1) * dv]
        parts.append(hh * lax.rsqrt(jnp.mean(hh * hh, axis=-1, keepdims=True) + EPS))
    y = (jnp.concatenate(parts, axis=-1) * norm_g * sig_o.astype(F32)).astype(BF16)
    return x + gate * _rms(_bdot(y, w_ref[...]), post_g)


def _ffn_core(fc, x, sh, sc, gate, pre_g, post_g, wu_ref, wg_ref, cw_ref, cb_ref, wd_ref, act_ref):
    tm = x.shape[0]
    f = wu_ref.shape[1]
    hb = (_rms(x, pre_g) * (1.0 + sc) + sh).astype(BF16)
    pos = lax.broadcasted_iota(jnp.int32, (tm, fc), 0) % GRID_W
    has_prev = pos != 0
    has_next = pos != GRID_W - 1
    for j in range(f // fc):
        cs = slice(j * fc, (j + 1) * fc)
        u = _bdot(hb, wu_ref[:, cs])
        g = _bdot(hb, wg_ref[:, cs])
        g_prev = jnp.where(has_prev, pltpu.roll(g, 1, 0), 0.0)
        g_next = jnp.where(has_next, pltpu.roll(g, tm - 1, 0), 0.0)
        gc = g_prev * cw_ref[0:1, cs] + g * cw_ref[1:2, cs] + g_next * cw_ref[2:3, cs] + cb_ref[:, cs]
        act_ref[:, cs] = (gc * jax.nn.sigmoid(gc) * u).astype(BF16)
    y = _bdot(act_ref[...], wd_ref[...])
    return x + gate * _rms(y, post_g)


RADIX = 4


def _fourier_channel_core(xq, sh, sc, pre_g, cs_ref, g_ref):
    gd = cs_ref.shape[0]
    hb = [(_rms(x, pre_g) * (1.0 + sc) + sh).astype(BF16) for x in xq]
    for g in range(hb[0].shape[1] // gd):
        lanes = slice(g * gd, (g + 1) * gd)
        r = [_bdot(h[:, lanes], cs_ref[...]) for h in hb]
        a = [v[:, :gd] for v in r]
        b = [v[:, gd:] for v in r]
        a02p, a02m, a13p, a13m = a[0] + a[2], a[0] - a[2], a[1] + a[3], a[1] - a[3]
        b02p, b02m, b13p, b13m = b[0] + b[2], b[0] - b[2], b[1] + b[3], b[1] - b[3]
        re_im = ((a02p + a13p, -(b02p + b13p)),
                 (a02m - b13m, -(b02m + a13m)),
                 (a02p - a13p, b13p - b02p),
                 (a02m + b13m, a13m - b02m))
        for k, (re, im) in enumerate(re_im):
            g_ref[0, k, 0, :, lanes] = re.astype(BF16)
            g_ref[0, k, 1, :, lanes] = im.astype(BF16)


def _ffn_kernel(fc, x_ref, sh_ref, sc_ref, gate_ref, pre_ref, post_ref, wu_ref, wg_ref, cw_ref, cb_ref,
                wd_ref, out_ref, act_ref):
    out_ref[0] = _ffn_core(fc, x_ref[0], sh_ref[0], sc_ref[0], gate_ref[0], pre_ref[...], post_ref[...],
                           wu_ref, wg_ref, cw_ref, cb_ref, wd_ref, act_ref)


def _layer0_tail_kernel(fc, h_ref, o_ref, x_ref, g1_ref, ng_ref, wo_ref, pm_ref,
                        sh_ref, sc_ref, gate_ref, pre_ref, post_ref, wu_ref, wg_ref, cw_ref, cb_ref, wd_ref,
                        fsh_ref, fsc_ref, fpre_ref, cs_ref, out_ref, g_ref, act_ref):
    rows = x_ref.shape[2]
    cat = lambda ref: jnp.concatenate([ref[0, q] for q in range(RADIX)], axis=0)
    x = _readout_core(cat(h_ref), cat(o_ref), cat(x_ref), g1_ref[0], ng_ref[...], wo_ref, pm_ref[...])
    x = _ffn_core(fc, x, sh_ref[0], sc_ref[0], gate_ref[0], pre_ref[...], post_ref[...],
                  wu_ref, wg_ref, cw_ref, cb_ref, wd_ref, act_ref)
    xq = [x[q * rows:(q + 1) * rows] for q in range(RADIX)]
    for q in range(RADIX):
        out_ref[0, q] = xq[q]
    _fourier_channel_core(xq, fsh_ref[0], fsc_ref[0], fpre_ref[...], cs_ref, g_ref)


def _ffn_specs(layer, d, f, conv_w):
    def lay(rows, cols, col_block=0):
        return pl.BlockSpec((None, rows, cols), lambda b, i: (layer, 0, col_block),
                            pipeline_mode=pl.Buffered(1))
    return [lay(1, d), lay(1, d), lay(d, f, 0), lay(d, f, 1), lay(conv_w, f), lay(1, f), lay(f, d)]


def _ffn(x, sh, sc, gate, layer, pre_g, post_g, w_up, cw, cb, w_down, tm, fc):
    bsz, t, d = x.shape
    f = w_down.shape[1]
    tok = pl.BlockSpec((1, tm, d), lambda b, i: (b, i, 0))
    mod = pl.BlockSpec((1, 1, d), lambda b, i: (b, 0, 0))
    return pl.pallas_call(
        functools.partial(_ffn_kernel, fc),
        grid=(bsz, t // tm),
        in_specs=[tok, mod, mod, mod] + _ffn_specs(layer, d, f, cw.shape[1]),
        out_specs=tok,
        out_shape=jax.ShapeDtypeStruct((bsz, t, d), F32),
        scratch_shapes=[pltpu.VMEM((tm, f), BF16)],
        compiler_params=_params("parallel", "parallel"),
        name="conv_ffn",
    )(x, sh, sc, gate, pre_g, post_g, w_up, w_up, cw, cb, w_down)


def _layer0_tail(h, sig_o, x, g1, norm_g, w_out, post_mix_g, ffn_mods, layer, pre_g, post_g, w_up, cw, cb, w_down,
                 fsh, fsc, fpre_g, cs, rows, fc):
    bsz, t, d = x.shape
    hv = h.shape[2]
    f = w_down.shape[1]
    tq = t // RADIX
    quarters = lambda a: a.reshape(bsz, RADIX, tq, a.shape[2])
    tok = lambda n: pl.BlockSpec((1, RADIX, rows, n), lambda b, i: (b, 0, i, 0))
    mod = pl.BlockSpec((1, 1, d), lambda b, i: (b, 0, 0))
    const = lambda a: pl.BlockSpec(a.shape, lambda b, i: (0,) * a.ndim, pipeline_mode=pl.Buffered(1))
    out, g = pl.pallas_call(
        functools.partial(_layer0_tail_kernel, fc),
        grid=(bsz, tq // rows),
        in_specs=([tok(hv), tok(hv), tok(d), mod, const(norm_g), const(w_out), const(post_mix_g), mod, mod, mod]
                  + _ffn_specs(layer, d, f, cw.shape[1]) + [mod, mod, const(fpre_g), const(cs)]),
        out_specs=[tok(d), pl.BlockSpec((1, RADIX, 2, rows, d), lambda b, i: (b, 0, 0, i, 0))],
        out_shape=[jax.ShapeDtypeStruct((bsz, RADIX, tq, d), F32),
                   jax.ShapeDtypeStruct((bsz, RADIX, 2, tq, d), BF16)],
        scratch_shapes=[pltpu.VMEM((RADIX * rows, f), BF16)],
        compiler_params=_params("parallel", "parallel"),
        name="layer0_tail",
    )(quarters(h), quarters(sig_o), quarters(x), g1, norm_g, w_out, post_mix_g, *ffn_mods,
      pre_g, post_g, w_up, w_up, cw, cb, w_down, fsh, fsc, fpre_g, cs)
    return out.reshape(bsz, t, d), g


def _fourier_token_kernel(scale, dft_ref, g_ref, w_ref, bias_ref, post_ref, gate_ref, x_ref, out_ref,
                          x_s, out_s):
    tm = dft_ref.shape[1]
    nblk = x_s.shape[0]
    for c in range(nblk):
        x_s[c] = x_ref[0, :, c * LANES:(c + 1) * LANES]
    for r in range(RADIX):
        rows = pl.ds(r, tm, stride=RADIX)
        y = _bdot(dft_ref[r], g_ref[0, r]) * scale
        z = _bdot(y.astype(BF16), w_ref[...]) + bias_ref[...]
        xr = jnp.concatenate([x_s[c, rows, :] for c in range(nblk)], axis=1)
        res = xr + gate_ref[0] * _rms(z, post_ref[...])
        for c in range(nblk):
            out_s[c, rows, :] = res[:, c * LANES:(c + 1) * LANES]
    for c in range(nblk):
        out_ref[0, :, c * LANES:(c + 1) * LANES] = out_s[c]


def _fourier_token(dft, g, w, bias, post_g, gate, x, scale, tm):
    bsz, t, d = x.shape
    tq = t // RADIX
    full = lambda a: pl.BlockSpec(a.shape, lambda b, i: (0, 0))
    tok = pl.BlockSpec((1, RADIX * tm, d), lambda b, i: (b, i, 0))
    return pl.pallas_call(
        functools.partial(_fourier_token_kernel, scale),
        grid=(bsz, tq // tm),
        in_specs=[pl.BlockSpec((RADIX, tm, 2 * tq), lambda b, i: (0, i, 0)),
                  pl.BlockSpec((1, RADIX, 2 * tq, d), lambda b, i: (b, 0, 0, 0)),
                  full(w), full(bias), full(post_g), pl.BlockSpec((1, 1, d), lambda b, i: (b, 0, 0)), tok],
        out_specs=tok,
        out_shape=jax.ShapeDtypeStruct((bsz, t, d), F32),
        scratch_shapes=[pltpu.VMEM((d // LANES, RADIX * tm, LANES), F32)] * 2,
        compiler_params=_params("parallel", "arbitrary"),
        name="fourier_token_dft",
    )(dft, g.reshape(bsz, RADIX, 2 * tq, d), w, bias, post_g, gate, x)


def _dft_tables(t, gd):
    idx = np.arange(gd, dtype=np.int64)
    ang = 2.0 * np.pi * ((idx[:, None] * idx[None, :]) % gd).astype(np.float64) / gd
    chan = np.concatenate([np.cos(ang), np.sin(ang)], axis=1).astype(np.float32)
    tq = t // RADIX
    k = RADIX * np.arange(tq, dtype=np.int64)[None, :, None] + np.arange(RADIX, dtype=np.int64)[:, None, None]
    ang = 2.0 * np.pi * ((k * np.arange(tq, dtype=np.int64)[None, None, :]) % t).astype(np.float64) / t
    tok = np.concatenate([np.cos(ang), np.sin(ang)], axis=2).astype(np.float32)
    return jnp.asarray(chan).astype(BF16), jnp.asarray(tok).astype(BF16)


def _row(v):
    return v.reshape(1, -1)


def kernel(x, c, ctx, c_ctx, ada_w, ada_b, pre_mix_g, post_mix_g, pre_ffn_g, post_ffn_g, ffn_up_w, ffn_conv_w,
           ffn_conv_b, ffn_down_w, m_in_w, m_in_b, m_norm_g, m_out_w, f_out_w, f_out_b):
    bsz, t, d = x.shape
    tc = ctx.shape[1]
    f = ffn_down_w.shape[1]
    assert ada_w.shape[0] == 2 and m_in_w.shape[0] == 1 and f_out_w.shape[0] == 1
    assert bsz + 1 <= COND_ROWS and t % M_CHUNK == 0 and tc % M_CHUNK == 0 and t % GRID_W == 0
    hv = m_out_w.shape[1]
    hk = (m_in_w.shape[2] - 2 * hv - 4 * M_HEADS) // 2
    tm = min(512, t)
    fc = 256
    assert f % fc == 0 and t % tm == 0 and tm % GRID_W == 0

    cond = jnp.zeros((COND_ROWS, d), F32).at[:bsz].set(c).at[bsz].set(c_ctx)
    mod = _ada(cond, ada_w, ada_b)
    lat = mod[:, :bsz].reshape(2, bsz, N_MOD, 1, d)
    cmod = mod[0, bsz].reshape(N_MOD, 1, 1, d)

    w_up = ffn_up_w.astype(BF16)
    w_down = ffn_down_w.astype(BF16)
    ffn_rows = [a.reshape(a.shape[0], 1, a.shape[1]) for a in (pre_ffn_g, post_ffn_g, ffn_conv_b)]

    def ffn(xin, i):
        return _ffn(xin, lat[i, :, 3], lat[i, :, 4], lat[i, :, 5], i, ffn_rows[0], ffn_rows[1],
                    w_up, ffn_conv_w, ffn_rows[2], w_down, tm, fc)

    lane_pad = (-m_in_w.shape[2]) % LANES
    w_in = jnp.pad(m_in_w[0].astype(BF16), ((0, 0), (0, lane_pad)))
    b_in = _row(jnp.pad(m_in_b[0], (0, lane_pad)))
    pre0 = _row(pre_mix_g[0])

    _, kc, vc, _, gc = _inproj(ctx, cmod[0], cmod[1], pre0, w_in, b_in, hk, hv, min(tm, tc))
    qx, kx, vx, ox, gx = _inproj(x, lat[0, :, 0], lat[0, :, 1], pre0, w_in, b_in, hk, hv, tm)
    gates_f, gates_b = _gate_prep(gc, gx)
    state = _scan(None, kc, vc, gates_f, gates_b, 0, t // M_CHUNK, None)
    h = _scan(qx, kx, vx, gates_f, gates_b, tc // M_CHUNK, 0, state)
    gd = d // F_GROUPS
    tq = t // RADIX
    rows, tm_tok = min(128, tq), min(256, tq)
    assert t % RADIX == 0 and tq % rows == 0 and tq % tm_tok == 0 and rows % GRID_W == 0
    chan, tok = _dft_tables(t, gd)
    x, g = _layer0_tail(h, ox, x, lat[0, :, 2], _row(m_norm_g[0]), m_out_w[0].astype(BF16), _row(post_mix_g[0]),
                        (lat[0, :, 3], lat[0, :, 4], lat[0, :, 5]), 0, ffn_rows[0], ffn_rows[1], w_up, ffn_conv_w,
                        ffn_rows[2], w_down, lat[1, :, 0], lat[1, :, 1], _row(pre_mix_g[1]), chan, rows, fc)

    x = _fourier_token(tok, g, f_out_w[0].astype(BF16), _row(f_out_b[0]),
                       _row(post_mix_g[1]), lat[1, :, 2], x, float(1.0 / np.sqrt(t * gd)), tm_tok)
    x = ffn(x, 1)
    return x
```

```python
import functools

import numpy as np
import jax
import jax.numpy as jnp
from jax import lax
from jax.experimental import pallas as pl
from jax.experimental.pallas import tpu as pltpu

F32 = jnp.float32
BF16 = jnp.bfloat16

M_HEADS = 4
M_CHUNK = 128
F_GROUPS = 8
GRID_W = 64
EPS = 1e-6
N_MOD = 6
COND_ROWS = 16

VMEM_LIMIT_BYTES = 56 * 1024 * 1024
LANES = 128


def _params(*sem):
    return pltpu.CompilerParams(dimension_semantics=sem, vmem_limit_bytes=VMEM_LIMIT_BYTES)


def _rms(x, g):
    return x * lax.rsqrt(jnp.mean(x * x, axis=-1, keepdims=True) + EPS) * g


def _bdot(a, b):
    return jnp.dot(a, b, preferred_element_type=F32)


def _ada_kernel(c_ref, w_ref, b_ref, o_ref):
    c = c_ref[...]
    s = c * jax.nn.sigmoid(c)
    o_ref[0] = _bdot(s.astype(BF16), w_ref[0].astype(BF16)) + b_ref[0]


def _ada(cond, ada_w, ada_b):
    depth, d, n = ada_w.shape
    tn = n // 4
    return pl.pallas_call(
        _ada_kernel,
        grid=(depth, n // tn),
        in_specs=[pl.BlockSpec((COND_ROWS, d), lambda i, j: (0, 0)),
                  pl.BlockSpec((1, d, tn), lambda i, j: (i, 0, j)),
                  pl.BlockSpec((1, 1, tn), lambda i, j: (i, 0, j))],
        out_specs=pl.BlockSpec((1, COND_ROWS, tn), lambda i, j: (i, 0, j)),
        out_shape=jax.ShapeDtypeStruct((depth, COND_ROWS, n), F32),
        compiler_params=_params("arbitrary", "arbitrary"),
        name="ada",
    )(cond, ada_w, ada_b.reshape(depth, 1, n))


def _inproj_kernel(nconv, x_ref, sh_ref, sc_ref, g_ref, w_ref, b_ref, *refs):
    conv_in, (q_out, kt_out, v_out, o_out, gate_out), conv_out = refs[:nconv], refs[nconv:nconv + 5], refs[nconv + 5:]
    hk, hv = q_out.shape[2], v_out.shape[2]
    dk = hk // M_HEADS
    h = _rms(x_ref[0], g_ref[...]) * (1.0 + sc_ref[0]) + sh_ref[0]
    hb = h.astype(BF16)

    def proj(lo, hi):
        return _bdot(hb, w_ref[:, lo:hi]) + b_ref[:, lo:hi]

    q_out[0] = proj(0, hk).astype(BF16)
    kt_out[0] = jnp.transpose(proj(hk, 2 * hk) * (dk ** -0.5)).astype(BF16)
    v_out[0] = proj(2 * hk, 2 * hk + hv).astype(BF16)
    o_out[0] = jax.nn.sigmoid(proj(2 * hk + hv, 2 * hk + 2 * hv)).astype(BF16)
    gate_out[0] = jnp.transpose(proj(2 * hk + 2 * hv, w_ref.shape[1]))[:gate_out.shape[1]]
    for src_ref, dst_ref in zip(conv_in, conv_out):
        dst_ref[...] = src_ref[...].astype(BF16)


def _inproj(x, sh, sc, g, w, bias, hk, hv, tm, convert=()):
    bsz, t, d = x.shape
    nt = t // tm
    per_batch = sh.shape[0] > 1
    mod_spec = pl.BlockSpec((1, 1, d), (lambda b, i: (b, 0, 0)) if per_batch else (lambda b, i: (0, 0, 0)))
    full = lambda a: pl.BlockSpec(a.shape, lambda b, i: (0, 0))
    tok = lambda n: pl.BlockSpec((1, tm, n), lambda b, i: (b, i, 0))
    tok_t = lambda n: pl.BlockSpec((1, n, tm), lambda b, i: (b, 0, i))
    flat = [a.reshape(-1, a.shape[-1]) for a in convert]
    conv_specs = [pl.BlockSpec((a.shape[0] // (bsz * nt), a.shape[1]), lambda b, i: (b * nt + i, 0)) for a in flat]
    assert all(a.shape[0] % (16 * bsz * nt) == 0 for a in flat)
    outs = pl.pallas_call(
        functools.partial(_inproj_kernel, len(flat)),
        grid=(bsz, nt),
        in_specs=[tok(d), mod_spec, mod_spec, full(g), full(w), full(bias)] + conv_specs,
        out_specs=[tok(hk), tok_t(hk), tok(hv), tok(hv), tok_t(4 * M_HEADS)] + conv_specs,
        out_shape=[jax.ShapeDtypeStruct((bsz, t, hk), BF16), jax.ShapeDtypeStruct((bsz, hk, t), BF16),
                   jax.ShapeDtypeStruct((bsz, t, hv), BF16), jax.ShapeDtypeStruct((bsz, t, hv), BF16),
                   jax.ShapeDtypeStruct((bsz, 4 * M_HEADS, t), F32)]
                  + [jax.ShapeDtypeStruct(a.shape, BF16) for a in flat],
        compiler_params=_params("parallel", "parallel"),
        name="mlstm_inproj",
    )(x, sh, sc, g, w, bias, *flat)
    return list(outs[:5]) + [o.reshape(a.shape) for o, a in zip(outs[5:], convert)]


def _lane_scan(x, op, fill, reverse, seg):
    n = x.shape[-1]
    lane = lax.broadcasted_iota(jnp.int32, x.shape, x.ndim - 1) % seg
    d = 1
    while d < seg:
        if reverse:
            shifted = jnp.where(lane < seg - d, pltpu.roll(x, n - d, x.ndim - 1), fill)
        else:
            shifted = jnp.where(lane >= d, pltpu.roll(x, d, x.ndim - 1), fill)
        x = op(x, shifted)
        d *= 2
    return x


def _log_sigmoid(x):
    return jnp.minimum(x, 0.0) - jnp.log1p(jnp.exp(-jnp.abs(x)))


def _gate_kernel(gf_ref, gb_ref, of_ref, ob_ref):
    L = M_CHUNK
    nc = gf_ref.shape[-1] // L
    for d, (g_ref, o_ref) in enumerate(((gf_ref, of_ref), (gb_ref, ob_ref))):
        edge = L - 1 if d == 0 else 0
        lf = _log_sigmoid(g_ref[:, 1])
        b_all = _lane_scan(lf, jnp.add, 0.0, d == 1, L)
        r1_all = g_ref[:, 0] - b_all
        cm_all = _lane_scan(r1_all, jnp.maximum, -jnp.inf, d == 1, L)
        m = jnp.zeros(lf.shape[:-1] + (1,), F32)
        for c in (range(nc) if d == 0 else range(nc - 1, -1, -1)):
            lanes = slice(c * L, (c + 1) * L)
            b, r1, cm = b_all[..., lanes], r1_all[..., lanes], cm_all[..., lanes]
            mx = jnp.maximum(m, cm[..., edge:edge + 1])
            mc = jnp.maximum(m, cm)
            o_ref[0, :, :, lanes] = r1
            o_ref[1, :, :, lanes] = mc
            o_ref[2, :, :, lanes] = jnp.broadcast_to(m, r1.shape)
            o_ref[3, :, :, lanes] = jnp.exp(-(b + mc))
            o_ref[4, :, :, lanes] = jnp.exp(r1 - mx)
            o_ref[5, :, :, lanes] = jnp.broadcast_to(jnp.exp(m - mx), r1.shape)
            m = b[..., edge:edge + 1] + mx


def _gate_prep(gates_c, gates_x):
    bsz = gates_c.shape[0]
    n = gates_c.shape[2] + gates_x.shape[2]
    gf = jnp.concatenate([gates_c, gates_x], axis=2).reshape(bsz, 4, M_HEADS, n)
    gb = jnp.concatenate([gates_x, gates_c], axis=2).reshape(bsz, 4, M_HEADS, n)
    out_spec = pl.BlockSpec((6, bsz, M_HEADS, n), lambda j: (0, 0, 0, 0))
    return pl.pallas_call(
        _gate_kernel,
        grid=(1,),
        in_specs=[pl.BlockSpec((bsz, 2, M_HEADS, n), lambda j: (0, 0, 0, 0)),
                  pl.BlockSpec((bsz, 2, M_HEADS, n), lambda j: (0, 1, 0, 0))],
        out_specs=[out_spec, out_spec],
        out_shape=[jax.ShapeDtypeStruct((6, bsz, M_HEADS, n), F32)] * 2,
        compiler_params=_params("arbitrary"),
        name="mlstm_gates",
    )(gf, gb)


def _scan_kernel(nchunks, cps, with_outputs, *refs):
    if with_outputs:
        (qf, ktf, vf, rf, qb, ktb, vb, rb, c0, h_out, c_s) = refs
        streams = ((qf, ktf, vf, rf), (qb, ktb, vb, rb))
    else:
        (ktf, vf, rf, ktb, vb, rb, c_s) = refs
        streams = ((None, ktf, vf, rf), (None, ktb, vb, rb))
    s = pl.program_id(1)
    L = M_CHUNK
    dk = ktf.shape[1] // M_HEADS
    dv = vf.shape[2] // M_HEADS

    @pl.when(s == 0)
    def _():
        if with_outputs:
            c_s[...] = c0[...]
            h_out[...] = jnp.zeros(h_out.shape, F32)
        else:
            c_s[...] = jnp.zeros(c_s.shape, F32)

    row = lax.broadcasted_iota(jnp.int32, (L, L), 0)
    col = lax.broadcasted_iota(jnp.int32, (L, L), 1)
    for cc in range(cps):
        sub = (cc, cps - 1 - cc)
        blk = [slice(i * L, (i + 1) * L) for i in sub]
        chunk = (s * cps + cc, nchunks - 1 - (s * cps + cc))
        tok = [pl.ds(pl.multiple_of(c * L, L), L) for c in chunk]
        if with_outputs:
            stack = [r_ref[j, 0, :, blk[d]] for d, r_ref in enumerate((rf, rb)) for j in (1, 3)]
            stack.append(jnp.zeros((L - 4 * M_HEADS, L), F32))
            cols = jnp.transpose(jnp.concatenate(stack, axis=0))
        for d in range(2):
            q_ref, kt_ref, v_ref, row_ref = streams[d]
            mask = (col <= row) if d == 0 else (col >= row)
            for h in range(M_HEADS):
                kt = kt_ref[0, h * dk:(h + 1) * dk, blk[d]]
                v = v_ref[0, blk[d], h * dv:(h + 1) * dv]
                c_old = c_s[0, d, h]
                kw = kt.astype(F32) * row_ref[4, 0, h:h + 1, blk[d]]
                n_upd = jnp.broadcast_to(jnp.sum(kw, axis=1, keepdims=True), (dk, LANES))
                upd = jnp.concatenate([_bdot(kw.astype(BF16), v), n_upd], axis=1)
                c_s[0, d, h] = row_ref[5, 0, h:h + 1, sub[d] * L:sub[d] * L + 1] * c_old + upd
                if not with_outputs:
                    continue
                q = q_ref[0, blk[d], h * dk:(h + 1) * dk]
                cb = c_old.astype(BF16)
                j = 2 * M_HEADS * d + h
                mc = cols[:, j:j + 1]
                fl = cols[:, j + M_HEADS:j + M_HEADS + 1]
                p = jnp.exp(jnp.where(mask, row_ref[0, 0, h:h + 1, blk[d]] - mc, -jnp.inf))
                wi = jnp.exp(row_ref[2, 0, h:h + 1, blk[d]] - mc)
                sq = _bdot(q, jnp.concatenate([kt, cb[:, dv:]], axis=1))
                sp = sq[:, :L] * p
                lhs = jnp.concatenate([sp, q.astype(F32) * wi], axis=1).astype(BF16)
                num = _bdot(lhs, jnp.concatenate([v, cb[:, :dv]], axis=0))
                den = wi * sq[:, L:] + jnp.sum(sp, axis=1, keepdims=True)
                inv = 1.0 / jnp.maximum(jnp.abs(den), fl)
                h_out[0, tok[d], h * dv:(h + 1) * dv] += num * jnp.concatenate([inv] * (dv // LANES), axis=1)


def _scan(q, kt, v, gates_f, gates_b, off_f, off_b, init):
    bsz, hk, t = kt.shape
    hv = v.shape[2]
    dk, dv = hk // M_HEADS, hv // M_HEADS
    assert dk == M_CHUNK == LANES and dv % LANES == 0
    nc = t // M_CHUNK
    cps = 2 if (nc % 2 == 0 and off_f % 2 == 0 and off_b % 2 == 0) else 1
    nsteps = nc // cps
    L = cps * M_CHUNK
    with_outputs = init is not None

    def specs(cidx, off):
        sp = [pl.BlockSpec((1, hk, L), lambda b, s: (b, 0, cidx(s))),
              pl.BlockSpec((1, L, hv), lambda b, s: (b, cidx(s), 0)),
              pl.BlockSpec((6, 1, M_HEADS, L), lambda b, s: (0, b, 0, off // cps + cidx(s)))]
        if with_outputs:
            sp = [pl.BlockSpec((1, L, hk), lambda b, s: (b, cidx(s), 0))] + sp
        return sp

    lead = [q] if with_outputs else []
    args = lead + [kt, v, gates_f] + lead + [kt, v, gates_b]
    in_specs = specs(lambda s: s, off_f) + specs(lambda s: nsteps - 1 - s, off_b)
    state_shape = (bsz, 2, M_HEADS, dk, dv + LANES)
    state_spec = pl.BlockSpec((1,) + state_shape[1:], lambda b, s: (b, 0, 0, 0, 0))
    if with_outputs:
        args.append(init)
        in_specs.append(state_spec)
        out_specs = pl.BlockSpec((1, t, hv), lambda b, s: (b, 0, 0))
        out_shape = jax.ShapeDtypeStruct((bsz, t, hv), F32)
        scratch = [pltpu.VMEM((1,) + state_shape[1:], F32)]
    else:
        out_specs = state_spec
        out_shape = jax.ShapeDtypeStruct(state_shape, F32)
        scratch = []
    return pl.pallas_call(
        functools.partial(_scan_kernel, nc, cps, with_outputs),
        grid=(bsz, nsteps),
        in_specs=in_specs,
        out_specs=out_specs,
        out_shape=out_shape,
        scratch_shapes=scratch,
        compiler_params=_params("parallel", "arbitrary"),
        name="mlstm_scan" if with_outputs else "mlstm_ctx_state",
    )(*args)


def _readout_core(h, sig_o, x, gate, norm_g, w_ref, post_g):
    dv = h.shape[1] // M_HEADS
    parts = []
    for hd in range(M_HEADS):
        hh = h[:, hd * dv:(hd + 1) * dv]
        parts.append(hh * lax.rsqrt(jnp.mean(hh * hh, axis=-1, keepdims=True) + EPS))
    y = (jnp.concatenate(parts, axis=-1) * norm_g * sig_o.astype(F32)).astype(BF16)
    return x + gate * _rms(_bdot(y, w_ref[...]), post_g)


def _ffn_core(fc, x, sh, sc, gate, pre_g, post_g, wu_ref, wg_ref, cw_ref, cb_ref, wd_ref, act_ref):
    tm = x.shape[0]
    f = wu_ref.shape[1]
    hb = (_rms(x, pre_g) * (1.0 + sc) + sh).astype(BF16)
    pos = lax.broadcasted_iota(jnp.int32, (tm, fc), 0) % GRID_W
    has_prev = pos != 0
    has_next = pos != GRID_W - 1
    for j in range(f // fc):
        cs = slice(j * fc, (j + 1) * fc)
        u = _bdot(hb, wu_ref[:, cs])
        g = _bdot(hb, wg_ref[:, cs])
        g_prev = jnp.where(has_prev, pltpu.roll(g, 1, 0), 0.0)
        g_next = jnp.where(has_next, pltpu.roll(g, tm - 1, 0), 0.0)
        gc = g_prev * cw_ref[0:1, cs] + g * cw_ref[1:2, cs] + g_next * cw_ref[2:3, cs] + cb_ref[:, cs]
        act_ref[:, cs] = (gc * jax.nn.sigmoid(gc) * u).astype(BF16)
    y = _bdot(act_ref[...], wd_ref[...])
    return x + gate * _rms(y, post_g)


RADIX = 4


def _fourier_channel_core(xq, sh, sc, pre_g, cs_ref, g_ref):
    gd = cs_ref.shape[0]
    hb = [(_rms(x, pre_g) * (1.0 + sc) + sh).astype(BF16) for x in xq]
    for g in range(hb[0].shape[1] // gd):
        lanes = slice(g * gd, (g + 1) * gd)
        r = [_bdot(h[:, lanes], cs_ref[...]) for h in hb]
        a = [v[:, :gd] for v in r]
        b = [v[:, gd:] for v in r]
        a02p, a02m, a13p, a13m = a[0] + a[2], a[0] - a[2], a[1] + a[3], a[1] - a[3]
        b02p, b02m, b13p, b13m = b[0] + b[2], b[0] - b[2], b[1] + b[3], b[1] - b[3]
        re_im = ((a02p + a13p, -(b02p + b13p)),
                 (a02m - b13m, -(b02m + a13m)),
                 (a02p - a13p, b13p - b02p),
                 (a02m + b13m, a13m - b02m))
        for k, (re, im) in enumerate(re_im):
            g_ref[0, k, 0, :, lanes] = re.astype(BF16)
            g_ref[0, k, 1, :, lanes] = im.astype(BF16)


def _ffn_kernel(fc, x_ref, sh_ref, sc_ref, gate_ref, pre_ref, post_ref, wu_ref, wg_ref, cw_ref, cb_ref,
                wd_ref, out_ref, act_ref):
    out_ref[0] = _ffn_core(fc, x_ref[0], sh_ref[0], sc_ref[0], gate_ref[0], pre_ref[...], post_ref[...],
                           wu_ref, wg_ref, cw_ref, cb_ref, wd_ref, act_ref)


def _layer0_tail_kernel(fc, h_ref, o_ref, x_ref, g1_ref, ng_ref, wo_ref, pm_ref,
                        sh_ref, sc_ref, gate_ref, pre_ref, post_ref, wu_ref, wg_ref, cw_ref, cb_ref, wd_ref,
                        fsh_ref, fsc_ref, fpre_ref, cs_ref, out_ref, g_ref, act_ref):
    rows = x_ref.shape[2]
    cat = lambda ref: jnp.concatenate([ref[0, q] for q in range(RADIX)], axis=0)
    x = _readout_core(cat(h_ref), cat(o_ref), cat(x_ref), g1_ref[0], ng_ref[...], wo_ref, pm_ref[...])
    x = _ffn_core(fc, x, sh_ref[0], sc_ref[0], gate_ref[0], pre_ref[...], post_ref[...],
                  wu_ref, wg_ref, cw_ref, cb_ref, wd_ref, act_ref)
    xq = [x[q * rows:(q + 1) * rows] for q in range(RADIX)]
    for q in range(RADIX):
        out_ref[0, q] = xq[q]
    _fourier_channel_core(xq, fsh_ref[0], fsc_ref[0], fpre_ref[...], cs_ref, g_ref)


def _ffn_specs(layer, d, f, conv_w):
    def lay(rows, cols, col_block=0):
        return pl.BlockSpec((None, rows, cols), lambda b, i: (layer, 0, col_block),
                            pipeline_mode=pl.Buffered(1))
    return [lay(1, d), lay(1, d), lay(d, f, 0), lay(d, f, 1), lay(conv_w, f), lay(1, f), lay(f, d)]


def _ffn(x, sh, sc, gate, layer, pre_g, post_g, w_up, cw, cb, w_down, tm, fc):
    bsz, t, d = x.shape
    f = w_down.shape[1]
    tok = pl.BlockSpec((1, tm, d), lambda b, i: (b, i, 0))
    mod = pl.BlockSpec((1, 1, d), lambda b, i: (b, 0, 0))
    return pl.pallas_call(
        functools.partial(_ffn_kernel, fc),
        grid=(bsz, t // tm),
        in_specs=[tok, mod, mod, mod] + _ffn_specs(layer, d, f, cw.shape[1]),
        out_specs=tok,
        out_shape=jax.ShapeDtypeStruct((bsz, t, d), F32),
        scratch_shapes=[pltpu.VMEM((tm, f), BF16)],
        compiler_params=_params("parallel", "parallel"),
        name="conv_ffn",
    )(x, sh, sc, gate, pre_g, post_g, w_up, w_up, cw, cb, w_down)


def _layer0_tail(h, sig_o, x, g1, norm_g, w_out, post_mix_g, ffn_mods, layer, pre_g, post_g, w_up, cw, cb, w_down,
                 fsh, fsc, fpre_g, cs, rows, fc):
    bsz, t, d = x.shape
    hv = h.shape[2]
    f = w_down.shape[1]
    tq = t // RADIX
    quarters = lambda a: a.reshape(bsz, RADIX, tq, a.shape[2])
    tok = lambda n: pl.BlockSpec((1, RADIX, rows, n), lambda b, i: (b, 0, i, 0))
    mod = pl.BlockSpec((1, 1, d), lambda b, i: (b, 0, 0))
    const = lambda a: pl.BlockSpec(a.shape, lambda b, i: (0,) * a.ndim, pipeline_mode=pl.Buffered(1))
    out, g = pl.pallas_call(
        functools.partial(_layer0_tail_kernel, fc),
        grid=(bsz, tq // rows),
        in_specs=([tok(hv), tok(hv), tok(d), mod, const(norm_g), const(w_out), const(post_mix_g), mod, mod, mod]
                  + _ffn_specs(layer, d, f, cw.shape[1]) + [mod, mod, const(fpre_g), const(cs)]),
        out_specs=[tok(d), pl.BlockSpec((1, RADIX, 2, rows, d), lambda b, i: (b, 0, 0, i, 0))],
        out_shape=[jax.ShapeDtypeStruct((bsz, RADIX, tq, d), F32),
                   jax.ShapeDtypeStruct((bsz, RADIX, 2, tq, d), BF16)],
        scratch_shapes=[pltpu.VMEM((RADIX * rows, f), BF16)],
        compiler_params=_params("parallel", "parallel"),
        name="layer0_tail",
    )(quarters(h), quarters(sig_o), quarters(x), g1, norm_g, w_out, post_mix_g, *ffn_mods,
      pre_g, post_g, w_up, w_up, cw, cb, w_down, fsh, fsc, fpre_g, cs)
    return out.reshape(bsz, t, d), g


def _fourier_token_kernel(scale, dft_ref, g_ref, w_ref, bias_ref, post_ref, gate_ref, x_ref, out_ref,
                          x_s, out_s):
    tm = dft_ref.shape[1]
    nblk = x_s.shape[0]
    for c in range(nblk):
        x_s[c] = x_ref[0, :, c * LANES:(c + 1) * LANES]
    for r in range(RADIX):
        rows = pl.ds(r, tm, stride=RADIX)
        y = _bdot(dft_ref[r], g_ref[0, r]) * scale
        z = _bdot(y.astype(BF16), w_ref[...]) + bias_ref[...]
        xr = jnp.concatenate([x_s[c, rows, :] for c in range(nblk)], axis=1)
        res = xr + gate_ref[0] * _rms(z, post_ref[...])
        for c in range(nblk):
            out_s[c, rows, :] = res[:, c * LANES:(c + 1) * LANES]
    for c in range(nblk):
        out_ref[0, :, c * LANES:(c + 1) * LANES] = out_s[c]


def _fourier_token(dft, g, w, bias, post_g, gate, x, scale, tm):
    bsz, t, d = x.shape
    tq = t // RADIX
    full = lambda a: pl.BlockSpec(a.shape, lambda b, i: (0, 0))
    tok = pl.BlockSpec((1, RADIX * tm, d), lambda b, i: (b, i, 0))
    return pl.pallas_call(
        functools.partial(_fourier_token_kernel, scale),
        grid=(bsz, tq // tm),
        in_specs=[pl.BlockSpec((RADIX, tm, 2 * tq), lambda b, i: (0, i, 0)),
                  pl.BlockSpec((1, RADIX, 2 * tq, d), lambda b, i: (b, 0, 0, 0)),
                  full(w), full(bias), full(post_g), pl.BlockSpec((1, 1, d), lambda b, i: (b, 0, 0)), tok],
        out_specs=tok,
        out_shape=jax.ShapeDtypeStruct((bsz, t, d), F32),
        scratch_shapes=[pltpu.VMEM((d // LANES, RADIX * tm, LANES), F32)] * 2,
        compiler_params=_params("parallel", "arbitrary"),
        name="fourier_token_dft",
    )(dft, g.reshape(bsz, RADIX, 2 * tq, d), w, bias, post_g, gate, x)


def _dft_tables(t, gd):
    idx = np.arange(gd, dtype=np.int64)
    ang = 2.0 * np.pi * ((idx[:, None] * idx[None, :]) % gd).astype(np.float64) / gd
    chan = np.concatenate([np.cos(ang), np.sin(ang)], axis=1).astype(np.float32)
    tq = t // RADIX
    k = RADIX * np.arange(tq, dtype=np.int64)[None, :, None] + np.arange(RADIX, dtype=np.int64)[:, None, None]
    ang = 2.0 * np.pi * ((k * np.arange(tq, dtype=np.int64)[None, None, :]) % t).astype(np.float64) / t
    tok = np.concatenate([np.cos(ang), np.sin(ang)], axis=2).astype(np.float32)
    return jnp.asarray(chan).astype(BF16), jnp.asarray(tok).astype(BF16)


def _row(v):
    return v.reshape(1, -1)


def kernel(x, c, ctx, c_ctx, ada_w, ada_b, pre_mix_g, post_mix_g, pre_ffn_g, post_ffn_g, ffn_up_w, ffn_conv_w,
           ffn_conv_b, ffn_down_w, m_in_w, m_in_b, m_norm_g, m_out_w, f_out_w, f_out_b):
    bsz, t, d = x.shape
    tc = ctx.shape[1]
    f = ffn_down_w.shape[1]
    assert ada_w.shape[0] == 2 and m_in_w.shape[0] == 1 and f_out_w.shape[0] == 1
    assert bsz + 1 <= COND_ROWS and t % M_CHUNK == 0 and tc % M_CHUNK == 0 and t % GRID_W == 0
    hv = m_out_w.shape[1]
    hk = (m_in_w.shape[2] - 2 * hv - 4 * M_HEADS) // 2
    tm = min(512, t)
    fc = 256
    assert f % fc == 0 and t % tm == 0 and tm % GRID_W == 0

    cond = jnp.zeros((COND_ROWS, d), F32).at[:bsz].set(c).at[bsz].set(c_ctx)
    mod = _ada(cond, ada_w, ada_b)
    lat = mod[:, :bsz].reshape(2, bsz, N_MOD, 1, d)
    cmod = mod[0, bsz].reshape(N_MOD, 1, 1, d)

    ffn_rows = [a.reshape(a.shape[0], 1, a.shape[1]) for a in (pre_ffn_g, post_ffn_g, ffn_conv_b)]

    lane_pad = (-m_in_w.shape[2]) % LANES
    w_in = jnp.pad(m_in_w[0].astype(BF16), ((0, 0), (0, lane_pad)))
    b_in = _row(jnp.pad(m_in_b[0], (0, lane_pad)))
    pre0 = _row(pre_mix_g[0])

    _, kc, vc, _, gc = _inproj(ctx, cmod[0], cmod[1], pre0, w_in, b_in, hk, hv, min(tm, tc))
    qx, kx, vx, ox, gx, w_up, w_down, w_mo, w_fo = _inproj(
        x, lat[0, :, 0], lat[0, :, 1], pre0, w_in, b_in, hk, hv, tm, convert=(ffn_up_w, ffn_down_w, m_out_w, f_out_w))
    gates_f, gates_b = _gate_prep(gc, gx)
    state = _scan(None, kc, vc, gates_f, gates_b, 0, t // M_CHUNK, None)
    h = _scan(qx, kx, vx, gates_f, gates_b, tc // M_CHUNK, 0, state)
    gd = d // F_GROUPS
    tq = t // RADIX
    rows, tm_tok = min(128, tq), min(256, tq)
    assert t % RADIX == 0 and tq % rows == 0 and tq % tm_tok == 0 and rows % GRID_W == 0
    chan, tok = _dft_tables(t, gd)
    x, g = _layer0_tail(h, ox, x, lat[0, :, 2], _row(m_norm_g[0]), w_mo[0], _row(post_mix_g[0]),
                        (lat[0, :, 3], lat[0, :, 4], lat[0, :, 5]), 0, ffn_rows[0], ffn_rows[1], w_up, ffn_conv_w,
                        ffn_rows[2], w_down, lat[1, :, 0], lat[1, :, 1], _row(pre_mix_g[1]), chan, rows, fc)

    x = _fourier_token(tok, g, w_fo[0], _row(f_out_b[0]),
                       _row(post_mix_g[1]), lat[1, :, 2], x, float(1.0 / np.sqrt(t * gd)), tm_tok)
    return _ffn(x, lat[1, :, 3], lat[1, :, 4], lat[1, :, 5], 1, ffn_rows[0], ffn_rows[1], w_up, ffn_conv_w,
                ffn_rows[2], w_down, tm, fc)
```

```python
import functools

import numpy as np
import jax
import jax.numpy as jnp
from jax import lax
from jax.experimental import pallas as pl
from jax.experimental.pallas import tpu as pltpu

F32 = jnp.float32
BF16 = jnp.bfloat16

M_HEADS = 4
M_CHUNK = 128
F_GROUPS = 8
GRID_W = 64
EPS = 1e-6
N_MOD = 6
COND_ROWS = 16
SCAN_CHUNKS_PER_STEP = 4

VMEM_LIMIT_BYTES = 56 * 1024 * 1024
LANES = 128


def _params(*sem):
    return pltpu.CompilerParams(dimension_semantics=sem, vmem_limit_bytes=VMEM_LIMIT_BYTES)


def _rms(x, g):
    return x * lax.rsqrt(jnp.mean(x * x, axis=-1, keepdims=True) + EPS) * g


def _bdot(a, b):
    return jnp.dot(a, b, preferred_element_type=F32)


def _ada_kernel(c_ref, w_ref, b_ref, win_ref, o_ref, win_out):
    c = c_ref[...]
    s = c * jax.nn.sigmoid(c)
    o_ref[0] = _bdot(s.astype(BF16), w_ref[0].astype(BF16)) + b_ref[0]
    n = win_ref.shape[1]
    win_out[:, :n] = win_ref[...].astype(BF16)
    win_out[:, n:] = jnp.zeros((win_out.shape[0], win_out.shape[1] - n), BF16)


def _ada(cond, ada_w, ada_b, w_in):
    depth, d, n = ada_w.shape
    tn = n // 4
    steps = depth * (n // tn)
    rows, cols = w_in.shape
    padded = cols + (-cols) % LANES
    assert rows % (16 * steps) == 0 and padded > cols
    return pl.pallas_call(
        _ada_kernel,
        grid=(depth, n // tn),
        in_specs=[pl.BlockSpec((COND_ROWS, d), lambda i, j: (0, 0)),
                  pl.BlockSpec((1, d, tn), lambda i, j: (i, 0, j)),
                  pl.BlockSpec((1, 1, tn), lambda i, j: (i, 0, j)),
                  pl.BlockSpec((rows // steps, cols), lambda i, j: (i * (n // tn) + j, 0))],
        out_specs=[pl.BlockSpec((1, COND_ROWS, tn), lambda i, j: (i, 0, j)),
                   pl.BlockSpec((rows // steps, padded), lambda i, j: (i * (n // tn) + j, 0))],
        out_shape=[jax.ShapeDtypeStruct((depth, COND_ROWS, n), F32),
                   jax.ShapeDtypeStruct((rows, padded), BF16)],
        compiler_params=_params("arbitrary", "arbitrary"),
        name="ada",
    )(cond, ada_w, ada_b.reshape(depth, 1, n), w_in)


def _inproj_kernel(nconv, x_ref, sh_ref, sc_ref, g_ref, w_ref, b_ref, *refs):
    conv_in, (q_out, kt_out, v_out, o_out, gate_out), conv_out = refs[:nconv], refs[nconv:nconv + 5], refs[nconv + 5:]
    hk, hv = q_out.shape[2], v_out.shape[2]
    dk = hk // M_HEADS
    h = _rms(x_ref[0], g_ref[...]) * (1.0 + sc_ref[0]) + sh_ref[0]
    hb = h.astype(BF16)

    def proj(lo, hi):
        return _bdot(hb, w_ref[:, lo:hi]) + b_ref[:, lo:hi]

    q_out[0] = proj(0, hk).astype(BF16)
    kt_out[0] = jnp.transpose(proj(hk, 2 * hk) * (dk ** -0.5)).astype(BF16)
    v_out[0] = proj(2 * hk, 2 * hk + hv).astype(BF16)
    o_out[0] = jax.nn.sigmoid(proj(2 * hk + hv, 2 * hk + 2 * hv)).astype(BF16)
    gate_out[0] = jnp.transpose(proj(2 * hk + 2 * hv, w_ref.shape[1]))[:gate_out.shape[1]]
    for src_ref, dst_ref in zip(conv_in, conv_out):
        dst_ref[...] = src_ref[...].astype(BF16)


def _inproj(x, sh, sc, g, w, bias, hk, hv, tm, convert=()):
    bsz, t, d = x.shape
    nt = t // tm
    per_batch = sh.shape[0] > 1
    mod_spec = pl.BlockSpec((1, 1, d), (lambda b, i: (b, 0, 0)) if per_batch else (lambda b, i: (0, 0, 0)))
    full = lambda a: pl.BlockSpec(a.shape, lambda b, i: (0, 0))
    tok = lambda n: pl.BlockSpec((1, tm, n), lambda b, i: (b, i, 0))
    tok_t = lambda n: pl.BlockSpec((1, n, tm), lambda b, i: (b, 0, i))
    flat = [a.reshape(-1, a.shape[-1]) for a in convert]
    conv_specs = [pl.BlockSpec((a.shape[0] // (bsz * nt), a.shape[1]), lambda b, i: (b * nt + i, 0)) for a in flat]
    assert all(a.shape[0] % (16 * bsz * nt) == 0 for a in flat)
    outs = pl.pallas_call(
        functools.partial(_inproj_kernel, len(flat)),
        grid=(bsz, nt),
        in_specs=[tok(d), mod_spec, mod_spec, full(g), full(w), full(bias)] + conv_specs,
        out_specs=[tok(hk), tok_t(hk), tok(hv), tok(hv), tok_t(4 * M_HEADS)] + conv_specs,
        out_shape=[jax.ShapeDtypeStruct((bsz, t, hk), BF16), jax.ShapeDtypeStruct((bsz, hk, t), BF16),
                   jax.ShapeDtypeStruct((bsz, t, hv), BF16), jax.ShapeDtypeStruct((bsz, t, hv), BF16),
                   jax.ShapeDtypeStruct((bsz, 4 * M_HEADS, t), F32)]
                  + [jax.ShapeDtypeStruct(a.shape, BF16) for a in flat],
        compiler_params=_params("parallel", "parallel"),
        name="mlstm_inproj",
    )(x, sh, sc, g, w, bias, *flat)
    return list(outs[:5]) + [o.reshape(a.shape) for o, a in zip(outs[5:], convert)]


def _lane_scan(x, op, fill, reverse, seg):
    n = x.shape[-1]
    lane = lax.broadcasted_iota(jnp.int32, x.shape, x.ndim - 1) % seg
    d = 1
    while d < seg:
        if reverse:
            shifted = jnp.where(lane < seg - d, pltpu.roll(x, n - d, x.ndim - 1), fill)
        else:
            shifted = jnp.where(lane >= d, pltpu.roll(x, d, x.ndim - 1), fill)
        x = op(x, shifted)
        d *= 2
    return x


def _log_sigmoid(x):
    return jnp.minimum(x, 0.0) - jnp.log1p(jnp.exp(-jnp.abs(x)))


def _gate_kernel(orders, gf_ref, gb_ref, of_ref, ob_ref):
    L = M_CHUNK
    for d, (g_ref, o_ref) in enumerate(((gf_ref, of_ref), (gb_ref, ob_ref))):
        edge = L - 1 if d == 0 else 0
        lf = _log_sigmoid(g_ref[:, 1])
        b_all = _lane_scan(lf, jnp.add, 0.0, d == 1, L)
        r1_all = g_ref[:, 0] - b_all
        cm_all = _lane_scan(r1_all, jnp.maximum, -jnp.inf, d == 1, L)
        m = jnp.zeros(lf.shape[:-1] + (1,), F32)
        for c in orders[d]:
            lanes = slice(c * L, (c + 1) * L)
            b, r1, cm = b_all[..., lanes], r1_all[..., lanes], cm_all[..., lanes]
            mx = jnp.maximum(m, cm[..., edge:edge + 1])
            mc = jnp.maximum(m, cm)
            o_ref[0, :, :, lanes] = r1
            o_ref[1, :, :, lanes] = mc
            o_ref[2, :, :, lanes] = jnp.broadcast_to(m, r1.shape)
            o_ref[3, :, :, lanes] = jnp.exp(-(b + mc))
            o_ref[4, :, :, lanes] = jnp.exp(r1 - mx)
            o_ref[5, :, :, lanes] = jnp.broadcast_to(jnp.exp(m - mx), r1.shape)
            m = b[..., edge:edge + 1] + mx


def _gate_prep(gates_c, gates_x):
    bsz = gates_c.shape[0]
    nlt, nct = gates_x.shape[2] // M_CHUNK, gates_c.shape[2] // M_CHUNK
    n = (nlt + nct) * M_CHUNK
    fwd = tuple(range(nlt, nlt + nct)) + tuple(range(nlt))
    bwd = tuple(range(nlt + nct - 1, nlt - 1, -1)) + tuple(range(nlt - 1, -1, -1))
    gf = gb = jnp.concatenate([gates_x, gates_c], axis=2).reshape(bsz, 4, M_HEADS, n)
    out_spec = pl.BlockSpec((6, bsz, M_HEADS, n), lambda j: (0, 0, 0, 0))
    return pl.pallas_call(
        functools.partial(_gate_kernel, (fwd, bwd)),
        grid=(1,),
        in_specs=[pl.BlockSpec((bsz, 2, M_HEADS, n), lambda j: (0, 0, 0, 0)),
                  pl.BlockSpec((bsz, 2, M_HEADS, n), lambda j: (0, 1, 0, 0))],
        out_specs=[out_spec, out_spec],
        out_shape=[jax.ShapeDtypeStruct((6, bsz, M_HEADS, n), F32)] * 2,
        compiler_params=_params("arbitrary"),
        name="mlstm_gates",
    )(gf, gb)


def _scan_kernel(nchunks, cps, with_outputs, *refs):
    if with_outputs:
        (qf, ktf, vf, rf, qb, ktb, vb, rb, c0, h_out, c_s) = refs
        streams = ((qf, ktf, vf, rf), (qb, ktb, vb, rb))
    else:
        (ktf, vf, rf, ktb, vb, rb, c_s) = refs
        streams = ((None, ktf, vf, rf), (None, ktb, vb, rb))
    s = pl.program_id(1)
    L = M_CHUNK
    dk = ktf.shape[1] // M_HEADS
    dv = vf.shape[2] // M_HEADS

    @pl.when(s == 0)
    def _():
        if with_outputs:
            c_s[...] = c0[...]
            h_out[...] = jnp.zeros(h_out.shape, F32)
        else:
            c_s[...] = jnp.zeros(c_s.shape, F32)

    row = lax.broadcasted_iota(jnp.int32, (L, L), 0)
    col = lax.broadcasted_iota(jnp.int32, (L, L), 1)
    for cc in range(cps):
        sub = (cc, cps - 1 - cc)
        blk = [slice(i * L, (i + 1) * L) for i in sub]
        chunk = (s * cps + cc, nchunks - 1 - (s * cps + cc))
        tok = [pl.ds(pl.multiple_of(c * L, L), L) for c in chunk]
        if with_outputs:
            stack = [r_ref[j, 0, :, blk[d]] for d, r_ref in enumerate((rf, rb)) for j in (1, 3)]
            stack.append(jnp.zeros((L - 4 * M_HEADS, L), F32))
            cols = jnp.transpose(jnp.concatenate(stack, axis=0))
        for d in range(2):
            q_ref, kt_ref, v_ref, row_ref = streams[d]
            mask = (col <= row) if d == 0 else (col >= row)
            for h in range(M_HEADS):
                kt = kt_ref[0, h * dk:(h + 1) * dk, blk[d]]
                v = v_ref[0, blk[d], h * dv:(h + 1) * dv]
                c_old = c_s[0, d, h]
                kw = kt.astype(F32) * row_ref[4, 0, h:h + 1, blk[d]]
                n_upd = jnp.broadcast_to(jnp.sum(kw, axis=1, keepdims=True), (dk, LANES))
                upd = jnp.concatenate([_bdot(kw.astype(BF16), v), n_upd], axis=1)
                c_s[0, d, h] = row_ref[5, 0, h:h + 1, sub[d] * L:sub[d] * L + 1] * c_old + upd
                if not with_outputs:
                    continue
                q = q_ref[0, blk[d], h * dk:(h + 1) * dk]
                cb = c_old.astype(BF16)
                j = 2 * M_HEADS * d + h
                mc = cols[:, j:j + 1]
                fl = cols[:, j + M_HEADS:j + M_HEADS + 1]
                p = jnp.exp(jnp.where(mask, row_ref[0, 0, h:h + 1, blk[d]] - mc, -jnp.inf))
                wi = jnp.exp(row_ref[2, 0, h:h + 1, blk[d]] - mc)
                sq = _bdot(q, jnp.concatenate([kt, cb[:, dv:]], axis=1))
                sp = sq[:, :L] * p
                lhs = jnp.concatenate([sp, q.astype(F32) * wi], axis=1).astype(BF16)
                num = _bdot(lhs, jnp.concatenate([v, cb[:, :dv]], axis=0))
                den = wi * sq[:, L:] + jnp.sum(sp, axis=1, keepdims=True)
                inv = 1.0 / jnp.maximum(jnp.abs(den), fl)
                h_out[0, tok[d], h * dv:(h + 1) * dv] += num * jnp.concatenate([inv] * (dv // LANES), axis=1)


def _scan(q, kt, v, gates_f, gates_b, off, init):
    bsz, hk, t = kt.shape
    hv = v.shape[2]
    dk, dv = hk // M_HEADS, hv // M_HEADS
    assert dk == M_CHUNK == LANES and dv % LANES == 0
    nc = t // M_CHUNK
    cps = max(c for c in (SCAN_CHUNKS_PER_STEP, 2, 1) if nc % c == 0 and off % c == 0)
    nsteps = nc // cps
    L = cps * M_CHUNK
    with_outputs = init is not None

    def specs(cidx):
        sp = [pl.BlockSpec((1, hk, L), lambda b, s: (b, 0, cidx(s))),
              pl.BlockSpec((1, L, hv), lambda b, s: (b, cidx(s), 0)),
              pl.BlockSpec((6, 1, M_HEADS, L), lambda b, s: (0, b, 0, off // cps + cidx(s)))]
        if with_outputs:
            sp = [pl.BlockSpec((1, L, hk), lambda b, s: (b, cidx(s), 0))] + sp
        return sp

    lead = [q] if with_outputs else []
    args = lead + [kt, v, gates_f] + lead + [kt, v, gates_b]
    in_specs = specs(lambda s: s) + specs(lambda s: nsteps - 1 - s)
    state_shape = (bsz, 2, M_HEADS, dk, dv + LANES)
    state_spec = pl.BlockSpec((1,) + state_shape[1:], lambda b, s: (b, 0, 0, 0, 0))
    if with_outputs:
        args.append(init)
        in_specs.append(state_spec)
        out_specs = pl.BlockSpec((1, t, hv), lambda b, s: (b, 0, 0))
        out_shape = jax.ShapeDtypeStruct((bsz, t, hv), F32)
        scratch = [pltpu.VMEM((1,) + state_shape[1:], F32)]
    else:
        out_specs = state_spec
        out_shape = jax.ShapeDtypeStruct(state_shape, F32)
        scratch = []
    return pl.pallas_call(
        functools.partial(_scan_kernel, nc, cps, with_outputs),
        grid=(bsz, nsteps),
        in_specs=in_specs,
        out_specs=out_specs,
        out_shape=out_shape,
        scratch_shapes=scratch,
        compiler_params=_params("parallel", "arbitrary"),
        name="mlstm_scan" if with_outputs else "mlstm_ctx_state",
    )(*args)


def _readout_core(h, sig_o, x, gate, norm_g, w_ref, post_g):
    dv = h.shape[1] // M_HEADS
    parts = []
    for hd in range(M_HEADS):
        hh = h[:, hd * dv:(hd + 1) * dv]
        parts.append(hh * lax.rsqrt(jnp.mean(hh * hh, axis=-1, keepdims=True) + EPS))
    y = (jnp.concatenate(parts, axis=-1) * norm_g * sig_o.astype(F32)).astype(BF16)
    return x + gate * _rms(_bdot(y, w_ref[...]), post_g)


def _ffn_core(fc, x, sh, sc, gate, pre_g, post_g, wu_ref, wg_ref, cw_ref, cb_ref, wd_ref, act_ref):
    tm = x.shape[0]
    f = wu_ref.shape[1]
    hb = (_rms(x, pre_g) * (1.0 + sc) + sh).astype(BF16)
    pos = lax.broadcasted_iota(jnp.int32, (tm, fc), 0) % GRID_W
    has_prev = pos != 0
    has_next = pos != GRID_W - 1
    for j in range(f // fc):
        cs = slice(j * fc, (j + 1) * fc)
        u = _bdot(hb, wu_ref[:, cs])
        g = _bdot(hb, wg_ref[:, cs])
        g_prev = jnp.where(has_prev, pltpu.roll(g, 1, 0), 0.0)
        g_next = jnp.where(has_next, pltpu.roll(g, tm - 1, 0), 0.0)
        gc = g_prev * cw_ref[0:1, cs] + g * cw_ref[1:2, cs] + g_next * cw_ref[2:3, cs] + cb_ref[:, cs]
        act_ref[:, cs] = (gc * jax.nn.sigmoid(gc) * u).astype(BF16)
    y = _bdot(act_ref[...], wd_ref[...])
    return x + gate * _rms(y, post_g)


RADIX = 4


def _fourier_channel_core(xq, sh, sc, pre_g, cs_ref, g_ref):
    gd = cs_ref.shape[0]
    hb = [(_rms(x, pre_g) * (1.0 + sc) + sh).astype(BF16) for x in xq]
    for g in range(hb[0].shape[1] // gd):
        lanes = slice(g * gd, (g + 1) * gd)
        r = [_bdot(h[:, lanes], cs_ref[...]) for h in hb]
        a = [v[:, :gd] for v in r]
        b = [v[:, gd:] for v in r]
        a02p, a02m, a13p, a13m = a[0] + a[2], a[0] - a[2], a[1] + a[3], a[1] - a[3]
        b02p, b02m, b13p, b13m = b[0] + b[2], b[0] - b[2], b[1] + b[3], b[1] - b[3]
        re_im = ((a02p + a13p, -(b02p + b13p)),
                 (a02m - b13m, -(b02m + a13m)),
                 (a02p - a13p, b13p - b02p),
                 (a02m + b13m, a13m - b02m))
        for k, (re, im) in enumerate(re_im):
            g_ref[0, k, 0, :, lanes] = re.astype(BF16)
            g_ref[0, k, 1, :, lanes] = im.astype(BF16)


def _ffn_kernel(fc, x_ref, sh_ref, sc_ref, gate_ref, pre_ref, post_ref, wu_ref, wg_ref, cw_ref, cb_ref,
                wd_ref, out_ref, act_ref):
    out_ref[0] = _ffn_core(fc, x_ref[0], sh_ref[0], sc_ref[0], gate_ref[0], pre_ref[...], post_ref[...],
                           wu_ref, wg_ref, cw_ref, cb_ref, wd_ref, act_ref)


def _layer0_tail_kernel(fc, h_ref, o_ref, x_ref, g1_ref, ng_ref, wo_ref, pm_ref,
                        sh_ref, sc_ref, gate_ref, pre_ref, post_ref, wu_ref, wg_ref, cw_ref, cb_ref, wd_ref,
                        fsh_ref, fsc_ref, fpre_ref, cs_ref, out_ref, g_ref, act_ref):
    rows = x_ref.shape[2]
    cat = lambda ref: jnp.concatenate([ref[0, q] for q in range(RADIX)], axis=0)
    x = _readout_core(cat(h_ref), cat(o_ref), cat(x_ref), g1_ref[0], ng_ref[...], wo_ref, pm_ref[...])
    x = _ffn_core(fc, x, sh_ref[0], sc_ref[0], gate_ref[0], pre_ref[...], post_ref[...],
                  wu_ref, wg_ref, cw_ref, cb_ref, wd_ref, act_ref)
    xq = [x[q * rows:(q + 1) * rows] for q in range(RADIX)]
    for q in range(RADIX):
        out_ref[0, q] = xq[q]
    _fourier_channel_core(xq, fsh_ref[0], fsc_ref[0], fpre_ref[...], cs_ref, g_ref)


def _ffn_specs(layer, d, f, conv_w):
    def lay(rows, cols, col_block=0):
        return pl.BlockSpec((None, rows, cols), lambda b, i: (layer, 0, col_block),
                            pipeline_mode=pl.Buffered(1))
    return [lay(1, d), lay(1, d), lay(d, f, 0), lay(d, f, 1), lay(conv_w, f), lay(1, f), lay(f, d)]


def _ffn(x, sh, sc, gate, layer, pre_g, post_g, w_up, cw, cb, w_down, tm, fc):
    bsz, t, d = x.shape
    f = w_down.shape[1]
    tok = pl.BlockSpec((1, tm, d), lambda b, i: (b, i, 0))
    mod = pl.BlockSpec((1, 1, d), lambda b, i: (b, 0, 0))
    return pl.pallas_call(
        functools.partial(_ffn_kernel, fc),
        grid=(bsz, t // tm),
        in_specs=[tok, mod, mod, mod] + _ffn_specs(layer, d, f, cw.shape[1]),
        out_specs=tok,
        out_shape=jax.ShapeDtypeStruct((bsz, t, d), F32),
        scratch_shapes=[pltpu.VMEM((tm, f), BF16)],
        compiler_params=_params("parallel", "parallel"),
        name="conv_ffn",
    )(x, sh, sc, gate, pre_g, post_g, w_up, w_up, cw, cb, w_down)


def _layer0_tail(h, sig_o, x, g1, norm_g, w_out, post_mix_g, ffn_mods, layer, pre_g, post_g, w_up, cw, cb, w_down,
                 fsh, fsc, fpre_g, cs, rows, fc):
    bsz, t, d = x.shape
    hv = h.shape[2]
    f = w_down.shape[1]
    tq = t // RADIX
    quarters = lambda a: a.reshape(bsz, RADIX, tq, a.shape[2])
    tok = lambda n: pl.BlockSpec((1, RADIX, rows, n), lambda b, i: (b, 0, i, 0))
    mod = pl.BlockSpec((1, 1, d), lambda b, i: (b, 0, 0))
    const = lambda a: pl.BlockSpec(a.shape, lambda b, i: (0,) * a.ndim, pipeline_mode=pl.Buffered(1))
    out, g = pl.pallas_call(
        functools.partial(_layer0_tail_kernel, fc),
        grid=(bsz, tq // rows),
        in_specs=([tok(hv), tok(hv), tok(d), mod, const(norm_g), const(w_out), const(post_mix_g), mod, mod, mod]
                  + _ffn_specs(layer, d, f, cw.shape[1]) + [mod, mod, const(fpre_g), const(cs)]),
        out_specs=[tok(d), pl.BlockSpec((1, RADIX, 2, rows, d), lambda b, i: (b, 0, 0, i, 0))],
        out_shape=[jax.ShapeDtypeStruct((bsz, RADIX, tq, d), F32),
                   jax.ShapeDtypeStruct((bsz, RADIX, 2, tq, d), BF16)],
        scratch_shapes=[pltpu.VMEM((RADIX * rows, f), BF16)],
        compiler_params=_params("parallel", "parallel"),
        name="layer0_tail",
    )(quarters(h), quarters(sig_o), quarters(x), g1, norm_g, w_out, post_mix_g, *ffn_mods,
      pre_g, post_g, w_up, w_up, cw, cb, w_down, fsh, fsc, fpre_g, cs)
    return out.reshape(bsz, t, d), g


def _fourier_token_kernel(scale, dft_ref, g_ref, w_ref, bias_ref, post_ref, gate_ref, x_ref, out_ref,
                          x_s, out_s):
    tm = dft_ref.shape[1]
    nblk = x_s.shape[0]
    for c in range(nblk):
        x_s[c] = x_ref[0, :, c * LANES:(c + 1) * LANES]
    for r in range(RADIX):
        rows = pl.ds(r, tm, stride=RADIX)
        y = _bdot(dft_ref[r], g_ref[0, r]) * scale
        z = _bdot(y.astype(BF16), w_ref[...]) + bias_ref[...]
        xr = jnp.concatenate([x_s[c, rows, :] for c in range(nblk)], axis=1)
        res = xr + gate_ref[0] * _rms(z, post_ref[...])
        for c in range(nblk):
            out_s[c, rows, :] = res[:, c * LANES:(c + 1) * LANES]
    for c in range(nblk):
        out_ref[0, :, c * LANES:(c + 1) * LANES] = out_s[c]


def _fourier_token(dft, g, w, bias, post_g, gate, x, scale, tm):
    bsz, t, d = x.shape
    tq = t // RADIX
    full = lambda a: pl.BlockSpec(a.shape, lambda b, i: (0, 0))
    tok = pl.BlockSpec((1, RADIX * tm, d), lambda b, i: (b, i, 0))
    return pl.pallas_call(
        functools.partial(_fourier_token_kernel, scale),
        grid=(bsz, tq // tm),
        in_specs=[pl.BlockSpec((RADIX, tm, 2 * tq), lambda b, i: (0, i, 0)),
                  pl.BlockSpec((1, RADIX, 2 * tq, d), lambda b, i: (b, 0, 0, 0)),
                  full(w), full(bias), full(post_g), pl.BlockSpec((1, 1, d), lambda b, i: (b, 0, 0)), tok],
        out_specs=tok,
        out_shape=jax.ShapeDtypeStruct((bsz, t, d), F32),
        scratch_shapes=[pltpu.VMEM((d // LANES, RADIX * tm, LANES), F32)] * 2,
        compiler_params=_params("parallel", "arbitrary"),
        name="fourier_token_dft",
    )(dft, g.reshape(bsz, RADIX, 2 * tq, d), w, bias, post_g, gate, x)


def _dft_tables(t, gd):
    idx = np.arange(gd, dtype=np.int64)
    ang = 2.0 * np.pi * ((idx[:, None] * idx[None, :]) % gd).astype(np.float64) / gd
    chan = np.concatenate([np.cos(ang), np.sin(ang)], axis=1).astype(np.float32)
    tq = t // RADIX
    k = RADIX * np.arange(tq, dtype=np.int64)[None, :, None] + np.arange(RADIX, dtype=np.int64)[:, None, None]
    ang = 2.0 * np.pi * ((k * np.arange(tq, dtype=np.int64)[None, None, :]) % t).astype(np.float64) / t
    tok = np.concatenate([np.cos(ang), np.sin(ang)], axis=2).astype(np.float32)
    return jnp.asarray(chan).astype(BF16), jnp.asarray(tok).astype(BF16)


def _row(v):
    return v.reshape(1, -1)


def kernel(x, c, ctx, c_ctx, ada_w, ada_b, pre_mix_g, post_mix_g, pre_ffn_g, post_ffn_g, ffn_up_w, ffn_conv_w,
           ffn_conv_b, ffn_down_w, m_in_w, m_in_b, m_norm_g, m_out_w, f_out_w, f_out_b):
    bsz, t, d = x.shape
    tc = ctx.shape[1]
    f = ffn_down_w.shape[1]
    assert ada_w.shape[0] == 2 and m_in_w.shape[0] == 1 and f_out_w.shape[0] == 1
    assert bsz + 1 <= COND_ROWS and t % M_CHUNK == 0 and tc % M_CHUNK == 0 and t % GRID_W == 0
    hv = m_out_w.shape[1]
    hk = (m_in_w.shape[2] - 2 * hv - 4 * M_HEADS) // 2
    tm = min(512, t)
    fc = 256
    assert f % fc == 0 and t % tm == 0 and tm % GRID_W == 0

    cond = jnp.zeros((COND_ROWS, d), F32).at[:bsz].set(c).at[bsz].set(c_ctx)
    mod, w_in = _ada(cond, ada_w, ada_b, m_in_w[0])
    lat = mod[:, :bsz].reshape(2, bsz, N_MOD, 1, d)
    cmod = mod[0, bsz].reshape(N_MOD, 1, 1, d)

    ffn_rows = [a.reshape(a.shape[0], 1, a.shape[1]) for a in (pre_ffn_g, post_ffn_g, ffn_conv_b)]

    b_in = _row(jnp.pad(m_in_b[0], (0, w_in.shape[1] - m_in_b.shape[1])))
    pre0 = _row(pre_mix_g[0])

    _, kc, vc, _, gc = _inproj(ctx, cmod[0], cmod[1], pre0, w_in, b_in, hk, hv, min(tm, tc))
    qx, kx, vx, ox, gx, w_up, w_down, w_mo, w_fo = _inproj(
        x, lat[0, :, 0], lat[0, :, 1], pre0, w_in, b_in, hk, hv, tm, convert=(ffn_up_w, ffn_down_w, m_out_w, f_out_w))
    gates_f, gates_b = _gate_prep(gc, gx)
    state = _scan(None, kc, vc, gates_f, gates_b, t // M_CHUNK, None)
    h = _scan(qx, kx, vx, gates_f, gates_b, 0, state)
    gd = d // F_GROUPS
    tq = t // RADIX
    rows, tm_tok = min(128, tq), min(256, tq)
    assert t % RADIX == 0 and tq % rows == 0 and tq % tm_tok == 0 and rows % GRID_W == 0
    chan, tok = _dft_tables(t, gd)
    x, g = _layer0_tail(h, ox, x, lat[0, :, 2], _row(m_norm_g[0]), w_mo[0], _row(post_mix_g[0]),
                        (lat[0, :, 3], lat[0, :, 4], lat[0, :, 5]), 0, ffn_rows[0], ffn_rows[1], w_up, ffn_conv_w,
                        ffn_rows[2], w_down, lat[1, :, 0], lat[1, :, 1], _row(pre_mix_g[1]), chan, rows, fc)

    x = _fourier_token(tok, g, w_fo[0], _row(f_out_b[0]),
                       _row(post_mix_g[1]), lat[1, :, 2], x, float(1.0 / np.sqrt(t * gd)), tm_tok)
    return _ffn(x, lat[1, :, 3], lat[1, :, 4], lat[1, :, 5], 1, ffn_rows[0], ffn_rows[1], w_up, ffn_conv_w,
                ffn_rows[2], w_down, tm, fc)
```

```python
import functools

import numpy as np
import jax
import jax.numpy as jnp
from jax import lax
from jax.experimental import pallas as pl
from jax.experimental.pallas import tpu as pltpu

F32 = jnp.float32
BF16 = jnp.bfloat16

M_HEADS = 4
M_CHUNK = 128
F_GROUPS = 8
GRID_W = 64
EPS = 1e-6
N_MOD = 6
COND_ROWS = 16
SCAN_CHUNKS_PER_STEP = 4

VMEM_LIMIT_BYTES = 56 * 1024 * 1024
LANES = 128


def _params(*sem):
    return pltpu.CompilerParams(dimension_semantics=sem, vmem_limit_bytes=VMEM_LIMIT_BYTES)


def _rms(x, g):
    return x * lax.rsqrt(jnp.mean(x * x, axis=-1, keepdims=True) + EPS) * g


def _bdot(a, b):
    return jnp.dot(a, b, preferred_element_type=F32)


def _ada_kernel(valid_rows, c_ref, w_ref, b_ref, win_ref, o_ref, win_out):
    c = c_ref[...]
    s = c * jax.nn.sigmoid(c)
    o_ref[0] = _bdot(s.astype(BF16), w_ref[0].astype(BF16)) + b_ref[0]
    step = pl.program_id(0) * pl.num_programs(1) + pl.program_id(1)
    row = step * win_ref.shape[0] + lax.broadcasted_iota(jnp.int32, win_ref.shape, 0)
    win_out[...] = jnp.where(row < valid_rows, win_ref[...], 0.0).astype(BF16)


def _ada(cond, ada_w, ada_b, w_in_t):
    depth, d, n = ada_w.shape
    tn = n // 4
    steps = depth * (n // tn)
    rows, cols = w_in_t.shape
    padded = rows + (-rows) % LANES
    blk = padded // steps
    assert padded % (16 * steps) == 0 and (steps - 1) * blk < rows
    return pl.pallas_call(
        functools.partial(_ada_kernel, rows),
        grid=(depth, n // tn),
        in_specs=[pl.BlockSpec((COND_ROWS, d), lambda i, j: (0, 0)),
                  pl.BlockSpec((1, d, tn), lambda i, j: (i, 0, j)),
                  pl.BlockSpec((1, 1, tn), lambda i, j: (i, 0, j)),
                  pl.BlockSpec((blk, cols), lambda i, j: (i * (n // tn) + j, 0))],
        out_specs=[pl.BlockSpec((1, COND_ROWS, tn), lambda i, j: (i, 0, j)),
                   pl.BlockSpec((blk, cols), lambda i, j: (i * (n // tn) + j, 0))],
        out_shape=[jax.ShapeDtypeStruct((depth, COND_ROWS, n), F32),
                   jax.ShapeDtypeStruct((padded, cols), BF16)],
        compiler_params=_params("arbitrary", "arbitrary"),
        name="ada",
    )(cond, ada_w, ada_b.reshape(depth, 1, n), w_in_t)


def _inproj_kernel(nconv, with_qo, x_ref, sh_ref, sc_ref, g_ref, w_ref, b_ref, *refs):
    nout = 5 if with_qo else 3
    conv_in, outs, conv_out = refs[:nconv], refs[nconv:nconv + nout], refs[nconv + nout:]
    kt_out, v_out, gate_out = outs[-3:]
    hk, hv = kt_out.shape[1], v_out.shape[2]
    dk = hk // M_HEADS
    h = _rms(x_ref[0], g_ref[...]) * (1.0 + sc_ref[0]) + sh_ref[0]
    hb = h.astype(BF16)

    def proj(lo, hi):
        return lax.dot_general(hb, w_ref[lo:hi, :], (((1,), (1,)), ((), ())),
                               preferred_element_type=F32) + b_ref[:, lo:hi]

    if with_qo:
        q_out, o_out = outs[:2]
        q_out[0] = proj(0, hk).astype(BF16)
        o_out[0] = jax.nn.sigmoid(proj(2 * hk + hv, 2 * hk + 2 * hv)).astype(BF16)
    kt_out[0] = jnp.transpose(proj(hk, 2 * hk) * (dk ** -0.5)).astype(BF16)
    v_out[0] = proj(2 * hk, 2 * hk + hv).astype(BF16)
    gate_out[0] = jnp.transpose(proj(2 * hk + 2 * hv, w_ref.shape[0]))[:gate_out.shape[1]]
    for src_ref, dst_ref in zip(conv_in, conv_out):
        dst_ref[...] = src_ref[...].astype(BF16)


def _inproj(x, sh, sc, g, w, bias, hk, hv, tm, with_qo, convert=()):
    bsz, t, d = x.shape
    nt = t // tm
    per_batch = sh.shape[0] > 1
    mod_spec = pl.BlockSpec((1, 1, d), (lambda b, i: (b, 0, 0)) if per_batch else (lambda b, i: (0, 0, 0)))
    full = lambda a: pl.BlockSpec(a.shape, lambda b, i: (0, 0))
    tok = lambda n: pl.BlockSpec((1, tm, n), lambda b, i: (b, i, 0))
    tok_t = lambda n: pl.BlockSpec((1, n, tm), lambda b, i: (b, 0, i))
    flat = [a.reshape(-1, a.shape[-1]) for a in convert]
    conv_specs = [pl.BlockSpec((a.shape[0] // (bsz * nt), a.shape[1]), lambda b, i: (b * nt + i, 0)) for a in flat]
    assert all(a.shape[0] % (16 * bsz * nt) == 0 for a in flat)
    qo_specs = [tok(hk), tok(hv)] if with_qo else []
    qo_shapes = [jax.ShapeDtypeStruct((bsz, t, hk), BF16), jax.ShapeDtypeStruct((bsz, t, hv), BF16)] if with_qo else []
    outs = pl.pallas_call(
        functools.partial(_inproj_kernel, len(flat), with_qo),
        grid=(bsz, nt),
        in_specs=[tok(d), mod_spec, mod_spec, full(g), full(w), full(bias)] + conv_specs,
        out_specs=qo_specs + [tok_t(hk), tok(hv), tok_t(4 * M_HEADS)] + conv_specs,
        out_shape=qo_shapes + [jax.ShapeDtypeStruct((bsz, hk, t), BF16), jax.ShapeDtypeStruct((bsz, t, hv), BF16),
                               jax.ShapeDtypeStruct((bsz, 4 * M_HEADS, t), F32)]
                  + [jax.ShapeDtypeStruct(a.shape, BF16) for a in flat],
        compiler_params=_params("parallel", "parallel"),
        name="mlstm_inproj",
    )(x, sh, sc, g, w, bias, *flat)
    nout = len(outs) - len(flat)
    return list(outs[:nout]) + [o.reshape(a.shape) for o, a in zip(outs[nout:], convert)]


def _lane_scan(x, op, fill, reverse, seg):
    n = x.shape[-1]
    lane = lax.broadcasted_iota(jnp.int32, x.shape, x.ndim - 1) % seg
    d = 1
    while d < seg:
        if reverse:
            shifted = jnp.where(lane < seg - d, pltpu.roll(x, n - d, x.ndim - 1), fill)
        else:
            shifted = jnp.where(lane >= d, pltpu.roll(x, d, x.ndim - 1), fill)
        x = op(x, shifted)
        d *= 2
    return x


def _log_sigmoid(x):
    return jnp.minimum(x, 0.0) - jnp.log1p(jnp.exp(-jnp.abs(x)))


def _gate_kernel(orders, gf_ref, gb_ref, of_ref, ob_ref):
    L = M_CHUNK
    for d, (g_ref, o_ref) in enumerate(((gf_ref, of_ref), (gb_ref, ob_ref))):
        edge = L - 1 if d == 0 else 0
        lf = _log_sigmoid(g_ref[:, 1])
        b_all = _lane_scan(lf, jnp.add, 0.0, d == 1, L)
        r1_all = g_ref[:, 0] - b_all
        cm_all = _lane_scan(r1_all, jnp.maximum, -jnp.inf, d == 1, L)
        m = jnp.zeros(lf.shape[:-1] + (1,), F32)
        for c in orders[d]:
            lanes = slice(c * L, (c + 1) * L)
            b, r1, cm = b_all[..., lanes], r1_all[..., lanes], cm_all[..., lanes]
            mx = jnp.maximum(m, cm[..., edge:edge + 1])
            mc = jnp.maximum(m, cm)
            o_ref[0, :, :, lanes] = r1
            o_ref[1, :, :, lanes] = mc
            o_ref[2, :, :, lanes] = jnp.broadcast_to(m, r1.shape)
            o_ref[3, :, :, lanes] = jnp.exp(-(b + mc))
            o_ref[4, :, :, lanes] = jnp.exp(r1 - mx)
            o_ref[5, :, :, lanes] = jnp.broadcast_to(jnp.exp(m - mx), r1.shape)
            m = b[..., edge:edge + 1] + mx


def _gate_prep(gates_c, gates_x):
    bsz = gates_c.shape[0]
    nlt, nct = gates_x.shape[2] // M_CHUNK, gates_c.shape[2] // M_CHUNK
    n = (nlt + nct) * M_CHUNK
    fwd = tuple(range(nlt, nlt + nct)) + tuple(range(nlt))
    bwd = tuple(range(nlt + nct - 1, nlt - 1, -1)) + tuple(range(nlt - 1, -1, -1))
    gf = gb = jnp.concatenate([gates_x, gates_c], axis=2).reshape(bsz, 4, M_HEADS, n)
    out_spec = pl.BlockSpec((6, bsz, M_HEADS, n), lambda j: (0, 0, 0, 0))
    return pl.pallas_call(
        functools.partial(_gate_kernel, (fwd, bwd)),
        grid=(1,),
        in_specs=[pl.BlockSpec((bsz, 2, M_HEADS, n), lambda j: (0, 0, 0, 0)),
                  pl.BlockSpec((bsz, 2, M_HEADS, n), lambda j: (0, 1, 0, 0))],
        out_specs=[out_spec, out_spec],
        out_shape=[jax.ShapeDtypeStruct((6, bsz, M_HEADS, n), F32)] * 2,
        compiler_params=_params("arbitrary"),
        name="mlstm_gates",
    )(gf, gb)


def _scan_kernel(nchunks, cps, with_outputs, *refs):
    if with_outputs:
        (qf, ktf, vf, rf, qb, ktb, vb, rb, c0, h_out, c_s) = refs
        streams = ((qf, ktf, vf, rf), (qb, ktb, vb, rb))
    else:
        (ktf, vf, rf, ktb, vb, rb, c_s) = refs
        streams = ((None, ktf, vf, rf), (None, ktb, vb, rb))
    s = pl.program_id(1)
    L = M_CHUNK
    dk = ktf.shape[1] // M_HEADS
    dv = vf.shape[2] // M_HEADS

    @pl.when(s == 0)
    def _():
        if with_outputs:
            c_s[...] = c0[...]
            h_out[...] = jnp.zeros(h_out.shape, F32)
        else:
            c_s[...] = jnp.zeros(c_s.shape, F32)

    row = lax.broadcasted_iota(jnp.int32, (L, L), 0)
    col = lax.broadcasted_iota(jnp.int32, (L, L), 1)
    for cc in range(cps):
        sub = (cc, cps - 1 - cc)
        blk = [slice(i * L, (i + 1) * L) for i in sub]
        chunk = (s * cps + cc, nchunks - 1 - (s * cps + cc))
        tok = [pl.ds(pl.multiple_of(c * L, L), L) for c in chunk]
        if with_outputs:
            stack = [r_ref[j, 0, :, blk[d]] for d, r_ref in enumerate((rf, rb)) for j in (1, 3)]
            stack.append(jnp.zeros((L - 4 * M_HEADS, L), F32))
            cols = jnp.transpose(jnp.concatenate(stack, axis=0))
        for d in range(2):
            q_ref, kt_ref, v_ref, row_ref = streams[d]
            mask = (col <= row) if d == 0 else (col >= row)
            for h in range(M_HEADS):
                kt = kt_ref[0, h * dk:(h + 1) * dk, blk[d]]
                v = v_ref[0, blk[d], h * dv:(h + 1) * dv]
                c_old = c_s[0, d, h]
                kw = kt.astype(F32) * row_ref[4, 0, h:h + 1, blk[d]]
                n_upd = jnp.broadcast_to(jnp.sum(kw, axis=1, keepdims=True), (dk, LANES))
                upd = jnp.concatenate([_bdot(kw.astype(BF16), v), n_upd], axis=1)
                c_s[0, d, h] = row_ref[5, 0, h:h + 1, sub[d] * L:sub[d] * L + 1] * c_old + upd
                if not with_outputs:
                    continue
                q = q_ref[0, blk[d], h * dk:(h + 1) * dk]
                cb = c_old.astype(BF16)
                j = 2 * M_HEADS * d + h
                mc = cols[:, j:j + 1]
                fl = cols[:, j + M_HEADS:j + M_HEADS + 1]
                p = jnp.exp(jnp.where(mask, row_ref[0, 0, h:h + 1, blk[d]] - mc, -jnp.inf))
                wi = jnp.exp(row_ref[2, 0, h:h + 1, blk[d]] - mc)
                sq = _bdot(q, jnp.concatenate([kt, cb[:, dv:]], axis=1))
                sp = sq[:, :L] * p
                lhs = jnp.concatenate([sp, q.astype(F32) * wi], axis=1).astype(BF16)
                num = _bdot(lhs, jnp.concatenate([v, cb[:, :dv]], axis=0))
                den = wi * sq[:, L:] + jnp.sum(sp, axis=1, keepdims=True)
                inv = 1.0 / jnp.maximum(jnp.abs(den), fl)
                h_out[0, tok[d], h * dv:(h + 1) * dv] += num * jnp.concatenate([inv] * (dv // LANES), axis=1)


def _scan(q, kt, v, gates_f, gates_b, off, init):
    bsz, hk, t = kt.shape
    hv = v.shape[2]
    dk, dv = hk // M_HEADS, hv // M_HEADS
    assert dk == M_CHUNK == LANES and dv % LANES == 0
    nc = t // M_CHUNK
    cps = max(c for c in (SCAN_CHUNKS_PER_STEP, 2, 1) if nc % c == 0 and off % c == 0)
    nsteps = nc // cps
    L = cps * M_CHUNK
    with_outputs = init is not None

    def specs(cidx):
        sp = [pl.BlockSpec((1, hk, L), lambda b, s: (b, 0, cidx(s))),
              pl.BlockSpec((1, L, hv), lambda b, s: (b, cidx(s), 0)),
              pl.BlockSpec((6, 1, M_HEADS, L), lambda b, s: (0, b, 0, off // cps + cidx(s)))]
        if with_outputs:
            sp = [pl.BlockSpec((1, L, hk), lambda b, s: (b, cidx(s), 0))] + sp
        return sp

    lead = [q] if with_outputs else []
    args = lead + [kt, v, gates_f] + lead + [kt, v, gates_b]
    in_specs = specs(lambda s: s) + specs(lambda s: nsteps - 1 - s)
    state_shape = (bsz, 2, M_HEADS, dk, dv + LANES)
    state_spec = pl.BlockSpec((1,) + state_shape[1:], lambda b, s: (b, 0, 0, 0, 0))
    if with_outputs:
        args.append(init)
        in_specs.append(state_spec)
        out_specs = pl.BlockSpec((1, t, hv), lambda b, s: (b, 0, 0))
        out_shape = jax.ShapeDtypeStruct((bsz, t, hv), F32)
        scratch = [pltpu.VMEM((1,) + state_shape[1:], F32)]
    else:
        out_specs = state_spec
        out_shape = jax.ShapeDtypeStruct(state_shape, F32)
        scratch = []
    return pl.pallas_call(
        functools.partial(_scan_kernel, nc, cps, with_outputs),
        grid=(bsz, nsteps),
        in_specs=in_specs,
        out_specs=out_specs,
        out_shape=out_shape,
        scratch_shapes=scratch,
        compiler_params=_params("parallel", "arbitrary"),
        name="mlstm_scan" if with_outputs else "mlstm_ctx_state",
    )(*args)


def _readout_core(h, sig_o, x, gate, norm_g, w_ref, post_g):
    dv = h.shape[1] // M_HEADS
    parts = []
    for hd in range(M_HEADS):
        hh = h[:, hd * dv:(hd + 1) * dv]
        parts.append(hh * lax.rsqrt(jnp.mean(hh * hh, axis=-1, keepdims=True) + EPS))
    y = (jnp.concatenate(parts, axis=-1) * norm_g * sig_o.astype(F32)).astype(BF16)
    return x + gate * _rms(_bdot(y, w_ref[...]), post_g)


def _ffn_core(fc, x, sh, sc, gate, pre_g, post_g, wu_ref, wg_ref, cw_ref, cb_ref, wd_ref, act_ref):
    tm = x.shape[0]
    f = wu_ref.shape[1]
    hb = (_rms(x, pre_g) * (1.0 + sc) + sh).astype(BF16)
    pos = lax.broadcasted_iota(jnp.int32, (tm, fc), 0) % GRID_W
    has_prev = pos != 0
    has_next = pos != GRID_W - 1
    for j in range(f // fc):
        cs = slice(j * fc, (j + 1) * fc)
        u = _bdot(hb, wu_ref[:, cs])
        g = _bdot(hb, wg_ref[:, cs])
        g_prev = jnp.where(has_prev, pltpu.roll(g, 1, 0), 0.0)
        g_next = jnp.where(has_next, pltpu.roll(g, tm - 1, 0), 0.0)
        gc = g_prev * cw_ref[0:1, cs] + g * cw_ref[1:2, cs] + g_next * cw_ref[2:3, cs] + cb_ref[:, cs]
        act_ref[:, cs] = (gc * jax.nn.sigmoid(gc) * u).astype(BF16)
    y = _bdot(act_ref[...], wd_ref[...])
    return x + gate * _rms(y, post_g)


RADIX = 4


def _fourier_channel_core(xq, sh, sc, pre_g, cs_ref, g_ref):
    gd = cs_ref.shape[0]
    hb = [(_rms(x, pre_g) * (1.0 + sc) + sh).astype(BF16) for x in xq]
    for g in range(hb[0].shape[1] // gd):
        lanes = slice(g * gd, (g + 1) * gd)
        r = [_bdot(h[:, lanes], cs_ref[...]) for h in hb]
        a = [v[:, :gd] for v in r]
        b = [v[:, gd:] for v in r]
        a02p, a02m, a13p, a13m = a[0] + a[2], a[0] - a[2], a[1] + a[3], a[1] - a[3]
        b02p, b02m, b13p, b13m = b[0] + b[2], b[0] - b[2], b[1] + b[3], b[1] - b[3]
        re_im = ((a02p + a13p, -(b02p + b13p)),
                 (a02m - b13m, -(b02m + a13m)),
                 (a02p - a13p, b13p - b02p),
                 (a02m + b13m, a13m - b02m))
        for k, (re, im) in enumerate(re_im):
            g_ref[0, k, 0, :, lanes] = re.astype(BF16)
            g_ref[0, k, 1, :, lanes] = im.astype(BF16)


def _ffn_kernel(fc, x_ref, sh_ref, sc_ref, gate_ref, pre_ref, post_ref, wu_ref, wg_ref, cw_ref, cb_ref,
                wd_ref, out_ref, act_ref):
    out_ref[0] = _ffn_core(fc, x_ref[0], sh_ref[0], sc_ref[0], gate_ref[0], pre_ref[...], post_ref[...],
                           wu_ref, wg_ref, cw_ref, cb_ref, wd_ref, act_ref)


def _layer0_tail_kernel(fc, h_ref, o_ref, x_ref, g1_ref, ng_ref, wo_ref, pm_ref,
                        sh_ref, sc_ref, gate_ref, pre_ref, post_ref, wu_ref, wg_ref, cw_ref, cb_ref, wd_ref,
                        fsh_ref, fsc_ref, fpre_ref, cs_ref, out_ref, g_ref, act_ref):
    rows = x_ref.shape[2]
    cat = lambda ref: jnp.concatenate([ref[0, q] for q in range(RADIX)], axis=0)
    x = _readout_core(cat(h_ref), cat(o_ref), cat(x_ref), g1_ref[0], ng_ref[...], wo_ref, pm_ref[...])
    x = _ffn_core(fc, x, sh_ref[0], sc_ref[0], gate_ref[0], pre_ref[...], post_ref[...],
                  wu_ref, wg_ref, cw_ref, cb_ref, wd_ref, act_ref)
    xq = [x[q * rows:(q + 1) * rows] for q in range(RADIX)]
    for q in range(RADIX):
        out_ref[0, q] = xq[q]
    _fourier_channel_core(xq, fsh_ref[0], fsc_ref[0], fpre_ref[...], cs_ref, g_ref)


def _ffn_specs(layer, d, f, conv_w):
    def lay(rows, cols, col_block=0):
        return pl.BlockSpec((None, rows, cols), lambda b, i: (layer, 0, col_block),
                            pipeline_mode=pl.Buffered(1))
    return [lay(1, d), lay(1, d), lay(d, f, 0), lay(d, f, 1), lay(conv_w, f), lay(1, f), lay(f, d)]


def _ffn(x, sh, sc, gate, layer, pre_g, post_g, w_up, cw, cb, w_down, tm, fc):
    bsz, t, d = x.shape
    f = w_down.shape[1]
    tok = pl.BlockSpec((1, tm, d), lambda b, i: (b, i, 0))
    mod = pl.BlockSpec((1, 1, d), lambda b, i: (b, 0, 0))
    return pl.pallas_call(
        functools.partial(_ffn_kernel, fc),
        grid=(bsz, t // tm),
        in_specs=[tok, mod, mod, mod] + _ffn_specs(layer, d, f, cw.shape[1]),
        out_specs=tok,
        out_shape=jax.ShapeDtypeStruct((bsz, t, d), F32),
        scratch_shapes=[pltpu.VMEM((tm, f), BF16)],
        compiler_params=_params("parallel", "parallel"),
        name="conv_ffn",
    )(x, sh, sc, gate, pre_g, post_g, w_up, w_up, cw, cb, w_down)


def _layer0_tail(h, sig_o, x, g1, norm_g, w_out, post_mix_g, ffn_mods, layer, pre_g, post_g, w_up, cw, cb, w_down,
                 fsh, fsc, fpre_g, cs, rows, fc):
    bsz, t, d = x.shape
    hv = h.shape[2]
    f = w_down.shape[1]
    tq = t // RADIX
    quarters = lambda a: a.reshape(bsz, RADIX, tq, a.shape[2])
    tok = lambda n: pl.BlockSpec((1, RADIX, rows, n), lambda b, i: (b, 0, i, 0))
    mod = pl.BlockSpec((1, 1, d), lambda b, i: (b, 0, 0))
    const = lambda a: pl.BlockSpec(a.shape, lambda b, i: (0,) * a.ndim, pipeline_mode=pl.Buffered(1))
    out, g = pl.pallas_call(
        functools.partial(_layer0_tail_kernel, fc),
        grid=(bsz, tq // rows),
        in_specs=([tok(hv), tok(hv), tok(d), mod, const(norm_g), const(w_out), const(post_mix_g), mod, mod, mod]
                  + _ffn_specs(layer, d, f, cw.shape[1]) + [mod, mod, const(fpre_g), const(cs)]),
        out_specs=[tok(d), pl.BlockSpec((1, RADIX, 2, rows, d), lambda b, i: (b, 0, 0, i, 0))],
        out_shape=[jax.ShapeDtypeStruct((bsz, RADIX, tq, d), F32),
                   jax.ShapeDtypeStruct((bsz, RADIX, 2, tq, d), BF16)],
        scratch_shapes=[pltpu.VMEM((RADIX * rows, f), BF16)],
        compiler_params=_params("parallel", "parallel"),
        name="layer0_tail",
    )(quarters(h), quarters(sig_o), quarters(x), g1, norm_g, w_out, post_mix_g, *ffn_mods,
      pre_g, post_g, w_up, w_up, cw, cb, w_down, fsh, fsc, fpre_g, cs)
    return out.reshape(bsz, t, d), g


def _fourier_token_kernel(scale, dft_ref, g_ref, w_ref, bias_ref, post_ref, gate_ref, x_ref, out_ref,
                          x_s, out_s):
    tm = dft_ref.shape[1]
    nblk = x_s.shape[0]
    for c in range(nblk):
        x_s[c] = x_ref[0, :, c * LANES:(c + 1) * LANES]
    for r in range(RADIX):
        rows = pl.ds(r, tm, stride=RADIX)
        y = _bdot(dft_ref[r], g_ref[0, r]) * scale
        z = _bdot(y.astype(BF16), w_ref[...]) + bias_ref[...]
        xr = jnp.concatenate([x_s[c, rows, :] for c in range(nblk)], axis=1)
        res = xr + gate_ref[0] * _rms(z, post_ref[...])
        for c in range(nblk):
            out_s[c, rows, :] = res[:, c * LANES:(c + 1) * LANES]
    for c in range(nblk):
        out_ref[0, :, c * LANES:(c + 1) * LANES] = out_s[c]


def _fourier_token(dft, g, w, bias, post_g, gate, x, scale, tm):
    bsz, t, d = x.shape
    tq = t // RADIX
    full = lambda a: pl.BlockSpec(a.shape, lambda b, i: (0, 0))
    tok = pl.BlockSpec((1, RADIX * tm, d), lambda b, i: (b, i, 0))
    return pl.pallas_call(
        functools.partial(_fourier_token_kernel, scale),
        grid=(bsz, tq // tm),
        in_specs=[pl.BlockSpec((RADIX, tm, 2 * tq), lambda b, i: (0, i, 0)),
                  pl.BlockSpec((1, RADIX, 2 * tq, d), lambda b, i: (b, 0, 0, 0)),
                  full(w), full(bias), full(post_g), pl.BlockSpec((1, 1, d), lambda b, i: (b, 0, 0)), tok],
        out_specs=tok,
        out_shape=jax.ShapeDtypeStruct((bsz, t, d), F32),
        scratch_shapes=[pltpu.VMEM((d // LANES, RADIX * tm, LANES), F32)] * 2,
        compiler_params=_params("parallel", "arbitrary"),
        name="fourier_token_dft",
    )(dft, g.reshape(bsz, RADIX, 2 * tq, d), w, bias, post_g, gate, x)


def _dft_tables(t, gd):
    idx = np.arange(gd, dtype=np.int64)
    ang = 2.0 * np.pi * ((idx[:, None] * idx[None, :]) % gd).astype(np.float64) / gd
    chan = np.concatenate([np.cos(ang), np.sin(ang)], axis=1).astype(np.float32)
    tq = t // RADIX
    k = RADIX * np.arange(tq, dtype=np.int64)[None, :, None] + np.arange(RADIX, dtype=np.int64)[:, None, None]
    ang = 2.0 * np.pi * ((k * np.arange(tq, dtype=np.int64)[None, None, :]) % t).astype(np.float64) / t
    tok = np.concatenate([np.cos(ang), np.sin(ang)], axis=2).astype(np.float32)
    return jnp.asarray(chan).astype(BF16), jnp.asarray(tok).astype(BF16)


def _row(v):
    return v.reshape(1, -1)


def kernel(x, c, ctx, c_ctx, ada_w, ada_b, pre_mix_g, post_mix_g, pre_ffn_g, post_ffn_g, ffn_up_w, ffn_conv_w,
           ffn_conv_b, ffn_down_w, m_in_w, m_in_b, m_norm_g, m_out_w, f_out_w, f_out_b):
    bsz, t, d = x.shape
    tc = ctx.shape[1]
    f = ffn_down_w.shape[1]
    assert ada_w.shape[0] == 2 and m_in_w.shape[0] == 1 and f_out_w.shape[0] == 1
    assert bsz + 1 <= COND_ROWS and t % M_CHUNK == 0 and tc % M_CHUNK == 0 and t % GRID_W == 0
    hv = m_out_w.shape[1]
    hk = (m_in_w.shape[2] - 2 * hv - 4 * M_HEADS) // 2
    tm = min(512, t)
    fc = 256
    assert f % fc == 0 and t % tm == 0 and tm % GRID_W == 0

    cond = jnp.zeros((COND_ROWS, d), F32).at[:bsz].set(c).at[bsz].set(c_ctx)
    mod, w_in = _ada(cond, ada_w, ada_b, jnp.swapaxes(m_in_w[0], 0, 1))
    lat = mod[:, :bsz].reshape(2, bsz, N_MOD, 1, d)
    cmod = mod[0, bsz].reshape(N_MOD, 1, 1, d)

    ffn_rows = [a.reshape(a.shape[0], 1, a.shape[1]) for a in (pre_ffn_g, post_ffn_g, ffn_conv_b)]

    b_in = _row(jnp.pad(m_in_b[0], (0, w_in.shape[0] - m_in_b.shape[1])))
    pre0 = _row(pre_mix_g[0])

    kc, vc, gc = _inproj(ctx, cmod[0], cmod[1], pre0, w_in, b_in, hk, hv, min(tm, tc), False)
    qx, ox, kx, vx, gx, w_up, w_down, w_mo, w_fo = _inproj(
        x, lat[0, :, 0], lat[0, :, 1], pre0, w_in, b_in, hk, hv, tm, True,
        convert=(ffn_up_w, ffn_down_w, m_out_w, f_out_w))
    gates_f, gates_b = _gate_prep(gc, gx)
    state = _scan(None, kc, vc, gates_f, gates_b, t // M_CHUNK, None)
    h = _scan(qx, kx, vx, gates_f, gates_b, 0, state)
    gd = d // F_GROUPS
    tq = t // RADIX
    rows, tm_tok = min(128, tq), min(256, tq)
    assert t % RADIX == 0 and tq % rows == 0 and tq % tm_tok == 0 and rows % GRID_W == 0
    chan, tok = _dft_tables(t, gd)
    x, g = _layer0_tail(h, ox, x, lat[0, :, 2], _row(m_norm_g[0]), w_mo[0], _row(post_mix_g[0]),
                        (lat[0, :, 3], lat[0, :, 4], lat[0, :, 5]), 0, ffn_rows[0], ffn_rows[1], w_up, ffn_conv_w,
                        ffn_rows[2], w_down, lat[1, :, 0], lat[1, :, 1], _row(pre_mix_g[1]), chan, rows, fc)

    x = _fourier_token(tok, g, w_fo[0], _row(f_out_b[0]),
                       _row(post_mix_g[1]), lat[1, :, 2], x, float(1.0 / np.sqrt(t * gd)), tm_tok)
    return _ffn(x, lat[1, :, 3], lat[1, :, 4], lat[1, :, 5], 1, ffn_rows[0], ffn_rows[1], w_up, ffn_conv_w,
                ffn_rows[2], w_down, tm, fc)
```

```python
import functools

import numpy as np
import jax
import jax.numpy as jnp
from jax import lax
from jax.experimental import pallas as pl
from jax.experimental.pallas import tpu as pltpu

F32 = jnp.float32
BF16 = jnp.bfloat16

M_HEADS = 4
M_CHUNK = 128
F_GROUPS = 8
GRID_W = 64
EPS = 1e-6
N_MOD = 6
COND_ROWS = 16
SCAN_CHUNKS_PER_STEP = 4

VMEM_LIMIT_BYTES = 56 * 1024 * 1024
LANES = 128


def _params(*sem):
    return pltpu.CompilerParams(dimension_semantics=sem, vmem_limit_bytes=VMEM_LIMIT_BYTES)


def _rms(x, g):
    return x * lax.rsqrt(jnp.mean(x * x, axis=-1, keepdims=True) + EPS) * g


def _bdot(a, b):
    return jnp.dot(a, b, preferred_element_type=F32)


def _ada_kernel(valid_rows, c_ref, w_ref, b_ref, win_ref, o_ref, win_out):
    c = c_ref[...]
    s = c * jax.nn.sigmoid(c)
    o_ref[0] = _bdot(s.astype(BF16), w_ref[0].astype(BF16)) + b_ref[0]
    step = pl.program_id(0) * pl.num_programs(1) + pl.program_id(1)
    row = step * win_ref.shape[0] + lax.broadcasted_iota(jnp.int32, win_ref.shape, 0)
    win_out[...] = jnp.where(row < valid_rows, win_ref[...], 0.0).astype(BF16)


def _ada(cond, ada_w, ada_b, w_in_t):
    depth, d, n = ada_w.shape
    tn = n // 4
    steps = depth * (n // tn)
    rows, cols = w_in_t.shape
    padded = rows + (-rows) % LANES
    blk = padded // steps
    assert padded % (16 * steps) == 0 and (steps - 1) * blk < rows
    return pl.pallas_call(
        functools.partial(_ada_kernel, rows),
        grid=(depth, n // tn),
        in_specs=[pl.BlockSpec((COND_ROWS, d), lambda i, j: (0, 0)),
                  pl.BlockSpec((1, d, tn), lambda i, j: (i, 0, j)),
                  pl.BlockSpec((1, 1, tn), lambda i, j: (i, 0, j)),
                  pl.BlockSpec((blk, cols), lambda i, j: (i * (n // tn) + j, 0))],
        out_specs=[pl.BlockSpec((1, COND_ROWS, tn), lambda i, j: (i, 0, j)),
                   pl.BlockSpec((blk, cols), lambda i, j: (i * (n // tn) + j, 0))],
        out_shape=[jax.ShapeDtypeStruct((depth, COND_ROWS, n), F32),
                   jax.ShapeDtypeStruct((padded, cols), BF16)],
        compiler_params=_params("arbitrary", "arbitrary"),
        name="ada",
    )(cond, ada_w, ada_b.reshape(depth, 1, n), w_in_t)


def _inproj_kernel(nconv, with_qo, x_ref, sh_ref, sc_ref, g_ref, w_ref, b_ref, *refs):
    nout = 5 if with_qo else 3
    conv_in, outs, conv_out = refs[:nconv], refs[nconv:nconv + nout], refs[nconv + nout:]
    kt_out, v_out, gate_out = outs[-3:]
    hk, hv = kt_out.shape[1], v_out.shape[2]
    dk = hk // M_HEADS
    h = _rms(x_ref[0], g_ref[...]) * (1.0 + sc_ref[0]) + sh_ref[0]
    hb = h.astype(BF16)

    def proj(lo, hi):
        return lax.dot_general(hb, w_ref[lo:hi, :], (((1,), (1,)), ((), ())),
                               preferred_element_type=F32) + b_ref[:, lo:hi]

    if with_qo:
        q_out, o_out = outs[:2]
        q_out[0] = proj(0, hk).astype(BF16)
        o_out[0] = jax.nn.sigmoid(proj(2 * hk + hv, 2 * hk + 2 * hv)).astype(BF16)
    kt_out[0] = jnp.transpose(proj(hk, 2 * hk) * (dk ** -0.5)).astype(BF16)
    v_out[0] = proj(2 * hk, 2 * hk + hv).astype(BF16)
    gate_out[0] = jnp.transpose(proj(2 * hk + 2 * hv, w_ref.shape[0]))[:gate_out.shape[1]]
    for src_ref, dst_ref in zip(conv_in, conv_out):
        dst_ref[...] = src_ref[...].astype(BF16)


def _inproj(x, sh, sc, g, w, bias, hk, hv, tm, with_qo, convert=()):
    bsz, t, d = x.shape
    nt = t // tm
    per_batch = sh.shape[0] > 1
    mod_spec = pl.BlockSpec((1, 1, d), (lambda b, i: (b, 0, 0)) if per_batch else (lambda b, i: (0, 0, 0)))
    full = lambda a: pl.BlockSpec(a.shape, lambda b, i: (0, 0))
    tok = lambda n: pl.BlockSpec((1, tm, n), lambda b, i: (b, i, 0))
    tok_t = lambda n: pl.BlockSpec((1, n, tm), lambda b, i: (b, 0, i))
    flat = [a.reshape(-1, a.shape[-1]) for a in convert]
    conv_specs = [pl.BlockSpec((a.shape[0] // (bsz * nt), a.shape[1]), lambda b, i: (b * nt + i, 0)) for a in flat]
    assert all(a.shape[0] % (16 * bsz * nt) == 0 for a in flat)
    qo_specs = [tok(hk), tok(hv)] if with_qo else []
    qo_shapes = [jax.ShapeDtypeStruct((bsz, t, hk), BF16), jax.ShapeDtypeStruct((bsz, t, hv), BF16)] if with_qo else []
    outs = pl.pallas_call(
        functools.partial(_inproj_kernel, len(flat), with_qo),
        grid=(bsz, nt),
        in_specs=[tok(d), mod_spec, mod_spec, full(g), full(w), full(bias)] + conv_specs,
        out_specs=qo_specs + [tok_t(hk), tok(hv), tok_t(4 * M_HEADS)] + conv_specs,
        out_shape=qo_shapes + [jax.ShapeDtypeStruct((bsz, hk, t), BF16), jax.ShapeDtypeStruct((bsz, t, hv), BF16),
                               jax.ShapeDtypeStruct((bsz, 4 * M_HEADS, t), F32)]
                  + [jax.ShapeDtypeStruct(a.shape, BF16) for a in flat],
        compiler_params=_params("parallel", "parallel"),
        name="mlstm_inproj",
    )(x, sh, sc, g, w, bias, *flat)
    nout = len(outs) - len(flat)
    return list(outs[:nout]) + [o.reshape(a.shape) for o, a in zip(outs[nout:], convert)]


def _lane_scan(x, op, fill, reverse, seg):
    n = x.shape[-1]
    lane = lax.broadcasted_iota(jnp.int32, x.shape, x.ndim - 1) % seg
    d = 1
    while d < seg:
        if reverse:
            shifted = jnp.where(lane < seg - d, pltpu.roll(x, n - d, x.ndim - 1), fill)
        else:
            shifted = jnp.where(lane >= d, pltpu.roll(x, d, x.ndim - 1), fill)
        x = op(x, shifted)
        d *= 2
    return x


def _log_sigmoid(x):
    return jnp.minimum(x, 0.0) - jnp.log1p(jnp.exp(-jnp.abs(x)))


def _gate_kernel(orders, gf_ref, gb_ref, of_ref, ob_ref):
    L = M_CHUNK
    for d, (g_ref, o_ref) in enumerate(((gf_ref, of_ref), (gb_ref, ob_ref))):
        edge = L - 1 if d == 0 else 0
        lf = _log_sigmoid(g_ref[:, 1])
        b_all = _lane_scan(lf, jnp.add, 0.0, d == 1, L)
        r1_all = g_ref[:, 0] - b_all
        cm_all = _lane_scan(r1_all, jnp.maximum, -jnp.inf, d == 1, L)
        m = jnp.zeros(lf.shape[:-1] + (1,), F32)
        for c in orders[d]:
            lanes = slice(c * L, (c + 1) * L)
            b, r1, cm = b_all[..., lanes], r1_all[..., lanes], cm_all[..., lanes]
            mx = jnp.maximum(m, cm[..., edge:edge + 1])
            mc = jnp.maximum(m, cm)
            o_ref[0, :, :, lanes] = r1
            o_ref[1, :, :, lanes] = mc
            o_ref[2, :, :, lanes] = jnp.broadcast_to(m, r1.shape)
            o_ref[3, :, :, lanes] = jnp.exp(-(b + mc))
            o_ref[4, :, :, lanes] = jnp.exp(r1 - mx)
            o_ref[5, :, :, lanes] = jnp.broadcast_to(jnp.exp(m - mx), r1.shape)
            m = b[..., edge:edge + 1] + mx


def _gate_prep(gates_c, gates_x):
    bsz = gates_c.shape[0]
    nlt, nct = gates_x.shape[2] // M_CHUNK, gates_c.shape[2] // M_CHUNK
    n = (nlt + nct) * M_CHUNK
    fwd = tuple(range(nlt, nlt + nct)) + tuple(range(nlt))
    bwd = tuple(range(nlt + nct - 1, nlt - 1, -1)) + tuple(range(nlt - 1, -1, -1))
    gf = gb = jnp.concatenate([gates_x, gates_c], axis=2).reshape(bsz, 4, M_HEADS, n)
    out_spec = pl.BlockSpec((6, bsz, M_HEADS, n), lambda j: (0, 0, 0, 0))
    return pl.pallas_call(
        functools.partial(_gate_kernel, (fwd, bwd)),
        grid=(1,),
        in_specs=[pl.BlockSpec((bsz, 2, M_HEADS, n), lambda j: (0, 0, 0, 0)),
                  pl.BlockSpec((bsz, 2, M_HEADS, n), lambda j: (0, 1, 0, 0))],
        out_specs=[out_spec, out_spec],
        out_shape=[jax.ShapeDtypeStruct((6, bsz, M_HEADS, n), F32)] * 2,
        compiler_params=_params("arbitrary"),
        name="mlstm_gates",
    )(gf, gb)


def _scan_kernel(cps, with_outputs, *refs):
    if with_outputs:
        (qf, ktf, vf, rf, qb, ktb, vb, rb, c0, hf_out, hb_out, c_s) = refs
        h_outs = (hf_out, hb_out)
        streams = ((qf, ktf, vf, rf), (qb, ktb, vb, rb))
    else:
        (ktf, vf, rf, ktb, vb, rb, c_s) = refs
        streams = ((None, ktf, vf, rf), (None, ktb, vb, rb))
    s = pl.program_id(1)
    L = M_CHUNK
    dk = ktf.shape[1] // M_HEADS
    dv = vf.shape[2] // M_HEADS

    @pl.when(s == 0)
    def _():
        if with_outputs:
            c_s[...] = c0[...]
        else:
            c_s[...] = jnp.zeros(c_s.shape, F32)

    row = lax.broadcasted_iota(jnp.int32, (L, L), 0)
    col = lax.broadcasted_iota(jnp.int32, (L, L), 1)
    for cc in range(cps):
        sub = (cc, cps - 1 - cc)
        blk = [slice(i * L, (i + 1) * L) for i in sub]
        if with_outputs:
            stack = [r_ref[j, 0, :, blk[d]] for d, r_ref in enumerate((rf, rb)) for j in (1, 3)]
            stack.append(jnp.zeros((L - 4 * M_HEADS, L), F32))
            cols = jnp.transpose(jnp.concatenate(stack, axis=0))
        for d in range(2):
            q_ref, kt_ref, v_ref, row_ref = streams[d]
            mask = (col <= row) if d == 0 else (col >= row)
            for h in range(M_HEADS):
                kt = kt_ref[0, h * dk:(h + 1) * dk, blk[d]]
                v = v_ref[0, blk[d], h * dv:(h + 1) * dv]
                c_old = c_s[0, d, h]
                kw = kt.astype(F32) * row_ref[4, 0, h:h + 1, blk[d]]
                n_upd = jnp.broadcast_to(jnp.sum(kw, axis=1, keepdims=True), (dk, LANES))
                upd = jnp.concatenate([_bdot(kw.astype(BF16), v), n_upd], axis=1)
                c_s[0, d, h] = row_ref[5, 0, h:h + 1, sub[d] * L:sub[d] * L + 1] * c_old + upd
                if not with_outputs:
                    continue
                q = q_ref[0, blk[d], h * dk:(h + 1) * dk]
                cb = c_old.astype(BF16)
                j = 2 * M_HEADS * d + h
                mc = cols[:, j:j + 1]
                fl = cols[:, j + M_HEADS:j + M_HEADS + 1]
                p = jnp.exp(jnp.where(mask, row_ref[0, 0, h:h + 1, blk[d]] - mc, -jnp.inf))
                wi = jnp.exp(row_ref[2, 0, h:h + 1, blk[d]] - mc)
                sq = _bdot(q, jnp.concatenate([kt, cb[:, dv:]], axis=1))
                sp = sq[:, :L] * p
                lhs = jnp.concatenate([sp, q.astype(F32) * wi], axis=1).astype(BF16)
                num = _bdot(lhs, jnp.concatenate([v, cb[:, :dv]], axis=0))
                den = wi * sq[:, L:] + jnp.sum(sp, axis=1, keepdims=True)
                inv = 1.0 / jnp.maximum(jnp.abs(den), fl)
                h_outs[d][0, blk[d], h * dv:(h + 1) * dv] = (
                    num * jnp.concatenate([inv] * (dv // LANES), axis=1)).astype(BF16)


def _scan(q, kt, v, gates_f, gates_b, off, init):
    bsz, hk, t = kt.shape
    hv = v.shape[2]
    dk, dv = hk // M_HEADS, hv // M_HEADS
    assert dk == M_CHUNK == LANES and dv % LANES == 0
    nc = t // M_CHUNK
    cps = max(c for c in (SCAN_CHUNKS_PER_STEP, 2, 1) if nc % c == 0 and off % c == 0)
    nsteps = nc // cps
    L = cps * M_CHUNK
    with_outputs = init is not None

    def specs(cidx):
        sp = [pl.BlockSpec((1, hk, L), lambda b, s: (b, 0, cidx(s))),
              pl.BlockSpec((1, L, hv), lambda b, s: (b, cidx(s), 0)),
              pl.BlockSpec((6, 1, M_HEADS, L), lambda b, s: (0, b, 0, off // cps + cidx(s)))]
        if with_outputs:
            sp = [pl.BlockSpec((1, L, hk), lambda b, s: (b, cidx(s), 0))] + sp
        return sp

    lead = [q] if with_outputs else []
    args = lead + [kt, v, gates_f] + lead + [kt, v, gates_b]
    in_specs = specs(lambda s: s) + specs(lambda s: nsteps - 1 - s)
    state_shape = (bsz, 2, M_HEADS, dk, dv + LANES)
    state_spec = pl.BlockSpec((1,) + state_shape[1:], lambda b, s: (b, 0, 0, 0, 0))
    if with_outputs:
        args.append(init)
        in_specs.append(state_spec)
        out_specs = [pl.BlockSpec((1, L, hv), lambda b, s: (b, s, 0)),
                     pl.BlockSpec((1, L, hv), lambda b, s: (b, nsteps - 1 - s, 0))]
        out_shape = [jax.ShapeDtypeStruct((bsz, t, hv), BF16)] * 2
        scratch = [pltpu.VMEM((1,) + state_shape[1:], F32)]
    else:
        out_specs = state_spec
        out_shape = jax.ShapeDtypeStruct(state_shape, F32)
        scratch = []
    return pl.pallas_call(
        functools.partial(_scan_kernel, cps, with_outputs),
        grid=(bsz, nsteps),
        in_specs=in_specs,
        out_specs=out_specs,
        out_shape=out_shape,
        scratch_shapes=scratch,
        compiler_params=_params("parallel", "arbitrary"),
        name="mlstm_scan" if with_outputs else "mlstm_ctx_state",
    )(*args)


def _readout_core(h, sig_o, x, gate, norm_g, w_ref, post_g):
    dv = h.shape[1] // M_HEADS
    parts = []
    for hd in range(M_HEADS):
        hh = h[:, hd * dv:(hd + 1) * dv]
        parts.append(hh * lax.rsqrt(jnp.mean(hh * hh, axis=-1, keepdims=True) + EPS))
    y = (jnp.concatenate(parts, axis=-1) * norm_g * sig_o.astype(F32)).astype(BF16)
    return x + gate * _rms(_bdot(y, w_ref[...]), post_g)


def _ffn_core(fc, x, sh, sc, gate, pre_g, post_g, wu_ref, wg_ref, cw_ref, cb_ref, wd_ref, act_ref):
    tm = x.shape[0]
    f = wu_ref.shape[1]
    hb = (_rms(x, pre_g) * (1.0 + sc) + sh).astype(BF16)
    pos = lax.broadcasted_iota(jnp.int32, (tm, fc), 0) % GRID_W
    has_prev = pos != 0
    has_next = pos != GRID_W - 1
    for j in range(f // fc):
        cs = slice(j * fc, (j + 1) * fc)
        u = _bdot(hb, wu_ref[:, cs])
        g = _bdot(hb, wg_ref[:, cs])
        g_prev = jnp.where(has_prev, pltpu.roll(g, 1, 0), 0.0)
        g_next = jnp.where(has_next, pltpu.roll(g, tm - 1, 0), 0.0)
        gc = g_prev * cw_ref[0:1, cs] + g * cw_ref[1:2, cs] + g_next * cw_ref[2:3, cs] + cb_ref[:, cs]
        act_ref[:, cs] = (gc * jax.nn.sigmoid(gc) * u).astype(BF16)
    y = _bdot(act_ref[...], wd_ref[...])
    return x + gate * _rms(y, post_g)


RADIX = 4


def _fourier_channel_core(xq, sh, sc, pre_g, cs_ref, g_ref):
    gd = cs_ref.shape[0]
    hb = [(_rms(x, pre_g) * (1.0 + sc) + sh).astype(BF16) for x in xq]
    for g in range(hb[0].shape[1] // gd):
        lanes = slice(g * gd, (g + 1) * gd)
        r = [_bdot(h[:, lanes], cs_ref[...]) for h in hb]
        a = [v[:, :gd] for v in r]
        b = [v[:, gd:] for v in r]
        a02p, a02m, a13p, a13m = a[0] + a[2], a[0] - a[2], a[1] + a[3], a[1] - a[3]
        b02p, b02m, b13p, b13m = b[0] + b[2], b[0] - b[2], b[1] + b[3], b[1] - b[3]
        re_im = ((a02p + a13p, -(b02p + b13p)),
                 (a02m - b13m, -(b02m + a13m)),
                 (a02p - a13p, b13p - b02p),
                 (a02m + b13m, a13m - b02m))
        for k, (re, im) in enumerate(re_im):
            g_ref[0, k, 0, :, lanes] = re.astype(BF16)
            g_ref[0, k, 1, :, lanes] = im.astype(BF16)


def _ffn_kernel(fc, x_ref, sh_ref, sc_ref, gate_ref, pre_ref, post_ref, wu_ref, wg_ref, cw_ref, cb_ref,
                wd_ref, out_ref, act_ref):
    out_ref[0] = _ffn_core(fc, x_ref[0], sh_ref[0], sc_ref[0], gate_ref[0], pre_ref[...], post_ref[...],
                           wu_ref, wg_ref, cw_ref, cb_ref, wd_ref, act_ref)


def _layer0_tail_kernel(fc, hf_ref, hb_ref, o_ref, x_ref, g1_ref, ng_ref, wo_ref, pm_ref,
                        sh_ref, sc_ref, gate_ref, pre_ref, post_ref, wu_ref, wg_ref, cw_ref, cb_ref, wd_ref,
                        fsh_ref, fsc_ref, fpre_ref, cs_ref, out_ref, g_ref, act_ref):
    rows = x_ref.shape[2]
    cat = lambda ref: jnp.concatenate([ref[0, q] for q in range(RADIX)], axis=0)
    h = cat(hf_ref).astype(F32) + cat(hb_ref).astype(F32)
    x = _readout_core(h, cat(o_ref), cat(x_ref), g1_ref[0], ng_ref[...], wo_ref, pm_ref[...])
    x = _ffn_core(fc, x, sh_ref[0], sc_ref[0], gate_ref[0], pre_ref[...], post_ref[...],
                  wu_ref, wg_ref, cw_ref, cb_ref, wd_ref, act_ref)
    xq = [x[q * rows:(q + 1) * rows] for q in range(RADIX)]
    for q in range(RADIX):
        out_ref[0, q] = xq[q]
    _fourier_channel_core(xq, fsh_ref[0], fsc_ref[0], fpre_ref[...], cs_ref, g_ref)


def _ffn_specs(layer, d, f, conv_w):
    def lay(rows, cols, col_block=0):
        return pl.BlockSpec((None, rows, cols), lambda b, i: (layer, 0, col_block),
                            pipeline_mode=pl.Buffered(1))
    return [lay(1, d), lay(1, d), lay(d, f, 0), lay(d, f, 1), lay(conv_w, f), lay(1, f), lay(f, d)]


def _ffn(x, sh, sc, gate, layer, pre_g, post_g, w_up, cw, cb, w_down, tm, fc):
    bsz, t, d = x.shape
    f = w_down.shape[1]
    tok = pl.BlockSpec((1, tm, d), lambda b, i: (b, i, 0))
    mod = pl.BlockSpec((1, 1, d), lambda b, i: (b, 0, 0))
    return pl.pallas_call(
        functools.partial(_ffn_kernel, fc),
        grid=(bsz, t // tm),
        in_specs=[tok, mod, mod, mod] + _ffn_specs(layer, d, f, cw.shape[1]),
        out_specs=tok,
        out_shape=jax.ShapeDtypeStruct((bsz, t, d), F32),
        scratch_shapes=[pltpu.VMEM((tm, f), BF16)],
        compiler_params=_params("parallel", "parallel"),
        name="conv_ffn",
    )(x, sh, sc, gate, pre_g, post_g, w_up, w_up, cw, cb, w_down)


def _layer0_tail(hf, hb, sig_o, x, g1, norm_g, w_out, post_mix_g, ffn_mods, layer, pre_g, post_g, w_up, cw, cb, w_down,
                 fsh, fsc, fpre_g, cs, rows, fc):
    bsz, t, d = x.shape
    hv = hf.shape[2]
    f = w_down.shape[1]
    tq = t // RADIX
    quarters = lambda a: a.reshape(bsz, RADIX, tq, a.shape[2])
    tok = lambda n: pl.BlockSpec((1, RADIX, rows, n), lambda b, i: (b, 0, i, 0))
    mod = pl.BlockSpec((1, 1, d), lambda b, i: (b, 0, 0))
    const = lambda a: pl.BlockSpec(a.shape, lambda b, i: (0,) * a.ndim, pipeline_mode=pl.Buffered(1))
    out, g = pl.pallas_call(
        functools.partial(_layer0_tail_kernel, fc),
        grid=(bsz, tq // rows),
        in_specs=([tok(hv), tok(hv), tok(hv), tok(d), mod, const(norm_g), const(w_out), const(post_mix_g), mod, mod, mod]
                  + _ffn_specs(layer, d, f, cw.shape[1]) + [mod, mod, const(fpre_g), const(cs)]),
        out_specs=[tok(d), pl.BlockSpec((1, RADIX, 2, rows, d), lambda b, i: (b, 0, 0, i, 0))],
        out_shape=[jax.ShapeDtypeStruct((bsz, RADIX, tq, d), F32),
                   jax.ShapeDtypeStruct((bsz, RADIX, 2, tq, d), BF16)],
        scratch_shapes=[pltpu.VMEM((RADIX * rows, f), BF16)],
        compiler_params=_params("parallel", "parallel"),
        name="layer0_tail",
    )(quarters(hf), quarters(hb), quarters(sig_o), quarters(x), g1, norm_g, w_out, post_mix_g, *ffn_mods,
      pre_g, post_g, w_up, w_up, cw, cb, w_down, fsh, fsc, fpre_g, cs)
    return out.reshape(bsz, t, d), g


def _fourier_token_kernel(scale, dft_ref, g_ref, w_ref, bias_ref, post_ref, gate_ref, x_ref, out_ref,
                          x_s, out_s):
    tm = dft_ref.shape[1]
    nblk = x_s.shape[0]
    for c in range(nblk):
        x_s[c] = x_ref[0, :, c * LANES:(c + 1) * LANES]
    for r in range(RADIX):
        rows = pl.ds(r, tm, stride=RADIX)
        y = _bdot(dft_ref[r], g_ref[0, r]) * scale
        z = _bdot(y.astype(BF16), w_ref[...]) + bias_ref[...]
        xr = jnp.concatenate([x_s[c, rows, :] for c in range(nblk)], axis=1)
        res = xr + gate_ref[0] * _rms(z, post_ref[...])
        for c in range(nblk):
            out_s[c, rows, :] = res[:, c * LANES:(c + 1) * LANES]
    for c in range(nblk):
        out_ref[0, :, c * LANES:(c + 1) * LANES] = out_s[c]


def _fourier_token(dft, g, w, bias, post_g, gate, x, scale, tm):
    bsz, t, d = x.shape
    tq = t // RADIX
    full = lambda a: pl.BlockSpec(a.shape, lambda b, i: (0, 0))
    tok = pl.BlockSpec((1, RADIX * tm, d), lambda b, i: (b, i, 0))
    return pl.pallas_call(
        functools.partial(_fourier_token_kernel, scale),
        grid=(bsz, tq // tm),
        in_specs=[pl.BlockSpec((RADIX, tm, 2 * tq), lambda b, i: (0, i, 0)),
                  pl.BlockSpec((1, RADIX, 2 * tq, d), lambda b, i: (b, 0, 0, 0)),
                  full(w), full(bias), full(post_g), pl.BlockSpec((1, 1, d), lambda b, i: (b, 0, 0)), tok],
        out_specs=tok,
        out_shape=jax.ShapeDtypeStruct((bsz, t, d), F32),
        scratch_shapes=[pltpu.VMEM((d // LANES, RADIX * tm, LANES), F32)] * 2,
        compiler_params=_params("parallel", "arbitrary"),
        name="fourier_token_dft",
    )(dft, g.reshape(bsz, RADIX, 2 * tq, d), w, bias, post_g, gate, x)


def _dft_tables(t, gd):
    idx = np.arange(gd, dtype=np.int64)
    ang = 2.0 * np.pi * ((idx[:, None] * idx[None, :]) % gd).astype(np.float64) / gd
    chan = np.concatenate([np.cos(ang), np.sin(ang)], axis=1).astype(np.float32)
    tq = t // RADIX
    k = RADIX * np.arange(tq, dtype=np.int64)[None, :, None] + np.arange(RADIX, dtype=np.int64)[:, None, None]
    ang = 2.0 * np.pi * ((k * np.arange(tq, dtype=np.int64)[None, None, :]) % t).astype(np.float64) / t
    tok = np.concatenate([np.cos(ang), np.sin(ang)], axis=2).astype(np.float32)
    return jnp.asarray(chan).astype(BF16), jnp.asarray(tok).astype(BF16)


def _row(v):
    return v.reshape(1, -1)


def kernel(x, c, ctx, c_ctx, ada_w, ada_b, pre_mix_g, post_mix_g, pre_ffn_g, post_ffn_g, ffn_up_w, ffn_conv_w,
           ffn_conv_b, ffn_down_w, m_in_w, m_in_b, m_norm_g, m_out_w, f_out_w, f_out_b):
    bsz, t, d = x.shape
    tc = ctx.shape[1]
    f = ffn_down_w.shape[1]
    assert ada_w.shape[0] == 2 and m_in_w.shape[0] == 1 and f_out_w.shape[0] == 1
    assert bsz + 1 <= COND_ROWS and t % M_CHUNK == 0 and tc % M_CHUNK == 0 and t % GRID_W == 0
    hv = m_out_w.shape[1]
    hk = (m_in_w.shape[2] - 2 * hv - 4 * M_HEADS) // 2
    tm = min(512, t)
    fc = 256
    assert f % fc == 0 and t % tm == 0 and tm % GRID_W == 0

    cond = jnp.zeros((COND_ROWS, d), F32).at[:bsz].set(c).at[bsz].set(c_ctx)
    mod, w_in = _ada(cond, ada_w, ada_b, jnp.swapaxes(m_in_w[0], 0, 1))
    lat = mod[:, :bsz].reshape(2, bsz, N_MOD, 1, d)
    cmod = mod[0, bsz].reshape(N_MOD, 1, 1, d)

    ffn_rows = [a.reshape(a.shape[0], 1, a.shape[1]) for a in (pre_ffn_g, post_ffn_g, ffn_conv_b)]

    b_in = _row(jnp.pad(m_in_b[0], (0, w_in.shape[0] - m_in_b.shape[1])))
    pre0 = _row(pre_mix_g[0])

    kc, vc, gc = _inproj(ctx, cmod[0], cmod[1], pre0, w_in, b_in, hk, hv, min(tm, tc), False)
    qx, ox, kx, vx, gx, w_up, w_down, w_mo, w_fo = _inproj(
        x, lat[0, :, 0], lat[0, :, 1], pre0, w_in, b_in, hk, hv, tm, True,
        convert=(ffn_up_w, ffn_down_w, m_out_w, f_out_w))
    gates_f, gates_b = _gate_prep(gc, gx)
    state = _scan(None, kc, vc, gates_f, gates_b, t // M_CHUNK, None)
    hf, hb = _scan(qx, kx, vx, gates_f, gates_b, 0, state)
    gd = d // F_GROUPS
    tq = t // RADIX
    rows, tm_tok = min(128, tq), min(256, tq)
    assert t % RADIX == 0 and tq % rows == 0 and tq % tm_tok == 0 and rows % GRID_W == 0
    chan, tok = _dft_tables(t, gd)
    x, g = _layer0_tail(hf, hb, ox, x, lat[0, :, 2], _row(m_norm_g[0]), w_mo[0], _row(post_mix_g[0]),
                        (lat[0, :, 3], lat[0, :, 4], lat[0, :, 5]), 0, ffn_rows[0], ffn_rows[1], w_up, ffn_conv_w,
                        ffn_rows[2], w_down, lat[1, :, 0], lat[1, :, 1], _row(pre_mix_g[1]), chan, rows, fc)

    x = _fourier_token(tok, g, w_fo[0], _row(f_out_b[0]),
                       _row(post_mix_g[1]), lat[1, :, 2], x, float(1.0 / np.sqrt(t * gd)), tm_tok)
    return _ffn(x, lat[1, :, 3], lat[1, :, 4], lat[1, :, 5], 1, ffn_rows[0], ffn_rows[1], w_up, ffn_conv_w,
                ffn_rows[2], w_down, tm, fc)
```

```python
import functools

import numpy as np
import jax
import jax.numpy as jnp
from jax import lax
from jax.experimental import pallas as pl
from jax.experimental.pallas import tpu as pltpu

F32 = jnp.float32
BF16 = jnp.bfloat16

M_HEADS = 4
M_CHUNK = 128
F_GROUPS = 8
GRID_W = 64
EPS = 1e-6
N_MOD = 6
COND_ROWS = 16
SCAN_CHUNKS_PER_STEP = 4

VMEM_LIMIT_BYTES = 56 * 1024 * 1024
LANES = 128


def _params(*sem):
    return pltpu.CompilerParams(dimension_semantics=sem, vmem_limit_bytes=VMEM_LIMIT_BYTES)


def _rms(x, g):
    return x * lax.rsqrt(jnp.mean(x * x, axis=-1, keepdims=True) + EPS) * g


def _bdot(a, b):
    return jnp.dot(a, b, preferred_element_type=F32)


def _mod_rows(mod_ref, row):
    m = mod_ref[pl.ds(row, 1), :]
    d = m.shape[1] // N_MOD
    return [m[:, k * d:(k + 1) * d] for k in range(N_MOD)]


def _mod_spec(layer, cond_rows, width):
    return pl.BlockSpec((None, cond_rows, width), lambda b, i: (layer, 0, 0), pipeline_mode=pl.Buffered(1))


def _const_spec(a):
    return pl.BlockSpec(a.shape, lambda b, i: (0,) * a.ndim, pipeline_mode=pl.Buffered(1))


def _ada_kernel(valid_rows, c_ref, cc_ref, w_ref, b_ref, win_ref, o_ref, win_out):
    pad = jnp.zeros((o_ref.shape[1] - c_ref.shape[0] - 1, c_ref.shape[1]), F32)
    c = jnp.concatenate([c_ref[...], cc_ref[...], pad], axis=0)
    s = c * jax.nn.sigmoid(c)
    o_ref[0] = _bdot(s.astype(BF16), w_ref[0].astype(BF16)) + b_ref[pl.ds(pl.program_id(0), 1), :]
    step = pl.program_id(0) * pl.num_programs(1) + pl.program_id(1)
    row = step * win_ref.shape[0] + lax.broadcasted_iota(jnp.int32, win_ref.shape, 0)
    win_out[...] = jnp.where(row < valid_rows, win_ref[...], 0.0).astype(BF16)


def _ada(c, c_ctx, ada_w, ada_b, w_in_t):
    depth, d, n = ada_w.shape
    tn = n // 4
    steps = depth * (n // tn)
    rows, cols = w_in_t.shape
    padded = rows + (-rows) % LANES
    blk = padded // steps
    assert padded % (16 * steps) == 0 and (steps - 1) * blk < rows
    return pl.pallas_call(
        functools.partial(_ada_kernel, rows),
        grid=(depth, n // tn),
        in_specs=[pl.BlockSpec(c.shape, lambda i, j: (0, 0)),
                  pl.BlockSpec(c_ctx.shape, lambda i, j: (0, 0)),
                  pl.BlockSpec((1, d, tn), lambda i, j: (i, 0, j)),
                  pl.BlockSpec((depth, tn), lambda i, j: (0, j)),
                  pl.BlockSpec((blk, cols), lambda i, j: (i * (n // tn) + j, 0))],
        out_specs=[pl.BlockSpec((1, COND_ROWS, tn), lambda i, j: (i, 0, j)),
                   pl.BlockSpec((blk, cols), lambda i, j: (i * (n // tn) + j, 0))],
        out_shape=[jax.ShapeDtypeStruct((depth, COND_ROWS, n), F32),
                   jax.ShapeDtypeStruct((padded, cols), BF16)],
        compiler_params=_params("arbitrary", "arbitrary"),
        name="ada",
    )(c, c_ctx, ada_w, ada_b, w_in_t)


def _inproj_kernel(nconv, with_qo, cond_row, x_ref, mod_ref, g_ref, w_ref, b_ref, *refs):
    nout = 5 if with_qo else 3
    conv_in, outs, conv_out = refs[:nconv], refs[nconv:nconv + nout], refs[nconv + nout:]
    kt_out, v_out, gate_out = outs[-3:]
    hk, hv = kt_out.shape[1], v_out.shape[2]
    dk = hk // M_HEADS
    shift, scale = _mod_rows(mod_ref, pl.program_id(0) if cond_row is None else cond_row)[:2]
    h = _rms(x_ref[0], g_ref[0:1]) * (1.0 + scale) + shift
    hb = h.astype(BF16)

    def proj(lo, hi):
        return lax.dot_general(hb, w_ref[lo:hi, :], (((1,), (1,)), ((), ())),
                               preferred_element_type=F32) + b_ref[:, lo:hi]

    if with_qo:
        q_out, o_out = outs[:2]
        q_out[0] = proj(0, hk).astype(BF16)
        o_out[0] = jax.nn.sigmoid(proj(2 * hk + hv, 2 * hk + 2 * hv)).astype(BF16)
    kt_out[0] = jnp.transpose(proj(hk, 2 * hk) * (dk ** -0.5)).astype(BF16)
    v_out[0] = proj(2 * hk, 2 * hk + hv).astype(BF16)
    gate_out[0] = jnp.transpose(proj(2 * hk + 2 * hv, w_ref.shape[0]))[:gate_out.shape[1]]
    for src_ref, dst_ref in zip(conv_in, conv_out):
        dst_ref[...] = src_ref[...].astype(BF16)


def _inproj(x, mod, cond_row, g, w, bias, hk, hv, tm, with_qo, convert=()):
    bsz, t, d = x.shape
    nt = t // tm
    full = lambda a: pl.BlockSpec(a.shape, lambda b, i: (0, 0))
    tok = lambda n: pl.BlockSpec((1, tm, n), lambda b, i: (b, i, 0))
    tok_t = lambda n: pl.BlockSpec((1, n, tm), lambda b, i: (b, 0, i))
    flat = [a.reshape(-1, a.shape[-1]) for a in convert]
    conv_specs = [pl.BlockSpec((a.shape[0] // (bsz * nt), a.shape[1]), lambda b, i: (b * nt + i, 0)) for a in flat]
    assert all(a.shape[0] % (16 * bsz * nt) == 0 for a in flat)
    qo_specs = [tok(hk), tok(hv)] if with_qo else []
    qo_shapes = [jax.ShapeDtypeStruct((bsz, t, hk), BF16), jax.ShapeDtypeStruct((bsz, t, hv), BF16)] if with_qo else []
    outs = pl.pallas_call(
        functools.partial(_inproj_kernel, len(flat), with_qo, cond_row),
        grid=(bsz, nt),
        in_specs=[tok(d), _mod_spec(0, *mod.shape[1:]), full(g), full(w), full(bias)] + conv_specs,
        out_specs=qo_specs + [tok_t(hk), tok(hv), tok_t(4 * M_HEADS)] + conv_specs,
        out_shape=qo_shapes + [jax.ShapeDtypeStruct((bsz, hk, t), BF16), jax.ShapeDtypeStruct((bsz, t, hv), BF16),
                               jax.ShapeDtypeStruct((bsz, 4 * M_HEADS, t), F32)]
                  + [jax.ShapeDtypeStruct(a.shape, BF16) for a in flat],
        compiler_params=_params("parallel", "parallel"),
        name="mlstm_inproj",
    )(x, mod, g, w, bias, *flat)
    nout = len(outs) - len(flat)
    return list(outs[:nout]) + [o.reshape(a.shape) for o, a in zip(outs[nout:], convert)]


def _lane_scan(x, op, fill, reverse, seg):
    n = x.shape[-1]
    lane = lax.broadcasted_iota(jnp.int32, x.shape, x.ndim - 1) % seg
    d = 1
    while d < seg:
        if reverse:
            shifted = jnp.where(lane < seg - d, pltpu.roll(x, n - d, x.ndim - 1), fill)
        else:
            shifted = jnp.where(lane >= d, pltpu.roll(x, d, x.ndim - 1), fill)
        x = op(x, shifted)
        d *= 2
    return x


def _log_sigmoid(x):
    return jnp.minimum(x, 0.0) - jnp.log1p(jnp.exp(-jnp.abs(x)))


def _gate_kernel(orders, gf_ref, gb_ref, of_ref, ob_ref):
    L = M_CHUNK
    for d, (g_ref, o_ref) in enumerate(((gf_ref, of_ref), (gb_ref, ob_ref))):
        edge = L - 1 if d == 0 else 0
        lf = _log_sigmoid(g_ref[:, 1])
        b_all = _lane_scan(lf, jnp.add, 0.0, d == 1, L)
        r1_all = g_ref[:, 0] - b_all
        cm_all = _lane_scan(r1_all, jnp.maximum, -jnp.inf, d == 1, L)
        m = jnp.zeros(lf.shape[:-1] + (1,), F32)
        for c in orders[d]:
            lanes = slice(c * L, (c + 1) * L)
            b, r1, cm = b_all[..., lanes], r1_all[..., lanes], cm_all[..., lanes]
            mx = jnp.maximum(m, cm[..., edge:edge + 1])
            mc = jnp.maximum(m, cm)
            o_ref[0, :, :, lanes] = r1
            o_ref[1, :, :, lanes] = mc
            o_ref[2, :, :, lanes] = jnp.broadcast_to(m, r1.shape)
            o_ref[3, :, :, lanes] = jnp.exp(-(b + mc))
            o_ref[4, :, :, lanes] = jnp.exp(r1 - mx)
            o_ref[5, :, :, lanes] = jnp.broadcast_to(jnp.exp(m - mx), r1.shape)
            m = b[..., edge:edge + 1] + mx


def _gate_prep(gates_c, gates_x):
    bsz = gates_c.shape[0]
    nlt, nct = gates_x.shape[2] // M_CHUNK, gates_c.shape[2] // M_CHUNK
    n = (nlt + nct) * M_CHUNK
    fwd = tuple(range(nlt, nlt + nct)) + tuple(range(nlt))
    bwd = tuple(range(nlt + nct - 1, nlt - 1, -1)) + tuple(range(nlt - 1, -1, -1))
    gf = gb = jnp.concatenate([gates_x, gates_c], axis=2).reshape(bsz, 4, M_HEADS, n)
    out_spec = pl.BlockSpec((6, bsz, M_HEADS, n), lambda j: (0, 0, 0, 0))
    return pl.pallas_call(
        functools.partial(_gate_kernel, (fwd, bwd)),
        grid=(1,),
        in_specs=[pl.BlockSpec((bsz, 2, M_HEADS, n), lambda j: (0, 0, 0, 0)),
                  pl.BlockSpec((bsz, 2, M_HEADS, n), lambda j: (0, 1, 0, 0))],
        out_specs=[out_spec, out_spec],
        out_shape=[jax.ShapeDtypeStruct((6, bsz, M_HEADS, n), F32)] * 2,
        compiler_params=_params("arbitrary"),
        name="mlstm_gates",
    )(gf, gb)


def _scan_kernel(cps, with_outputs, *refs):
    if with_outputs:
        (qf, ktf, vf, rf, qb, ktb, vb, rb, c0, hf_out, hb_out, c_s) = refs
        h_outs = (hf_out, hb_out)
        streams = ((qf, ktf, vf, rf), (qb, ktb, vb, rb))
    else:
        (ktf, vf, rf, ktb, vb, rb, c_s) = refs
        streams = ((None, ktf, vf, rf), (None, ktb, vb, rb))
    s = pl.program_id(1)
    L = M_CHUNK
    dk = ktf.shape[1] // M_HEADS
    dv = vf.shape[2] // M_HEADS

    @pl.when(s == 0)
    def _():
        if with_outputs:
            c_s[...] = c0[...]
        else:
            c_s[...] = jnp.zeros(c_s.shape, F32)

    row = lax.broadcasted_iota(jnp.int32, (L, L), 0)
    col = lax.broadcasted_iota(jnp.int32, (L, L), 1)
    for cc in range(cps):
        sub = (cc, cps - 1 - cc)
        blk = [slice(i * L, (i + 1) * L) for i in sub]
        if with_outputs:
            stack = [r_ref[j, 0, :, blk[d]] for d, r_ref in enumerate((rf, rb)) for j in (1, 3)]
            stack.append(jnp.zeros((L - 4 * M_HEADS, L), F32))
            cols = jnp.transpose(jnp.concatenate(stack, axis=0))
        for d in range(2):
            q_ref, kt_ref, v_ref, row_ref = streams[d]
            mask = (col <= row) if d == 0 else (col >= row)
            for h in range(M_HEADS):
                kt = kt_ref[0, h * dk:(h + 1) * dk, blk[d]]
                v = v_ref[0, blk[d], h * dv:(h + 1) * dv]
                c_old = c_s[0, d, h]
                kw = kt.astype(F32) * row_ref[4, 0, h:h + 1, blk[d]]
                n_upd = jnp.broadcast_to(jnp.sum(kw, axis=1, keepdims=True), (dk, LANES))
                upd = jnp.concatenate([_bdot(kw.astype(BF16), v), n_upd], axis=1)
                c_s[0, d, h] = row_ref[5, 0, h:h + 1, sub[d] * L:sub[d] * L + 1] * c_old + upd
                if not with_outputs:
                    continue
                q = q_ref[0, blk[d], h * dk:(h + 1) * dk]
                cb = c_old.astype(BF16)
                j = 2 * M_HEADS * d + h
                mc = cols[:, j:j + 1]
                fl = cols[:, j + M_HEADS:j + M_HEADS + 1]
                p = jnp.exp(jnp.where(mask, row_ref[0, 0, h:h + 1, blk[d]] - mc, -jnp.inf))
                wi = jnp.exp(row_ref[2, 0, h:h + 1, blk[d]] - mc)
                sq = _bdot(q, jnp.concatenate([kt, cb[:, dv:]], axis=1))
                sp = sq[:, :L] * p
                lhs = jnp.concatenate([sp, q.astype(F32) * wi], axis=1).astype(BF16)
                num = _bdot(lhs, jnp.concatenate([v, cb[:, :dv]], axis=0))
                den = wi * sq[:, L:] + jnp.sum(sp, axis=1, keepdims=True)
                inv = 1.0 / jnp.maximum(jnp.abs(den), fl)
                h_outs[d][0, blk[d], h * dv:(h + 1) * dv] = (
                    num * jnp.concatenate([inv] * (dv // LANES), axis=1)).astype(BF16)


def _scan(q, kt, v, gates_f, gates_b, off, init):
    bsz, hk, t = kt.shape
    hv = v.shape[2]
    dk, dv = hk // M_HEADS, hv // M_HEADS
    assert dk == M_CHUNK == LANES and dv % LANES == 0
    nc = t // M_CHUNK
    cps = max(c for c in (SCAN_CHUNKS_PER_STEP, 2, 1) if nc % c == 0 and off % c == 0)
    nsteps = nc // cps
    L = cps * M_CHUNK
    with_outputs = init is not None

    def specs(cidx):
        sp = [pl.BlockSpec((1, hk, L), lambda b, s: (b, 0, cidx(s))),
              pl.BlockSpec((1, L, hv), lambda b, s: (b, cidx(s), 0)),
              pl.BlockSpec((6, 1, M_HEADS, L), lambda b, s: (0, b, 0, off // cps + cidx(s)))]
        if with_outputs:
            sp = [pl.BlockSpec((1, L, hk), lambda b, s: (b, cidx(s), 0))] + sp
        return sp

    lead = [q] if with_outputs else []
    args = lead + [kt, v, gates_f] + lead + [kt, v, gates_b]
    in_specs = specs(lambda s: s) + specs(lambda s: nsteps - 1 - s)
    state_shape = (bsz, 2, M_HEADS, dk, dv + LANES)
    state_spec = pl.BlockSpec((1,) + state_shape[1:], lambda b, s: (b, 0, 0, 0, 0))
    if with_outputs:
        args.append(init)
        in_specs.append(state_spec)
        out_specs = [pl.BlockSpec((1, L, hv), lambda b, s: (b, s, 0)),
                     pl.BlockSpec((1, L, hv), lambda b, s: (b, nsteps - 1 - s, 0))]
        out_shape = [jax.ShapeDtypeStruct((bsz, t, hv), BF16)] * 2
        scratch = [pltpu.VMEM((1,) + state_shape[1:], F32)]
    else:
        out_specs = state_spec
        out_shape = jax.ShapeDtypeStruct(state_shape, F32)
        scratch = []
    return pl.pallas_call(
        functools.partial(_scan_kernel, cps, with_outputs),
        grid=(bsz, nsteps),
        in_specs=in_specs,
        out_specs=out_specs,
        out_shape=out_shape,
        scratch_shapes=scratch,
        compiler_params=_params("parallel", "arbitrary"),
        name="mlstm_scan" if with_outputs else "mlstm_ctx_state",
    )(*args)


def _readout_core(h, sig_o, x, gate, norm_g, w_ref, post_g):
    dv = h.shape[1] // M_HEADS
    parts = []
    for hd in range(M_HEADS):
        hh = h[:, hd * dv:(hd + 1) * dv]
        parts.append(hh * lax.rsqrt(jnp.mean(hh * hh, axis=-1, keepdims=True) + EPS))
    y = (jnp.concatenate(parts, axis=-1) * norm_g * sig_o.astype(F32)).astype(BF16)
    return x + gate * _rms(_bdot(y, w_ref[...]), post_g)


def _ffn_core(fc, x, sh, sc, gate, pre_g, post_g, wu_ref, wg_ref, cw_ref, cb_ref, wd_ref, act_ref):
    tm = x.shape[0]
    f = wu_ref.shape[1]
    hb = (_rms(x, pre_g) * (1.0 + sc) + sh).astype(BF16)
    pos = lax.broadcasted_iota(jnp.int32, (tm, fc), 0) % GRID_W
    has_prev = pos != 0
    has_next = pos != GRID_W - 1
    for j in range(f // fc):
        cs = slice(j * fc, (j + 1) * fc)
        u = _bdot(hb, wu_ref[:, cs])
        g = _bdot(hb, wg_ref[:, cs])
        g_prev = jnp.where(has_prev, pltpu.roll(g, 1, 0), 0.0)
        g_next = jnp.where(has_next, pltpu.roll(g, tm - 1, 0), 0.0)
        gc = g_prev * cw_ref[0:1, cs] + g * cw_ref[1:2, cs] + g_next * cw_ref[2:3, cs] + cb_ref[:, cs]
        act_ref[:, cs] = (gc * jax.nn.sigmoid(gc) * u).astype(BF16)
    y = _bdot(act_ref[...], wd_ref[...])
    return x + gate * _rms(y, post_g)


RADIX = 4


def _fourier_channel_core(xq, sh, sc, pre_g, cs_ref, g_ref):
    gd = cs_ref.shape[0]
    hb = [(_rms(x, pre_g) * (1.0 + sc) + sh).astype(BF16) for x in xq]
    for g in range(hb[0].shape[1] // gd):
        lanes = slice(g * gd, (g + 1) * gd)
        r = [_bdot(h[:, lanes], cs_ref[...]) for h in hb]
        a = [v[:, :gd] for v in r]
        b = [v[:, gd:] for v in r]
        a02p, a02m, a13p, a13m = a[0] + a[2], a[0] - a[2], a[1] + a[3], a[1] - a[3]
        b02p, b02m, b13p, b13m = b[0] + b[2], b[0] - b[2], b[1] + b[3], b[1] - b[3]
        re_im = ((a02p + a13p, -(b02p + b13p)),
                 (a02m - b13m, -(b02m + a13m)),
                 (a02p - a13p, b13p - b02p),
                 (a02m + b13m, a13m - b02m))
        for k, (re, im) in enumerate(re_im):
            g_ref[0, k, 0, :, lanes] = re.astype(BF16)
            g_ref[0, k, 1, :, lanes] = im.astype(BF16)


def _ffn_params(layer, pre_ref, post_ref, cb_ref):
    return pre_ref[layer:layer + 1], post_ref[layer:layer + 1], cb_ref[layer:layer + 1]


def _ffn_kernel(fc, layer, x_ref, mod_ref, pre_ref, post_ref, wu_ref, wg_ref, cw_ref, cb_ref, wd_ref,
                out_ref, act_ref):
    shift, scale, gate = _mod_rows(mod_ref, pl.program_id(0))[3:]
    pre_g, post_g, cb = _ffn_params(layer, pre_ref, post_ref, cb_ref)
    out_ref[0] = _ffn_core(fc, x_ref[0], shift, scale, gate, pre_g, post_g, wu_ref, wg_ref, cw_ref, cb, wd_ref,
                           act_ref)


def _layer0_tail_kernel(fc, hf_ref, hb_ref, o_ref, x_ref, mod0_ref, mod1_ref, ng_ref, wo_ref, pmix_ref, premix_ref,
                        pre_ref, post_ref, wu_ref, wg_ref, cw_ref, cb_ref, wd_ref, cs_ref, out_ref, g_ref, act_ref):
    rows = x_ref.shape[2]
    _, _, gate1, shift2, scale2, gate2 = _mod_rows(mod0_ref, pl.program_id(0))
    next_shift, next_scale = _mod_rows(mod1_ref, pl.program_id(0))[:2]
    pre_g, post_g, cb = _ffn_params(0, pre_ref, post_ref, cb_ref)
    cat = lambda ref: jnp.concatenate([ref[0, q] for q in range(RADIX)], axis=0)
    h = cat(hf_ref).astype(F32) + cat(hb_ref).astype(F32)
    x = _readout_core(h, cat(o_ref), cat(x_ref), gate1, ng_ref[...], wo_ref, pmix_ref[0:1])
    x = _ffn_core(fc, x, shift2, scale2, gate2, pre_g, post_g, wu_ref, wg_ref, cw_ref, cb, wd_ref, act_ref)
    xq = [x[q * rows:(q + 1) * rows] for q in range(RADIX)]
    for q in range(RADIX):
        out_ref[0, q] = xq[q]
    _fourier_channel_core(xq, next_shift, next_scale, premix_ref[1:2], cs_ref, g_ref)


def _ffn_specs(layer, pre_g, post_g, d, f, cw, cb):
    def lay(rows, cols, col_block=0):
        return pl.BlockSpec((None, rows, cols), lambda b, i: (layer, 0, col_block),
                            pipeline_mode=pl.Buffered(1))
    return [_const_spec(pre_g), _const_spec(post_g), lay(d, f, 0), lay(d, f, 1), lay(cw.shape[1], f),
            _const_spec(cb), lay(f, d)]


def _ffn(x, mod, layer, pre_g, post_g, w_up, cw, cb, w_down, tm, fc):
    bsz, t, d = x.shape
    f = w_down.shape[1]
    tok = pl.BlockSpec((1, tm, d), lambda b, i: (b, i, 0))
    return pl.pallas_call(
        functools.partial(_ffn_kernel, fc, layer),
        grid=(bsz, t // tm),
        in_specs=[tok, _mod_spec(layer, *mod.shape[1:])] + _ffn_specs(layer, pre_g, post_g, d, f, cw, cb),
        out_specs=tok,
        out_shape=jax.ShapeDtypeStruct((bsz, t, d), F32),
        scratch_shapes=[pltpu.VMEM((tm, f), BF16)],
        compiler_params=_params("parallel", "parallel"),
        name="conv_ffn",
    )(x, mod, pre_g, post_g, w_up, w_up, cw, cb, w_down)


def _layer0_tail(hf, hb, sig_o, x, mod, norm_g, w_out, post_mix_g, pre_mix_g, pre_g, post_g, w_up, cw, cb, w_down,
                 cs, rows, fc):
    bsz, t, d = x.shape
    hv = hf.shape[2]
    f = w_down.shape[1]
    tq = t // RADIX
    quarters = lambda a: a.reshape(bsz, RADIX, tq, a.shape[2])
    tok = lambda n: pl.BlockSpec((1, RADIX, rows, n), lambda b, i: (b, 0, i, 0))
    out, g = pl.pallas_call(
        functools.partial(_layer0_tail_kernel, fc),
        grid=(bsz, tq // rows),
        in_specs=([tok(hv), tok(hv), tok(hv), tok(d), _mod_spec(0, *mod.shape[1:]), _mod_spec(1, *mod.shape[1:]),
                   _const_spec(norm_g), _const_spec(w_out), _const_spec(post_mix_g), _const_spec(pre_mix_g)]
                  + _ffn_specs(0, pre_g, post_g, d, f, cw, cb) + [_const_spec(cs)]),
        out_specs=[tok(d), pl.BlockSpec((1, RADIX, 2, rows, d), lambda b, i: (b, 0, 0, i, 0))],
        out_shape=[jax.ShapeDtypeStruct((bsz, RADIX, tq, d), F32),
                   jax.ShapeDtypeStruct((bsz, RADIX, 2, tq, d), BF16)],
        scratch_shapes=[pltpu.VMEM((RADIX * rows, f), BF16)],
        compiler_params=_params("parallel", "parallel"),
        name="layer0_tail",
    )(quarters(hf), quarters(hb), quarters(sig_o), quarters(x), mod, mod, norm_g, w_out, post_mix_g, pre_mix_g,
      pre_g, post_g, w_up, w_up, cw, cb, w_down, cs)
    return out.reshape(bsz, t, d), g


def _fourier_token_kernel(scale, dft_ref, g_ref, w_ref, bias_ref, post_ref, mod_ref, x_ref, out_ref,
                          x_s, out_s):
    tm = dft_ref.shape[1]
    nblk = x_s.shape[0]
    gate = _mod_rows(mod_ref, pl.program_id(0))[2]
    for c in range(nblk):
        x_s[c] = x_ref[0, :, c * LANES:(c + 1) * LANES]
    for r in range(RADIX):
        rows = pl.ds(r, tm, stride=RADIX)
        y = _bdot(dft_ref[r], g_ref[0, r]) * scale
        z = _bdot(y.astype(BF16), w_ref[...]) + bias_ref[...]
        xr = jnp.concatenate([x_s[c, rows, :] for c in range(nblk)], axis=1)
        res = xr + gate * _rms(z, post_ref[1:2])
        for c in range(nblk):
            out_s[c, rows, :] = res[:, c * LANES:(c + 1) * LANES]
    for c in range(nblk):
        out_ref[0, :, c * LANES:(c + 1) * LANES] = out_s[c]


def _fourier_token(dft, g, w, bias, post_mix_g, mod, x, scale, tm):
    bsz, t, d = x.shape
    tq = t // RADIX
    tok = pl.BlockSpec((1, RADIX * tm, d), lambda b, i: (b, i, 0))
    return pl.pallas_call(
        functools.partial(_fourier_token_kernel, scale),
        grid=(bsz, tq // tm),
        in_specs=[pl.BlockSpec((RADIX, tm, 2 * tq), lambda b, i: (0, i, 0)),
                  pl.BlockSpec((1, RADIX, 2 * tq, d), lambda b, i: (b, 0, 0, 0)),
                  _const_spec(w), _const_spec(bias), _const_spec(post_mix_g), _mod_spec(1, *mod.shape[1:]), tok],
        out_specs=tok,
        out_shape=jax.ShapeDtypeStruct((bsz, t, d), F32),
        scratch_shapes=[pltpu.VMEM((d // LANES, RADIX * tm, LANES), F32)] * 2,
        compiler_params=_params("parallel", "arbitrary"),
        name="fourier_token_dft",
    )(dft, g.reshape(bsz, RADIX, 2 * tq, d), w, bias, post_mix_g, mod, x)


def _dft_tables(t, gd):
    idx = np.arange(gd, dtype=np.int64)
    ang = 2.0 * np.pi * ((idx[:, None] * idx[None, :]) % gd).astype(np.float64) / gd
    chan = np.concatenate([np.cos(ang), np.sin(ang)], axis=1).astype(np.float32)
    tq = t // RADIX
    k = RADIX * np.arange(tq, dtype=np.int64)[None, :, None] + np.arange(RADIX, dtype=np.int64)[:, None, None]
    ang = 2.0 * np.pi * ((k * np.arange(tq, dtype=np.int64)[None, None, :]) % t).astype(np.float64) / t
    tok = np.concatenate([np.cos(ang), np.sin(ang)], axis=2).astype(np.float32)
    return jnp.asarray(chan).astype(BF16), jnp.asarray(tok).astype(BF16)


def kernel(x, c, ctx, c_ctx, ada_w, ada_b, pre_mix_g, post_mix_g, pre_ffn_g, post_ffn_g, ffn_up_w, ffn_conv_w,
           ffn_conv_b, ffn_down_w, m_in_w, m_in_b, m_norm_g, m_out_w, f_out_w, f_out_b):
    bsz, t, d = x.shape
    tc = ctx.shape[1]
    f = ffn_down_w.shape[1]
    assert ada_w.shape[0] == 2 and m_in_w.shape[0] == 1 and f_out_w.shape[0] == 1
    assert bsz + 1 <= COND_ROWS and t % M_CHUNK == 0 and tc % M_CHUNK == 0 and t % GRID_W == 0
    hv = m_out_w.shape[1]
    hk = (m_in_w.shape[2] - 2 * hv - 4 * M_HEADS) // 2
    tm = min(512, t)
    fc = 256
    assert f % fc == 0 and t % tm == 0 and tm % GRID_W == 0

    mod, w_in = _ada(c, c_ctx.reshape(1, d), ada_w, ada_b, jnp.swapaxes(m_in_w[0], 0, 1))

    b_in = jnp.pad(m_in_b, ((0, 0), (0, w_in.shape[0] - m_in_b.shape[1])))
    kc, vc, gc = _inproj(ctx, mod, bsz, pre_mix_g, w_in, b_in, hk, hv, min(tm, tc), False)
    qx, ox, kx, vx, gx, w_up, w_down, w_mo, w_fo = _inproj(
        x, mod, None, pre_mix_g, w_in, b_in, hk, hv, tm, True, convert=(ffn_up_w, ffn_down_w, m_out_w, f_out_w))
    gates_f, gates_b = _gate_prep(gc, gx)
    state = _scan(None, kc, vc, gates_f, gates_b, t // M_CHUNK, None)
    hf, hb = _scan(qx, kx, vx, gates_f, gates_b, 0, state)
    gd = d // F_GROUPS
    tq = t // RADIX
    rows, tm_tok = min(128, tq), min(256, tq)
    assert t % RADIX == 0 and tq % rows == 0 and tq % tm_tok == 0 and rows % GRID_W == 0
    chan, tok = _dft_tables(t, gd)
    x, g = _layer0_tail(hf, hb, ox, x, mod, m_norm_g, w_mo[0], post_mix_g, pre_mix_g, pre_ffn_g, post_ffn_g,
                        w_up, ffn_conv_w, ffn_conv_b, w_down, chan, rows, fc)

    x = _fourier_token(tok, g, w_fo[0], f_out_b, post_mix_g, mod, x, float(1.0 / np.sqrt(t * gd)), tm_tok)
    return _ffn(x, mod, 1, pre_ffn_g, post_ffn_g, w_up, ffn_conv_w, ffn_conv_b, w_down, tm, fc)
```

```python
import functools

import numpy as np
import jax
import jax.numpy as jnp
from jax import lax
from jax.experimental import pallas as pl
from jax.experimental.pallas import tpu as pltpu

F32 = jnp.float32
BF16 = jnp.bfloat16

M_HEADS = 4
M_CHUNK = 128
F_GROUPS = 8
GRID_W = 64
EPS = 1e-6
N_MOD = 6
COND_ROWS = 16
SCAN_CHUNKS_PER_STEP = 4

VMEM_LIMIT_BYTES = 56 * 1024 * 1024
LANES = 128


def _params(*sem):
    return pltpu.CompilerParams(dimension_semantics=sem, vmem_limit_bytes=VMEM_LIMIT_BYTES)


def _rms(x, g):
    return x * lax.rsqrt(jnp.mean(x * x, axis=-1, keepdims=True) + EPS) * g


def _bdot(a, b):
    return jnp.dot(a, b, preferred_element_type=F32)


def _mod_rows(mod_ref, row):
    m = mod_ref[pl.ds(row, 1), :]
    d = m.shape[1] // N_MOD
    return [m[:, k * d:(k + 1) * d] for k in range(N_MOD)]


def _mod_spec(layer, cond_rows, width):
    return pl.BlockSpec((None, cond_rows, width), lambda b, i: (layer, 0, 0), pipeline_mode=pl.Buffered(1))


def _const_spec(a):
    return pl.BlockSpec(a.shape, lambda b, i: (0,) * a.ndim, pipeline_mode=pl.Buffered(1))


def _ada_kernel(valid_rows, c_ref, cc_ref, w_ref, b_ref, win_ref, o_ref, win_out):
    pad = jnp.zeros((o_ref.shape[1] - c_ref.shape[0] - 1, c_ref.shape[1]), F32)
    c = jnp.concatenate([c_ref[...], cc_ref[...], pad], axis=0)
    s = c * jax.nn.sigmoid(c)
    o_ref[0] = _bdot(s.astype(BF16), w_ref[0].astype(BF16)) + b_ref[pl.ds(pl.program_id(0), 1), :]
    step = pl.program_id(0) * pl.num_programs(1) + pl.program_id(1)
    row = step * win_ref.shape[0] + lax.broadcasted_iota(jnp.int32, win_ref.shape, 0)
    win_out[...] = jnp.where(row < valid_rows, win_ref[...], 0.0).astype(BF16)


def _ada(c, c_ctx, ada_w, ada_b, w_in_t):
    depth, d, n = ada_w.shape
    tn = n // 4
    steps = depth * (n // tn)
    rows, cols = w_in_t.shape
    padded = rows + (-rows) % LANES
    blk = padded // steps
    assert padded % (16 * steps) == 0 and (steps - 1) * blk < rows
    return pl.pallas_call(
        functools.partial(_ada_kernel, rows),
        grid=(depth, n // tn),
        in_specs=[pl.BlockSpec(c.shape, lambda i, j: (0, 0)),
                  pl.BlockSpec(c_ctx.shape, lambda i, j: (0, 0)),
                  pl.BlockSpec((1, d, tn), lambda i, j: (i, 0, j)),
                  pl.BlockSpec((depth, tn), lambda i, j: (0, j)),
                  pl.BlockSpec((blk, cols), lambda i, j: (i * (n // tn) + j, 0))],
        out_specs=[pl.BlockSpec((1, COND_ROWS, tn), lambda i, j: (i, 0, j)),
                   pl.BlockSpec((blk, cols), lambda i, j: (i * (n // tn) + j, 0))],
        out_shape=[jax.ShapeDtypeStruct((depth, COND_ROWS, n), F32),
                   jax.ShapeDtypeStruct((padded, cols), BF16)],
        compiler_params=_params("arbitrary", "arbitrary"),
        name="ada",
    )(c, c_ctx, ada_w, ada_b, w_in_t)


def _inproj_kernel(nconv, with_qo, cond_row, x_ref, mod_ref, g_ref, w_ref, b_ref, *refs):
    nout = 5 if with_qo else 3
    conv_in, outs, conv_out = refs[:nconv], refs[nconv:nconv + nout], refs[nconv + nout:]
    kt_out, v_out, gate_out = outs[-3:]
    hk, hv = kt_out.shape[1], v_out.shape[2]
    dk = hk // M_HEADS
    shift, scale = _mod_rows(mod_ref, pl.program_id(0) if cond_row is None else cond_row)[:2]
    h = _rms(x_ref[0], g_ref[0:1]) * (1.0 + scale) + shift
    hb = h.astype(BF16)

    def proj(lo, hi):
        return lax.dot_general(hb, w_ref[lo:hi, :], (((1,), (1,)), ((), ())),
                               preferred_element_type=F32) + b_ref[:, lo:hi]

    if with_qo:
        q_out, o_out = outs[:2]
        q_out[0] = proj(0, hk).astype(BF16)
        o_out[0] = jax.nn.sigmoid(proj(2 * hk + hv, 2 * hk + 2 * hv)).astype(BF16)
    kt_out[0] = jnp.transpose(proj(hk, 2 * hk) * (dk ** -0.5)).astype(BF16)
    v_out[0] = proj(2 * hk, 2 * hk + hv).astype(BF16)
    gate_out[0] = jnp.transpose(proj(2 * hk + 2 * hv, w_ref.shape[0]))[:gate_out.shape[1]]
    for src_ref, dst_ref in zip(conv_in, conv_out):
        dst_ref[...] = src_ref[...].astype(BF16)


def _inproj(x, mod, cond_row, g, w, bias, hk, hv, tm, with_qo, convert=()):
    bsz, t, d = x.shape
    nt = t // tm
    full = lambda a: pl.BlockSpec(a.shape, lambda b, i: (0, 0))
    tok = lambda n: pl.BlockSpec((1, tm, n), lambda b, i: (b, i, 0))
    tok_t = lambda n: pl.BlockSpec((1, n, tm), lambda b, i: (b, 0, i))
    flat = [a.reshape(-1, a.shape[-1]) for a in convert]
    conv_specs = [pl.BlockSpec((a.shape[0] // (bsz * nt), a.shape[1]), lambda b, i: (b * nt + i, 0)) for a in flat]
    assert all(a.shape[0] % (16 * bsz * nt) == 0 for a in flat)
    qo_specs = [tok(hk), tok(hv)] if with_qo else []
    qo_shapes = [jax.ShapeDtypeStruct((bsz, t, hk), BF16), jax.ShapeDtypeStruct((bsz, t, hv), BF16)] if with_qo else []
    outs = pl.pallas_call(
        functools.partial(_inproj_kernel, len(flat), with_qo, cond_row),
        grid=(bsz, nt),
        in_specs=[tok(d), _mod_spec(0, *mod.shape[1:]), full(g), full(w), full(bias)] + conv_specs,
        out_specs=qo_specs + [tok_t(hk), tok(hv), tok_t(4 * M_HEADS)] + conv_specs,
        out_shape=qo_shapes + [jax.ShapeDtypeStruct((bsz, hk, t), BF16), jax.ShapeDtypeStruct((bsz, t, hv), BF16),
                               jax.ShapeDtypeStruct((bsz, 4 * M_HEADS, t), F32)]
                  + [jax.ShapeDtypeStruct(a.shape, BF16) for a in flat],
        compiler_params=_params("parallel", "parallel"),
        name="mlstm_inproj",
    )(x, mod, g, w, bias, *flat)
    nout = len(outs) - len(flat)
    return list(outs[:nout]) + [o.reshape(a.shape) for o, a in zip(outs[nout:], convert)]


def _lane_scan(x, op, fill, reverse, seg):
    n = x.shape[-1]
    lane = lax.broadcasted_iota(jnp.int32, x.shape, x.ndim - 1) % seg
    d = 1
    while d < seg:
        if reverse:
            shifted = jnp.where(lane < seg - d, pltpu.roll(x, n - d, x.ndim - 1), fill)
        else:
            shifted = jnp.where(lane >= d, pltpu.roll(x, d, x.ndim - 1), fill)
        x = op(x, shifted)
        d *= 2
    return x


def _log_sigmoid(x):
    return jnp.minimum(x, 0.0) - jnp.log1p(jnp.exp(-jnp.abs(x)))


def _gate_kernel(orders, gf_ref, gb_ref, of_ref, ob_ref):
    L = M_CHUNK
    for d, (g_ref, o_ref) in enumerate(((gf_ref, of_ref), (gb_ref, ob_ref))):
        edge = L - 1 if d == 0 else 0
        lf = _log_sigmoid(g_ref[:, 1])
        b_all = _lane_scan(lf, jnp.add, 0.0, d == 1, L)
        r1_all = g_ref[:, 0] - b_all
        cm_all = _lane_scan(r1_all, jnp.maximum, -jnp.inf, d == 1, L)
        m = jnp.zeros(lf.shape[:-1] + (1,), F32)
        for c in orders[d]:
            lanes = slice(c * L, (c + 1) * L)
            b, r1, cm = b_all[..., lanes], r1_all[..., lanes], cm_all[..., lanes]
            mx = jnp.maximum(m, cm[..., edge:edge + 1])
            mc = jnp.maximum(m, cm)
            o_ref[0, :, :, lanes] = r1
            o_ref[1, :, :, lanes] = mc
            o_ref[2, :, :, lanes] = jnp.broadcast_to(m, r1.shape)
            o_ref[3, :, :, lanes] = jnp.exp(-(b + mc))
            o_ref[4, :, :, lanes] = jnp.exp(r1 - mx)
            o_ref[5, :, :, lanes] = jnp.broadcast_to(jnp.exp(m - mx), r1.shape)
            m = b[..., edge:edge + 1] + mx


def _gate_prep(gates_c, gates_x):
    bsz = gates_c.shape[0]
    nlt, nct = gates_x.shape[2] // M_CHUNK, gates_c.shape[2] // M_CHUNK
    n = (nlt + nct) * M_CHUNK
    fwd = tuple(range(nlt, nlt + nct)) + tuple(range(nlt))
    bwd = tuple(range(nlt + nct - 1, nlt - 1, -1)) + tuple(range(nlt - 1, -1, -1))
    gf = gb = jnp.concatenate([gates_x, gates_c], axis=2).reshape(bsz, 4, M_HEADS, n)
    out_spec = pl.BlockSpec((6, bsz, M_HEADS, n), lambda j: (0, 0, 0, 0))
    return pl.pallas_call(
        functools.partial(_gate_kernel, (fwd, bwd)),
        grid=(1,),
        in_specs=[pl.BlockSpec((bsz, 2, M_HEADS, n), lambda j: (0, 0, 0, 0)),
                  pl.BlockSpec((bsz, 2, M_HEADS, n), lambda j: (0, 1, 0, 0))],
        out_specs=[out_spec, out_spec],
        out_shape=[jax.ShapeDtypeStruct((6, bsz, M_HEADS, n), F32)] * 2,
        compiler_params=_params("arbitrary"),
        name="mlstm_gates",
    )(gf, gb)


def _scan_kernel(cps, with_outputs, *refs):
    if with_outputs:
        (qf, ktf, vf, rf, qb, ktb, vb, rb, c0, hf_out, hb_out, c_s) = refs
        h_outs = (hf_out, hb_out)
        streams = ((qf, ktf, vf, rf), (qb, ktb, vb, rb))
    else:
        (ktf, vf, rf, ktb, vb, rb, c_s) = refs
        streams = ((None, ktf, vf, rf), (None, ktb, vb, rb))
    s = pl.program_id(1)
    L = M_CHUNK
    dk = ktf.shape[1] // M_HEADS
    dv = vf.shape[2] // M_HEADS

    @pl.when(s == 0)
    def _():
        if with_outputs:
            c_s[...] = c0[...]
        else:
            c_s[...] = jnp.zeros(c_s.shape, F32)

    row = lax.broadcasted_iota(jnp.int32, (L, L), 0)
    col = lax.broadcasted_iota(jnp.int32, (L, L), 1)
    for cc in range(cps):
        sub = (cc, cps - 1 - cc)
        blk = [slice(i * L, (i + 1) * L) for i in sub]
        if with_outputs:
            stack = [r_ref[j, 0, :, blk[d]] for d, r_ref in enumerate((rf, rb)) for j in (1, 3)]
            stack.append(jnp.zeros((L - 4 * M_HEADS, L), F32))
            cols = jnp.transpose(jnp.concatenate(stack, axis=0))
        for d in range(2):
            q_ref, kt_ref, v_ref, row_ref = streams[d]
            mask = (col <= row) if d == 0 else (col >= row)
            for h in range(M_HEADS):
                kt = kt_ref[0, h * dk:(h + 1) * dk, blk[d]]
                v = v_ref[0, blk[d], h * dv:(h + 1) * dv]
                c_old = c_s[0, d, h]
                kw = kt.astype(F32) * row_ref[4, 0, h:h + 1, blk[d]]
                n_upd = jnp.broadcast_to(jnp.sum(kw, axis=1, keepdims=True), (dk, LANES))
                upd = jnp.concatenate([_bdot(kw.astype(BF16), v), n_upd], axis=1)
                c_s[0, d, h] = row_ref[5, 0, h:h + 1, sub[d] * L:sub[d] * L + 1] * c_old + upd
                if not with_outputs:
                    continue
                q = q_ref[0, blk[d], h * dk:(h + 1) * dk]
                cb = c_old.astype(BF16)
                j = 2 * M_HEADS * d + h
                mc = cols[:, j:j + 1]
                fl = cols[:, j + M_HEADS:j + M_HEADS + 1]
                p = jnp.exp(jnp.where(mask, row_ref[0, 0, h:h + 1, blk[d]] - mc, -jnp.inf))
                wi = jnp.exp(row_ref[2, 0, h:h + 1, blk[d]] - mc)
                sq = _bdot(q, jnp.concatenate([kt, cb[:, dv:]], axis=1))
                sp = sq[:, :L] * p
                lhs = jnp.concatenate([sp, q.astype(F32) * wi], axis=1).astype(BF16)
                num = _bdot(lhs, jnp.concatenate([v, cb[:, :dv]], axis=0))
                den = wi * sq[:, L:] + jnp.sum(sp, axis=1, keepdims=True)
                inv = 1.0 / jnp.maximum(jnp.abs(den), fl)
                h_outs[d][0, blk[d], h * dv:(h + 1) * dv] = (
                    num * jnp.concatenate([inv] * (dv // LANES), axis=1)).astype(BF16)


def _scan(q, kt, v, gates_f, gates_b, off, init):
    bsz, hk, t = kt.shape
    hv = v.shape[2]
    dk, dv = hk // M_HEADS, hv // M_HEADS
    assert dk == M_CHUNK == LANES and dv % LANES == 0
    nc = t // M_CHUNK
    cps = max(c for c in (SCAN_CHUNKS_PER_STEP, 2, 1) if nc % c == 0 and off % c == 0)
    nsteps = nc // cps
    L = cps * M_CHUNK
    with_outputs = init is not None

    def specs(cidx):
        sp = [pl.BlockSpec((1, hk, L), lambda b, s: (b, 0, cidx(s))),
              pl.BlockSpec((1, L, hv), lambda b, s: (b, cidx(s), 0)),
              pl.BlockSpec((6, 1, M_HEADS, L), lambda b, s: (0, b, 0, off // cps + cidx(s)))]
        if with_outputs:
            sp = [pl.BlockSpec((1, L, hk), lambda b, s: (b, cidx(s), 0))] + sp
        return sp

    lead = [q] if with_outputs else []
    args = lead + [kt, v, gates_f] + lead + [kt, v, gates_b]
    in_specs = specs(lambda s: s) + specs(lambda s: nsteps - 1 - s)
    state_shape = (bsz, 2, M_HEADS, dk, dv + LANES)
    state_spec = pl.BlockSpec((1,) + state_shape[1:], lambda b, s: (b, 0, 0, 0, 0))
    if with_outputs:
        args.append(init)
        in_specs.append(state_spec)
        out_specs = [pl.BlockSpec((1, L, hv), lambda b, s: (b, s, 0)),
                     pl.BlockSpec((1, L, hv), lambda b, s: (b, nsteps - 1 - s, 0))]
        out_shape = [jax.ShapeDtypeStruct((bsz, t, hv), BF16)] * 2
        scratch = [pltpu.VMEM((1,) + state_shape[1:], F32)]
    else:
        out_specs = state_spec
        out_shape = jax.ShapeDtypeStruct(state_shape, F32)
        scratch = []
    return pl.pallas_call(
        functools.partial(_scan_kernel, cps, with_outputs),
        grid=(bsz, nsteps),
        in_specs=in_specs,
        out_specs=out_specs,
        out_shape=out_shape,
        scratch_shapes=scratch,
        compiler_params=_params("parallel", "arbitrary"),
        name="mlstm_scan" if with_outputs else "mlstm_ctx_state",
    )(*args)


def _readout_core(h, sig_o, x, gate, norm_g, w_ref, post_g):
    dv = h.shape[1] // M_HEADS
    parts = []
    for hd in range(M_HEADS):
        hh = h[:, hd * dv:(hd + 1) * dv]
        parts.append(hh * lax.rsqrt(jnp.mean(hh * hh, axis=-1, keepdims=True) + EPS))
    y = (jnp.concatenate(parts, axis=-1) * norm_g * sig_o.astype(F32)).astype(BF16)
    return x + gate * _rms(_bdot(y, w_ref[...]), post_g)


def _ffn_core(fc, x, sh, sc, gate, pre_g, post_g, wu_ref, wg_ref, cw_ref, cb_ref, wd_ref, act_ref):
    tm = x.shape[0]
    f = wu_ref.shape[1]
    hb = (_rms(x, pre_g) * (1.0 + sc) + sh).astype(BF16)
    pos = lax.broadcasted_iota(jnp.int32, (tm, fc), 0) % GRID_W
    has_prev = pos != 0
    has_next = pos != GRID_W - 1
    for j in range(f // fc):
        cs = slice(j * fc, (j + 1) * fc)
        u = _bdot(hb, wu_ref[:, cs])
        g = _bdot(hb, wg_ref[:, cs])
        g_prev = jnp.where(has_prev, pltpu.roll(g, 1, 0), 0.0)
        g_next = jnp.where(has_next, pltpu.roll(g, tm - 1, 0), 0.0)
        gc = g_prev * cw_ref[0:1, cs] + g * cw_ref[1:2, cs] + g_next * cw_ref[2:3, cs] + cb_ref[:, cs]
        act_ref[:, cs] = (gc * jax.nn.sigmoid(gc) * u).astype(BF16)
    y = _bdot(act_ref[...], wd_ref[...])
    return x + gate * _rms(y, post_g)


RADIX = 4


def _fourier_channel_core(xq, sh, sc, pre_g, cs_ref, g_ref, rs=slice(None)):
    gd = cs_ref.shape[0]
    hb = [(_rms(x, pre_g) * (1.0 + sc) + sh).astype(BF16) for x in xq]
    for g in range(hb[0].shape[1] // gd):
        lanes = slice(g * gd, (g + 1) * gd)
        r = [_bdot(h[:, lanes], cs_ref[...]) for h in hb]
        a = [v[:, :gd] for v in r]
        b = [v[:, gd:] for v in r]
        a02p, a02m, a13p, a13m = a[0] + a[2], a[0] - a[2], a[1] + a[3], a[1] - a[3]
        b02p, b02m, b13p, b13m = b[0] + b[2], b[0] - b[2], b[1] + b[3], b[1] - b[3]
        re_im = ((a02p + a13p, -(b02p + b13p)),
                 (a02m - b13m, -(b02m + a13m)),
                 (a02p - a13p, b13p - b02p),
                 (a02m + b13m, a13m - b02m))
        for k, (re, im) in enumerate(re_im):
            g_ref[0, k, 0, rs, lanes] = re.astype(BF16)
            g_ref[0, k, 1, rs, lanes] = im.astype(BF16)


def _ffn_params(layer, pre_ref, post_ref, cb_ref):
    return pre_ref[layer:layer + 1], post_ref[layer:layer + 1], cb_ref[layer:layer + 1]


def _ffn_kernel(fc, layer, x_ref, mod_ref, pre_ref, post_ref, wu_ref, wg_ref, cw_ref, cb_ref, wd_ref,
                out_ref, act_ref):
    shift, scale, gate = _mod_rows(mod_ref, pl.program_id(0))[3:]
    pre_g, post_g, cb = _ffn_params(layer, pre_ref, post_ref, cb_ref)
    half = x_ref.shape[1] // 2
    for s in range(2):
        rs = slice(s * half, (s + 1) * half)
        out_ref[0, rs] = _ffn_core(fc, x_ref[0, rs], shift, scale, gate, pre_g, post_g, wu_ref, wg_ref, cw_ref, cb,
                                   wd_ref, act_ref.at[s])


def _layer0_tail_kernel(fc, hf_ref, hb_ref, o_ref, x_ref, mod0_ref, mod1_ref, ng_ref, wo_ref, pmix_ref, premix_ref,
                        pre_ref, post_ref, wu_ref, wg_ref, cw_ref, cb_ref, wd_ref, cs_ref, out_ref, g_ref, act_ref):
    rows = x_ref.shape[2]
    _, _, gate1, shift2, scale2, gate2 = _mod_rows(mod0_ref, pl.program_id(0))
    next_shift, next_scale = _mod_rows(mod1_ref, pl.program_id(0))[:2]
    pre_g, post_g, cb = _ffn_params(0, pre_ref, post_ref, cb_ref)
    sub = rows // 2
    slices = [slice(s * sub, (s + 1) * sub) for s in range(2)]
    xs = []
    for rs in slices:
        cat = lambda ref: jnp.concatenate([ref[0, q, rs] for q in range(RADIX)], axis=0)
        h = cat(hf_ref).astype(F32) + cat(hb_ref).astype(F32)
        xs.append(_readout_core(h, cat(o_ref), cat(x_ref), gate1, ng_ref[...], wo_ref, pmix_ref[0:1]))
    for s in range(2):
        xs[s] = _ffn_core(fc, xs[s], shift2, scale2, gate2, pre_g, post_g, wu_ref, wg_ref, cw_ref, cb, wd_ref,
                          act_ref.at[s])
    for s, rs in enumerate(slices):
        xq = [xs[s][q * sub:(q + 1) * sub] for q in range(RADIX)]
        for q in range(RADIX):
            out_ref[0, q, rs] = xq[q]
        _fourier_channel_core(xq, next_shift, next_scale, premix_ref[1:2], cs_ref, g_ref, rs)


def _ffn_specs(layer, pre_g, post_g, d, f, cw, cb):
    def lay(rows, cols, col_block=0):
        return pl.BlockSpec((None, rows, cols), lambda b, i: (layer, 0, col_block),
                            pipeline_mode=pl.Buffered(1))
    return [_const_spec(pre_g), _const_spec(post_g), lay(d, f, 0), lay(d, f, 1), lay(cw.shape[1], f),
            _const_spec(cb), lay(f, d)]


def _ffn(x, mod, layer, pre_g, post_g, w_up, cw, cb, w_down, tm, fc):
    bsz, t, d = x.shape
    f = w_down.shape[1]
    tok = pl.BlockSpec((1, tm, d), lambda b, i: (b, i, 0))
    return pl.pallas_call(
        functools.partial(_ffn_kernel, fc, layer),
        grid=(bsz, t // tm),
        in_specs=[tok, _mod_spec(layer, *mod.shape[1:])] + _ffn_specs(layer, pre_g, post_g, d, f, cw, cb),
        out_specs=tok,
        out_shape=jax.ShapeDtypeStruct((bsz, t, d), F32),
        scratch_shapes=[pltpu.VMEM((2, tm // 2, f), BF16)],
        compiler_params=_params("parallel", "parallel"),
        name="conv_ffn",
    )(x, mod, pre_g, post_g, w_up, w_up, cw, cb, w_down)


def _layer0_tail(hf, hb, sig_o, x, mod, norm_g, w_out, post_mix_g, pre_mix_g, pre_g, post_g, w_up, cw, cb, w_down,
                 cs, rows, fc):
    bsz, t, d = x.shape
    hv = hf.shape[2]
    f = w_down.shape[1]
    tq = t // RADIX
    quarters = lambda a: a.reshape(bsz, RADIX, tq, a.shape[2])
    tok = lambda n: pl.BlockSpec((1, RADIX, rows, n), lambda b, i: (b, 0, i, 0))
    out, g = pl.pallas_call(
        functools.partial(_layer0_tail_kernel, fc),
        grid=(bsz, tq // rows),
        in_specs=([tok(hv), tok(hv), tok(hv), tok(d), _mod_spec(0, *mod.shape[1:]), _mod_spec(1, *mod.shape[1:]),
                   _const_spec(norm_g), _const_spec(w_out), _const_spec(post_mix_g), _const_spec(pre_mix_g)]
                  + _ffn_specs(0, pre_g, post_g, d, f, cw, cb) + [_const_spec(cs)]),
        out_specs=[tok(d), pl.BlockSpec((1, RADIX, 2, rows, d), lambda b, i: (b, 0, 0, i, 0))],
        out_shape=[jax.ShapeDtypeStruct((bsz, RADIX, tq, d), F32),
                   jax.ShapeDtypeStruct((bsz, RADIX, 2, tq, d), BF16)],
        scratch_shapes=[pltpu.VMEM((2, RADIX * rows // 2, f), BF16)],
        compiler_params=_params("parallel", "parallel"),
        name="layer0_tail",
    )(quarters(hf), quarters(hb), quarters(sig_o), quarters(x), mod, mod, norm_g, w_out, post_mix_g, pre_mix_g,
      pre_g, post_g, w_up, w_up, cw, cb, w_down, cs)
    return out.reshape(bsz, t, d), g


def _fourier_token_kernel(scale, dft_ref, g_ref, w_ref, bias_ref, post_ref, mod_ref, x_ref, out_ref,
                          x_s, out_s):
    tm = dft_ref.shape[1]
    nblk = x_s.shape[0]
    gate = _mod_rows(mod_ref, pl.program_id(0))[2]
    for c in range(nblk):
        x_s[c] = x_ref[0, :, c * LANES:(c + 1) * LANES]
    for r in range(RADIX):
        rows = pl.ds(r, tm, stride=RADIX)
        y = _bdot(dft_ref[r], g_ref[0, r]) * scale
        z = _bdot(y.astype(BF16), w_ref[...]) + bias_ref[...]
        xr = jnp.concatenate([x_s[c, rows, :] for c in range(nblk)], axis=1)
        res = xr + gate * _rms(z, post_ref[1:2])
        for c in range(nblk):
            out_s[c, rows, :] = res[:, c * LANES:(c + 1) * LANES]
    for c in range(nblk):
        out_ref[0, :, c * LANES:(c + 1) * LANES] = out_s[c]


def _fourier_token(dft, g, w, bias, post_mix_g, mod, x, scale, tm):
    bsz, t, d = x.shape
    tq = t // RADIX
    tok = pl.BlockSpec((1, RADIX * tm, d), lambda b, i: (b, i, 0))
    return pl.pallas_call(
        functools.partial(_fourier_token_kernel, scale),
        grid=(bsz, tq // tm),
        in_specs=[pl.BlockSpec((RADIX, tm, 2 * tq), lambda b, i: (0, i, 0)),
                  pl.BlockSpec((1, RADIX, 2 * tq, d), lambda b, i: (b, 0, 0, 0)),
                  _const_spec(w), _const_spec(bias), _const_spec(post_mix_g), _mod_spec(1, *mod.shape[1:]), tok],
        out_specs=tok,
        out_shape=jax.ShapeDtypeStruct((bsz, t, d), F32),
        scratch_shapes=[pltpu.VMEM((d // LANES, RADIX * tm, LANES), F32)] * 2,
        compiler_params=_params("parallel", "arbitrary"),
        name="fourier_token_dft",
    )(dft, g.reshape(bsz, RADIX, 2 * tq, d), w, bias, post_mix_g, mod, x)


def _dft_tables(t, gd):
    idx = np.arange(gd, dtype=np.int64)
    ang = 2.0 * np.pi * ((idx[:, None] * idx[None, :]) % gd).astype(np.float64) / gd
    chan = np.concatenate([np.cos(ang), np.sin(ang)], axis=1).astype(np.float32)
    tq = t // RADIX
    k = RADIX * np.arange(tq, dtype=np.int64)[None, :, None] + np.arange(RADIX, dtype=np.int64)[:, None, None]
    ang = 2.0 * np.pi * ((k * np.arange(tq, dtype=np.int64)[None, None, :]) % t).astype(np.float64) / t
    tok = np.concatenate([np.cos(ang), np.sin(ang)], axis=2).astype(np.float32)
    return jnp.asarray(chan).astype(BF16), jnp.asarray(tok).astype(BF16)


def kernel(x, c, ctx, c_ctx, ada_w, ada_b, pre_mix_g, post_mix_g, pre_ffn_g, post_ffn_g, ffn_up_w, ffn_conv_w,
           ffn_conv_b, ffn_down_w, m_in_w, m_in_b, m_norm_g, m_out_w, f_out_w, f_out_b):
    bsz, t, d = x.shape
    tc = ctx.shape[1]
    f = ffn_down_w.shape[1]
    assert ada_w.shape[0] == 2 and m_in_w.shape[0] == 1 and f_out_w.shape[0] == 1
    assert bsz + 1 <= COND_ROWS and t % M_CHUNK == 0 and tc % M_CHUNK == 0 and t % GRID_W == 0
    hv = m_out_w.shape[1]
    hk = (m_in_w.shape[2] - 2 * hv - 4 * M_HEADS) // 2
    tm = min(512, t)
    fc = 256
    assert f % fc == 0 and t % tm == 0 and tm % GRID_W == 0

    mod, w_in = _ada(c, c_ctx.reshape(1, d), ada_w, ada_b, jnp.swapaxes(m_in_w[0], 0, 1))

    b_in = jnp.pad(m_in_b, ((0, 0), (0, w_in.shape[0] - m_in_b.shape[1])))
    kc, vc, gc = _inproj(ctx, mod, bsz, pre_mix_g, w_in, b_in, hk, hv, min(tm, tc), False)
    qx, ox, kx, vx, gx, w_up, w_down, w_mo, w_fo = _inproj(
        x, mod, None, pre_mix_g, w_in, b_in, hk, hv, tm, True, convert=(ffn_up_w, ffn_down_w, m_out_w, f_out_w))
    gates_f, gates_b = _gate_prep(gc, gx)
    state = _scan(None, kc, vc, gates_f, gates_b, t // M_CHUNK, None)
    hf, hb = _scan(qx, kx, vx, gates_f, gates_b, 0, state)
    gd = d // F_GROUPS
    tq = t // RADIX
    rows, tm_tok = min(128, tq), min(256, tq)
    assert t % RADIX == 0 and tq % rows == 0 and tq % tm_tok == 0 and rows % GRID_W == 0
    chan, tok = _dft_tables(t, gd)
    x, g = _layer0_tail(hf, hb, ox, x, mod, m_norm_g, w_mo[0], post_mix_g, pre_mix_g, pre_ffn_g, post_ffn_g,
                        w_up, ffn_conv_w, ffn_conv_b, w_down, chan, rows, fc)

    x = _fourier_token(tok, g, w_fo[0], f_out_b, post_mix_g, mod, x, float(1.0 / np.sqrt(t * gd)), tm_tok)
    return _ffn(x, mod, 1, pre_ffn_g, post_ffn_g, w_up, ffn_conv_w, ffn_conv_b, w_down, tm, fc)
```

```python
import functools

import numpy as np
import jax
import jax.numpy as jnp
from jax import lax
from jax.experimental import pallas as pl
from jax.experimental.pallas import tpu as pltpu

F32 = jnp.float32
BF16 = jnp.bfloat16

M_HEADS = 4
M_CHUNK = 128
F_GROUPS = 8
GRID_W = 64
EPS = 1e-6
N_MOD = 6
COND_ROWS = 16
SCAN_CHUNKS_PER_STEP = 4

VMEM_LIMIT_BYTES = 56 * 1024 * 1024
LANES = 128


def _params(*sem):
    return pltpu.CompilerParams(dimension_semantics=sem, vmem_limit_bytes=VMEM_LIMIT_BYTES)


def _rms(x, g):
    return x * lax.rsqrt(jnp.mean(x * x, axis=-1, keepdims=True) + EPS) * g


def _bdot(a, b):
    return jnp.dot(a, b, preferred_element_type=F32)


def _mod_rows(mod_ref, row):
    m = mod_ref[pl.ds(row, 1), :]
    d = m.shape[1] // N_MOD
    return [m[:, k * d:(k + 1) * d] for k in range(N_MOD)]


def _mod_spec(layer, cond_rows, width):
    return pl.BlockSpec((None, cond_rows, width), lambda b, i: (layer, 0, 0), pipeline_mode=pl.Buffered(1))


def _const_spec(a):
    return pl.BlockSpec(a.shape, lambda b, i: (0,) * a.ndim, pipeline_mode=pl.Buffered(1))


def _ada_kernel(valid_rows, c_ref, cc_ref, w_ref, b_ref, win_ref, o_ref, win_out):
    pad = jnp.zeros((o_ref.shape[1] - c_ref.shape[0] - 1, c_ref.shape[1]), F32)
    c = jnp.concatenate([c_ref[...], cc_ref[...], pad], axis=0)
    s = c * jax.nn.sigmoid(c)
    o_ref[0] = _bdot(s.astype(BF16), w_ref[0].astype(BF16)) + b_ref[pl.ds(pl.program_id(0), 1), :]
    step = pl.program_id(0) * pl.num_programs(1) + pl.program_id(1)
    row = step * win_ref.shape[0] + lax.broadcasted_iota(jnp.int32, win_ref.shape, 0)
    win_out[...] = jnp.where(row < valid_rows, win_ref[...], 0.0).astype(BF16)


def _ada(c, c_ctx, ada_w, ada_b, w_in_t):
    depth, d, n = ada_w.shape
    tn = n // 4
    steps = depth * (n // tn)
    rows, cols = w_in_t.shape
    padded = rows + (-rows) % LANES
    blk = padded // steps
    assert padded % (16 * steps) == 0 and (steps - 1) * blk < rows
    return pl.pallas_call(
        functools.partial(_ada_kernel, rows),
        grid=(depth, n // tn),
        in_specs=[pl.BlockSpec(c.shape, lambda i, j: (0, 0)),
                  pl.BlockSpec(c_ctx.shape, lambda i, j: (0, 0)),
                  pl.BlockSpec((1, d, tn), lambda i, j: (i, 0, j)),
                  pl.BlockSpec((depth, tn), lambda i, j: (0, j)),
                  pl.BlockSpec((blk, cols), lambda i, j: (i * (n // tn) + j, 0))],
        out_specs=[pl.BlockSpec((1, COND_ROWS, tn), lambda i, j: (i, 0, j)),
                   pl.BlockSpec((blk, cols), lambda i, j: (i * (n // tn) + j, 0))],
        out_shape=[jax.ShapeDtypeStruct((depth, COND_ROWS, n), F32),
                   jax.ShapeDtypeStruct((padded, cols), BF16)],
        compiler_params=_params("arbitrary", "arbitrary"),
        name="ada",
    )(c, c_ctx, ada_w, ada_b, w_in_t)


def _inproj_kernel(nconv, with_qo, cond_row, x_ref, mod_ref, g_ref, w_ref, b_ref, *refs):
    nout = 5 if with_qo else 3
    conv_in, outs, conv_out = refs[:nconv], refs[nconv:nconv + nout], refs[nconv + nout:]
    kt_out, v_out, gate_out = outs[-3:]
    hk, hv = kt_out.shape[1], v_out.shape[2]
    dk = hk // M_HEADS
    shift, scale = _mod_rows(mod_ref, pl.program_id(0) if cond_row is None else cond_row)[:2]
    h = _rms(x_ref[0], g_ref[0:1]) * (1.0 + scale) + shift
    hb = h.astype(BF16)

    def proj(lo, hi):
        return lax.dot_general(hb, w_ref[lo:hi, :], (((1,), (1,)), ((), ())),
                               preferred_element_type=F32) + b_ref[:, lo:hi]

    if with_qo:
        q_out, o_out = outs[:2]
        q_out[0] = proj(0, hk).astype(BF16)
        o_out[0] = jax.nn.sigmoid(proj(2 * hk + hv, 2 * hk + 2 * hv)).astype(BF16)
    kt_out[0] = jnp.transpose(proj(hk, 2 * hk) * (dk ** -0.5)).astype(BF16)
    v_out[0] = proj(2 * hk, 2 * hk + hv).astype(BF16)
    gate_out[0] = jnp.transpose(proj(2 * hk + 2 * hv, w_ref.shape[0]))[:gate_out.shape[1]]
    for src_ref, dst_ref in zip(conv_in, conv_out):
        dst_ref[...] = src_ref[...].astype(BF16)


def _inproj(x, mod, cond_row, g, w, bias, hk, hv, tm, with_qo, convert=()):
    bsz, t, d = x.shape
    nt = t // tm
    full = lambda a: pl.BlockSpec(a.shape, lambda b, i: (0, 0))
    tok = lambda n: pl.BlockSpec((1, tm, n), lambda b, i: (b, i, 0))
    tok_t = lambda n: pl.BlockSpec((1, n, tm), lambda b, i: (b, 0, i))
    flat = [a.reshape(-1, a.shape[-1]) for a in convert]
    conv_specs = [pl.BlockSpec((a.shape[0] // (bsz * nt), a.shape[1]), lambda b, i: (b * nt + i, 0)) for a in flat]
    assert all(a.shape[0] % (16 * bsz * nt) == 0 for a in flat)
    qo_specs = [tok(hk), tok(hv)] if with_qo else []
    qo_shapes = [jax.ShapeDtypeStruct((bsz, t, hk), BF16), jax.ShapeDtypeStruct((bsz, t, hv), BF16)] if with_qo else []
    outs = pl.pallas_call(
        functools.partial(_inproj_kernel, len(flat), with_qo, cond_row),
        grid=(bsz, nt),
        in_specs=[tok(d), _mod_spec(0, *mod.shape[1:]), full(g), full(w), full(bias)] + conv_specs,
        out_specs=qo_specs + [tok_t(hk), tok(hv), tok_t(4 * M_HEADS)] + conv_specs,
        out_shape=qo_shapes + [jax.ShapeDtypeStruct((bsz, hk, t), BF16), jax.ShapeDtypeStruct((bsz, t, hv), BF16),
                               jax.ShapeDtypeStruct((bsz, 4 * M_HEADS, t), F32)]
                  + [jax.ShapeDtypeStruct(a.shape, BF16) for a in flat],
        compiler_params=_params("parallel", "parallel"),
        name="mlstm_inproj",
    )(x, mod, g, w, bias, *flat)
    nout = len(outs) - len(flat)
    return list(outs[:nout]) + [o.reshape(a.shape) for o, a in zip(outs[nout:], convert)]


def _lane_scan(x, op, fill, reverse, seg):
    n = x.shape[-1]
    lane = lax.broadcasted_iota(jnp.int32, x.shape, x.ndim - 1) % seg
    d = 1
    while d < seg:
        if reverse:
            shifted = jnp.where(lane < seg - d, pltpu.roll(x, n - d, x.ndim - 1), fill)
        else:
            shifted = jnp.where(lane >= d, pltpu.roll(x, d, x.ndim - 1), fill)
        x = op(x, shifted)
        d *= 2
    return x


def _log_sigmoid(x):
    return jnp.minimum(x, 0.0) - jnp.log1p(jnp.exp(-jnp.abs(x)))


def _gate_kernel(orders, gf_ref, gb_ref, of_ref, ob_ref):
    L = M_CHUNK
    for d, (g_ref, o_ref) in enumerate(((gf_ref, of_ref), (gb_ref, ob_ref))):
        edge = L - 1 if d == 0 else 0
        lf = _log_sigmoid(g_ref[:, 1])
        b_all = _lane_scan(lf, jnp.add, 0.0, d == 1, L)
        r1_all = g_ref[:, 0] - b_all
        cm_all = _lane_scan(r1_all, jnp.maximum, -jnp.inf, d == 1, L)
        m = jnp.zeros(lf.shape[:-1] + (1,), F32)
        for c in orders[d]:
            lanes = slice(c * L, (c + 1) * L)
            b, r1, cm = b_all[..., lanes], r1_all[..., lanes], cm_all[..., lanes]
            mx = jnp.maximum(m, cm[..., edge:edge + 1])
            mc = jnp.maximum(m, cm)
            o_ref[0, :, :, lanes] = r1
            o_ref[1, :, :, lanes] = mc
            o_ref[2, :, :, lanes] = jnp.broadcast_to(m, r1.shape)
            o_ref[3, :, :, lanes] = jnp.exp(-(b + mc))
            o_ref[4, :, :, lanes] = jnp.exp(r1 - mx)
            o_ref[5, :, :, lanes] = jnp.broadcast_to(jnp.exp(m - mx), r1.shape)
            m = b[..., edge:edge + 1] + mx


def _gate_prep(gates_c, gates_x):
    bsz = gates_c.shape[0]
    nlt, nct = gates_x.shape[2] // M_CHUNK, gates_c.shape[2] // M_CHUNK
    n = (nlt + nct) * M_CHUNK
    fwd = tuple(range(nlt, nlt + nct)) + tuple(range(nlt))
    bwd = tuple(range(nlt + nct - 1, nlt - 1, -1)) + tuple(range(nlt - 1, -1, -1))
    gf = gb = jnp.concatenate([gates_x, gates_c], axis=2).reshape(bsz, 4, M_HEADS, n)
    out_spec = pl.BlockSpec((6, bsz, M_HEADS, n), lambda j: (0, 0, 0, 0))
    return pl.pallas_call(
        functools.partial(_gate_kernel, (fwd, bwd)),
        grid=(1,),
        in_specs=[pl.BlockSpec((bsz, 2, M_HEADS, n), lambda j: (0, 0, 0, 0)),
                  pl.BlockSpec((bsz, 2, M_HEADS, n), lambda j: (0, 1, 0, 0))],
        out_specs=[out_spec, out_spec],
        out_shape=[jax.ShapeDtypeStruct((6, bsz, M_HEADS, n), F32)] * 2,
        compiler_params=_params("arbitrary"),
        name="mlstm_gates",
    )(gf, gb)


def _scan_kernel(cps, with_outputs, *refs):
    if with_outputs:
        (qf, ktf, vf, rf, qb, ktb, vb, rb, c0, hf_out, hb_out, c_s) = refs
        h_outs = (hf_out, hb_out)
        streams = ((qf, ktf, vf, rf), (qb, ktb, vb, rb))
    else:
        (ktf, vf, rf, ktb, vb, rb, c_s) = refs
        streams = ((None, ktf, vf, rf), (None, ktb, vb, rb))
    s = pl.program_id(1)
    L = M_CHUNK
    dk = ktf.shape[1] // M_HEADS
    dv = vf.shape[2] // M_HEADS

    @pl.when(s == 0)
    def _():
        if with_outputs:
            c_s[...] = c0[...]
        else:
            c_s[...] = jnp.zeros(c_s.shape, F32)

    row = lax.broadcasted_iota(jnp.int32, (L, L), 0)
    col = lax.broadcasted_iota(jnp.int32, (L, L), 1)
    for cc in range(cps):
        sub = (cc, cps - 1 - cc)
        blk = [slice(i * L, (i + 1) * L) for i in sub]
        if with_outputs:
            stack = [r_ref[j, 0, :, blk[d]] for d, r_ref in enumerate((rf, rb)) for j in (1, 3)]
            stack.append(jnp.zeros((L - 4 * M_HEADS, L), F32))
            cols = jnp.transpose(jnp.concatenate(stack, axis=0))
        for d in range(2):
            q_ref, kt_ref, v_ref, row_ref = streams[d]
            mask = (col <= row) if d == 0 else (col >= row)
            for h in range(M_HEADS):
                kt = kt_ref[0, h * dk:(h + 1) * dk, blk[d]]
                v = v_ref[0, blk[d], h * dv:(h + 1) * dv]
                c_old = c_s[0, d, h]
                kw = kt.astype(F32) * row_ref[4, 0, h:h + 1, blk[d]]
                n_upd = jnp.broadcast_to(jnp.sum(kw, axis=1, keepdims=True), (dk, LANES))
                upd = jnp.concatenate([_bdot(kw.astype(BF16), v), n_upd], axis=1)
                c_s[0, d, h] = row_ref[5, 0, h:h + 1, sub[d] * L:sub[d] * L + 1] * c_old + upd
                if not with_outputs:
                    continue
                q = q_ref[0, blk[d], h * dk:(h + 1) * dk]
                cb = c_old.astype(BF16)
                j = 2 * M_HEADS * d + h
                mc = cols[:, j:j + 1]
                fl = cols[:, j + M_HEADS:j + M_HEADS + 1]
                p = jnp.exp(jnp.where(mask, row_ref[0, 0, h:h + 1, blk[d]] - mc, -jnp.inf))
                wi = jnp.exp(row_ref[2, 0, h:h + 1, blk[d]] - mc)
                sq = _bdot(q, jnp.concatenate([kt, cb[:, dv:]], axis=1))
                sp = sq[:, :L] * p
                lhs = jnp.concatenate([sp, q.astype(F32) * wi], axis=1).astype(BF16)
                num = _bdot(lhs, jnp.concatenate([v, cb[:, :dv]], axis=0))
                den = wi * sq[:, L:] + jnp.sum(sp, axis=1, keepdims=True)
                inv = 1.0 / jnp.maximum(jnp.abs(den), fl)
                h_outs[d][0, blk[d], h * dv:(h + 1) * dv] = (
                    num * jnp.concatenate([inv] * (dv // LANES), axis=1)).astype(BF16)


def _scan(q, kt, v, gates_f, gates_b, off, init):
    bsz, hk, t = kt.shape
    hv = v.shape[2]
    dk, dv = hk // M_HEADS, hv // M_HEADS
    assert dk == M_CHUNK == LANES and dv % LANES == 0
    nc = t // M_CHUNK
    cps = max(c for c in (SCAN_CHUNKS_PER_STEP, 2, 1) if nc % c == 0 and off % c == 0)
    nsteps = nc // cps
    L = cps * M_CHUNK
    with_outputs = init is not None

    def specs(cidx):
        sp = [pl.BlockSpec((1, hk, L), lambda b, s: (b, 0, cidx(s))),
              pl.BlockSpec((1, L, hv), lambda b, s: (b, cidx(s), 0)),
              pl.BlockSpec((6, 1, M_HEADS, L), lambda b, s: (0, b, 0, off // cps + cidx(s)))]
        if with_outputs:
            sp = [pl.BlockSpec((1, L, hk), lambda b, s: (b, cidx(s), 0))] + sp
        return sp

    lead = [q] if with_outputs else []
    args = lead + [kt, v, gates_f] + lead + [kt, v, gates_b]
    in_specs = specs(lambda s: s) + specs(lambda s: nsteps - 1 - s)
    state_shape = (bsz, 2, M_HEADS, dk, dv + LANES)
    state_spec = pl.BlockSpec((1,) + state_shape[1:], lambda b, s: (b, 0, 0, 0, 0))
    if with_outputs:
        args.append(init)
        in_specs.append(state_spec)
        out_specs = [pl.BlockSpec((1, L, hv), lambda b, s: (b, s, 0)),
                     pl.BlockSpec((1, L, hv), lambda b, s: (b, nsteps - 1 - s, 0))]
        out_shape = [jax.ShapeDtypeStruct((bsz, t, hv), BF16)] * 2
        scratch = [pltpu.VMEM((1,) + state_shape[1:], F32)]
    else:
        out_specs = state_spec
        out_shape = jax.ShapeDtypeStruct(state_shape, F32)
        scratch = []
    return pl.pallas_call(
        functools.partial(_scan_kernel, cps, with_outputs),
        grid=(bsz, nsteps),
        in_specs=in_specs,
        out_specs=out_specs,
        out_shape=out_shape,
        scratch_shapes=scratch,
        compiler_params=_params("parallel", "arbitrary"),
        name="mlstm_scan" if with_outputs else "mlstm_ctx_state",
    )(*args)


def _readout_core(h, sig_o, x, gate, norm_g, w_ref, post_g):
    dv = h.shape[1] // M_HEADS
    parts = []
    for hd in range(M_HEADS):
        hh = h[:, hd * dv:(hd + 1) * dv]
        parts.append(hh * lax.rsqrt(jnp.mean(hh * hh, axis=-1, keepdims=True) + EPS))
    y = (jnp.concatenate(parts, axis=-1) * norm_g * sig_o.astype(F32)).astype(BF16)
    return x + gate * _rms(_bdot(y, w_ref[...]), post_g)


def _ffn_core(fc, x, sh, sc, gate, pre_g, post_g, wu_ref, wg_ref, cw_ref, cb_ref, wd_ref, act_ref):
    tm = x.shape[0]
    f = wu_ref.shape[1]
    hb = (_rms(x, pre_g) * (1.0 + sc) + sh).astype(BF16)
    pos = lax.broadcasted_iota(jnp.int32, (tm, fc), 0) % GRID_W
    has_prev = pos != 0
    has_next = pos != GRID_W - 1
    for j in range(f // fc):
        cs = slice(j * fc, (j + 1) * fc)
        u = _bdot(hb, wu_ref[:, cs])
        g = _bdot(hb, wg_ref[:, cs])
        g_prev = jnp.where(has_prev, pltpu.roll(g, 1, 0), 0.0)
        g_next = jnp.where(has_next, pltpu.roll(g, tm - 1, 0), 0.0)
        gc = g_prev * cw_ref[0:1, cs] + g * cw_ref[1:2, cs] + g_next * cw_ref[2:3, cs] + cb_ref[:, cs]
        act_ref[:, cs] = (gc * jax.nn.sigmoid(gc) * u).astype(BF16)
    y = _bdot(act_ref[...], wd_ref[...])
    return x + gate * _rms(y, post_g)


RADIX = 4


def _fourier_channel_core(xq, sh, sc, pre_g, cs_ref, g_ref, rs=slice(None)):
    gd = cs_ref.shape[0]
    hb = [(_rms(x, pre_g) * (1.0 + sc) + sh).astype(BF16) for x in xq]
    for g in range(hb[0].shape[1] // gd):
        lanes = slice(g * gd, (g + 1) * gd)
        r = [_bdot(h[:, lanes], cs_ref[...]) for h in hb]
        a = [v[:, :gd] for v in r]
        b = [v[:, gd:] for v in r]
        a02p, a02m, a13p, a13m = a[0] + a[2], a[0] - a[2], a[1] + a[3], a[1] - a[3]
        b02p, b02m, b13p, b13m = b[0] + b[2], b[0] - b[2], b[1] + b[3], b[1] - b[3]
        re_im = ((a02p + a13p, -(b02p + b13p)),
                 (a02m - b13m, -(b02m + a13m)),
                 (a02p - a13p, b13p - b02p),
                 (a02m + b13m, a13m - b02m))
        for k, (re, im) in enumerate(re_im):
            g_ref[0, k, 0, rs, lanes] = re.astype(BF16)
            g_ref[0, k, 1, rs, lanes] = im.astype(BF16)


def _ffn_params(layer, pre_ref, post_ref, cb_ref):
    return pre_ref[layer:layer + 1], post_ref[layer:layer + 1], cb_ref[layer:layer + 1]


def _ffn_kernel(fc, layer, x_ref, mod_ref, pre_ref, post_ref, wu_ref, wg_ref, cw_ref, cb_ref, wd_ref,
                out_ref, act_ref):
    shift, scale, gate = _mod_rows(mod_ref, pl.program_id(0))[3:]
    pre_g, post_g, cb = _ffn_params(layer, pre_ref, post_ref, cb_ref)
    half = x_ref.shape[1] // 2
    for s in range(2):
        rs = slice(s * half, (s + 1) * half)
        out_ref[0, rs] = _ffn_core(fc, x_ref[0, rs], shift, scale, gate, pre_g, post_g, wu_ref, wg_ref, cw_ref, cb,
                                   wd_ref, act_ref.at[s])


def _layer0_tail_kernel(fc, hf_ref, hb_ref, o_ref, x_ref, mod0_ref, mod1_ref, ng_ref, wo_ref, pmix_ref, premix_ref,
                        pre_ref, post_ref, wu_ref, wg_ref, cw_ref, cb_ref, wd_ref, cs_ref, out_ref, g_ref, act_ref):
    rows = x_ref.shape[2]
    _, _, gate1, shift2, scale2, gate2 = _mod_rows(mod0_ref, pl.program_id(0))
    next_shift, next_scale = _mod_rows(mod1_ref, pl.program_id(0))[:2]
    pre_g, post_g, cb = _ffn_params(0, pre_ref, post_ref, cb_ref)
    sub = rows // 2
    slices = [slice(s * sub, (s + 1) * sub) for s in range(2)]
    xs = []
    for rs in slices:
        cat = lambda ref: jnp.concatenate([ref[0, q, rs] for q in range(RADIX)], axis=0)
        h = cat(hf_ref).astype(F32) + cat(hb_ref).astype(F32)
        xs.append(_readout_core(h, cat(o_ref), cat(x_ref), gate1, ng_ref[...], wo_ref, pmix_ref[0:1]))
    for s in range(2):
        xs[s] = _ffn_core(fc, xs[s], shift2, scale2, gate2, pre_g, post_g, wu_ref, wg_ref, cw_ref, cb, wd_ref,
                          act_ref.at[s])
    for s, rs in enumerate(slices):
        xq = [xs[s][q * sub:(q + 1) * sub] for q in range(RADIX)]
        for q in range(RADIX):
            out_ref[0, q, rs] = xq[q]
        _fourier_channel_core(xq, next_shift, next_scale, premix_ref[1:2], cs_ref, g_ref, rs)


def _ffn_specs(layer, pre_g, post_g, d, f, cw, cb):
    def lay(rows, cols, col_block=0):
        return pl.BlockSpec((None, rows, cols), lambda b, i: (layer, 0, col_block),
                            pipeline_mode=pl.Buffered(1))
    return [_const_spec(pre_g), _const_spec(post_g), lay(d, f, 0), lay(d, f, 1), lay(cw.shape[1], f),
            _const_spec(cb), lay(f, d)]


def _ffn(x, mod, layer, pre_g, post_g, w_up, cw, cb, w_down, tm, fc):
    bsz, t, d = x.shape
    f = w_down.shape[1]
    tok = pl.BlockSpec((1, tm, d), lambda b, i: (b, i, 0))
    return pl.pallas_call(
        functools.partial(_ffn_kernel, fc, layer),
        grid=(bsz, t // tm),
        in_specs=[tok, _mod_spec(layer, *mod.shape[1:])] + _ffn_specs(layer, pre_g, post_g, d, f, cw, cb),
        out_specs=tok,
        out_shape=jax.ShapeDtypeStruct((bsz, t, d), F32),
        scratch_shapes=[pltpu.VMEM((2, tm // 2, f), BF16)],
        compiler_params=_params("parallel", "parallel"),
        name="conv_ffn",
    )(x, mod, pre_g, post_g, w_up, w_up, cw, cb, w_down)


def _layer0_tail(hf, hb, sig_o, x, mod, norm_g, w_out, post_mix_g, pre_mix_g, pre_g, post_g, w_up, cw, cb, w_down,
                 cs, rows, fc):
    bsz, t, d = x.shape
    hv = hf.shape[2]
    f = w_down.shape[1]
    tq = t // RADIX
    quarters = lambda a: a.reshape(bsz, RADIX, tq, a.shape[2])
    tok = lambda n: pl.BlockSpec((1, RADIX, rows, n), lambda b, i: (b, 0, i, 0))
    out, g = pl.pallas_call(
        functools.partial(_layer0_tail_kernel, fc),
        grid=(bsz, tq // rows),
        in_specs=([tok(hv), tok(hv), tok(hv), tok(d), _mod_spec(0, *mod.shape[1:]), _mod_spec(1, *mod.shape[1:]),
                   _const_spec(norm_g), _const_spec(w_out), _const_spec(post_mix_g), _const_spec(pre_mix_g)]
                  + _ffn_specs(0, pre_g, post_g, d, f, cw, cb) + [_const_spec(cs)]),
        out_specs=[tok(d), pl.BlockSpec((1, RADIX, 2, rows, d), lambda b, i: (b, 0, 0, i, 0))],
        out_shape=[jax.ShapeDtypeStruct((bsz, RADIX, tq, d), F32),
                   jax.ShapeDtypeStruct((bsz, RADIX, 2, tq, d), BF16)],
        scratch_shapes=[pltpu.VMEM((2, RADIX * rows // 2, f), BF16)],
        compiler_params=_params("parallel", "parallel"),
        name="layer0_tail",
    )(quarters(hf), quarters(hb), quarters(sig_o), quarters(x), mod, mod, norm_g, w_out, post_mix_g, pre_mix_g,
      pre_g, post_g, w_up, w_up, cw, cb, w_down, cs)
    return out.reshape(bsz, t, d), g


def _fourier_token_kernel(scale, dft_ref, g_ref, w_ref, bias_ref, post_ref, mod_ref, x_ref, out_ref,
                          x_s, out_s):
    tm = dft_ref.shape[1]
    nblk = x_s.shape[0]
    gate = _mod_rows(mod_ref, pl.program_id(0))[2]
    for c in range(nblk):
        x_s[c] = x_ref[0, :, c * LANES:(c + 1) * LANES]
    ys = [(_bdot(dft_ref[r], g_ref[0, r]) * scale).astype(BF16) for r in range(RADIX)]
    zs = [_bdot(y, w_ref[...]) + bias_ref[...] for y in ys]
    for r in range(RADIX):
        rows = pl.ds(r, tm, stride=RADIX)
        xr = jnp.concatenate([x_s[c, rows, :] for c in range(nblk)], axis=1)
        res = xr + gate * _rms(zs[r], post_ref[1:2])
        for c in range(nblk):
            out_s[c, rows, :] = res[:, c * LANES:(c + 1) * LANES]
    for c in range(nblk):
        out_ref[0, :, c * LANES:(c + 1) * LANES] = out_s[c]


def _fourier_token(dft, g, w, bias, post_mix_g, mod, x, scale, tm):
    bsz, t, d = x.shape
    tq = t // RADIX
    tok = pl.BlockSpec((1, RADIX * tm, d), lambda b, i: (b, i, 0))
    return pl.pallas_call(
        functools.partial(_fourier_token_kernel, scale),
        grid=(bsz, tq // tm),
        in_specs=[pl.BlockSpec((RADIX, tm, 2 * tq), lambda b, i: (0, i, 0)),
                  pl.BlockSpec((1, RADIX, 2 * tq, d), lambda b, i: (b, 0, 0, 0)),
                  _const_spec(w), _const_spec(bias), _const_spec(post_mix_g), _mod_spec(1, *mod.shape[1:]), tok],
        out_specs=tok,
        out_shape=jax.ShapeDtypeStruct((bsz, t, d), F32),
        scratch_shapes=[pltpu.VMEM((d // LANES, RADIX * tm, LANES), F32)] * 2,
        compiler_params=_params("parallel", "arbitrary"),
        name="fourier_token_dft",
    )(dft, g.reshape(bsz, RADIX, 2 * tq, d), w, bias, post_mix_g, mod, x)


def _dft_tables(t, gd):
    idx = np.arange(gd, dtype=np.int64)
    ang = 2.0 * np.pi * ((idx[:, None] * idx[None, :]) % gd).astype(np.float64) / gd
    chan = np.concatenate([np.cos(ang), np.sin(ang)], axis=1).astype(np.float32)
    tq = t // RADIX
    k = RADIX * np.arange(tq, dtype=np.int64)[None, :, None] + np.arange(RADIX, dtype=np.int64)[:, None, None]
    ang = 2.0 * np.pi * ((k * np.arange(tq, dtype=np.int64)[None, None, :]) % t).astype(np.float64) / t
    tok = np.concatenate([np.cos(ang), np.sin(ang)], axis=2).astype(np.float32)
    return jnp.asarray(chan).astype(BF16), jnp.asarray(tok).astype(BF16)


def kernel(x, c, ctx, c_ctx, ada_w, ada_b, pre_mix_g, post_mix_g, pre_ffn_g, post_ffn_g, ffn_up_w, ffn_conv_w,
           ffn_conv_b, ffn_down_w, m_in_w, m_in_b, m_norm_g, m_out_w, f_out_w, f_out_b):
    bsz, t, d = x.shape
    tc = ctx.shape[1]
    f = ffn_down_w.shape[1]
    assert ada_w.shape[0] == 2 and m_in_w.shape[0] == 1 and f_out_w.shape[0] == 1
    assert bsz + 1 <= COND_ROWS and t % M_CHUNK == 0 and tc % M_CHUNK == 0 and t % GRID_W == 0
    hv = m_out_w.shape[1]
    hk = (m_in_w.shape[2] - 2 * hv - 4 * M_HEADS) // 2
    tm = min(512, t)
    fc = 256
    assert f % fc == 0 and t % tm == 0 and tm % GRID_W == 0

    mod, w_in = _ada(c, c_ctx.reshape(1, d), ada_w, ada_b, jnp.swapaxes(m_in_w[0], 0, 1))

    b_in = jnp.pad(m_in_b, ((0, 0), (0, w_in.shape[0] - m_in_b.shape[1])))
    kc, vc, gc = _inproj(ctx, mod, bsz, pre_mix_g, w_in, b_in, hk, hv, min(tm, tc), False)
    qx, ox, kx, vx, gx, w_up, w_down, w_mo, w_fo = _inproj(
        x, mod, None, pre_mix_g, w_in, b_in, hk, hv, tm, True, convert=(ffn_up_w, ffn_down_w, m_out_w, f_out_w))
    gates_f, gates_b = _gate_prep(gc, gx)
    state = _scan(None, kc, vc, gates_f, gates_b, t // M_CHUNK, None)
    hf, hb = _scan(qx, kx, vx, gates_f, gates_b, 0, state)
    gd = d // F_GROUPS
    tq = t // RADIX
    rows, tm_tok = min(128, tq), min(256, tq)
    assert t % RADIX == 0 and tq % rows == 0 and tq % tm_tok == 0 and rows % GRID_W == 0
    chan, tok = _dft_tables(t, gd)
    x, g = _layer0_tail(hf, hb, ox, x, mod, m_norm_g, w_mo[0], post_mix_g, pre_mix_g, pre_ffn_g, post_ffn_g,
                        w_up, ffn_conv_w, ffn_conv_b, w_down, chan, rows, fc)

    x = _fourier_token(tok, g, w_fo[0], f_out_b, post_mix_g, mod, x, float(1.0 / np.sqrt(t * gd)), tm_tok)
    return _ffn(x, mod, 1, pre_ffn_g, post_ffn_g, w_up, ffn_conv_w, ffn_conv_b, w_down, tm, fc)
```

```python
import functools

import numpy as np
import jax
import jax.numpy as jnp
from jax import lax
from jax.experimental import pallas as pl
from jax.experimental.pallas import tpu as pltpu

F32 = jnp.float32
BF16 = jnp.bfloat16

M_HEADS = 4
M_CHUNK = 128
F_GROUPS = 8
GRID_W = 64
EPS = 1e-6
N_MOD = 6
COND_ROWS = 16
SCAN_CHUNKS_PER_STEP = 8

VMEM_LIMIT_BYTES = 56 * 1024 * 1024
LANES = 128


def _params(*sem):
    return pltpu.CompilerParams(dimension_semantics=sem, vmem_limit_bytes=VMEM_LIMIT_BYTES)


def _rms(x, g):
    return x * lax.rsqrt(jnp.mean(x * x, axis=-1, keepdims=True) + EPS) * g


def _bdot(a, b):
    return jnp.dot(a, b, preferred_element_type=F32)


def _mod_rows(mod_ref, row):
    m = mod_ref[pl.ds(row, 1), :]
    d = m.shape[1] // N_MOD
    return [m[:, k * d:(k + 1) * d] for k in range(N_MOD)]


def _mod_spec(layer, cond_rows, width):
    return pl.BlockSpec((None, cond_rows, width), lambda b, i: (layer, 0, 0), pipeline_mode=pl.Buffered(1))


def _const_spec(a):
    return pl.BlockSpec(a.shape, lambda b, i: (0,) * a.ndim, pipeline_mode=pl.Buffered(1))


def _ada_kernel(valid_rows, c_ref, cc_ref, w_ref, b_ref, win_ref, o_ref, win_out):
    pad = jnp.zeros((o_ref.shape[1] - c_ref.shape[0] - 1, c_ref.shape[1]), F32)
    c = jnp.concatenate([c_ref[...], cc_ref[...], pad], axis=0)
    s = c * jax.nn.sigmoid(c)
    o_ref[0] = _bdot(s.astype(BF16), w_ref[0].astype(BF16)) + b_ref[pl.ds(pl.program_id(0), 1), :]
    step = pl.program_id(0) * pl.num_programs(1) + pl.program_id(1)
    row = step * win_ref.shape[0] + lax.broadcasted_iota(jnp.int32, win_ref.shape, 0)
    win_out[...] = jnp.where(row < valid_rows, win_ref[...], 0.0).astype(BF16)


def _ada(c, c_ctx, ada_w, ada_b, w_in_t):
    depth, d, n = ada_w.shape
    tn = n // 4
    steps = depth * (n // tn)
    rows, cols = w_in_t.shape
    padded = rows + (-rows) % LANES
    blk = padded // steps
    assert padded % (16 * steps) == 0 and (steps - 1) * blk < rows
    return pl.pallas_call(
        functools.partial(_ada_kernel, rows),
        grid=(depth, n // tn),
        in_specs=[pl.BlockSpec(c.shape, lambda i, j: (0, 0)),
                  pl.BlockSpec(c_ctx.shape, lambda i, j: (0, 0)),
                  pl.BlockSpec((1, d, tn), lambda i, j: (i, 0, j)),
                  pl.BlockSpec((depth, tn), lambda i, j: (0, j)),
                  pl.BlockSpec((blk, cols), lambda i, j: (i * (n // tn) + j, 0))],
        out_specs=[pl.BlockSpec((1, COND_ROWS, tn), lambda i, j: (i, 0, j)),
                   pl.BlockSpec((blk, cols), lambda i, j: (i * (n // tn) + j, 0))],
        out_shape=[jax.ShapeDtypeStruct((depth, COND_ROWS, n), F32),
                   jax.ShapeDtypeStruct((padded, cols), BF16)],
        compiler_params=_params("arbitrary", "arbitrary"),
        name="ada",
    )(c, c_ctx, ada_w, ada_b, w_in_t)


def _inproj_kernel(nconv, with_qo, cond_row, x_ref, mod_ref, g_ref, w_ref, b_ref, *refs):
    nout = 5 if with_qo else 3
    conv_in, outs, conv_out = refs[:nconv], refs[nconv:nconv + nout], refs[nconv + nout:]
    kt_out, v_out, gate_out = outs[-3:]
    hk, hv = kt_out.shape[1], v_out.shape[2]
    dk = hk // M_HEADS
    shift, scale = _mod_rows(mod_ref, pl.program_id(0) if cond_row is None else cond_row)[:2]
    h = _rms(x_ref[0], g_ref[0:1]) * (1.0 + scale) + shift
    hb = h.astype(BF16)

    def proj(lo, hi):
        return lax.dot_general(hb, w_ref[lo:hi, :], (((1,), (1,)), ((), ())),
                               preferred_element_type=F32) + b_ref[:, lo:hi]

    if with_qo:
        q_out, o_out = outs[:2]
        q_out[0] = proj(0, hk).astype(BF16)
        o_out[0] = jax.nn.sigmoid(proj(2 * hk + hv, 2 * hk + 2 * hv)).astype(BF16)
    kt_out[0] = jnp.transpose(proj(hk, 2 * hk) * (dk ** -0.5)).astype(BF16)
    v_out[0] = proj(2 * hk, 2 * hk + hv).astype(BF16)
    gate_out[0] = jnp.transpose(proj(2 * hk + 2 * hv, w_ref.shape[0]))[:gate_out.shape[1]]
    for src_ref, dst_ref in zip(conv_in, conv_out):
        dst_ref[...] = src_ref[...].astype(BF16)


def _inproj(x, mod, cond_row, g, w, bias, hk, hv, tm, with_qo, convert=()):
    bsz, t, d = x.shape
    nt = t // tm
    full = lambda a: pl.BlockSpec(a.shape, lambda b, i: (0, 0))
    tok = lambda n: pl.BlockSpec((1, tm, n), lambda b, i: (b, i, 0))
    tok_t = lambda n: pl.BlockSpec((1, n, tm), lambda b, i: (b, 0, i))
    flat = [a.reshape(-1, a.shape[-1]) for a in convert]
    conv_specs = [pl.BlockSpec((a.shape[0] // (bsz * nt), a.shape[1]), lambda b, i: (b * nt + i, 0)) for a in flat]
    assert all(a.shape[0] % (16 * bsz * nt) == 0 for a in flat)
    qo_specs = [tok(hk), tok(hv)] if with_qo else []
    qo_shapes = [jax.ShapeDtypeStruct((bsz, t, hk), BF16), jax.ShapeDtypeStruct((bsz, t, hv), BF16)] if with_qo else []
    outs = pl.pallas_call(
        functools.partial(_inproj_kernel, len(flat), with_qo, cond_row),
        grid=(bsz, nt),
        in_specs=[tok(d), _mod_spec(0, *mod.shape[1:]), full(g), full(w), full(bias)] + conv_specs,
        out_specs=qo_specs + [tok_t(hk), tok(hv), tok_t(4 * M_HEADS)] + conv_specs,
        out_shape=qo_shapes + [jax.ShapeDtypeStruct((bsz, hk, t), BF16), jax.ShapeDtypeStruct((bsz, t, hv), BF16),
                               jax.ShapeDtypeStruct((bsz, 4 * M_HEADS, t), F32)]
                  + [jax.ShapeDtypeStruct(a.shape, BF16) for a in flat],
        compiler_params=_params("parallel", "parallel"),
        name="mlstm_inproj",
    )(x, mod, g, w, bias, *flat)
    nout = len(outs) - len(flat)
    return list(outs[:nout]) + [o.reshape(a.shape) for o, a in zip(outs[nout:], convert)]


def _lane_scan(x, op, fill, reverse, seg):
    n = x.shape[-1]
    lane = lax.broadcasted_iota(jnp.int32, x.shape, x.ndim - 1) % seg
    d = 1
    while d < seg:
        if reverse:
            shifted = jnp.where(lane < seg - d, pltpu.roll(x, n - d, x.ndim - 1), fill)
        else:
            shifted = jnp.where(lane >= d, pltpu.roll(x, d, x.ndim - 1), fill)
        x = op(x, shifted)
        d *= 2
    return x


def _log_sigmoid(x):
    return jnp.minimum(x, 0.0) - jnp.log1p(jnp.exp(-jnp.abs(x)))


def _gate_kernel(orders, gf_ref, gb_ref, of_ref, ob_ref):
    L = M_CHUNK
    for d, (g_ref, o_ref) in enumerate(((gf_ref, of_ref), (gb_ref, ob_ref))):
        edge = L - 1 if d == 0 else 0
        lf = _log_sigmoid(g_ref[:, 1])
        b_all = _lane_scan(lf, jnp.add, 0.0, d == 1, L)
        r1_all = g_ref[:, 0] - b_all
        cm_all = _lane_scan(r1_all, jnp.maximum, -jnp.inf, d == 1, L)
        m = jnp.zeros(lf.shape[:-1] + (1,), F32)
        for c in orders[d]:
            lanes = slice(c * L, (c + 1) * L)
            b, r1, cm = b_all[..., lanes], r1_all[..., lanes], cm_all[..., lanes]
            mx = jnp.maximum(m, cm[..., edge:edge + 1])
            mc = jnp.maximum(m, cm)
            o_ref[0, :, :, lanes] = r1
            o_ref[1, :, :, lanes] = mc
            o_ref[2, :, :, lanes] = jnp.broadcast_to(m, r1.shape)
            o_ref[3, :, :, lanes] = jnp.exp(-(b + mc))
            o_ref[4, :, :, lanes] = jnp.exp(r1 - mx)
            o_ref[5, :, :, lanes] = jnp.broadcast_to(jnp.exp(m - mx), r1.shape)
            m = b[..., edge:edge + 1] + mx


def _gate_prep(gates_c, gates_x):
    bsz = gates_c.shape[0]
    nlt, nct = gates_x.shape[2] // M_CHUNK, gates_c.shape[2] // M_CHUNK
    n = (nlt + nct) * M_CHUNK
    fwd = tuple(range(nlt, nlt + nct)) + tuple(range(nlt))
    bwd = tuple(range(nlt + nct - 1, nlt - 1, -1)) + tuple(range(nlt - 1, -1, -1))
    gf = gb = jnp.concatenate([gates_x, gates_c], axis=2).reshape(bsz, 4, M_HEADS, n)
    out_spec = pl.BlockSpec((6, bsz, M_HEADS, n), lambda j: (0, 0, 0, 0))
    return pl.pallas_call(
        functools.partial(_gate_kernel, (fwd, bwd)),
        grid=(1,),
        in_specs=[pl.BlockSpec((bsz, 2, M_HEADS, n), lambda j: (0, 0, 0, 0)),
                  pl.BlockSpec((bsz, 2, M_HEADS, n), lambda j: (0, 1, 0, 0))],
        out_specs=[out_spec, out_spec],
        out_shape=[jax.ShapeDtypeStruct((6, bsz, M_HEADS, n), F32)] * 2,
        compiler_params=_params("arbitrary"),
        name="mlstm_gates",
    )(gf, gb)


def _scan_kernel(cps, with_outputs, *refs):
    if with_outputs:
        (qf, ktf, vf, rf, qb, ktb, vb, rb, c0, hf_out, hb_out, c_s) = refs
        h_outs = (hf_out, hb_out)
        streams = ((qf, ktf, vf, rf), (qb, ktb, vb, rb))
    else:
        (ktf, vf, rf, ktb, vb, rb, c_s) = refs
        streams = ((None, ktf, vf, rf), (None, ktb, vb, rb))
    s = pl.program_id(1)
    L = M_CHUNK
    dk = ktf.shape[1] // M_HEADS
    dv = vf.shape[2] // M_HEADS

    @pl.when(s == 0)
    def _():
        if with_outputs:
            c_s[...] = c0[...]
        else:
            c_s[...] = jnp.zeros(c_s.shape, F32)

    row = lax.broadcasted_iota(jnp.int32, (L, L), 0)
    col = lax.broadcasted_iota(jnp.int32, (L, L), 1)
    for cc in range(cps):
        sub = (cc, cps - 1 - cc)
        blk = [slice(i * L, (i + 1) * L) for i in sub]
        if with_outputs:
            stack = [r_ref[j, 0, :, blk[d]] for d, r_ref in enumerate((rf, rb)) for j in (1, 3)]
            stack.append(jnp.zeros((L - 4 * M_HEADS, L), F32))
            cols = jnp.transpose(jnp.concatenate(stack, axis=0))
        for d in range(2):
            q_ref, kt_ref, v_ref, row_ref = streams[d]
            mask = (col <= row) if d == 0 else (col >= row)
            for h in range(M_HEADS):
                kt = kt_ref[0, h * dk:(h + 1) * dk, blk[d]]
                v = v_ref[0, blk[d], h * dv:(h + 1) * dv]
                c_old = c_s[0, d, h]
                kw = kt.astype(F32) * row_ref[4, 0, h:h + 1, blk[d]]
                n_upd = jnp.broadcast_to(jnp.sum(kw, axis=1, keepdims=True), (dk, LANES))
                upd = jnp.concatenate([_bdot(kw.astype(BF16), v), n_upd], axis=1)
                c_s[0, d, h] = row_ref[5, 0, h:h + 1, sub[d] * L:sub[d] * L + 1] * c_old + upd
                if not with_outputs:
                    continue
                q = q_ref[0, blk[d], h * dk:(h + 1) * dk]
                cb = c_old.astype(BF16)
                j = 2 * M_HEADS * d + h
                mc = cols[:, j:j + 1]
                fl = cols[:, j + M_HEADS:j + M_HEADS + 1]
                p = jnp.exp(jnp.where(mask, row_ref[0, 0, h:h + 1, blk[d]] - mc, -jnp.inf))
                wi = jnp.exp(row_ref[2, 0, h:h + 1, blk[d]] - mc)
                sq = _bdot(q, jnp.concatenate([kt, cb[:, dv:]], axis=1))
                sp = sq[:, :L] * p
                lhs = jnp.concatenate([sp, q.astype(F32) * wi], axis=1).astype(BF16)
                num = _bdot(lhs, jnp.concatenate([v, cb[:, :dv]], axis=0))
                den = wi * sq[:, L:] + jnp.sum(sp, axis=1, keepdims=True)
                inv = 1.0 / jnp.maximum(jnp.abs(den), fl)
                h_outs[d][0, blk[d], h * dv:(h + 1) * dv] = (
                    num * jnp.concatenate([inv] * (dv // LANES), axis=1)).astype(BF16)


def _scan(q, kt, v, gates_f, gates_b, off, init):
    bsz, hk, t = kt.shape
    hv = v.shape[2]
    dk, dv = hk // M_HEADS, hv // M_HEADS
    assert dk == M_CHUNK == LANES and dv % LANES == 0
    nc = t // M_CHUNK
    cps = max(c for c in (SCAN_CHUNKS_PER_STEP, 2, 1) if nc % c == 0 and off % c == 0)
    nsteps = nc // cps
    L = cps * M_CHUNK
    with_outputs = init is not None

    def specs(cidx):
        sp = [pl.BlockSpec((1, hk, L), lambda b, s: (b, 0, cidx(s))),
              pl.BlockSpec((1, L, hv), lambda b, s: (b, cidx(s), 0)),
              pl.BlockSpec((6, 1, M_HEADS, L), lambda b, s: (0, b, 0, off // cps + cidx(s)))]
        if with_outputs:
            sp = [pl.BlockSpec((1, L, hk), lambda b, s: (b, cidx(s), 0))] + sp
        return sp

    lead = [q] if with_outputs else []
    args = lead + [kt, v, gates_f] + lead + [kt, v, gates_b]
    in_specs = specs(lambda s: s) + specs(lambda s: nsteps - 1 - s)
    state_shape = (bsz, 2, M_HEADS, dk, dv + LANES)
    state_spec = pl.BlockSpec((1,) + state_shape[1:], lambda b, s: (b, 0, 0, 0, 0))
    if with_outputs:
        args.append(init)
        in_specs.append(state_spec)
        out_specs = [pl.BlockSpec((1, L, hv), lambda b, s: (b, s, 0)),
                     pl.BlockSpec((1, L, hv), lambda b, s: (b, nsteps - 1 - s, 0))]
        out_shape = [jax.ShapeDtypeStruct((bsz, t, hv), BF16)] * 2
        scratch = [pltpu.VMEM((1,) + state_shape[1:], F32)]
    else:
        out_specs = state_spec
        out_shape = jax.ShapeDtypeStruct(state_shape, F32)
        scratch = []
    return pl.pallas_call(
        functools.partial(_scan_kernel, cps, with_outputs),
        grid=(bsz, nsteps),
        in_specs=in_specs,
        out_specs=out_specs,
        out_shape=out_shape,
        scratch_shapes=scratch,
        compiler_params=_params("parallel", "arbitrary"),
        name="mlstm_scan" if with_outputs else "mlstm_ctx_state",
    )(*args)


def _readout_core(h, sig_o, x, gate, norm_g, w_ref, post_g):
    dv = h.shape[1] // M_HEADS
    parts = []
    for hd in range(M_HEADS):
        hh = h[:, hd * dv:(hd + 1) * dv]
        parts.append(hh * lax.rsqrt(jnp.mean(hh * hh, axis=-1, keepdims=True) + EPS))
    y = (jnp.concatenate(parts, axis=-1) * norm_g * sig_o.astype(F32)).astype(BF16)
    return x + gate * _rms(_bdot(y, w_ref[...]), post_g)


def _ffn_up(fc, x, sh, sc, pre_g, wu_ref, wg_ref, cw_ref, cb, act_ref):
    tm = x.shape[0]
    f = wu_ref.shape[1]
    hb = (_rms(x, pre_g) * (1.0 + sc) + sh).astype(BF16)
    pos = lax.broadcasted_iota(jnp.int32, (tm, fc), 0) % GRID_W
    has_prev = pos != 0
    has_next = pos != GRID_W - 1
    for j in range(f // fc):
        cs = slice(j * fc, (j + 1) * fc)
        u = _bdot(hb, wu_ref[:, cs])
        g = _bdot(hb, wg_ref[:, cs])
        g_prev = jnp.where(has_prev, pltpu.roll(g, 1, 0), 0.0)
        g_next = jnp.where(has_next, pltpu.roll(g, tm - 1, 0), 0.0)
        gc = g_prev * cw_ref[0:1, cs] + g * cw_ref[1:2, cs] + g_next * cw_ref[2:3, cs] + cb[:, cs]
        act_ref[:, cs] = (gc * jax.nn.sigmoid(gc) * u).astype(BF16)


def _ffn_down(x, gate, post_g, wd_ref, act_ref):
    return x + gate * _rms(_bdot(act_ref[...], wd_ref[...]), post_g)


def _ffn_halves(fc, xs, sh, sc, gate, pre_g, post_g, wu_ref, wg_ref, cw_ref, cb, wd_ref, act_ref):
    for s, x in enumerate(xs):
        _ffn_up(fc, x, sh, sc, pre_g, wu_ref, wg_ref, cw_ref, cb, act_ref.at[s])
    return [_ffn_down(x, gate, post_g, wd_ref, act_ref.at[s]) for s, x in enumerate(xs)]


RADIX = 4


def _fourier_channel_core(xq, sh, sc, pre_g, cs_ref, g_ref, rs=slice(None)):
    gd = cs_ref.shape[0]
    hb = [(_rms(x, pre_g) * (1.0 + sc) + sh).astype(BF16) for x in xq]
    for g in range(hb[0].shape[1] // gd):
        lanes = slice(g * gd, (g + 1) * gd)
        r = [_bdot(h[:, lanes], cs_ref[...]) for h in hb]
        a = [v[:, :gd] for v in r]
        b = [v[:, gd:] for v in r]
        a02p, a02m, a13p, a13m = a[0] + a[2], a[0] - a[2], a[1] + a[3], a[1] - a[3]
        b02p, b02m, b13p, b13m = b[0] + b[2], b[0] - b[2], b[1] + b[3], b[1] - b[3]
        re_im = ((a02p + a13p, -(b02p + b13p)),
                 (a02m - b13m, -(b02m + a13m)),
                 (a02p - a13p, b13p - b02p),
                 (a02m + b13m, a13m - b02m))
        for k, (re, im) in enumerate(re_im):
            g_ref[0, k, 0, rs, lanes] = re.astype(BF16)
            g_ref[0, k, 1, rs, lanes] = im.astype(BF16)


def _ffn_params(layer, pre_ref, post_ref, cb_ref):
    return pre_ref[layer:layer + 1], post_ref[layer:layer + 1], cb_ref[layer:layer + 1]


def _ffn_kernel(fc, layer, x_ref, mod_ref, pre_ref, post_ref, wu_ref, wg_ref, cw_ref, cb_ref, wd_ref,
                out_ref, act_ref):
    shift, scale, gate = _mod_rows(mod_ref, pl.program_id(0))[3:]
    pre_g, post_g, cb = _ffn_params(layer, pre_ref, post_ref, cb_ref)
    half = x_ref.shape[1] // 2
    halves = [slice(s * half, (s + 1) * half) for s in range(2)]
    ys = _ffn_halves(fc, [x_ref[0, rs] for rs in halves], shift, scale, gate, pre_g, post_g, wu_ref, wg_ref, cw_ref,
                     cb, wd_ref, act_ref)
    for rs, y in zip(halves, ys):
        out_ref[0, rs] = y


def _layer0_tail_kernel(fc, hf_ref, hb_ref, o_ref, x_ref, mod0_ref, mod1_ref, ng_ref, wo_ref, pmix_ref, premix_ref,
                        pre_ref, post_ref, wu_ref, wg_ref, cw_ref, cb_ref, wd_ref, cs_ref, out_ref, g_ref, act_ref):
    rows = x_ref.shape[2]
    _, _, gate1, shift2, scale2, gate2 = _mod_rows(mod0_ref, pl.program_id(0))
    next_shift, next_scale = _mod_rows(mod1_ref, pl.program_id(0))[:2]
    pre_g, post_g, cb = _ffn_params(0, pre_ref, post_ref, cb_ref)
    sub = rows // 2
    slices = [slice(s * sub, (s + 1) * sub) for s in range(2)]
    xs = []
    for rs in slices:
        cat = lambda ref: jnp.concatenate([ref[0, q, rs] for q in range(RADIX)], axis=0)
        h = cat(hf_ref).astype(F32) + cat(hb_ref).astype(F32)
        xs.append(_readout_core(h, cat(o_ref), cat(x_ref), gate1, ng_ref[...], wo_ref, pmix_ref[0:1]))
    xs = _ffn_halves(fc, xs, shift2, scale2, gate2, pre_g, post_g, wu_ref, wg_ref, cw_ref, cb, wd_ref, act_ref)
    for s, rs in enumerate(slices):
        xq = [xs[s][q * sub:(q + 1) * sub] for q in range(RADIX)]
        for q in range(RADIX):
            out_ref[0, q, rs] = xq[q]
        _fourier_channel_core(xq, next_shift, next_scale, premix_ref[1:2], cs_ref, g_ref, rs)


def _ffn_specs(layer, pre_g, post_g, d, f, cw, cb):
    def lay(rows, cols, col_block=0):
        return pl.BlockSpec((None, rows, cols), lambda b, i: (layer, 0, col_block),
                            pipeline_mode=pl.Buffered(1))
    return [_const_spec(pre_g), _const_spec(post_g), lay(d, f, 0), lay(d, f, 1), lay(cw.shape[1], f),
            _const_spec(cb), lay(f, d)]


def _ffn(x, mod, layer, pre_g, post_g, w_up, cw, cb, w_down, tm, fc):
    bsz, t, d = x.shape
    f = w_down.shape[1]
    tok = pl.BlockSpec((1, tm, d), lambda b, i: (b, i, 0))
    return pl.pallas_call(
        functools.partial(_ffn_kernel, fc, layer),
        grid=(bsz, t // tm),
        in_specs=[tok, _mod_spec(layer, *mod.shape[1:])] + _ffn_specs(layer, pre_g, post_g, d, f, cw, cb),
        out_specs=tok,
        out_shape=jax.ShapeDtypeStruct((bsz, t, d), F32),
        scratch_shapes=[pltpu.VMEM((2, tm // 2, f), BF16)],
        compiler_params=_params("parallel", "parallel"),
        name="conv_ffn",
    )(x, mod, pre_g, post_g, w_up, w_up, cw, cb, w_down)


def _layer0_tail(hf, hb, sig_o, x, mod, norm_g, w_out, post_mix_g, pre_mix_g, pre_g, post_g, w_up, cw, cb, w_down,
                 cs, rows, fc):
    bsz, t, d = x.shape
    hv = hf.shape[2]
    f = w_down.shape[1]
    tq = t // RADIX
    quarters = lambda a: a.reshape(bsz, RADIX, tq, a.shape[2])
    tok = lambda n: pl.BlockSpec((1, RADIX, rows, n), lambda b, i: (b, 0, i, 0))
    out, g = pl.pallas_call(
        functools.partial(_layer0_tail_kernel, fc),
        grid=(bsz, tq // rows),
        in_specs=([tok(hv), tok(hv), tok(hv), tok(d), _mod_spec(0, *mod.shape[1:]), _mod_spec(1, *mod.shape[1:]),
                   _const_spec(norm_g), _const_spec(w_out), _const_spec(post_mix_g), _const_spec(pre_mix_g)]
                  + _ffn_specs(0, pre_g, post_g, d, f, cw, cb) + [_const_spec(cs)]),
        out_specs=[tok(d), pl.BlockSpec((1, RADIX, 2, rows, d), lambda b, i: (b, 0, 0, i, 0))],
        out_shape=[jax.ShapeDtypeStruct((bsz, RADIX, tq, d), F32),
                   jax.ShapeDtypeStruct((bsz, RADIX, 2, tq, d), BF16)],
        scratch_shapes=[pltpu.VMEM((2, RADIX * rows // 2, f), BF16)],
        compiler_params=_params("parallel", "parallel"),
        name="layer0_tail",
    )(quarters(hf), quarters(hb), quarters(sig_o), quarters(x), mod, mod, norm_g, w_out, post_mix_g, pre_mix_g,
      pre_g, post_g, w_up, w_up, cw, cb, w_down, cs)
    return out.reshape(bsz, t, d), g


def _fourier_token_kernel(scale, dft_ref, g_ref, w_ref, bias_ref, post_ref, mod_ref, x_ref, out_ref,
                          x_s, out_s):
    tm = dft_ref.shape[1]
    nblk = x_s.shape[0]
    gate = _mod_rows(mod_ref, pl.program_id(0))[2]
    for c in range(nblk):
        x_s[c] = x_ref[0, :, c * LANES:(c + 1) * LANES]
    ys = [(_bdot(dft_ref[r], g_ref[0, r]) * scale).astype(BF16) for r in range(RADIX)]
    zs = [_bdot(y, w_ref[...]) + bias_ref[...] for y in ys]
    for r in range(RADIX):
        rows = pl.ds(r, tm, stride=RADIX)
        xr = jnp.concatenate([x_s[c, rows, :] for c in range(nblk)], axis=1)
        res = xr + gate * _rms(zs[r], post_ref[1:2])
        for c in range(nblk):
            out_s[c, rows, :] = res[:, c * LANES:(c + 1) * LANES]
    for c in range(nblk):
        out_ref[0, :, c * LANES:(c + 1) * LANES] = out_s[c]


def _fourier_token(dft, g, w, bias, post_mix_g, mod, x, scale, tm):
    bsz, t, d = x.shape
    tq = t // RADIX
    tok = pl.BlockSpec((1, RADIX * tm, d), lambda b, i: (b, i, 0))
    return pl.pallas_call(
        functools.partial(_fourier_token_kernel, scale),
        grid=(bsz, tq // tm),
        in_specs=[pl.BlockSpec((RADIX, tm, 2 * tq), lambda b, i: (0, i, 0)),
                  pl.BlockSpec((1, RADIX, 2 * tq, d), lambda b, i: (b, 0, 0, 0)),
                  _const_spec(w), _const_spec(bias), _const_spec(post_mix_g), _mod_spec(1, *mod.shape[1:]), tok],
        out_specs=tok,
        out_shape=jax.ShapeDtypeStruct((bsz, t, d), F32),
        scratch_shapes=[pltpu.VMEM((d // LANES, RADIX * tm, LANES), F32)] * 2,
        compiler_params=_params("parallel", "arbitrary"),
        name="fourier_token_dft",
    )(dft, g.reshape(bsz, RADIX, 2 * tq, d), w, bias, post_mix_g, mod, x)


def _dft_tables(t, gd):
    idx = np.arange(gd, dtype=np.int64)
    ang = 2.0 * np.pi * ((idx[:, None] * idx[None, :]) % gd).astype(np.float64) / gd
    chan = np.concatenate([np.cos(ang), np.sin(ang)], axis=1).astype(np.float32)
    tq = t // RADIX
    k = RADIX * np.arange(tq, dtype=np.int64)[None, :, None] + np.arange(RADIX, dtype=np.int64)[:, None, None]
    ang = 2.0 * np.pi * ((k * np.arange(tq, dtype=np.int64)[None, None, :]) % t).astype(np.float64) / t
    tok = np.concatenate([np.cos(ang), np.sin(ang)], axis=2).astype(np.float32)
    return jnp.asarray(chan).astype(BF16), jnp.asarray(tok).astype(BF16)


def kernel(x, c, ctx, c_ctx, ada_w, ada_b, pre_mix_g, post_mix_g, pre_ffn_g, post_ffn_g, ffn_up_w, ffn_conv_w,
           ffn_conv_b, ffn_down_w, m_in_w, m_in_b, m_norm_g, m_out_w, f_out_w, f_out_b):
    bsz, t, d = x.shape
    tc = ctx.shape[1]
    f = ffn_down_w.shape[1]
    assert ada_w.shape[0] == 2 and m_in_w.shape[0] == 1 and f_out_w.shape[0] == 1
    assert bsz + 1 <= COND_ROWS and t % M_CHUNK == 0 and tc % M_CHUNK == 0 and t % GRID_W == 0
    hv = m_out_w.shape[1]
    hk = (m_in_w.shape[2] - 2 * hv - 4 * M_HEADS) // 2
    tm = min(512, t)
    fc = 256
    assert f % fc == 0 and t % tm == 0 and tm % GRID_W == 0

    mod, w_in = _ada(c, c_ctx.reshape(1, d), ada_w, ada_b, jnp.swapaxes(m_in_w[0], 0, 1))

    b_in = jnp.pad(m_in_b, ((0, 0), (0, w_in.shape[0] - m_in_b.shape[1])))
    kc, vc, gc = _inproj(ctx, mod, bsz, pre_mix_g, w_in, b_in, hk, hv, min(tm, tc), False)
    qx, ox, kx, vx, gx, w_up, w_down, w_mo, w_fo = _inproj(
        x, mod, None, pre_mix_g, w_in, b_in, hk, hv, tm, True, convert=(ffn_up_w, ffn_down_w, m_out_w, f_out_w))
    gates_f, gates_b = _gate_prep(gc, gx)
    state = _scan(None, kc, vc, gates_f, gates_b, t // M_CHUNK, None)
    hf, hb = _scan(qx, kx, vx, gates_f, gates_b, 0, state)
    gd = d // F_GROUPS
    tq = t // RADIX
    rows, tm_tok = min(128, tq), min(256, tq)
    assert t % RADIX == 0 and tq % rows == 0 and tq % tm_tok == 0 and rows % GRID_W == 0
    chan, tok = _dft_tables(t, gd)
    x, g = _layer0_tail(hf, hb, ox, x, mod, m_norm_g, w_mo[0], post_mix_g, pre_mix_g, pre_ffn_g, post_ffn_g,
                        w_up, ffn_conv_w, ffn_conv_b, w_down, chan, rows, fc)

    x = _fourier_token(tok, g, w_fo[0], f_out_b, post_mix_g, mod, x, float(1.0 / np.sqrt(t * gd)), tm_tok)
    return _ffn(x, mod, 1, pre_ffn_g, post_ffn_g, w_up, ffn_conv_w, ffn_conv_b, w_down, tm, fc)
```

```python
import functools

import numpy as np
import jax
import jax.numpy as jnp
from jax import lax
from jax.experimental import pallas as pl
from jax.experimental.pallas import tpu as pltpu

F32 = jnp.float32
BF16 = jnp.bfloat16

M_HEADS = 4
M_CHUNK = 128
F_GROUPS = 8
GRID_W = 64
EPS = 1e-6
N_MOD = 6
COND_ROWS = 16
FFN_SUBTILE_ROWS = 256
SCAN_CHUNKS_PER_STEP = 8

VMEM_LIMIT_BYTES = 56 * 1024 * 1024
LANES = 128


def _params(*sem):
    return pltpu.CompilerParams(dimension_semantics=sem, vmem_limit_bytes=VMEM_LIMIT_BYTES)


def _rms(x, g):
    return x * lax.rsqrt(jnp.mean(x * x, axis=-1, keepdims=True) + EPS) * g


def _bdot(a, b):
    return jnp.dot(a, b, preferred_element_type=F32)


def _mod_rows(mod_ref, row):
    m = mod_ref[pl.ds(row, 1), :]
    d = m.shape[1] // N_MOD
    return [m[:, k * d:(k + 1) * d] for k in range(N_MOD)]


def _mod_spec(layer, cond_rows, width):
    return pl.BlockSpec((None, cond_rows, width), lambda b, i: (layer, 0, 0), pipeline_mode=pl.Buffered(1))


def _const_spec(a):
    return pl.BlockSpec(a.shape, lambda b, i: (0,) * a.ndim, pipeline_mode=pl.Buffered(1))


def _ada_kernel(valid_rows, c_ref, cc_ref, w_ref, b_ref, win_ref, o_ref, win_out):
    pad = jnp.zeros((o_ref.shape[1] - c_ref.shape[0] - 1, c_ref.shape[1]), F32)
    c = jnp.concatenate([c_ref[...], cc_ref[...], pad], axis=0)
    s = c * jax.nn.sigmoid(c)
    o_ref[0] = _bdot(s.astype(BF16), w_ref[0].astype(BF16)) + b_ref[pl.ds(pl.program_id(0), 1), :]
    step = pl.program_id(0) * pl.num_programs(1) + pl.program_id(1)
    row = step * win_ref.shape[0] + lax.broadcasted_iota(jnp.int32, win_ref.shape, 0)
    win_out[...] = jnp.where(row < valid_rows, win_ref[...], 0.0).astype(BF16)


def _ada(c, c_ctx, ada_w, ada_b, w_in_t):
    depth, d, n = ada_w.shape
    tn = n // 4
    steps = depth * (n // tn)
    rows, cols = w_in_t.shape
    padded = rows + (-rows) % LANES
    blk = padded // steps
    assert padded % (16 * steps) == 0 and (steps - 1) * blk < rows
    return pl.pallas_call(
        functools.partial(_ada_kernel, rows),
        grid=(depth, n // tn),
        in_specs=[pl.BlockSpec(c.shape, lambda i, j: (0, 0)),
                  pl.BlockSpec(c_ctx.shape, lambda i, j: (0, 0)),
                  pl.BlockSpec((1, d, tn), lambda i, j: (i, 0, j)),
                  pl.BlockSpec((depth, tn), lambda i, j: (0, j)),
                  pl.BlockSpec((blk, cols), lambda i, j: (i * (n // tn) + j, 0))],
        out_specs=[pl.BlockSpec((1, COND_ROWS, tn), lambda i, j: (i, 0, j)),
                   pl.BlockSpec((blk, cols), lambda i, j: (i * (n // tn) + j, 0))],
        out_shape=[jax.ShapeDtypeStruct((depth, COND_ROWS, n), F32),
                   jax.ShapeDtypeStruct((padded, cols), BF16)],
        compiler_params=_params("arbitrary", "arbitrary"),
        name="ada",
    )(c, c_ctx, ada_w, ada_b, w_in_t)


def _inproj_kernel(nconv, with_qo, cond_row, x_ref, mod_ref, g_ref, w_ref, b_ref, *refs):
    nout = 5 if with_qo else 3
    conv_in, outs, conv_out = refs[:nconv], refs[nconv:nconv + nout], refs[nconv + nout:]
    kt_out, v_out, gate_out = outs[-3:]
    hk, hv = kt_out.shape[1], v_out.shape[2]
    dk = hk // M_HEADS
    shift, scale = _mod_rows(mod_ref, pl.program_id(0) if cond_row is None else cond_row)[:2]
    h = _rms(x_ref[0], g_ref[0:1]) * (1.0 + scale) + shift
    hb = h.astype(BF16)

    def proj(lo, hi):
        return lax.dot_general(hb, w_ref[lo:hi, :], (((1,), (1,)), ((), ())),
                               preferred_element_type=F32) + b_ref[:, lo:hi]

    if with_qo:
        q_out, o_out = outs[:2]
        q_out[0] = proj(0, hk).astype(BF16)
        o_out[0] = jax.nn.sigmoid(proj(2 * hk + hv, 2 * hk + 2 * hv)).astype(BF16)
    kt_out[0] = jnp.transpose(proj(hk, 2 * hk) * (dk ** -0.5)).astype(BF16)
    v_out[0] = proj(2 * hk, 2 * hk + hv).astype(BF16)
    gate_out[0] = jnp.transpose(proj(2 * hk + 2 * hv, w_ref.shape[0]))[:gate_out.shape[1]]
    for src_ref, dst_ref in zip(conv_in, conv_out):
        dst_ref[...] = src_ref[...].astype(BF16)


def _inproj(x, mod, cond_row, g, w, bias, hk, hv, tm, with_qo, convert=()):
    bsz, t, d = x.shape
    nt = t // tm
    full = lambda a: pl.BlockSpec(a.shape, lambda b, i: (0, 0))
    tok = lambda n: pl.BlockSpec((1, tm, n), lambda b, i: (b, i, 0))
    tok_t = lambda n: pl.BlockSpec((1, n, tm), lambda b, i: (b, 0, i))
    flat = [a.reshape(-1, a.shape[-1]) for a in convert]
    conv_specs = [pl.BlockSpec((a.shape[0] // (bsz * nt), a.shape[1]), lambda b, i: (b * nt + i, 0)) for a in flat]
    assert all(a.shape[0] % (16 * bsz * nt) == 0 for a in flat)
    qo_specs = [tok(hk), tok(hv)] if with_qo else []
    qo_shapes = [jax.ShapeDtypeStruct((bsz, t, hk), BF16), jax.ShapeDtypeStruct((bsz, t, hv), BF16)] if with_qo else []
    outs = pl.pallas_call(
        functools.partial(_inproj_kernel, len(flat), with_qo, cond_row),
        grid=(bsz, nt),
        in_specs=[tok(d), _mod_spec(0, *mod.shape[1:]), full(g), full(w), full(bias)] + conv_specs,
        out_specs=qo_specs + [tok_t(hk), tok(hv), tok_t(4 * M_HEADS)] + conv_specs,
        out_shape=qo_shapes + [jax.ShapeDtypeStruct((bsz, hk, t), BF16), jax.ShapeDtypeStruct((bsz, t, hv), BF16),
                               jax.ShapeDtypeStruct((bsz, 4 * M_HEADS, t), F32)]
                  + [jax.ShapeDtypeStruct(a.shape, BF16) for a in flat],
        compiler_params=_params("parallel", "parallel"),
        name="mlstm_inproj",
    )(x, mod, g, w, bias, *flat)
    nout = len(outs) - len(flat)
    return list(outs[:nout]) + [o.reshape(a.shape) for o, a in zip(outs[nout:], convert)]


def _lane_scan(x, op, fill, reverse, seg):
    n = x.shape[-1]
    lane = lax.broadcasted_iota(jnp.int32, x.shape, x.ndim - 1) % seg
    d = 1
    while d < seg:
        if reverse:
            shifted = jnp.where(lane < seg - d, pltpu.roll(x, n - d, x.ndim - 1), fill)
        else:
            shifted = jnp.where(lane >= d, pltpu.roll(x, d, x.ndim - 1), fill)
        x = op(x, shifted)
        d *= 2
    return x


def _log_sigmoid(x):
    return jnp.minimum(x, 0.0) - jnp.log1p(jnp.exp(-jnp.abs(x)))


def _gate_kernel(orders, gf_ref, gb_ref, of_ref, ob_ref):
    L = M_CHUNK
    for d, (g_ref, o_ref) in enumerate(((gf_ref, of_ref), (gb_ref, ob_ref))):
        edge = L - 1 if d == 0 else 0
        lf = _log_sigmoid(g_ref[:, 1])
        b_all = _lane_scan(lf, jnp.add, 0.0, d == 1, L)
        r1_all = g_ref[:, 0] - b_all
        cm_all = _lane_scan(r1_all, jnp.maximum, -jnp.inf, d == 1, L)
        m = jnp.zeros(lf.shape[:-1] + (1,), F32)
        for c in orders[d]:
            lanes = slice(c * L, (c + 1) * L)
            b, r1, cm = b_all[..., lanes], r1_all[..., lanes], cm_all[..., lanes]
            mx = jnp.maximum(m, cm[..., edge:edge + 1])
            mc = jnp.maximum(m, cm)
            o_ref[0, :, :, lanes] = r1
            o_ref[1, :, :, lanes] = mc
            o_ref[2, :, :, lanes] = jnp.broadcast_to(m, r1.shape)
            o_ref[3, :, :, lanes] = jnp.exp(-(b + mc))
            o_ref[4, :, :, lanes] = jnp.exp(r1 - mx)
            o_ref[5, :, :, lanes] = jnp.broadcast_to(jnp.exp(m - mx), r1.shape)
            m = b[..., edge:edge + 1] + mx


def _gate_prep(gates_c, gates_x):
    bsz = gates_c.shape[0]
    nlt, nct = gates_x.shape[2] // M_CHUNK, gates_c.shape[2] // M_CHUNK
    n = (nlt + nct) * M_CHUNK
    fwd = tuple(range(nlt, nlt + nct)) + tuple(range(nlt))
    bwd = tuple(range(nlt + nct - 1, nlt - 1, -1)) + tuple(range(nlt - 1, -1, -1))
    gf = gb = jnp.concatenate([gates_x, gates_c], axis=2).reshape(bsz, 4, M_HEADS, n)
    out_spec = pl.BlockSpec((6, bsz, M_HEADS, n), lambda j: (0, 0, 0, 0))
    return pl.pallas_call(
        functools.partial(_gate_kernel, (fwd, bwd)),
        grid=(1,),
        in_specs=[pl.BlockSpec((bsz, 2, M_HEADS, n), lambda j: (0, 0, 0, 0)),
                  pl.BlockSpec((bsz, 2, M_HEADS, n), lambda j: (0, 1, 0, 0))],
        out_specs=[out_spec, out_spec],
        out_shape=[jax.ShapeDtypeStruct((6, bsz, M_HEADS, n), F32)] * 2,
        compiler_params=_params("arbitrary"),
        name="mlstm_gates",
    )(gf, gb)


def _scan_kernel(cps, with_outputs, *refs):
    if with_outputs:
        (qf, ktf, vf, rf, qb, ktb, vb, rb, c0, hf_out, hb_out, c_s) = refs
        h_outs = (hf_out, hb_out)
        streams = ((qf, ktf, vf, rf), (qb, ktb, vb, rb))
    else:
        (ktf, vf, rf, ktb, vb, rb, c_s) = refs
        streams = ((None, ktf, vf, rf), (None, ktb, vb, rb))
    s = pl.program_id(1)
    L = M_CHUNK
    dk = ktf.shape[1] // M_HEADS
    dv = vf.shape[2] // M_HEADS

    @pl.when(s == 0)
    def _():
        if with_outputs:
            c_s[...] = c0[...]
        else:
            c_s[...] = jnp.zeros(c_s.shape, F32)

    row = lax.broadcasted_iota(jnp.int32, (L, L), 0)
    col = lax.broadcasted_iota(jnp.int32, (L, L), 1)
    for cc in range(cps):
        sub = (cc, cps - 1 - cc)
        blk = [slice(i * L, (i + 1) * L) for i in sub]
        if with_outputs:
            stack = [r_ref[j, 0, :, blk[d]] for d, r_ref in enumerate((rf, rb)) for j in (1, 3)]
            stack.append(jnp.zeros((L - 4 * M_HEADS, L), F32))
            cols = jnp.transpose(jnp.concatenate(stack, axis=0))
        for d in range(2):
            q_ref, kt_ref, v_ref, row_ref = streams[d]
            mask = (col <= row) if d == 0 else (col >= row)
            for h in range(M_HEADS):
                kt = kt_ref[0, h * dk:(h + 1) * dk, blk[d]]
                v = v_ref[0, blk[d], h * dv:(h + 1) * dv]
                c_old = c_s[0, d, h]
                kw = kt.astype(F32) * row_ref[4, 0, h:h + 1, blk[d]]
                n_upd = jnp.broadcast_to(jnp.sum(kw, axis=1, keepdims=True), (dk, LANES))
                upd = jnp.concatenate([_bdot(kw.astype(BF16), v), n_upd], axis=1)
                c_s[0, d, h] = row_ref[5, 0, h:h + 1, sub[d] * L:sub[d] * L + 1] * c_old + upd
                if not with_outputs:
                    continue
                q = q_ref[0, blk[d], h * dk:(h + 1) * dk]
                cb = c_old.astype(BF16)
                j = 2 * M_HEADS * d + h
                mc = cols[:, j:j + 1]
                fl = cols[:, j + M_HEADS:j + M_HEADS + 1]
                p = jnp.exp(jnp.where(mask, row_ref[0, 0, h:h + 1, blk[d]] - mc, -jnp.inf))
                wi = jnp.exp(row_ref[2, 0, h:h + 1, blk[d]] - mc)
                sq = _bdot(q, jnp.concatenate([kt, cb[:, dv:]], axis=1))
                sp = sq[:, :L] * p
                lhs = jnp.concatenate([sp, q.astype(F32) * wi], axis=1).astype(BF16)
                num = _bdot(lhs, jnp.concatenate([v, cb[:, :dv]], axis=0))
                den = wi * sq[:, L:] + jnp.sum(sp, axis=1, keepdims=True)
                inv = 1.0 / jnp.maximum(jnp.abs(den), fl)
                h_outs[d][0, blk[d], h * dv:(h + 1) * dv] = (
                    num * jnp.concatenate([inv] * (dv // LANES), axis=1)).astype(BF16)


def _scan(q, kt, v, gates_f, gates_b, off, init):
    bsz, hk, t = kt.shape
    hv = v.shape[2]
    dk, dv = hk // M_HEADS, hv // M_HEADS
    assert dk == M_CHUNK == LANES and dv % LANES == 0
    nc = t // M_CHUNK
    cps = max(c for c in (SCAN_CHUNKS_PER_STEP, 2, 1) if nc % c == 0 and off % c == 0)
    nsteps = nc // cps
    L = cps * M_CHUNK
    with_outputs = init is not None

    def specs(cidx):
        sp = [pl.BlockSpec((1, hk, L), lambda b, s: (b, 0, cidx(s))),
              pl.BlockSpec((1, L, hv), lambda b, s: (b, cidx(s), 0)),
              pl.BlockSpec((6, 1, M_HEADS, L), lambda b, s: (0, b, 0, off // cps + cidx(s)))]
        if with_outputs:
            sp = [pl.BlockSpec((1, L, hk), lambda b, s: (b, cidx(s), 0))] + sp
        return sp

    lead = [q] if with_outputs else []
    args = lead + [kt, v, gates_f] + lead + [kt, v, gates_b]
    in_specs = specs(lambda s: s) + specs(lambda s: nsteps - 1 - s)
    state_shape = (bsz, 2, M_HEADS, dk, dv + LANES)
    state_spec = pl.BlockSpec((1,) + state_shape[1:], lambda b, s: (b, 0, 0, 0, 0))
    if with_outputs:
        args.append(init)
        in_specs.append(state_spec)
        out_specs = [pl.BlockSpec((1, L, hv), lambda b, s: (b, s, 0)),
                     pl.BlockSpec((1, L, hv), lambda b, s: (b, nsteps - 1 - s, 0))]
        out_shape = [jax.ShapeDtypeStruct((bsz, t, hv), BF16)] * 2
        scratch = [pltpu.VMEM((1,) + state_shape[1:], F32)]
    else:
        out_specs = state_spec
        out_shape = jax.ShapeDtypeStruct(state_shape, F32)
        scratch = []
    return pl.pallas_call(
        functools.partial(_scan_kernel, cps, with_outputs),
        grid=(bsz, nsteps),
        in_specs=in_specs,
        out_specs=out_specs,
        out_shape=out_shape,
        scratch_shapes=scratch,
        compiler_params=_params("parallel", "arbitrary"),
        name="mlstm_scan" if with_outputs else "mlstm_ctx_state",
    )(*args)


def _readout_core(h, sig_o, x, gate, norm_g, w_ref, post_g):
    dv = h.shape[1] // M_HEADS
    parts = []
    for hd in range(M_HEADS):
        hh = h[:, hd * dv:(hd + 1) * dv]
        parts.append(hh * lax.rsqrt(jnp.mean(hh * hh, axis=-1, keepdims=True) + EPS))
    y = (jnp.concatenate(parts, axis=-1) * norm_g * sig_o.astype(F32)).astype(BF16)
    return x + gate * _rms(_bdot(y, w_ref[...]), post_g)


def _ffn_up(fc, x, sh, sc, pre_g, wu_ref, wg_ref, cw_ref, cb, act_ref):
    tm = x.shape[0]
    f = wu_ref.shape[1]
    hb = (_rms(x, pre_g) * (1.0 + sc) + sh).astype(BF16)
    pos = lax.broadcasted_iota(jnp.int32, (tm, fc), 0) % GRID_W
    has_prev = pos != 0
    has_next = pos != GRID_W - 1
    for j in range(f // fc):
        cs = slice(j * fc, (j + 1) * fc)
        u = _bdot(hb, wu_ref[:, cs])
        g = _bdot(hb, wg_ref[:, cs])
        g_prev = jnp.where(has_prev, pltpu.roll(g, 1, 0), 0.0)
        g_next = jnp.where(has_next, pltpu.roll(g, tm - 1, 0), 0.0)
        gc = g_prev * cw_ref[0:1, cs] + g * cw_ref[1:2, cs] + g_next * cw_ref[2:3, cs] + cb[:, cs]
        act_ref[:, cs] = (gc * jax.nn.sigmoid(gc) * u).astype(BF16)


def _ffn_down(x, gate, post_g, wd_ref, act_ref):
    return x + gate * _rms(_bdot(act_ref[...], wd_ref[...]), post_g)


def _ffn_halves(fc, xs, sh, sc, gate, pre_g, post_g, wu_ref, wg_ref, cw_ref, cb, wd_ref, act_ref):
    for s, x in enumerate(xs):
        _ffn_up(fc, x, sh, sc, pre_g, wu_ref, wg_ref, cw_ref, cb, act_ref.at[s])
    return [_ffn_down(x, gate, post_g, wd_ref, act_ref.at[s]) for s, x in enumerate(xs)]


RADIX = 4


def _fourier_channel_core(xq, sh, sc, pre_g, cs_ref, g_ref, rs=slice(None)):
    gd = cs_ref.shape[0]
    hb = [(_rms(x, pre_g) * (1.0 + sc) + sh).astype(BF16) for x in xq]
    for g in range(hb[0].shape[1] // gd):
        lanes = slice(g * gd, (g + 1) * gd)
        r = [_bdot(h[:, lanes], cs_ref[...]) for h in hb]
        a = [v[:, :gd] for v in r]
        b = [v[:, gd:] for v in r]
        a02p, a02m, a13p, a13m = a[0] + a[2], a[0] - a[2], a[1] + a[3], a[1] - a[3]
        b02p, b02m, b13p, b13m = b[0] + b[2], b[0] - b[2], b[1] + b[3], b[1] - b[3]
        re_im = ((a02p + a13p, -(b02p + b13p)),
                 (a02m - b13m, -(b02m + a13m)),
                 (a02p - a13p, b13p - b02p),
                 (a02m + b13m, a13m - b02m))
        for k, (re, im) in enumerate(re_im):
            g_ref[0, k, 0, rs, lanes] = re.astype(BF16)
            g_ref[0, k, 1, rs, lanes] = im.astype(BF16)


def _ffn_params(layer, pre_ref, post_ref, cb_ref):
    return pre_ref[layer:layer + 1], post_ref[layer:layer + 1], cb_ref[layer:layer + 1]


def _ffn_kernel(fc, layer, x_ref, mod_ref, pre_ref, post_ref, wu_ref, wg_ref, cw_ref, cb_ref, wd_ref,
                out_ref, act_ref):
    shift, scale, gate = _mod_rows(mod_ref, pl.program_id(0))[3:]
    pre_g, post_g, cb = _ffn_params(layer, pre_ref, post_ref, cb_ref)
    nsub, sub = act_ref.shape[:2]
    tiles = [slice(s * sub, (s + 1) * sub) for s in range(nsub)]
    ys = _ffn_halves(fc, [x_ref[0, rs] for rs in tiles], shift, scale, gate, pre_g, post_g, wu_ref, wg_ref, cw_ref,
                     cb, wd_ref, act_ref)
    for rs, y in zip(tiles, ys):
        out_ref[0, rs] = y


def _layer0_tail_kernel(fc, hf_ref, hb_ref, o_ref, x_ref, mod0_ref, mod1_ref, ng_ref, wo_ref, pmix_ref, premix_ref,
                        pre_ref, post_ref, wu_ref, wg_ref, cw_ref, cb_ref, wd_ref, cs_ref, out_ref, g_ref, act_ref):
    rows = x_ref.shape[2]
    _, _, gate1, shift2, scale2, gate2 = _mod_rows(mod0_ref, pl.program_id(0))
    next_shift, next_scale = _mod_rows(mod1_ref, pl.program_id(0))[:2]
    pre_g, post_g, cb = _ffn_params(0, pre_ref, post_ref, cb_ref)
    sub = act_ref.shape[1] // RADIX
    slices = [slice(s * sub, (s + 1) * sub) for s in range(act_ref.shape[0])]
    xs = []
    for rs in slices:
        cat = lambda ref: jnp.concatenate([ref[0, q, rs] for q in range(RADIX)], axis=0)
        h = cat(hf_ref).astype(F32) + cat(hb_ref).astype(F32)
        xs.append(_readout_core(h, cat(o_ref), cat(x_ref), gate1, ng_ref[...], wo_ref, pmix_ref[0:1]))
    xs = _ffn_halves(fc, xs, shift2, scale2, gate2, pre_g, post_g, wu_ref, wg_ref, cw_ref, cb, wd_ref, act_ref)
    for s, rs in enumerate(slices):
        xq = [xs[s][q * sub:(q + 1) * sub] for q in range(RADIX)]
        for q in range(RADIX):
            out_ref[0, q, rs] = xq[q]
        _fourier_channel_core(xq, next_shift, next_scale, premix_ref[1:2], cs_ref, g_ref, rs)


def _ffn_specs(layer, pre_g, post_g, d, f, cw, cb):
    def lay(rows, cols, col_block=0):
        return pl.BlockSpec((None, rows, cols), lambda b, i: (layer, 0, col_block),
                            pipeline_mode=pl.Buffered(1))
    return [_const_spec(pre_g), _const_spec(post_g), lay(d, f, 0), lay(d, f, 1), lay(cw.shape[1], f),
            _const_spec(cb), lay(f, d)]


def _ffn(x, mod, layer, pre_g, post_g, w_up, cw, cb, w_down, tm, fc):
    bsz, t, d = x.shape
    f = w_down.shape[1]
    tok = pl.BlockSpec((1, tm, d), lambda b, i: (b, i, 0))
    return pl.pallas_call(
        functools.partial(_ffn_kernel, fc, layer),
        grid=(bsz, t // tm),
        in_specs=[tok, _mod_spec(layer, *mod.shape[1:])] + _ffn_specs(layer, pre_g, post_g, d, f, cw, cb),
        out_specs=tok,
        out_shape=jax.ShapeDtypeStruct((bsz, t, d), F32),
        scratch_shapes=[pltpu.VMEM((tm // FFN_SUBTILE_ROWS, FFN_SUBTILE_ROWS, f), BF16)],
        compiler_params=_params("parallel", "parallel"),
        name="conv_ffn",
    )(x, mod, pre_g, post_g, w_up, w_up, cw, cb, w_down)


def _layer0_tail(hf, hb, sig_o, x, mod, norm_g, w_out, post_mix_g, pre_mix_g, pre_g, post_g, w_up, cw, cb, w_down,
                 cs, rows, fc):
    bsz, t, d = x.shape
    hv = hf.shape[2]
    f = w_down.shape[1]
    tq = t // RADIX
    quarters = lambda a: a.reshape(bsz, RADIX, tq, a.shape[2])
    tok = lambda n: pl.BlockSpec((1, RADIX, rows, n), lambda b, i: (b, 0, i, 0))
    out, g = pl.pallas_call(
        functools.partial(_layer0_tail_kernel, fc),
        grid=(bsz, tq // rows),
        in_specs=([tok(hv), tok(hv), tok(hv), tok(d), _mod_spec(0, *mod.shape[1:]), _mod_spec(1, *mod.shape[1:]),
                   _const_spec(norm_g), _const_spec(w_out), _const_spec(post_mix_g), _const_spec(pre_mix_g)]
                  + _ffn_specs(0, pre_g, post_g, d, f, cw, cb) + [_const_spec(cs)]),
        out_specs=[tok(d), pl.BlockSpec((1, RADIX, 2, rows, d), lambda b, i: (b, 0, 0, i, 0))],
        out_shape=[jax.ShapeDtypeStruct((bsz, RADIX, tq, d), F32),
                   jax.ShapeDtypeStruct((bsz, RADIX, 2, tq, d), BF16)],
        scratch_shapes=[pltpu.VMEM((RADIX * rows // FFN_SUBTILE_ROWS, FFN_SUBTILE_ROWS, f), BF16)],
        compiler_params=_params("parallel", "parallel"),
        name="layer0_tail",
    )(quarters(hf), quarters(hb), quarters(sig_o), quarters(x), mod, mod, norm_g, w_out, post_mix_g, pre_mix_g,
      pre_g, post_g, w_up, w_up, cw, cb, w_down, cs)
    return out.reshape(bsz, t, d), g


def _fourier_token_kernel(scale, dft_ref, g_ref, w_ref, bias_ref, post_ref, mod_ref, x_ref, out_ref,
                          x_s, out_s):
    tm = dft_ref.shape[1]
    nblk = x_s.shape[0]
    gate = _mod_rows(mod_ref, pl.program_id(0))[2]
    for c in range(nblk):
        x_s[c] = x_ref[0, :, c * LANES:(c + 1) * LANES]
    ys = [(_bdot(dft_ref[r], g_ref[0, r]) * scale).astype(BF16) for r in range(RADIX)]
    zs = [_bdot(y, w_ref[...]) + bias_ref[...] for y in ys]
    for r in range(RADIX):
        rows = pl.ds(r, tm, stride=RADIX)
        xr = jnp.concatenate([x_s[c, rows, :] for c in range(nblk)], axis=1)
        res = xr + gate * _rms(zs[r], post_ref[1:2])
        for c in range(nblk):
            out_s[c, rows, :] = res[:, c * LANES:(c + 1) * LANES]
    for c in range(nblk):
        out_ref[0, :, c * LANES:(c + 1) * LANES] = out_s[c]


def _fourier_token(dft, g, w, bias, post_mix_g, mod, x, scale, tm):
    bsz, t, d = x.shape
    tq = t // RADIX
    tok = pl.BlockSpec((1, RADIX * tm, d), lambda b, i: (b, i, 0))
    return pl.pallas_call(
        functools.partial(_fourier_token_kernel, scale),
        grid=(bsz, tq // tm),
        in_specs=[pl.BlockSpec((RADIX, tm, 2 * tq), lambda b, i: (0, i, 0)),
                  pl.BlockSpec((1, RADIX, 2 * tq, d), lambda b, i: (b, 0, 0, 0)),
                  _const_spec(w), _const_spec(bias), _const_spec(post_mix_g), _mod_spec(1, *mod.shape[1:]), tok],
        out_specs=tok,
        out_shape=jax.ShapeDtypeStruct((bsz, t, d), F32),
        scratch_shapes=[pltpu.VMEM((d // LANES, RADIX * tm, LANES), F32)] * 2,
        compiler_params=_params("parallel", "arbitrary"),
        name="fourier_token_dft",
    )(dft, g.reshape(bsz, RADIX, 2 * tq, d), w, bias, post_mix_g, mod, x)


def _dft_tables(t, gd):
    idx = np.arange(gd, dtype=np.int64)
    ang = 2.0 * np.pi * ((idx[:, None] * idx[None, :]) % gd).astype(np.float64) / gd
    chan = np.concatenate([np.cos(ang), np.sin(ang)], axis=1).astype(np.float32)
    tq = t // RADIX
    k = RADIX * np.arange(tq, dtype=np.int64)[None, :, None] + np.arange(RADIX, dtype=np.int64)[:, None, None]
    ang = 2.0 * np.pi * ((k * np.arange(tq, dtype=np.int64)[None, None, :]) % t).astype(np.float64) / t
    tok = np.concatenate([np.cos(ang), np.sin(ang)], axis=2).astype(np.float32)
    return jnp.asarray(chan).astype(BF16), jnp.asarray(tok).astype(BF16)


def kernel(x, c, ctx, c_ctx, ada_w, ada_b, pre_mix_g, post_mix_g, pre_ffn_g, post_ffn_g, ffn_up_w, ffn_conv_w,
           ffn_conv_b, ffn_down_w, m_in_w, m_in_b, m_norm_g, m_out_w, f_out_w, f_out_b):
    bsz, t, d = x.shape
    tc = ctx.shape[1]
    f = ffn_down_w.shape[1]
    assert ada_w.shape[0] == 2 and m_in_w.shape[0] == 1 and f_out_w.shape[0] == 1
    assert bsz + 1 <= COND_ROWS and t % M_CHUNK == 0 and tc % M_CHUNK == 0 and t % GRID_W == 0
    hv = m_out_w.shape[1]
    hk = (m_in_w.shape[2] - 2 * hv - 4 * M_HEADS) // 2
    tm = min(512, t)
    fc = 256
    assert f % fc == 0 and t % tm == 0 and tm % GRID_W == 0

    mod, w_in = _ada(c, c_ctx.reshape(1, d), ada_w, ada_b, jnp.swapaxes(m_in_w[0], 0, 1))

    b_in = jnp.pad(m_in_b, ((0, 0), (0, w_in.shape[0] - m_in_b.shape[1])))
    kc, vc, gc = _inproj(ctx, mod, bsz, pre_mix_g, w_in, b_in, hk, hv, min(tm, tc), False)
    qx, ox, kx, vx, gx, w_up, w_down, w_mo, w_fo = _inproj(
        x, mod, None, pre_mix_g, w_in, b_in, hk, hv, tm, True, convert=(ffn_up_w, ffn_down_w, m_out_w, f_out_w))
    gates_f, gates_b = _gate_prep(gc, gx)
    state = _scan(None, kc, vc, gates_f, gates_b, t // M_CHUNK, None)
    hf, hb = _scan(qx, kx, vx, gates_f, gates_b, 0, state)
    gd = d // F_GROUPS
    tq = t // RADIX
    rows, tm_tok = min(128, tq), min(256, tq)
    assert t % RADIX == 0 and tq % rows == 0 and tq % tm_tok == 0
    assert (RADIX * rows) % FFN_SUBTILE_ROWS == 0 and FFN_SUBTILE_ROWS % (RADIX * GRID_W) == 0
    chan, tok = _dft_tables(t, gd)
    x, g = _layer0_tail(hf, hb, ox, x, mod, m_norm_g, w_mo[0], post_mix_g, pre_mix_g, pre_ffn_g, post_ffn_g,
                        w_up, ffn_conv_w, ffn_conv_b, w_down, chan, rows, fc)

    x = _fourier_token(tok, g, w_fo[0], f_out_b, post_mix_g, mod, x, float(1.0 / np.sqrt(t * gd)), tm_tok)
    return _ffn(x, mod, 1, pre_ffn_g, post_ffn_g, w_up, ffn_conv_w, ffn_conv_b, w_down, min(1024, t), fc)
```

```python
import functools

import numpy as np
import jax
import jax.numpy as jnp
from jax import lax
from jax.experimental import pallas as pl
from jax.experimental.pallas import tpu as pltpu

F32 = jnp.float32
BF16 = jnp.bfloat16

M_HEADS = 4
M_CHUNK = 128
F_GROUPS = 8
GRID_W = 64
EPS = 1e-6
N_MOD = 6
COND_ROWS = 16
FFN_SUBTILE_ROWS = 256
SCAN_CHUNKS_PER_STEP = 8

VMEM_LIMIT_BYTES = 56 * 1024 * 1024
LANES = 128


def _params(*sem):
    return pltpu.CompilerParams(dimension_semantics=sem, vmem_limit_bytes=VMEM_LIMIT_BYTES)


def _rms(x, g):
    return x * lax.rsqrt(jnp.mean(x * x, axis=-1, keepdims=True) + EPS) * g


def _bdot(a, b):
    return jnp.dot(a, b, preferred_element_type=F32)


def _mod_rows(mod_ref, row):
    m = mod_ref[pl.ds(row, 1), :]
    d = m.shape[1] // N_MOD
    return [m[:, k * d:(k + 1) * d] for k in range(N_MOD)]


def _mod_spec(layer, cond_rows, width):
    return pl.BlockSpec((None, cond_rows, width), lambda b, i: (layer, 0, 0), pipeline_mode=pl.Buffered(1))


def _const_spec(a):
    return pl.BlockSpec(a.shape, lambda b, i: (0,) * a.ndim, pipeline_mode=pl.Buffered(1))


def _ada_kernel(valid_rows, c_ref, cc_ref, w_ref, b_ref, win_ref, o_ref, win_out):
    pad = jnp.zeros((o_ref.shape[1] - c_ref.shape[0] - 1, c_ref.shape[1]), F32)
    c = jnp.concatenate([c_ref[...], cc_ref[...], pad], axis=0)
    s = c * jax.nn.sigmoid(c)
    o_ref[0] = _bdot(s.astype(BF16), w_ref[0].astype(BF16)) + b_ref[pl.ds(pl.program_id(0), 1), :]
    step = pl.program_id(0) * pl.num_programs(1) + pl.program_id(1)
    row = step * win_ref.shape[0] + lax.broadcasted_iota(jnp.int32, win_ref.shape, 0)
    win_out[...] = jnp.where(row < valid_rows, win_ref[...], 0.0).astype(BF16)


def _ada(c, c_ctx, ada_w, ada_b, w_in_t):
    depth, d, n = ada_w.shape
    tn = n // 4
    steps = depth * (n // tn)
    rows, cols = w_in_t.shape
    padded = rows + (-rows) % LANES
    blk = padded // steps
    assert padded % (16 * steps) == 0 and (steps - 1) * blk < rows
    return pl.pallas_call(
        functools.partial(_ada_kernel, rows),
        grid=(depth, n // tn),
        in_specs=[pl.BlockSpec(c.shape, lambda i, j: (0, 0)),
                  pl.BlockSpec(c_ctx.shape, lambda i, j: (0, 0)),
                  pl.BlockSpec((1, d, tn), lambda i, j: (i, 0, j)),
                  pl.BlockSpec((depth, tn), lambda i, j: (0, j)),
                  pl.BlockSpec((blk, cols), lambda i, j: (i * (n // tn) + j, 0))],
        out_specs=[pl.BlockSpec((1, COND_ROWS, tn), lambda i, j: (i, 0, j)),
                   pl.BlockSpec((blk, cols), lambda i, j: (i * (n // tn) + j, 0))],
        out_shape=[jax.ShapeDtypeStruct((depth, COND_ROWS, n), F32),
                   jax.ShapeDtypeStruct((padded, cols), BF16)],
        compiler_params=_params("arbitrary", "arbitrary"),
        name="ada",
    )(c, c_ctx, ada_w, ada_b, w_in_t)


def _inproj_kernel(nconv, with_qo, cond_row, x_ref, mod_ref, g_ref, w_ref, b_ref, *refs):
    nout = 5 if with_qo else 3
    conv_in, outs, conv_out = refs[:nconv], refs[nconv:nconv + nout], refs[nconv + nout:]
    kt_out, v_out, gate_out = outs[-3:]
    hk, hv = kt_out.shape[1], v_out.shape[2]
    dk = hk // M_HEADS
    shift, scale = _mod_rows(mod_ref, pl.program_id(0) if cond_row is None else cond_row)[:2]
    sub = min(x_ref.shape[1], FFN_SUBTILE_ROWS)
    for rs in [slice(s, s + sub) for s in range(0, x_ref.shape[1], sub)]:
        h = _rms(x_ref[0, rs], g_ref[0:1]) * (1.0 + scale) + shift
        hb = h.astype(BF16)

        def proj(lo, hi):
            return lax.dot_general(hb, w_ref[lo:hi, :], (((1,), (1,)), ((), ())),
                                   preferred_element_type=F32) + b_ref[:, lo:hi]

        if with_qo:
            q_out, o_out = outs[:2]
            q_out[0, rs] = proj(0, hk).astype(BF16)
            o_out[0, rs] = jax.nn.sigmoid(proj(2 * hk + hv, 2 * hk + 2 * hv)).astype(BF16)
        kt_out[0, :, rs] = jnp.transpose(proj(hk, 2 * hk) * (dk ** -0.5)).astype(BF16)
        v_out[0, rs] = proj(2 * hk, 2 * hk + hv).astype(BF16)
        gate_out[0, :, rs] = jnp.transpose(proj(2 * hk + 2 * hv, w_ref.shape[0]))[:gate_out.shape[1]]
    for src_ref, dst_ref in zip(conv_in, conv_out):
        dst_ref[...] = src_ref[...].astype(BF16)


def _inproj(x, mod, cond_row, g, w, bias, hk, hv, tm, with_qo, convert=()):
    bsz, t, d = x.shape
    nt = t // tm
    full = lambda a: pl.BlockSpec(a.shape, lambda b, i: (0, 0))
    tok = lambda n: pl.BlockSpec((1, tm, n), lambda b, i: (b, i, 0))
    tok_t = lambda n: pl.BlockSpec((1, n, tm), lambda b, i: (b, 0, i))
    flat = [a.reshape(-1, a.shape[-1]) for a in convert]
    conv_specs = [pl.BlockSpec((a.shape[0] // (bsz * nt), a.shape[1]), lambda b, i: (b * nt + i, 0)) for a in flat]
    assert all(a.shape[0] % (16 * bsz * nt) == 0 for a in flat)
    qo_specs = [tok(hk), tok(hv)] if with_qo else []
    qo_shapes = [jax.ShapeDtypeStruct((bsz, t, hk), BF16), jax.ShapeDtypeStruct((bsz, t, hv), BF16)] if with_qo else []
    outs = pl.pallas_call(
        functools.partial(_inproj_kernel, len(flat), with_qo, cond_row),
        grid=(bsz, nt),
        in_specs=[tok(d), _mod_spec(0, *mod.shape[1:]), full(g), full(w), full(bias)] + conv_specs,
        out_specs=qo_specs + [tok_t(hk), tok(hv), tok_t(4 * M_HEADS)] + conv_specs,
        out_shape=qo_shapes + [jax.ShapeDtypeStruct((bsz, hk, t), BF16), jax.ShapeDtypeStruct((bsz, t, hv), BF16),
                               jax.ShapeDtypeStruct((bsz, 4 * M_HEADS, t), F32)]
                  + [jax.ShapeDtypeStruct(a.shape, BF16) for a in flat],
        compiler_params=_params("parallel", "parallel"),
        name="mlstm_inproj",
    )(x, mod, g, w, bias, *flat)
    nout = len(outs) - len(flat)
    return list(outs[:nout]) + [o.reshape(a.shape) for o, a in zip(outs[nout:], convert)]


def _lane_scan(x, op, fill, reverse, seg):
    n = x.shape[-1]
    lane = lax.broadcasted_iota(jnp.int32, x.shape, x.ndim - 1) % seg
    d = 1
    while d < seg:
        if reverse:
            shifted = jnp.where(lane < seg - d, pltpu.roll(x, n - d, x.ndim - 1), fill)
        else:
            shifted = jnp.where(lane >= d, pltpu.roll(x, d, x.ndim - 1), fill)
        x = op(x, shifted)
        d *= 2
    return x


def _log_sigmoid(x):
    return jnp.minimum(x, 0.0) - jnp.log1p(jnp.exp(-jnp.abs(x)))


def _gate_kernel(orders, gf_ref, gb_ref, of_ref, ob_ref):
    L = M_CHUNK
    for d, (g_ref, o_ref) in enumerate(((gf_ref, of_ref), (gb_ref, ob_ref))):
        edge = L - 1 if d == 0 else 0
        lf = _log_sigmoid(g_ref[:, 1])
        b_all = _lane_scan(lf, jnp.add, 0.0, d == 1, L)
        r1_all = g_ref[:, 0] - b_all
        cm_all = _lane_scan(r1_all, jnp.maximum, -jnp.inf, d == 1, L)
        m = jnp.zeros(lf.shape[:-1] + (1,), F32)
        for c in orders[d]:
            lanes = slice(c * L, (c + 1) * L)
            b, r1, cm = b_all[..., lanes], r1_all[..., lanes], cm_all[..., lanes]
            mx = jnp.maximum(m, cm[..., edge:edge + 1])
            mc = jnp.maximum(m, cm)
            o_ref[0, :, :, lanes] = r1
            o_ref[1, :, :, lanes] = mc
            o_ref[2, :, :, lanes] = jnp.broadcast_to(m, r1.shape)
            o_ref[3, :, :, lanes] = jnp.exp(-(b + mc))
            o_ref[4, :, :, lanes] = jnp.exp(r1 - mx)
            o_ref[5, :, :, lanes] = jnp.broadcast_to(jnp.exp(m - mx), r1.shape)
            m = b[..., edge:edge + 1] + mx


def _gate_prep(gates_c, gates_x):
    bsz = gates_c.shape[0]
    nlt, nct = gates_x.shape[2] // M_CHUNK, gates_c.shape[2] // M_CHUNK
    n = (nlt + nct) * M_CHUNK
    fwd = tuple(range(nlt, nlt + nct)) + tuple(range(nlt))
    bwd = tuple(range(nlt + nct - 1, nlt - 1, -1)) + tuple(range(nlt - 1, -1, -1))
    gf = gb = jnp.concatenate([gates_x, gates_c], axis=2).reshape(bsz, 4, M_HEADS, n)
    out_spec = pl.BlockSpec((6, bsz, M_HEADS, n), lambda j: (0, 0, 0, 0))
    return pl.pallas_call(
        functools.partial(_gate_kernel, (fwd, bwd)),
        grid=(1,),
        in_specs=[pl.BlockSpec((bsz, 2, M_HEADS, n), lambda j: (0, 0, 0, 0)),
                  pl.BlockSpec((bsz, 2, M_HEADS, n), lambda j: (0, 1, 0, 0))],
        out_specs=[out_spec, out_spec],
        out_shape=[jax.ShapeDtypeStruct((6, bsz, M_HEADS, n), F32)] * 2,
        compiler_params=_params("arbitrary"),
        name="mlstm_gates",
    )(gf, gb)


def _scan_kernel(cps, with_outputs, *refs):
    if with_outputs:
        (qf, ktf, vf, rf, qb, ktb, vb, rb, c0, hf_out, hb_out, c_s) = refs
        h_outs = (hf_out, hb_out)
        streams = ((qf, ktf, vf, rf), (qb, ktb, vb, rb))
    else:
        (ktf, vf, rf, ktb, vb, rb, c_s) = refs
        streams = ((None, ktf, vf, rf), (None, ktb, vb, rb))
    s = pl.program_id(1)
    L = M_CHUNK
    dk = ktf.shape[1] // M_HEADS
    dv = vf.shape[2] // M_HEADS

    @pl.when(s == 0)
    def _():
        if with_outputs:
            c_s[...] = c0[...]
        else:
            c_s[...] = jnp.zeros(c_s.shape, F32)

    row = lax.broadcasted_iota(jnp.int32, (L, L), 0)
    col = lax.broadcasted_iota(jnp.int32, (L, L), 1)
    for cc in range(cps):
        sub = (cc, cps - 1 - cc)
        blk = [slice(i * L, (i + 1) * L) for i in sub]
        if with_outputs:
            stack = [r_ref[j, 0, :, blk[d]] for d, r_ref in enumerate((rf, rb)) for j in (1, 3)]
            stack.append(jnp.zeros((L - 4 * M_HEADS, L), F32))
            cols = jnp.transpose(jnp.concatenate(stack, axis=0))
        for d in range(2):
            q_ref, kt_ref, v_ref, row_ref = streams[d]
            mask = (col <= row) if d == 0 else (col >= row)
            for h in range(M_HEADS):
                kt = kt_ref[0, h * dk:(h + 1) * dk, blk[d]]
                v = v_ref[0, blk[d], h * dv:(h + 1) * dv]
                c_old = c_s[0, d, h]
                kw = kt.astype(F32) * row_ref[4, 0, h:h + 1, blk[d]]
                n_upd = jnp.broadcast_to(jnp.sum(kw, axis=1, keepdims=True), (dk, LANES))
                upd = jnp.concatenate([_bdot(kw.astype(BF16), v), n_upd], axis=1)
                c_s[0, d, h] = row_ref[5, 0, h:h + 1, sub[d] * L:sub[d] * L + 1] * c_old + upd
                if not with_outputs:
                    continue
                q = q_ref[0, blk[d], h * dk:(h + 1) * dk]
                cb = c_old.astype(BF16)
                j = 2 * M_HEADS * d + h
                mc = cols[:, j:j + 1]
                fl = cols[:, j + M_HEADS:j + M_HEADS + 1]
                p = jnp.exp(jnp.where(mask, row_ref[0, 0, h:h + 1, blk[d]] - mc, -jnp.inf))
                wi = jnp.exp(row_ref[2, 0, h:h + 1, blk[d]] - mc)
                sq = _bdot(q, jnp.concatenate([kt, cb[:, dv:]], axis=1))
                sp = sq[:, :L] * p
                lhs = jnp.concatenate([sp, q.astype(F32) * wi], axis=1).astype(BF16)
                num = _bdot(lhs, jnp.concatenate([v, cb[:, :dv]], axis=0))
                den = wi * sq[:, L:] + jnp.sum(sp, axis=1, keepdims=True)
                inv = 1.0 / jnp.maximum(jnp.abs(den), fl)
                h_outs[d][0, blk[d], h * dv:(h + 1) * dv] = (
                    num * jnp.concatenate([inv] * (dv // LANES), axis=1)).astype(BF16)


def _scan(q, kt, v, gates_f, gates_b, off, init):
    bsz, hk, t = kt.shape
    hv = v.shape[2]
    dk, dv = hk // M_HEADS, hv // M_HEADS
    assert dk == M_CHUNK == LANES and dv % LANES == 0
    nc = t // M_CHUNK
    cps = max(c for c in (SCAN_CHUNKS_PER_STEP, 2, 1) if nc % c == 0 and off % c == 0)
    nsteps = nc // cps
    L = cps * M_CHUNK
    with_outputs = init is not None

    def specs(cidx):
        sp = [pl.BlockSpec((1, hk, L), lambda b, s: (b, 0, cidx(s))),
              pl.BlockSpec((1, L, hv), lambda b, s: (b, cidx(s), 0)),
              pl.BlockSpec((6, 1, M_HEADS, L), lambda b, s: (0, b, 0, off // cps + cidx(s)))]
        if with_outputs:
            sp = [pl.BlockSpec((1, L, hk), lambda b, s: (b, cidx(s), 0))] + sp
        return sp

    lead = [q] if with_outputs else []
    args = lead + [kt, v, gates_f] + lead + [kt, v, gates_b]
    in_specs = specs(lambda s: s) + specs(lambda s: nsteps - 1 - s)
    state_shape = (bsz, 2, M_HEADS, dk, dv + LANES)
    state_spec = pl.BlockSpec((1,) + state_shape[1:], lambda b, s: (b, 0, 0, 0, 0))
    if with_outputs:
        args.append(init)
        in_specs.append(state_spec)
        out_specs = [pl.BlockSpec((1, L, hv), lambda b, s: (b, s, 0)),
                     pl.BlockSpec((1, L, hv), lambda b, s: (b, nsteps - 1 - s, 0))]
        out_shape = [jax.ShapeDtypeStruct((bsz, t, hv), BF16)] * 2
        scratch = [pltpu.VMEM((1,) + state_shape[1:], F32)]
    else:
        out_specs = state_spec
        out_shape = jax.ShapeDtypeStruct(state_shape, F32)
        scratch = []
    return pl.pallas_call(
        functools.partial(_scan_kernel, cps, with_outputs),
        grid=(bsz, nsteps),
        in_specs=in_specs,
        out_specs=out_specs,
        out_shape=out_shape,
        scratch_shapes=scratch,
        compiler_params=_params("parallel", "arbitrary"),
        name="mlstm_scan" if with_outputs else "mlstm_ctx_state",
    )(*args)


def _readout_core(h, sig_o, x, gate, norm_g, w_ref, post_g):
    dv = h.shape[1] // M_HEADS
    parts = []
    for hd in range(M_HEADS):
        hh = h[:, hd * dv:(hd + 1) * dv]
        parts.append(hh * lax.rsqrt(jnp.mean(hh * hh, axis=-1, keepdims=True) + EPS))
    y = (jnp.concatenate(parts, axis=-1) * norm_g * sig_o.astype(F32)).astype(BF16)
    return x + gate * _rms(_bdot(y, w_ref[...]), post_g)


def _ffn_up(fc, x, sh, sc, pre_g, wu_ref, wg_ref, cw_ref, cb, act_ref):
    tm = x.shape[0]
    f = wu_ref.shape[1]
    hb = (_rms(x, pre_g) * (1.0 + sc) + sh).astype(BF16)
    pos = lax.broadcasted_iota(jnp.int32, (tm, fc), 0) % GRID_W
    has_prev = pos != 0
    has_next = pos != GRID_W - 1
    for j in range(f // fc):
        cs = slice(j * fc, (j + 1) * fc)
        u = _bdot(hb, wu_ref[:, cs])
        g = _bdot(hb, wg_ref[:, cs])
        g_prev = jnp.where(has_prev, pltpu.roll(g, 1, 0), 0.0)
        g_next = jnp.where(has_next, pltpu.roll(g, tm - 1, 0), 0.0)
        gc = g_prev * cw_ref[0:1, cs] + g * cw_ref[1:2, cs] + g_next * cw_ref[2:3, cs] + cb[:, cs]
        act_ref[:, cs] = (gc * jax.nn.sigmoid(gc) * u).astype(BF16)


def _ffn_down(x, gate, post_g, wd_ref, act_ref):
    return x + gate * _rms(_bdot(act_ref[...], wd_ref[...]), post_g)


def _ffn_halves(fc, xs, sh, sc, gate, pre_g, post_g, wu_ref, wg_ref, cw_ref, cb, wd_ref, act_ref):
    for s, x in enumerate(xs):
        _ffn_up(fc, x, sh, sc, pre_g, wu_ref, wg_ref, cw_ref, cb, act_ref.at[s])
    return [_ffn_down(x, gate, post_g, wd_ref, act_ref.at[s]) for s, x in enumerate(xs)]


RADIX = 4


def _fourier_channel_core(xq, sh, sc, pre_g, cs_ref, g_ref, rs=slice(None)):
    gd = cs_ref.shape[0]
    hb = [(_rms(x, pre_g) * (1.0 + sc) + sh).astype(BF16) for x in xq]
    for g in range(hb[0].shape[1] // gd):
        lanes = slice(g * gd, (g + 1) * gd)
        r = [_bdot(h[:, lanes], cs_ref[...]) for h in hb]
        a = [v[:, :gd] for v in r]
        b = [v[:, gd:] for v in r]
        a02p, a02m, a13p, a13m = a[0] + a[2], a[0] - a[2], a[1] + a[3], a[1] - a[3]
        b02p, b02m, b13p, b13m = b[0] + b[2], b[0] - b[2], b[1] + b[3], b[1] - b[3]
        re_im = ((a02p + a13p, -(b02p + b13p)),
                 (a02m - b13m, -(b02m + a13m)),
                 (a02p - a13p, b13p - b02p),
                 (a02m + b13m, a13m - b02m))
        for k, (re, im) in enumerate(re_im):
            g_ref[0, k, 0, rs, lanes] = re.astype(BF16)
            g_ref[0, k, 1, rs, lanes] = im.astype(BF16)


def _ffn_params(layer, pre_ref, post_ref, cb_ref):
    return pre_ref[layer:layer + 1], post_ref[layer:layer + 1], cb_ref[layer:layer + 1]


def _ffn_kernel(fc, layer, x_ref, mod_ref, pre_ref, post_ref, wu_ref, wg_ref, cw_ref, cb_ref, wd_ref,
                out_ref, act_ref):
    shift, scale, gate = _mod_rows(mod_ref, pl.program_id(0))[3:]
    pre_g, post_g, cb = _ffn_params(layer, pre_ref, post_ref, cb_ref)
    nsub, sub = act_ref.shape[:2]
    tiles = [slice(s * sub, (s + 1) * sub) for s in range(nsub)]
    ys = _ffn_halves(fc, [x_ref[0, rs] for rs in tiles], shift, scale, gate, pre_g, post_g, wu_ref, wg_ref, cw_ref,
                     cb, wd_ref, act_ref)
    for rs, y in zip(tiles, ys):
        out_ref[0, rs] = y


def _layer0_tail_kernel(fc, hf_ref, hb_ref, o_ref, x_ref, mod0_ref, mod1_ref, ng_ref, wo_ref, pmix_ref, premix_ref,
                        pre_ref, post_ref, wu_ref, wg_ref, cw_ref, cb_ref, wd_ref, cs_ref, out_ref, g_ref, act_ref):
    rows = x_ref.shape[2]
    _, _, gate1, shift2, scale2, gate2 = _mod_rows(mod0_ref, pl.program_id(0))
    next_shift, next_scale = _mod_rows(mod1_ref, pl.program_id(0))[:2]
    pre_g, post_g, cb = _ffn_params(0, pre_ref, post_ref, cb_ref)
    sub = act_ref.shape[1] // RADIX
    slices = [slice(s * sub, (s + 1) * sub) for s in range(act_ref.shape[0])]
    xs = []
    for rs in slices:
        cat = lambda ref: jnp.concatenate([ref[0, q, rs] for q in range(RADIX)], axis=0)
        h = cat(hf_ref).astype(F32) + cat(hb_ref).astype(F32)
        xs.append(_readout_core(h, cat(o_ref), cat(x_ref), gate1, ng_ref[...], wo_ref, pmix_ref[0:1]))
    xs = _ffn_halves(fc, xs, shift2, scale2, gate2, pre_g, post_g, wu_ref, wg_ref, cw_ref, cb, wd_ref, act_ref)
    for s, rs in enumerate(slices):
        xq = [xs[s][q * sub:(q + 1) * sub] for q in range(RADIX)]
        for q in range(RADIX):
            out_ref[0, q, rs] = xq[q]
        _fourier_channel_core(xq, next_shift, next_scale, premix_ref[1:2], cs_ref, g_ref, rs)


def _ffn_specs(layer, pre_g, post_g, d, f, cw, cb):
    def lay(rows, cols, col_block=0):
        return pl.BlockSpec((None, rows, cols), lambda b, i: (layer, 0, col_block),
                            pipeline_mode=pl.Buffered(1))
    return [_const_spec(pre_g), _const_spec(post_g), lay(d, f, 0), lay(d, f, 1), lay(cw.shape[1], f),
            _const_spec(cb), lay(f, d)]


def _ffn(x, mod, layer, pre_g, post_g, w_up, cw, cb, w_down, tm, fc):
    bsz, t, d = x.shape
    f = w_down.shape[1]
    tok = pl.BlockSpec((1, tm, d), lambda b, i: (b, i, 0))
    return pl.pallas_call(
        functools.partial(_ffn_kernel, fc, layer),
        grid=(bsz, t // tm),
        in_specs=[tok, _mod_spec(layer, *mod.shape[1:])] + _ffn_specs(layer, pre_g, post_g, d, f, cw, cb),
        out_specs=tok,
        out_shape=jax.ShapeDtypeStruct((bsz, t, d), F32),
        scratch_shapes=[pltpu.VMEM((tm // FFN_SUBTILE_ROWS, FFN_SUBTILE_ROWS, f), BF16)],
        compiler_params=_params("parallel", "parallel"),
        name="conv_ffn",
    )(x, mod, pre_g, post_g, w_up, w_up, cw, cb, w_down)


def _layer0_tail(hf, hb, sig_o, x, mod, norm_g, w_out, post_mix_g, pre_mix_g, pre_g, post_g, w_up, cw, cb, w_down,
                 cs, rows, fc):
    bsz, t, d = x.shape
    hv = hf.shape[2]
    f = w_down.shape[1]
    tq = t // RADIX
    quarters = lambda a: a.reshape(bsz, RADIX, tq, a.shape[2])
    tok = lambda n: pl.BlockSpec((1, RADIX, rows, n), lambda b, i: (b, 0, i, 0))
    out, g = pl.pallas_call(
        functools.partial(_layer0_tail_kernel, fc),
        grid=(bsz, tq // rows),
        in_specs=([tok(hv), tok(hv), tok(hv), tok(d), _mod_spec(0, *mod.shape[1:]), _mod_spec(1, *mod.shape[1:]),
                   _const_spec(norm_g), _const_spec(w_out), _const_spec(post_mix_g), _const_spec(pre_mix_g)]
                  + _ffn_specs(0, pre_g, post_g, d, f, cw, cb) + [_const_spec(cs)]),
        out_specs=[tok(d), pl.BlockSpec((1, RADIX, 2, rows, d), lambda b, i: (b, 0, 0, i, 0))],
        out_shape=[jax.ShapeDtypeStruct((bsz, RADIX, tq, d), F32),
                   jax.ShapeDtypeStruct((bsz, RADIX, 2, tq, d), BF16)],
        scratch_shapes=[pltpu.VMEM((RADIX * rows // FFN_SUBTILE_ROWS, FFN_SUBTILE_ROWS, f), BF16)],
        compiler_params=_params("parallel", "parallel"),
        name="layer0_tail",
    )(quarters(hf), quarters(hb), quarters(sig_o), quarters(x), mod, mod, norm_g, w_out, post_mix_g, pre_mix_g,
      pre_g, post_g, w_up, w_up, cw, cb, w_down, cs)
    return out.reshape(bsz, t, d), g


def _fourier_token_kernel(scale, dft_ref, g_ref, w_ref, bias_ref, post_ref, mod_ref, x_ref, out_ref,
                          x_s, out_s):
    tm = dft_ref.shape[1]
    nblk = x_s.shape[0]
    gate = _mod_rows(mod_ref, pl.program_id(0))[2]
    for c in range(nblk):
        x_s[c] = x_ref[0, :, c * LANES:(c + 1) * LANES]
    ys = [(_bdot(dft_ref[r], g_ref[0, r]) * scale).astype(BF16) for r in range(RADIX)]
    zs = [_bdot(y, w_ref[...]) + bias_ref[...] for y in ys]
    for r in range(RADIX):
        rows = pl.ds(r, tm, stride=RADIX)
        xr = jnp.concatenate([x_s[c, rows, :] for c in range(nblk)], axis=1)
        res = xr + gate * _rms(zs[r], post_ref[1:2])
        for c in range(nblk):
            out_s[c, rows, :] = res[:, c * LANES:(c + 1) * LANES]
    for c in range(nblk):
        out_ref[0, :, c * LANES:(c + 1) * LANES] = out_s[c]


def _fourier_token(dft, g, w, bias, post_mix_g, mod, x, scale, tm):
    bsz, t, d = x.shape
    tq = t // RADIX
    tok = pl.BlockSpec((1, RADIX * tm, d), lambda b, i: (b, i, 0))
    return pl.pallas_call(
        functools.partial(_fourier_token_kernel, scale),
        grid=(bsz, tq // tm),
        in_specs=[pl.BlockSpec((RADIX, tm, 2 * tq), lambda b, i: (0, i, 0)),
                  pl.BlockSpec((1, RADIX, 2 * tq, d), lambda b, i: (b, 0, 0, 0)),
                  _const_spec(w), _const_spec(bias), _const_spec(post_mix_g), _mod_spec(1, *mod.shape[1:]), tok],
        out_specs=tok,
        out_shape=jax.ShapeDtypeStruct((bsz, t, d), F32),
        scratch_shapes=[pltpu.VMEM((d // LANES, RADIX * tm, LANES), F32)] * 2,
        compiler_params=_params("parallel", "arbitrary"),
        name="fourier_token_dft",
    )(dft, g.reshape(bsz, RADIX, 2 * tq, d), w, bias, post_mix_g, mod, x)


def _dft_tables(t, gd):
    idx = np.arange(gd, dtype=np.int64)
    ang = 2.0 * np.pi * ((idx[:, None] * idx[None, :]) % gd).astype(np.float64) / gd
    chan = np.concatenate([np.cos(ang), np.sin(ang)], axis=1).astype(np.float32)
    tq = t // RADIX
    k = RADIX * np.arange(tq, dtype=np.int64)[None, :, None] + np.arange(RADIX, dtype=np.int64)[:, None, None]
    ang = 2.0 * np.pi * ((k * np.arange(tq, dtype=np.int64)[None, None, :]) % t).astype(np.float64) / t
    tok = np.concatenate([np.cos(ang), np.sin(ang)], axis=2).astype(np.float32)
    return jnp.asarray(chan).astype(BF16), jnp.asarray(tok).astype(BF16)


def kernel(x, c, ctx, c_ctx, ada_w, ada_b, pre_mix_g, post_mix_g, pre_ffn_g, post_ffn_g, ffn_up_w, ffn_conv_w,
           ffn_conv_b, ffn_down_w, m_in_w, m_in_b, m_norm_g, m_out_w, f_out_w, f_out_b):
    bsz, t, d = x.shape
    tc = ctx.shape[1]
    f = ffn_down_w.shape[1]
    assert ada_w.shape[0] == 2 and m_in_w.shape[0] == 1 and f_out_w.shape[0] == 1
    assert bsz + 1 <= COND_ROWS and t % M_CHUNK == 0 and tc % M_CHUNK == 0 and t % GRID_W == 0
    hv = m_out_w.shape[1]
    hk = (m_in_w.shape[2] - 2 * hv - 4 * M_HEADS) // 2
    tm = min(512, t)
    fc = 256
    assert f % fc == 0 and t % tm == 0 and tm % GRID_W == 0

    mod, w_in = _ada(c, c_ctx.reshape(1, d), ada_w, ada_b, jnp.swapaxes(m_in_w[0], 0, 1))

    b_in = jnp.pad(m_in_b, ((0, 0), (0, w_in.shape[0] - m_in_b.shape[1])))
    kc, vc, gc = _inproj(ctx, mod, bsz, pre_mix_g, w_in, b_in, hk, hv, min(tm, tc), False)
    qx, ox, kx, vx, gx, w_up, w_down, w_mo, w_fo = _inproj(
        x, mod, None, pre_mix_g, w_in, b_in, hk, hv, min(1024, t), True,
        convert=(ffn_up_w, ffn_down_w, m_out_w, f_out_w))
    gates_f, gates_b = _gate_prep(gc, gx)
    state = _scan(None, kc, vc, gates_f, gates_b, t // M_CHUNK, None)
    hf, hb = _scan(qx, kx, vx, gates_f, gates_b, 0, state)
    gd = d // F_GROUPS
    tq = t // RADIX
    rows, tm_tok = min(128, tq), min(256, tq)
    assert t % RADIX == 0 and tq % rows == 0 and tq % tm_tok == 0
    assert (RADIX * rows) % FFN_SUBTILE_ROWS == 0 and FFN_SUBTILE_ROWS % (RADIX * GRID_W) == 0
    chan, tok = _dft_tables(t, gd)
    x, g = _layer0_tail(hf, hb, ox, x, mod, m_norm_g, w_mo[0], post_mix_g, pre_mix_g, pre_ffn_g, post_ffn_g,
                        w_up, ffn_conv_w, ffn_conv_b, w_down, chan, rows, fc)

    x = _fourier_token(tok, g, w_fo[0], f_out_b, post_mix_g, mod, x, float(1.0 / np.sqrt(t * gd)), tm_tok)
    return _ffn(x, mod, 1, pre_ffn_g, post_ffn_g, w_up, ffn_conv_w, ffn_conv_b, w_down, min(1024, t), fc)
```

```python
import functools

import numpy as np
import jax
import jax.numpy as jnp
from jax import lax
from jax.experimental import pallas as pl
from jax.experimental.pallas import tpu as pltpu

F32 = jnp.float32
BF16 = jnp.bfloat16

M_HEADS = 4
M_CHUNK = 128
F_GROUPS = 8
GRID_W = 64
EPS = 1e-6
N_MOD = 6
COND_ROWS = 16
FFN_SUBTILE_ROWS = 256
SCAN_CHUNKS_PER_STEP = 8

VMEM_LIMIT_BYTES = 56 * 1024 * 1024
LANES = 128


def _params(*sem):
    return pltpu.CompilerParams(dimension_semantics=sem, vmem_limit_bytes=VMEM_LIMIT_BYTES)


def _rms(x, g):
    return x * lax.rsqrt(jnp.mean(x * x, axis=-1, keepdims=True) + EPS) * g


def _bdot(a, b):
    return jnp.dot(a, b, preferred_element_type=F32)


def _mod_rows(mod_ref, row):
    m = mod_ref[pl.ds(row, 1), :]
    d = m.shape[1] // N_MOD
    return [m[:, k * d:(k + 1) * d] for k in range(N_MOD)]


def _mod_spec(layer, cond_rows, width):
    return pl.BlockSpec((None, cond_rows, width), lambda b, i: (layer, 0, 0), pipeline_mode=pl.Buffered(1))


def _const_spec(a):
    return pl.BlockSpec(a.shape, lambda b, i: (0,) * a.ndim, pipeline_mode=pl.Buffered(1))


def _ada_kernel(valid_rows, c_ref, cc_ref, w_ref, b_ref, win_ref, o_ref, win_out):
    pad = jnp.zeros((o_ref.shape[1] - c_ref.shape[0] - 1, c_ref.shape[1]), F32)
    c = jnp.concatenate([c_ref[...], cc_ref[...], pad], axis=0)
    s = c * jax.nn.sigmoid(c)
    o_ref[0] = _bdot(s.astype(BF16), w_ref[0].astype(BF16)) + b_ref[pl.ds(pl.program_id(0), 1), :]
    step = pl.program_id(0) * pl.num_programs(1) + pl.program_id(1)
    row = step * win_ref.shape[0] + lax.broadcasted_iota(jnp.int32, win_ref.shape, 0)
    win_out[...] = jnp.where(row < valid_rows, win_ref[...], 0.0).astype(BF16)


def _ada(c, c_ctx, ada_w, ada_b, w_in_t):
    depth, d, n = ada_w.shape
    tn = n // 4
    steps = depth * (n // tn)
    rows, cols = w_in_t.shape
    padded = rows + (-rows) % LANES
    blk = padded // steps
    assert padded % (16 * steps) == 0 and (steps - 1) * blk < rows
    return pl.pallas_call(
        functools.partial(_ada_kernel, rows),
        grid=(depth, n // tn),
        in_specs=[pl.BlockSpec(c.shape, lambda i, j: (0, 0)),
                  pl.BlockSpec(c_ctx.shape, lambda i, j: (0, 0)),
                  pl.BlockSpec((1, d, tn), lambda i, j: (i, 0, j)),
                  pl.BlockSpec((depth, tn), lambda i, j: (0, j)),
                  pl.BlockSpec((blk, cols), lambda i, j: (i * (n // tn) + j, 0))],
        out_specs=[pl.BlockSpec((1, COND_ROWS, tn), lambda i, j: (i, 0, j)),
                   pl.BlockSpec((blk, cols), lambda i, j: (i * (n // tn) + j, 0))],
        out_shape=[jax.ShapeDtypeStruct((depth, COND_ROWS, n), F32),
                   jax.ShapeDtypeStruct((padded, cols), BF16)],
        compiler_params=_params("arbitrary", "arbitrary"),
        name="ada",
    )(c, c_ctx, ada_w, ada_b, w_in_t)


def _inproj_kernel(nconv, with_qo, cond_row, x_ref, mod_ref, g_ref, w_ref, b_ref, *refs):
    nout = 5 if with_qo else 3
    conv_in, outs, conv_out = refs[:nconv], refs[nconv:nconv + nout], refs[nconv + nout:]
    kt_out, v_out, gate_out = outs[-3:]
    hk, hv = kt_out.shape[1], v_out.shape[2]
    dk = hk // M_HEADS
    shift, scale = _mod_rows(mod_ref, pl.program_id(0) if cond_row is None else cond_row)[:2]
    sub = min(x_ref.shape[1], FFN_SUBTILE_ROWS)
    for rs in [slice(s, s + sub) for s in range(0, x_ref.shape[1], sub)]:
        h = _rms(x_ref[0, rs], g_ref[0:1]) * (1.0 + scale) + shift
        hb = h.astype(BF16)

        def proj(lo, hi):
            return lax.dot_general(hb, w_ref[lo:hi, :], (((1,), (1,)), ((), ())),
                                   preferred_element_type=F32) + b_ref[:, lo:hi]

        if with_qo:
            q_out, o_out = outs[:2]
            q_out[0, rs] = proj(0, hk).astype(BF16)
            o_out[0, rs] = jax.nn.sigmoid(proj(2 * hk + hv, 2 * hk + 2 * hv)).astype(BF16)
        kt_out[0, :, rs] = jnp.transpose(proj(hk, 2 * hk) * (dk ** -0.5)).astype(BF16)
        v_out[0, rs] = proj(2 * hk, 2 * hk + hv).astype(BF16)
        gate_out[0, :, rs] = jnp.transpose(proj(2 * hk + 2 * hv, w_ref.shape[0]))[:gate_out.shape[1]]
    for src_ref, dst_ref in zip(conv_in, conv_out):
        dst_ref[...] = src_ref[...].astype(BF16)


def _inproj(x, mod, cond_row, g, w, bias, hk, hv, tm, with_qo, convert=()):
    bsz, t, d = x.shape
    nt = t // tm
    full = lambda a: pl.BlockSpec(a.shape, lambda b, i: (0, 0))
    tok = lambda n: pl.BlockSpec((1, tm, n), lambda b, i: (b, i, 0))
    tok_t = lambda n: pl.BlockSpec((1, n, tm), lambda b, i: (b, 0, i))
    flat = [a.reshape(-1, a.shape[-1]) for a in convert]
    conv_specs = [pl.BlockSpec((a.shape[0] // (bsz * nt), a.shape[1]), lambda b, i: (b * nt + i, 0)) for a in flat]
    assert all(a.shape[0] % (16 * bsz * nt) == 0 for a in flat)
    qo_specs = [tok(hk), tok(hv)] if with_qo else []
    qo_shapes = [jax.ShapeDtypeStruct((bsz, t, hk), BF16), jax.ShapeDtypeStruct((bsz, t, hv), BF16)] if with_qo else []
    outs = pl.pallas_call(
        functools.partial(_inproj_kernel, len(flat), with_qo, cond_row),
        grid=(bsz, nt),
        in_specs=[tok(d), _mod_spec(0, *mod.shape[1:]), full(g), full(w), full(bias)] + conv_specs,
        out_specs=qo_specs + [tok_t(hk), tok(hv), tok_t(4 * M_HEADS)] + conv_specs,
        out_shape=qo_shapes + [jax.ShapeDtypeStruct((bsz, hk, t), BF16), jax.ShapeDtypeStruct((bsz, t, hv), BF16),
                               jax.ShapeDtypeStruct((bsz, 4 * M_HEADS, t), F32)]
                  + [jax.ShapeDtypeStruct(a.shape, BF16) for a in flat],
        compiler_params=_params("parallel", "parallel"),
        name="mlstm_inproj",
    )(x, mod, g, w, bias, *flat)
    nout = len(outs) - len(flat)
    return list(outs[:nout]) + [o.reshape(a.shape) for o, a in zip(outs[nout:], convert)]


def _lane_scan(x, op, fill, reverse, seg):
    n = x.shape[-1]
    lane = lax.broadcasted_iota(jnp.int32, x.shape, x.ndim - 1) % seg
    d = 1
    while d < seg:
        if reverse:
            shifted = jnp.where(lane < seg - d, pltpu.roll(x, n - d, x.ndim - 1), fill)
        else:
            shifted = jnp.where(lane >= d, pltpu.roll(x, d, x.ndim - 1), fill)
        x = op(x, shifted)
        d *= 2
    return x


def _log_sigmoid(x):
    return jnp.minimum(x, 0.0) - jnp.log1p(jnp.exp(-jnp.abs(x)))


def _gate_kernel(orders, gf_ref, gb_ref, of_ref, ob_ref):
    L = M_CHUNK
    for d, (g_ref, o_ref) in enumerate(((gf_ref, of_ref), (gb_ref, ob_ref))):
        edge = L - 1 if d == 0 else 0
        lf = _log_sigmoid(g_ref[:, 1])
        b_all = _lane_scan(lf, jnp.add, 0.0, d == 1, L)
        r1_all = g_ref[:, 0] - b_all
        cm_all = _lane_scan(r1_all, jnp.maximum, -jnp.inf, d == 1, L)
        m = jnp.zeros(lf.shape[:-1] + (1,), F32)
        for c in orders[d]:
            lanes = slice(c * L, (c + 1) * L)
            b, r1, cm = b_all[..., lanes], r1_all[..., lanes], cm_all[..., lanes]
            mx = jnp.maximum(m, cm[..., edge:edge + 1])
            mc = jnp.maximum(m, cm)
            o_ref[0, :, :, lanes] = r1
            o_ref[1, :, :, lanes] = mc
            o_ref[2, :, :, lanes] = jnp.broadcast_to(m, r1.shape)
            o_ref[3, :, :, lanes] = jnp.exp(-(b + mc))
            o_ref[4, :, :, lanes] = jnp.exp(r1 - mx)
            o_ref[5, :, :, lanes] = jnp.broadcast_to(jnp.exp(m - mx), r1.shape)
            m = b[..., edge:edge + 1] + mx


def _gate_prep(gates_c, gates_x):
    bsz = gates_c.shape[0]
    nlt, nct = gates_x.shape[2] // M_CHUNK, gates_c.shape[2] // M_CHUNK
    n = (nlt + nct) * M_CHUNK
    fwd = tuple(range(nlt, nlt + nct)) + tuple(range(nlt))
    bwd = tuple(range(nlt + nct - 1, nlt - 1, -1)) + tuple(range(nlt - 1, -1, -1))
    gf = gb = jnp.concatenate([gates_x, gates_c], axis=2).reshape(bsz, 4, M_HEADS, n)
    out_spec = pl.BlockSpec((6, bsz, M_HEADS, n), lambda j: (0, 0, 0, 0))
    return pl.pallas_call(
        functools.partial(_gate_kernel, (fwd, bwd)),
        grid=(1,),
        in_specs=[pl.BlockSpec((bsz, 2, M_HEADS, n), lambda j: (0, 0, 0, 0)),
                  pl.BlockSpec((bsz, 2, M_HEADS, n), lambda j: (0, 1, 0, 0))],
        out_specs=[out_spec, out_spec],
        out_shape=[jax.ShapeDtypeStruct((6, bsz, M_HEADS, n), F32)] * 2,
        compiler_params=_params("arbitrary"),
        name="mlstm_gates",
    )(gf, gb)


def _scan_kernel(cps, with_outputs, *refs):
    if with_outputs:
        (qf, ktf, vf, rf, qb, ktb, vb, rb, c0, hf_out, hb_out, c_s) = refs
        h_outs = (hf_out, hb_out)
        streams = ((qf, ktf, vf, rf), (qb, ktb, vb, rb))
    else:
        (ktf, vf, rf, ktb, vb, rb, c_s) = refs
        streams = ((None, ktf, vf, rf), (None, ktb, vb, rb))
    s = pl.program_id(1)
    L = M_CHUNK
    dk = ktf.shape[1] // M_HEADS
    dv = vf.shape[2] // M_HEADS

    @pl.when(s == 0)
    def _():
        if with_outputs:
            c_s[...] = c0[...]
        else:
            c_s[...] = jnp.zeros(c_s.shape, F32)

    row = lax.broadcasted_iota(jnp.int32, (L, L), 0)
    col = lax.broadcasted_iota(jnp.int32, (L, L), 1)
    for cc in range(cps):
        sub = (cc, cps - 1 - cc)
        blk = [slice(i * L, (i + 1) * L) for i in sub]
        if with_outputs:
            stack = [r_ref[j, 0, :, blk[d]] for d, r_ref in enumerate((rf, rb)) for j in (1, 3)]
            stack.append(jnp.zeros((L - 4 * M_HEADS, L), F32))
            cols = jnp.transpose(jnp.concatenate(stack, axis=0))
        masks = ((col <= row), (col >= row))
        for h in range(M_HEADS):
            opnds, scores, weighted = {}, {}, {}
            for d in range(2):
                _, kt_ref, v_ref, row_ref = streams[d]
                kt = kt_ref[0, h * dk:(h + 1) * dk, blk[d]]
                v = v_ref[0, blk[d], h * dv:(h + 1) * dv]
                c_old = c_s[0, d, h]
                kw = kt.astype(F32) * row_ref[4, 0, h:h + 1, blk[d]]
                n_upd = jnp.broadcast_to(jnp.sum(kw, axis=1, keepdims=True), (dk, LANES))
                upd = jnp.concatenate([_bdot(kw.astype(BF16), v), n_upd], axis=1)
                c_s[0, d, h] = row_ref[5, 0, h:h + 1, sub[d] * L:sub[d] * L + 1] * c_old + upd
                if with_outputs:
                    opnds[d] = (kt, v, c_old.astype(BF16))
            if not with_outputs:
                continue
            for d in range(2):
                kt, _, cb = opnds[d]
                q = streams[d][0][0, blk[d], h * dk:(h + 1) * dk]
                scores[d] = (q, _bdot(q, jnp.concatenate([kt, cb[:, dv:]], axis=1)))
            for d in range(2):
                row_ref = streams[d][3]
                q, sq = scores[d]
                mc = cols[:, 2 * M_HEADS * d + h:2 * M_HEADS * d + h + 1]
                p = jnp.exp(jnp.where(masks[d], row_ref[0, 0, h:h + 1, blk[d]] - mc, -jnp.inf))
                wi = jnp.exp(row_ref[2, 0, h:h + 1, blk[d]] - mc)
                sp = sq[:, :L] * p
                lhs = jnp.concatenate([sp, q.astype(F32) * wi], axis=1).astype(BF16)
                den = wi * sq[:, L:] + jnp.sum(sp, axis=1, keepdims=True)
                weighted[d] = (lhs, den)
            for d in range(2):
                _, v, cb = opnds[d]
                lhs, den = weighted[d]
                j = 2 * M_HEADS * d + h + M_HEADS
                num = _bdot(lhs, jnp.concatenate([v, cb[:, :dv]], axis=0))
                inv = 1.0 / jnp.maximum(jnp.abs(den), cols[:, j:j + 1])
                h_outs[d][0, blk[d], h * dv:(h + 1) * dv] = (
                    num * jnp.concatenate([inv] * (dv // LANES), axis=1)).astype(BF16)


def _scan(q, kt, v, gates_f, gates_b, off, init):
    bsz, hk, t = kt.shape
    hv = v.shape[2]
    dk, dv = hk // M_HEADS, hv // M_HEADS
    assert dk == M_CHUNK == LANES and dv % LANES == 0
    nc = t // M_CHUNK
    cps = max(c for c in (SCAN_CHUNKS_PER_STEP, 2, 1) if nc % c == 0 and off % c == 0)
    nsteps = nc // cps
    L = cps * M_CHUNK
    with_outputs = init is not None

    def specs(cidx):
        sp = [pl.BlockSpec((1, hk, L), lambda b, s: (b, 0, cidx(s))),
              pl.BlockSpec((1, L, hv), lambda b, s: (b, cidx(s), 0)),
              pl.BlockSpec((6, 1, M_HEADS, L), lambda b, s: (0, b, 0, off // cps + cidx(s)))]
        if with_outputs:
            sp = [pl.BlockSpec((1, L, hk), lambda b, s: (b, cidx(s), 0))] + sp
        return sp

    lead = [q] if with_outputs else []
    args = lead + [kt, v, gates_f] + lead + [kt, v, gates_b]
    in_specs = specs(lambda s: s) + specs(lambda s: nsteps - 1 - s)
    state_shape = (bsz, 2, M_HEADS, dk, dv + LANES)
    state_spec = pl.BlockSpec((1,) + state_shape[1:], lambda b, s: (b, 0, 0, 0, 0))
    if with_outputs:
        args.append(init)
        in_specs.append(state_spec)
        out_specs = [pl.BlockSpec((1, L, hv), lambda b, s: (b, s, 0)),
                     pl.BlockSpec((1, L, hv), lambda b, s: (b, nsteps - 1 - s, 0))]
        out_shape = [jax.ShapeDtypeStruct((bsz, t, hv), BF16)] * 2
        scratch = [pltpu.VMEM((1,) + state_shape[1:], F32)]
    else:
        out_specs = state_spec
        out_shape = jax.ShapeDtypeStruct(state_shape, F32)
        scratch = []
    return pl.pallas_call(
        functools.partial(_scan_kernel, cps, with_outputs),
        grid=(bsz, nsteps),
        in_specs=in_specs,
        out_specs=out_specs,
        out_shape=out_shape,
        scratch_shapes=scratch,
        compiler_params=_params("parallel", "arbitrary"),
        name="mlstm_scan" if with_outputs else "mlstm_ctx_state",
    )(*args)


def _readout_core(h, sig_o, x, gate, norm_g, w_ref, post_g):
    dv = h.shape[1] // M_HEADS
    parts = []
    for hd in range(M_HEADS):
        hh = h[:, hd * dv:(hd + 1) * dv]
        parts.append(hh * lax.rsqrt(jnp.mean(hh * hh, axis=-1, keepdims=True) + EPS))
    y = (jnp.concatenate(parts, axis=-1) * norm_g * sig_o.astype(F32)).astype(BF16)
    return x + gate * _rms(_bdot(y, w_ref[...]), post_g)


def _ffn_up(fc, x, sh, sc, pre_g, wu_ref, wg_ref, cw_ref, cb, act_ref):
    tm = x.shape[0]
    f = wu_ref.shape[1]
    hb = (_rms(x, pre_g) * (1.0 + sc) + sh).astype(BF16)
    pos = lax.broadcasted_iota(jnp.int32, (tm, fc), 0) % GRID_W
    has_prev = pos != 0
    has_next = pos != GRID_W - 1
    for j in range(f // fc):
        cs = slice(j * fc, (j + 1) * fc)
        u = _bdot(hb, wu_ref[:, cs])
        g = _bdot(hb, wg_ref[:, cs])
        g_prev = jnp.where(has_prev, pltpu.roll(g, 1, 0), 0.0)
        g_next = jnp.where(has_next, pltpu.roll(g, tm - 1, 0), 0.0)
        gc = g_prev * cw_ref[0:1, cs] + g * cw_ref[1:2, cs] + g_next * cw_ref[2:3, cs] + cb[:, cs]
        act_ref[:, cs] = (gc * jax.nn.sigmoid(gc) * u).astype(BF16)


def _ffn_down(x, gate, post_g, wd_ref, act_ref):
    return x + gate * _rms(_bdot(act_ref[...], wd_ref[...]), post_g)


def _ffn_halves(fc, xs, sh, sc, gate, pre_g, post_g, wu_ref, wg_ref, cw_ref, cb, wd_ref, act_ref):
    for s, x in enumerate(xs):
        _ffn_up(fc, x, sh, sc, pre_g, wu_ref, wg_ref, cw_ref, cb, act_ref.at[s])
    return [_ffn_down(x, gate, post_g, wd_ref, act_ref.at[s]) for s, x in enumerate(xs)]


RADIX = 4


def _fourier_channel_core(xq, sh, sc, pre_g, cs_ref, g_ref, rs=slice(None)):
    gd = cs_ref.shape[0]
    hb = [(_rms(x, pre_g) * (1.0 + sc) + sh).astype(BF16) for x in xq]
    for g in range(hb[0].shape[1] // gd):
        lanes = slice(g * gd, (g + 1) * gd)
        r = [_bdot(h[:, lanes], cs_ref[...]) for h in hb]
        a = [v[:, :gd] for v in r]
        b = [v[:, gd:] for v in r]
        a02p, a02m, a13p, a13m = a[0] + a[2], a[0] - a[2], a[1] + a[3], a[1] - a[3]
        b02p, b02m, b13p, b13m = b[0] + b[2], b[0] - b[2], b[1] + b[3], b[1] - b[3]
        re_im = ((a02p + a13p, -(b02p + b13p)),
                 (a02m - b13m, -(b02m + a13m)),
                 (a02p - a13p, b13p - b02p),
                 (a02m + b13m, a13m - b02m))
        for k, (re, im) in enumerate(re_im):
            g_ref[0, k, 0, rs, lanes] = re.astype(BF16)
            g_ref[0, k, 1, rs, lanes] = im.astype(BF16)


def _ffn_params(layer, pre_ref, post_ref, cb_ref):
    return pre_ref[layer:layer + 1], post_ref[layer:layer + 1], cb_ref[layer:layer + 1]


def _ffn_kernel(fc, layer, x_ref, mod_ref, pre_ref, post_ref, wu_ref, wg_ref, cw_ref, cb_ref, wd_ref,
                out_ref, act_ref):
    shift, scale, gate = _mod_rows(mod_ref, pl.program_id(0))[3:]
    pre_g, post_g, cb = _ffn_params(layer, pre_ref, post_ref, cb_ref)
    nsub, sub = act_ref.shape[:2]
    tiles = [slice(s * sub, (s + 1) * sub) for s in range(nsub)]
    ys = _ffn_halves(fc, [x_ref[0, rs] for rs in tiles], shift, scale, gate, pre_g, post_g, wu_ref, wg_ref, cw_ref,
                     cb, wd_ref, act_ref)
    for rs, y in zip(tiles, ys):
        out_ref[0, rs] = y


def _layer0_tail_kernel(fc, hf_ref, hb_ref, o_ref, x_ref, mod0_ref, mod1_ref, ng_ref, wo_ref, pmix_ref, premix_ref,
                        pre_ref, post_ref, wu_ref, wg_ref, cw_ref, cb_ref, wd_ref, cs_ref, out_ref, g_ref, act_ref):
    rows = x_ref.shape[2]
    _, _, gate1, shift2, scale2, gate2 = _mod_rows(mod0_ref, pl.program_id(0))
    next_shift, next_scale = _mod_rows(mod1_ref, pl.program_id(0))[:2]
    pre_g, post_g, cb = _ffn_params(0, pre_ref, post_ref, cb_ref)
    sub = act_ref.shape[1] // RADIX
    slices = [slice(s * sub, (s + 1) * sub) for s in range(act_ref.shape[0])]
    xs = []
    for rs in slices:
        cat = lambda ref: jnp.concatenate([ref[0, q, rs] for q in range(RADIX)], axis=0)
        h = cat(hf_ref).astype(F32) + cat(hb_ref).astype(F32)
        xs.append(_readout_core(h, cat(o_ref), cat(x_ref), gate1, ng_ref[...], wo_ref, pmix_ref[0:1]))
    xs = _ffn_halves(fc, xs, shift2, scale2, gate2, pre_g, post_g, wu_ref, wg_ref, cw_ref, cb, wd_ref, act_ref)
    for s, rs in enumerate(slices):
        xq = [xs[s][q * sub:(q + 1) * sub] for q in range(RADIX)]
        for q in range(RADIX):
            out_ref[0, q, rs] = xq[q]
        _fourier_channel_core(xq, next_shift, next_scale, premix_ref[1:2], cs_ref, g_ref, rs)


def _ffn_specs(layer, pre_g, post_g, d, f, cw, cb):
    def lay(rows, cols, col_block=0):
        return pl.BlockSpec((None, rows, cols), lambda b, i: (layer, 0, col_block),
                            pipeline_mode=pl.Buffered(1))
    return [_const_spec(pre_g), _const_spec(post_g), lay(d, f, 0), lay(d, f, 1), lay(cw.shape[1], f),
            _const_spec(cb), lay(f, d)]


def _ffn(x, mod, layer, pre_g, post_g, w_up, cw, cb, w_down, tm, fc):
    bsz, t, d = x.shape
    f = w_down.shape[1]
    tok = pl.BlockSpec((1, tm, d), lambda b, i: (b, i, 0))
    return pl.pallas_call(
        functools.partial(_ffn_kernel, fc, layer),
        grid=(bsz, t // tm),
        in_specs=[tok, _mod_spec(layer, *mod.shape[1:])] + _ffn_specs(layer, pre_g, post_g, d, f, cw, cb),
        out_specs=tok,
        out_shape=jax.ShapeDtypeStruct((bsz, t, d), F32),
        scratch_shapes=[pltpu.VMEM((tm // FFN_SUBTILE_ROWS, FFN_SUBTILE_ROWS, f), BF16)],
        compiler_params=_params("parallel", "parallel"),
        name="conv_ffn",
    )(x, mod, pre_g, post_g, w_up, w_up, cw, cb, w_down)


def _layer0_tail(hf, hb, sig_o, x, mod, norm_g, w_out, post_mix_g, pre_mix_g, pre_g, post_g, w_up, cw, cb, w_down,
                 cs, rows, fc):
    bsz, t, d = x.shape
    hv = hf.shape[2]
    f = w_down.shape[1]
    tq = t // RADIX
    quarters = lambda a: a.reshape(bsz, RADIX, tq, a.shape[2])
    tok = lambda n: pl.BlockSpec((1, RADIX, rows, n), lambda b, i: (b, 0, i, 0))
    out, g = pl.pallas_call(
        functools.partial(_layer0_tail_kernel, fc),
        grid=(bsz, tq // rows),
        in_specs=([tok(hv), tok(hv), tok(hv), tok(d), _mod_spec(0, *mod.shape[1:]), _mod_spec(1, *mod.shape[1:]),
                   _const_spec(norm_g), _const_spec(w_out), _const_spec(post_mix_g), _const_spec(pre_mix_g)]
                  + _ffn_specs(0, pre_g, post_g, d, f, cw, cb) + [_const_spec(cs)]),
        out_specs=[tok(d), pl.BlockSpec((1, RADIX, 2, rows, d), lambda b, i: (b, 0, 0, i, 0))],
        out_shape=[jax.ShapeDtypeStruct((bsz, RADIX, tq, d), F32),
                   jax.ShapeDtypeStruct((bsz, RADIX, 2, tq, d), BF16)],
        scratch_shapes=[pltpu.VMEM((RADIX * rows // FFN_SUBTILE_ROWS, FFN_SUBTILE_ROWS, f), BF16)],
        compiler_params=_params("parallel", "parallel"),
        name="layer0_tail",
    )(quarters(hf), quarters(hb), quarters(sig_o), quarters(x), mod, mod, norm_g, w_out, post_mix_g, pre_mix_g,
      pre_g, post_g, w_up, w_up, cw, cb, w_down, cs)
    return out.reshape(bsz, t, d), g


def _fourier_token_kernel(scale, dft_ref, g_ref, w_ref, bias_ref, post_ref, mod_ref, x_ref, out_ref,
                          x_s, out_s):
    tm = dft_ref.shape[1]
    nblk = x_s.shape[0]
    gate = _mod_rows(mod_ref, pl.program_id(0))[2]
    for c in range(nblk):
        x_s[c] = x_ref[0, :, c * LANES:(c + 1) * LANES]
    ys = [(_bdot(dft_ref[r], g_ref[0, r]) * scale).astype(BF16) for r in range(RADIX)]
    zs = [_bdot(y, w_ref[...]) + bias_ref[...] for y in ys]
    for r in range(RADIX):
        rows = pl.ds(r, tm, stride=RADIX)
        xr = jnp.concatenate([x_s[c, rows, :] for c in range(nblk)], axis=1)
        res = xr + gate * _rms(zs[r], post_ref[1:2])
        for c in range(nblk):
            out_s[c, rows, :] = res[:, c * LANES:(c + 1) * LANES]
    for c in range(nblk):
        out_ref[0, :, c * LANES:(c + 1) * LANES] = out_s[c]


def _fourier_token(dft, g, w, bias, post_mix_g, mod, x, scale, tm):
    bsz, t, d = x.shape
    tq = t // RADIX
    tok = pl.BlockSpec((1, RADIX * tm, d), lambda b, i: (b, i, 0))
    return pl.pallas_call(
        functools.partial(_fourier_token_kernel, scale),
        grid=(bsz, tq // tm),
        in_specs=[pl.BlockSpec((RADIX, tm, 2 * tq), lambda b, i: (0, i, 0)),
                  pl.BlockSpec((1, RADIX, 2 * tq, d), lambda b, i: (b, 0, 0, 0)),
                  _const_spec(w), _const_spec(bias), _const_spec(post_mix_g), _mod_spec(1, *mod.shape[1:]), tok],
        out_specs=tok,
        out_shape=jax.ShapeDtypeStruct((bsz, t, d), F32),
        scratch_shapes=[pltpu.VMEM((d // LANES, RADIX * tm, LANES), F32)] * 2,
        compiler_params=_params("parallel", "arbitrary"),
        name="fourier_token_dft",
    )(dft, g.reshape(bsz, RADIX, 2 * tq, d), w, bias, post_mix_g, mod, x)


def _dft_tables(t, gd):
    idx = np.arange(gd, dtype=np.int64)
    ang = 2.0 * np.pi * ((idx[:, None] * idx[None, :]) % gd).astype(np.float64) / gd
    chan = np.concatenate([np.cos(ang), np.sin(ang)], axis=1).astype(np.float32)
    tq = t // RADIX
    k = RADIX * np.arange(tq, dtype=np.int64)[None, :, None] + np.arange(RADIX, dtype=np.int64)[:, None, None]
    ang = 2.0 * np.pi * ((k * np.arange(tq, dtype=np.int64)[None, None, :]) % t).astype(np.float64) / t
    tok = np.concatenate([np.cos(ang), np.sin(ang)], axis=2).astype(np.float32)
    return jnp.asarray(chan).astype(BF16), jnp.asarray(tok).astype(BF16)


def kernel(x, c, ctx, c_ctx, ada_w, ada_b, pre_mix_g, post_mix_g, pre_ffn_g, post_ffn_g, ffn_up_w, ffn_conv_w,
           ffn_conv_b, ffn_down_w, m_in_w, m_in_b, m_norm_g, m_out_w, f_out_w, f_out_b):
    bsz, t, d = x.shape
    tc = ctx.shape[1]
    f = ffn_down_w.shape[1]
    assert ada_w.shape[0] == 2 and m_in_w.shape[0] == 1 and f_out_w.shape[0] == 1
    assert bsz + 1 <= COND_ROWS and t % M_CHUNK == 0 and tc % M_CHUNK == 0 and t % GRID_W == 0
    hv = m_out_w.shape[1]
    hk = (m_in_w.shape[2] - 2 * hv - 4 * M_HEADS) // 2
    tm = min(512, t)
    fc = 256
    assert f % fc == 0 and t % tm == 0 and tm % GRID_W == 0

    mod, w_in = _ada(c, c_ctx.reshape(1, d), ada_w, ada_b, jnp.swapaxes(m_in_w[0], 0, 1))

    b_in = jnp.pad(m_in_b, ((0, 0), (0, w_in.shape[0] - m_in_b.shape[1])))
    kc, vc, gc = _inproj(ctx, mod, bsz, pre_mix_g, w_in, b_in, hk, hv, min(tm, tc), False)
    qx, ox, kx, vx, gx, w_up, w_down, w_mo, w_fo = _inproj(
        x, mod, None, pre_mix_g, w_in, b_in, hk, hv, min(1024, t), True,
        convert=(ffn_up_w, ffn_down_w, m_out_w, f_out_w))
    gates_f, gates_b = _gate_prep(gc, gx)
    state = _scan(None, kc, vc, gates_f, gates_b, t // M_CHUNK, None)
    hf, hb = _scan(qx, kx, vx, gates_f, gates_b, 0, state)
    gd = d // F_GROUPS
    tq = t // RADIX
    rows, tm_tok = min(128, tq), min(256, tq)
    assert t % RADIX == 0 and tq % rows == 0 and tq % tm_tok == 0
    assert (RADIX * rows) % FFN_SUBTILE_ROWS == 0 and FFN_SUBTILE_ROWS % (RADIX * GRID_W) == 0
    chan, tok = _dft_tables(t, gd)
    x, g = _layer0_tail(hf, hb, ox, x, mod, m_norm_g, w_mo[0], post_mix_g, pre_mix_g, pre_ffn_g, post_ffn_g,
                        w_up, ffn_conv_w, ffn_conv_b, w_down, chan, rows, fc)

    x = _fourier_token(tok, g, w_fo[0], f_out_b, post_mix_g, mod, x, float(1.0 / np.sqrt(t * gd)), tm_tok)
    return _ffn(x, mod, 1, pre_ffn_g, post_ffn_g, w_up, ffn_conv_w, ffn_conv_b, w_down, min(1024, t), fc)
```

```python
import functools

import numpy as np
import jax
import jax.numpy as jnp
from jax import lax
from jax.experimental import pallas as pl
from jax.experimental.pallas import tpu as pltpu

F32 = jnp.float32
BF16 = jnp.bfloat16

M_HEADS = 4
M_CHUNK = 128
F_GROUPS = 8
GRID_W = 64
EPS = 1e-6
N_MOD = 6
COND_ROWS = 16
FFN_SUBTILE_ROWS = 256
SCAN_CHUNKS_PER_STEP = 8

VMEM_LIMIT_BYTES = 56 * 1024 * 1024
LANES = 128


def _params(*sem):
    return pltpu.CompilerParams(dimension_semantics=sem, vmem_limit_bytes=VMEM_LIMIT_BYTES)


def _rms(x, g):
    return x * lax.rsqrt(jnp.mean(x * x, axis=-1, keepdims=True) + EPS) * g


def _bdot(a, b):
    return jnp.dot(a, b, preferred_element_type=F32)


def _mod_rows(mod_ref, row):
    m = mod_ref[pl.ds(row, 1), :]
    d = m.shape[1] // N_MOD
    return [m[:, k * d:(k + 1) * d] for k in range(N_MOD)]


def _mod_spec(layer, cond_rows, width):
    return pl.BlockSpec((None, cond_rows, width), lambda b, i: (layer, 0, 0), pipeline_mode=pl.Buffered(1))


def _const_spec(a):
    return pl.BlockSpec(a.shape, lambda b, i: (0,) * a.ndim, pipeline_mode=pl.Buffered(1))


def _ada_kernel(valid_rows, c_ref, cc_ref, w_ref, b_ref, win_ref, o_ref, win_out):
    pad = jnp.zeros((o_ref.shape[1] - c_ref.shape[0] - 1, c_ref.shape[1]), F32)
    c = jnp.concatenate([c_ref[...], cc_ref[...], pad], axis=0)
    s = c * jax.nn.sigmoid(c)
    o_ref[0] = _bdot(s.astype(BF16), w_ref[0].astype(BF16)) + b_ref[pl.ds(pl.program_id(0), 1), :]
    step = pl.program_id(0) * pl.num_programs(1) + pl.program_id(1)
    row = step * win_ref.shape[0] + lax.broadcasted_iota(jnp.int32, win_ref.shape, 0)
    win_out[...] = jnp.where(row < valid_rows, win_ref[...], 0.0).astype(BF16)


def _ada(c, c_ctx, ada_w, ada_b, w_in_t):
    depth, d, n = ada_w.shape
    tn = n // 4
    steps = depth * (n // tn)
    rows, cols = w_in_t.shape
    padded = rows + (-rows) % LANES
    blk = padded // steps
    assert padded % (16 * steps) == 0 and (steps - 1) * blk < rows
    return pl.pallas_call(
        functools.partial(_ada_kernel, rows),
        grid=(depth, n // tn),
        in_specs=[pl.BlockSpec(c.shape, lambda i, j: (0, 0)),
                  pl.BlockSpec(c_ctx.shape, lambda i, j: (0, 0)),
                  pl.BlockSpec((1, d, tn), lambda i, j: (i, 0, j)),
                  pl.BlockSpec((depth, tn), lambda i, j: (0, j)),
                  pl.BlockSpec((blk, cols), lambda i, j: (i * (n // tn) + j, 0))],
        out_specs=[pl.BlockSpec((1, COND_ROWS, tn), lambda i, j: (i, 0, j)),
                   pl.BlockSpec((blk, cols), lambda i, j: (i * (n // tn) + j, 0))],
        out_shape=[jax.ShapeDtypeStruct((depth, COND_ROWS, n), F32),
                   jax.ShapeDtypeStruct((padded, cols), BF16)],
        compiler_params=_params("arbitrary", "arbitrary"),
        name="ada",
    )(c, c_ctx, ada_w, ada_b, w_in_t)


def _inproj_kernel(nconv, with_qo, cond_row, x_ref, mod_ref, g_ref, w_ref, b_ref, *refs):
    nout = 5 if with_qo else 3
    conv_in, outs, conv_out = refs[:nconv], refs[nconv:nconv + nout], refs[nconv + nout:]
    kt_out, v_out, gate_out = outs[-3:]
    hk, hv = kt_out.shape[1], v_out.shape[2]
    dk = hk // M_HEADS
    shift, scale = _mod_rows(mod_ref, pl.program_id(0) if cond_row is None else cond_row)[:2]
    sub = min(x_ref.shape[1], FFN_SUBTILE_ROWS)
    for rs in [slice(s, s + sub) for s in range(0, x_ref.shape[1], sub)]:
        h = _rms(x_ref[0, rs], g_ref[0:1]) * (1.0 + scale) + shift
        hb = h.astype(BF16)

        def proj(lo, hi):
            return lax.dot_general(hb, w_ref[lo:hi, :], (((1,), (1,)), ((), ())),
                                   preferred_element_type=F32) + b_ref[:, lo:hi]

        if with_qo:
            q_out, o_out = outs[:2]
            q_out[0, rs] = proj(0, hk).astype(BF16)
            o_out[0, rs] = jax.nn.sigmoid(proj(2 * hk + hv, 2 * hk + 2 * hv)).astype(BF16)
        kt_out[0, :, rs] = jnp.transpose(proj(hk, 2 * hk) * (dk ** -0.5)).astype(BF16)
        v_out[0, rs] = proj(2 * hk, 2 * hk + hv).astype(BF16)
        gate_out[0, :, rs] = jnp.transpose(proj(2 * hk + 2 * hv, w_ref.shape[0]))[:gate_out.shape[1]]
    for src_ref, dst_ref in zip(conv_in, conv_out):
        dst_ref[...] = src_ref[...].astype(BF16)


def _inproj(x, mod, cond_row, g, w, bias, hk, hv, tm, with_qo, convert=()):
    bsz, t, d = x.shape
    nt = t // tm
    full = lambda a: pl.BlockSpec(a.shape, lambda b, i: (0, 0))
    tok = lambda n: pl.BlockSpec((1, tm, n), lambda b, i: (b, i, 0))
    tok_t = lambda n: pl.BlockSpec((1, n, tm), lambda b, i: (b, 0, i))
    flat = [a.reshape(-1, a.shape[-1]) for a in convert]
    conv_specs = [pl.BlockSpec((a.shape[0] // (bsz * nt), a.shape[1]), lambda b, i: (b * nt + i, 0)) for a in flat]
    assert all(a.shape[0] % (16 * bsz * nt) == 0 for a in flat)
    qo_specs = [tok(hk), tok(hv)] if with_qo else []
    qo_shapes = [jax.ShapeDtypeStruct((bsz, t, hk), BF16), jax.ShapeDtypeStruct((bsz, t, hv), BF16)] if with_qo else []
    outs = pl.pallas_call(
        functools.partial(_inproj_kernel, len(flat), with_qo, cond_row),
        grid=(bsz, nt),
        in_specs=[tok(d), _mod_spec(0, *mod.shape[1:]), full(g), full(w), full(bias)] + conv_specs,
        out_specs=qo_specs + [tok_t(hk), tok(hv), tok_t(4 * M_HEADS)] + conv_specs,
        out_shape=qo_shapes + [jax.ShapeDtypeStruct((bsz, hk, t), BF16), jax.ShapeDtypeStruct((bsz, t, hv), BF16),
                               jax.ShapeDtypeStruct((bsz, 4 * M_HEADS, t), F32)]
                  + [jax.ShapeDtypeStruct(a.shape, BF16) for a in flat],
        compiler_params=_params("parallel", "parallel"),
        name="mlstm_inproj",
    )(x, mod, g, w, bias, *flat)
    nout = len(outs) - len(flat)
    return list(outs[:nout]) + [o.reshape(a.shape) for o, a in zip(outs[nout:], convert)]


def _lane_scan(x, op, fill, reverse, seg):
    n = x.shape[-1]
    lane = lax.broadcasted_iota(jnp.int32, x.shape, x.ndim - 1) % seg
    d = 1
    while d < seg:
        if reverse:
            shifted = jnp.where(lane < seg - d, pltpu.roll(x, n - d, x.ndim - 1), fill)
        else:
            shifted = jnp.where(lane >= d, pltpu.roll(x, d, x.ndim - 1), fill)
        x = op(x, shifted)
        d *= 2
    return x


def _log_sigmoid(x):
    return jnp.minimum(x, 0.0) - jnp.log1p(jnp.exp(-jnp.abs(x)))


def _gate_kernel(orders, gf_ref, gb_ref, of_ref, ob_ref):
    L = M_CHUNK
    for d, (g_ref, o_ref) in enumerate(((gf_ref, of_ref), (gb_ref, ob_ref))):
        edge = L - 1 if d == 0 else 0
        lf = _log_sigmoid(g_ref[:, 1])
        b_all = _lane_scan(lf, jnp.add, 0.0, d == 1, L)
        r1_all = g_ref[:, 0] - b_all
        cm_all = _lane_scan(r1_all, jnp.maximum, -jnp.inf, d == 1, L)
        m = jnp.zeros(lf.shape[:-1] + (1,), F32)
        for c in orders[d]:
            lanes = slice(c * L, (c + 1) * L)
            b, r1, cm = b_all[..., lanes], r1_all[..., lanes], cm_all[..., lanes]
            mx = jnp.maximum(m, cm[..., edge:edge + 1])
            mc = jnp.maximum(m, cm)
            o_ref[0, :, :, lanes] = r1
            o_ref[1, :, :, lanes] = mc
            o_ref[2, :, :, lanes] = jnp.broadcast_to(m, r1.shape)
            o_ref[3, :, :, lanes] = jnp.exp(-(b + mc))
            o_ref[4, :, :, lanes] = jnp.exp(r1 - mx)
            o_ref[5, :, :, lanes] = jnp.broadcast_to(jnp.exp(m - mx), r1.shape)
            m = b[..., edge:edge + 1] + mx


def _gate_prep(gates_c, gates_x):
    bsz = gates_c.shape[0]
    nlt, nct = gates_x.shape[2] // M_CHUNK, gates_c.shape[2] // M_CHUNK
    n = (nlt + nct) * M_CHUNK
    fwd = tuple(range(nlt, nlt + nct)) + tuple(range(nlt))
    bwd = tuple(range(nlt + nct - 1, nlt - 1, -1)) + tuple(range(nlt - 1, -1, -1))
    gf = gb = jnp.concatenate([gates_x, gates_c], axis=2).reshape(bsz, 4, M_HEADS, n)
    out_spec = pl.BlockSpec((6, bsz, M_HEADS, n), lambda j: (0, 0, 0, 0))
    return pl.pallas_call(
        functools.partial(_gate_kernel, (fwd, bwd)),
        grid=(1,),
        in_specs=[pl.BlockSpec((bsz, 2, M_HEADS, n), lambda j: (0, 0, 0, 0)),
                  pl.BlockSpec((bsz, 2, M_HEADS, n), lambda j: (0, 1, 0, 0))],
        out_specs=[out_spec, out_spec],
        out_shape=[jax.ShapeDtypeStruct((6, bsz, M_HEADS, n), F32)] * 2,
        compiler_params=_params("arbitrary"),
        name="mlstm_gates",
    )(gf, gb)


def _scan_kernel(cps, with_outputs, *refs):
    if with_outputs:
        (qf, ktf, vf, rf, qb, ktb, vb, rb, c0, hf_out, hb_out, c_s) = refs
        h_outs = (hf_out, hb_out)
        streams = ((qf, ktf, vf, rf), (qb, ktb, vb, rb))
    else:
        (ktf, vf, rf, ktb, vb, rb, c_s) = refs
        streams = ((None, ktf, vf, rf), (None, ktb, vb, rb))
    s = pl.program_id(1)
    L = M_CHUNK
    dk = ktf.shape[1] // M_HEADS
    dv = vf.shape[2] // M_HEADS

    @pl.when(s == 0)
    def _():
        if with_outputs:
            c_s[...] = c0[...]
        else:
            c_s[...] = jnp.zeros(c_s.shape, F32)

    row = lax.broadcasted_iota(jnp.int32, (L, L), 0)
    col = lax.broadcasted_iota(jnp.int32, (L, L), 1)
    for cc in range(cps):
        sub = (cc, cps - 1 - cc)
        blk = [slice(i * L, (i + 1) * L) for i in sub]
        if with_outputs:
            stack = [r_ref[j, 0, :, blk[d]] for d, r_ref in enumerate((rf, rb)) for j in (1, 3)]
            stack.append(jnp.zeros((L - 4 * M_HEADS, L), F32))
            cols = jnp.transpose(jnp.concatenate(stack, axis=0))
        masks = ((col <= row), (col >= row))
        for h in range(M_HEADS):
            opnds, scores, weighted = {}, {}, {}
            for d in range(2):
                _, kt_ref, v_ref, row_ref = streams[d]
                kt = kt_ref[0, h * dk:(h + 1) * dk, blk[d]]
                v = v_ref[0, blk[d], h * dv:(h + 1) * dv]
                c_old = c_s[0, d, h]
                kw = kt.astype(F32) * row_ref[4, 0, h:h + 1, blk[d]]
                n_upd = jnp.broadcast_to(jnp.sum(kw, axis=1, keepdims=True), (dk, LANES))
                upd = jnp.concatenate([_bdot(kw.astype(BF16), v), n_upd], axis=1)
                c_s[0, d, h] = row_ref[5, 0, h:h + 1, sub[d] * L:sub[d] * L + 1] * c_old + upd
                if with_outputs:
                    opnds[d] = (kt, v, c_old.astype(BF16))
            if not with_outputs:
                continue
            decay = {}
            for d in range(2):
                row_ref = streams[d][3]
                mc = cols[:, 2 * M_HEADS * d + h:2 * M_HEADS * d + h + 1]
                p = jnp.exp(jnp.where(masks[d], row_ref[0, 0, h:h + 1, blk[d]] - mc, -jnp.inf))
                wi = jnp.exp(row_ref[2, 0, h:h + 1, blk[d]] - mc)
                decay[d] = (p, wi)
            for d in range(2):
                kt, _, cb = opnds[d]
                q = streams[d][0][0, blk[d], h * dk:(h + 1) * dk]
                scores[d] = (q, _bdot(q, jnp.concatenate([kt, cb[:, dv:]], axis=1)))
            for d in range(2):
                q, sq = scores[d]
                p, wi = decay[d]
                sp = sq[:, :L] * p
                lhs = jnp.concatenate([sp, q.astype(F32) * wi], axis=1).astype(BF16)
                den = wi * sq[:, L:] + jnp.sum(sp, axis=1, keepdims=True)
                weighted[d] = (lhs, den)
            for d in range(2):
                _, v, cb = opnds[d]
                lhs, den = weighted[d]
                j = 2 * M_HEADS * d + h + M_HEADS
                num = _bdot(lhs, jnp.concatenate([v, cb[:, :dv]], axis=0))
                inv = 1.0 / jnp.maximum(jnp.abs(den), cols[:, j:j + 1])
                h_outs[d][0, blk[d], h * dv:(h + 1) * dv] = (
                    num * jnp.concatenate([inv] * (dv // LANES), axis=1)).astype(BF16)


def _scan(q, kt, v, gates_f, gates_b, off, init):
    bsz, hk, t = kt.shape
    hv = v.shape[2]
    dk, dv = hk // M_HEADS, hv // M_HEADS
    assert dk == M_CHUNK == LANES and dv % LANES == 0
    nc = t // M_CHUNK
    cps = max(c for c in (SCAN_CHUNKS_PER_STEP, 2, 1) if nc % c == 0 and off % c == 0)
    nsteps = nc // cps
    L = cps * M_CHUNK
    with_outputs = init is not None

    def specs(cidx):
        sp = [pl.BlockSpec((1, hk, L), lambda b, s: (b, 0, cidx(s))),
              pl.BlockSpec((1, L, hv), lambda b, s: (b, cidx(s), 0)),
              pl.BlockSpec((6, 1, M_HEADS, L), lambda b, s: (0, b, 0, off // cps + cidx(s)))]
        if with_outputs:
            sp = [pl.BlockSpec((1, L, hk), lambda b, s: (b, cidx(s), 0))] + sp
        return sp

    lead = [q] if with_outputs else []
    args = lead + [kt, v, gates_f] + lead + [kt, v, gates_b]
    in_specs = specs(lambda s: s) + specs(lambda s: nsteps - 1 - s)
    state_shape = (bsz, 2, M_HEADS, dk, dv + LANES)
    state_spec = pl.BlockSpec((1,) + state_shape[1:], lambda b, s: (b, 0, 0, 0, 0))
    if with_outputs:
        args.append(init)
        in_specs.append(state_spec)
        out_specs = [pl.BlockSpec((1, L, hv), lambda b, s: (b, s, 0)),
                     pl.BlockSpec((1, L, hv), lambda b, s: (b, nsteps - 1 - s, 0))]
        out_shape = [jax.ShapeDtypeStruct((bsz, t, hv), BF16)] * 2
        scratch = [pltpu.VMEM((1,) + state_shape[1:], F32)]
    else:
        out_specs = state_spec
        out_shape = jax.ShapeDtypeStruct(state_shape, F32)
        scratch = []
    return pl.pallas_call(
        functools.partial(_scan_kernel, cps, with_outputs),
        grid=(bsz, nsteps),
        in_specs=in_specs,
        out_specs=out_specs,
        out_shape=out_shape,
        scratch_shapes=scratch,
        compiler_params=_params("parallel", "arbitrary"),
        name="mlstm_scan" if with_outputs else "mlstm_ctx_state",
    )(*args)


def _readout_core(h, sig_o, x, gate, norm_g, w_ref, post_g):
    dv = h.shape[1] // M_HEADS
    parts = []
    for hd in range(M_HEADS):
        hh = h[:, hd * dv:(hd + 1) * dv]
        parts.append(hh * lax.rsqrt(jnp.mean(hh * hh, axis=-1, keepdims=True) + EPS))
    y = (jnp.concatenate(parts, axis=-1) * norm_g * sig_o.astype(F32)).astype(BF16)
    return x + gate * _rms(_bdot(y, w_ref[...]), post_g)


def _ffn_up(fc, x, sh, sc, pre_g, wu_ref, wg_ref, cw_ref, cb, act_ref):
    tm = x.shape[0]
    f = wu_ref.shape[1]
    hb = (_rms(x, pre_g) * (1.0 + sc) + sh).astype(BF16)
    pos = lax.broadcasted_iota(jnp.int32, (tm, fc), 0) % GRID_W
    has_prev = pos != 0
    has_next = pos != GRID_W - 1
    for j in range(f // fc):
        cs = slice(j * fc, (j + 1) * fc)
        u = _bdot(hb, wu_ref[:, cs])
        g = _bdot(hb, wg_ref[:, cs])
        g_prev = jnp.where(has_prev, pltpu.roll(g, 1, 0), 0.0)
        g_next = jnp.where(has_next, pltpu.roll(g, tm - 1, 0), 0.0)
        gc = g_prev * cw_ref[0:1, cs] + g * cw_ref[1:2, cs] + g_next * cw_ref[2:3, cs] + cb[:, cs]
        act_ref[:, cs] = (gc * jax.nn.sigmoid(gc) * u).astype(BF16)


def _ffn_down(x, gate, post_g, wd_ref, act_ref):
    return x + gate * _rms(_bdot(act_ref[...], wd_ref[...]), post_g)


def _ffn_halves(fc, xs, sh, sc, gate, pre_g, post_g, wu_ref, wg_ref, cw_ref, cb, wd_ref, act_ref):
    for s, x in enumerate(xs):
        _ffn_up(fc, x, sh, sc, pre_g, wu_ref, wg_ref, cw_ref, cb, act_ref.at[s])
    return [_ffn_down(x, gate, post_g, wd_ref, act_ref.at[s]) for s, x in enumerate(xs)]


RADIX = 4


def _fourier_channel_core(xq, sh, sc, pre_g, cs_ref, g_ref, rs=slice(None)):
    gd = cs_ref.shape[0]
    hb = [(_rms(x, pre_g) * (1.0 + sc) + sh).astype(BF16) for x in xq]
    for g in range(hb[0].shape[1] // gd):
        lanes = slice(g * gd, (g + 1) * gd)
        r = [_bdot(h[:, lanes], cs_ref[...]) for h in hb]
        a = [v[:, :gd] for v in r]
        b = [v[:, gd:] for v in r]
        a02p, a02m, a13p, a13m = a[0] + a[2], a[0] - a[2], a[1] + a[3], a[1] - a[3]
        b02p, b02m, b13p, b13m = b[0] + b[2], b[0] - b[2], b[1] + b[3], b[1] - b[3]
        re_im = ((a02p + a13p, -(b02p + b13p)),
                 (a02m - b13m, -(b02m + a13m)),
                 (a02p - a13p, b13p - b02p),
                 (a02m + b13m, a13m - b02m))
        for k, (re, im) in enumerate(re_im):
            g_ref[0, k, 0, rs, lanes] = re.astype(BF16)
            g_ref[0, k, 1, rs, lanes] = im.astype(BF16)


def _ffn_params(layer, pre_ref, post_ref, cb_ref):
    return pre_ref[layer:layer + 1], post_ref[layer:layer + 1], cb_ref[layer:layer + 1]


def _ffn_kernel(fc, layer, x_ref, mod_ref, pre_ref, post_ref, wu_ref, wg_ref, cw_ref, cb_ref, wd_ref,
                out_ref, act_ref):
    shift, scale, gate = _mod_rows(mod_ref, pl.program_id(0))[3:]
    pre_g, post_g, cb = _ffn_params(layer, pre_ref, post_ref, cb_ref)
    nsub, sub = act_ref.shape[:2]
    tiles = [slice(s * sub, (s + 1) * sub) for s in range(nsub)]
    ys = _ffn_halves(fc, [x_ref[0, rs] for rs in tiles], shift, scale, gate, pre_g, post_g, wu_ref, wg_ref, cw_ref,
                     cb, wd_ref, act_ref)
    for rs, y in zip(tiles, ys):
        out_ref[0, rs] = y


def _layer0_tail_kernel(fc, hf_ref, hb_ref, o_ref, x_ref, mod0_ref, mod1_ref, ng_ref, wo_ref, pmix_ref, premix_ref,
                        pre_ref, post_ref, wu_ref, wg_ref, cw_ref, cb_ref, wd_ref, cs_ref, out_ref, g_ref, act_ref):
    rows = x_ref.shape[2]
    _, _, gate1, shift2, scale2, gate2 = _mod_rows(mod0_ref, pl.program_id(0))
    next_shift, next_scale = _mod_rows(mod1_ref, pl.program_id(0))[:2]
    pre_g, post_g, cb = _ffn_params(0, pre_ref, post_ref, cb_ref)
    sub = act_ref.shape[1] // RADIX
    slices = [slice(s * sub, (s + 1) * sub) for s in range(act_ref.shape[0])]
    xs = []
    for rs in slices:
        cat = lambda ref: jnp.concatenate([ref[0, q, rs] for q in range(RADIX)], axis=0)
        h = cat(hf_ref).astype(F32) + cat(hb_ref).astype(F32)
        xs.append(_readout_core(h, cat(o_ref), cat(x_ref), gate1, ng_ref[...], wo_ref, pmix_ref[0:1]))
    xs = _ffn_halves(fc, xs, shift2, scale2, gate2, pre_g, post_g, wu_ref, wg_ref, cw_ref, cb, wd_ref, act_ref)
    for s, rs in enumerate(slices):
        xq = [xs[s][q * sub:(q + 1) * sub] for q in range(RADIX)]
        for q in range(RADIX):
            out_ref[0, q, rs] = xq[q]
        _fourier_channel_core(xq, next_shift, next_scale, premix_ref[1:2], cs_ref, g_ref, rs)


def _ffn_specs(layer, pre_g, post_g, d, f, cw, cb):
    def lay(rows, cols, col_block=0):
        return pl.BlockSpec((None, rows, cols), lambda b, i: (layer, 0, col_block),
                            pipeline_mode=pl.Buffered(1))
    return [_const_spec(pre_g), _const_spec(post_g), lay(d, f, 0), lay(d, f, 1), lay(cw.shape[1], f),
            _const_spec(cb), lay(f, d)]


def _ffn(x, mod, layer, pre_g, post_g, w_up, cw, cb, w_down, tm, fc):
    bsz, t, d = x.shape
    f = w_down.shape[1]
    tok = pl.BlockSpec((1, tm, d), lambda b, i: (b, i, 0))
    return pl.pallas_call(
        functools.partial(_ffn_kernel, fc, layer),
        grid=(bsz, t // tm),
        in_specs=[tok, _mod_spec(layer, *mod.shape[1:])] + _ffn_specs(layer, pre_g, post_g, d, f, cw, cb),
        out_specs=tok,
        out_shape=jax.ShapeDtypeStruct((bsz, t, d), F32),
        scratch_shapes=[pltpu.VMEM((tm // FFN_SUBTILE_ROWS, FFN_SUBTILE_ROWS, f), BF16)],
        compiler_params=_params("parallel", "parallel"),
        name="conv_ffn",
    )(x, mod, pre_g, post_g, w_up, w_up, cw, cb, w_down)


def _layer0_tail(hf, hb, sig_o, x, mod, norm_g, w_out, post_mix_g, pre_mix_g, pre_g, post_g, w_up, cw, cb, w_down,
                 cs, rows, fc):
    bsz, t, d = x.shape
    hv = hf.shape[2]
    f = w_down.shape[1]
    tq = t // RADIX
    quarters = lambda a: a.reshape(bsz, RADIX, tq, a.shape[2])
    tok = lambda n: pl.BlockSpec((1, RADIX, rows, n), lambda b, i: (b, 0, i, 0))
    out, g = pl.pallas_call(
        functools.partial(_layer0_tail_kernel, fc),
        grid=(bsz, tq // rows),
        in_specs=([tok(hv), tok(hv), tok(hv), tok(d), _mod_spec(0, *mod.shape[1:]), _mod_spec(1, *mod.shape[1:]),
                   _const_spec(norm_g), _const_spec(w_out), _const_spec(post_mix_g), _const_spec(pre_mix_g)]
                  + _ffn_specs(0, pre_g, post_g, d, f, cw, cb) + [_const_spec(cs)]),
        out_specs=[tok(d), pl.BlockSpec((1, RADIX, 2, rows, d), lambda b, i: (b, 0, 0, i, 0))],
        out_shape=[jax.ShapeDtypeStruct((bsz, RADIX, tq, d), F32),
                   jax.ShapeDtypeStruct((bsz, RADIX, 2, tq, d), BF16)],
        scratch_shapes=[pltpu.VMEM((RADIX * rows // FFN_SUBTILE_ROWS, FFN_SUBTILE_ROWS, f), BF16)],
        compiler_params=_params("parallel", "parallel"),
        name="layer0_tail",
    )(quarters(hf), quarters(hb), quarters(sig_o), quarters(x), mod, mod, norm_g, w_out, post_mix_g, pre_mix_g,
      pre_g, post_g, w_up, w_up, cw, cb, w_down, cs)
    return out.reshape(bsz, t, d), g


def _fourier_token_kernel(scale, dft_ref, g_ref, w_ref, bias_ref, post_ref, mod_ref, x_ref, out_ref,
                          x_s, out_s):
    tm = dft_ref.shape[1]
    nblk = x_s.shape[0]
    gate = _mod_rows(mod_ref, pl.program_id(0))[2]
    for c in range(nblk):
        x_s[c] = x_ref[0, :, c * LANES:(c + 1) * LANES]
    ys = [(_bdot(dft_ref[r], g_ref[0, r]) * scale).astype(BF16) for r in range(RADIX)]
    zs = [_bdot(y, w_ref[...]) + bias_ref[...] for y in ys]
    for r in range(RADIX):
        rows = pl.ds(r, tm, stride=RADIX)
        xr = jnp.concatenate([x_s[c, rows, :] for c in range(nblk)], axis=1)
        res = xr + gate * _rms(zs[r], post_ref[1:2])
        for c in range(nblk):
            out_s[c, rows, :] = res[:, c * LANES:(c + 1) * LANES]
    for c in range(nblk):
        out_ref[0, :, c * LANES:(c + 1) * LANES] = out_s[c]


def _fourier_token(dft, g, w, bias, post_mix_g, mod, x, scale, tm):
    bsz, t, d = x.shape
    tq = t // RADIX
    tok = pl.BlockSpec((1, RADIX * tm, d), lambda b, i: (b, i, 0))
    return pl.pallas_call(
        functools.partial(_fourier_token_kernel, scale),
        grid=(bsz, tq // tm),
        in_specs=[pl.BlockSpec((RADIX, tm, 2 * tq), lambda b, i: (0, i, 0)),
                  pl.BlockSpec((1, RADIX, 2 * tq, d), lambda b, i: (b, 0, 0, 0)),
                  _const_spec(w), _const_spec(bias), _const_spec(post_mix_g), _mod_spec(1, *mod.shape[1:]), tok],
        out_specs=tok,
        out_shape=jax.ShapeDtypeStruct((bsz, t, d), F32),
        scratch_shapes=[pltpu.VMEM((d // LANES, RADIX * tm, LANES), F32)] * 2,
        compiler_params=_params("parallel", "arbitrary"),
        name="fourier_token_dft",
    )(dft, g.reshape(bsz, RADIX, 2 * tq, d), w, bias, post_mix_g, mod, x)


def _dft_tables(t, gd):
    idx = np.arange(gd, dtype=np.int64)
    ang = 2.0 * np.pi * ((idx[:, None] * idx[None, :]) % gd).astype(np.float64) / gd
    chan = np.concatenate([np.cos(ang), np.sin(ang)], axis=1).astype(np.float32)
    tq = t // RADIX
    k = RADIX * np.arange(tq, dtype=np.int64)[None, :, None] + np.arange(RADIX, dtype=np.int64)[:, None, None]
    ang = 2.0 * np.pi * ((k * np.arange(tq, dtype=np.int64)[None, None, :]) % t).astype(np.float64) / t
    tok = np.concatenate([np.cos(ang), np.sin(ang)], axis=2).astype(np.float32)
    return jnp.asarray(chan).astype(BF16), jnp.asarray(tok).astype(BF16)


def kernel(x, c, ctx, c_ctx, ada_w, ada_b, pre_mix_g, post_mix_g, pre_ffn_g, post_ffn_g, ffn_up_w, ffn_conv_w,
           ffn_conv_b, ffn_down_w, m_in_w, m_in_b, m_norm_g, m_out_w, f_out_w, f_out_b):
    bsz, t, d = x.shape
    tc = ctx.shape[1]
    f = ffn_down_w.shape[1]
    assert ada_w.shape[0] == 2 and m_in_w.shape[0] == 1 and f_out_w.shape[0] == 1
    assert bsz + 1 <= COND_ROWS and t % M_CHUNK == 0 and tc % M_CHUNK == 0 and t % GRID_W == 0
    hv = m_out_w.shape[1]
    hk = (m_in_w.shape[2] - 2 * hv - 4 * M_HEADS) // 2
    tm = min(512, t)
    fc = 256
    assert f % fc == 0 and t % tm == 0 and tm % GRID_W == 0

    mod, w_in = _ada(c, c_ctx.reshape(1, d), ada_w, ada_b, jnp.swapaxes(m_in_w[0], 0, 1))

    b_in = jnp.pad(m_in_b, ((0, 0), (0, w_in.shape[0] - m_in_b.shape[1])))
    kc, vc, gc = _inproj(ctx, mod, bsz, pre_mix_g, w_in, b_in, hk, hv, min(tm, tc), False)
    qx, ox, kx, vx, gx, w_up, w_down, w_mo, w_fo = _inproj(
        x, mod, None, pre_mix_g, w_in, b_in, hk, hv, min(1024, t), True,
        convert=(ffn_up_w, ffn_down_w, m_out_w, f_out_w))
    gates_f, gates_b = _gate_prep(gc, gx)
    state = _scan(None, kc, vc, gates_f, gates_b, t // M_CHUNK, None)
    hf, hb = _scan(qx, kx, vx, gates_f, gates_b, 0, state)
    gd = d // F_GROUPS
    tq = t // RADIX
    rows, tm_tok = min(128, tq), min(256, tq)
    assert t % RADIX == 0 and tq % rows == 0 and tq % tm_tok == 0
    assert (RADIX * rows) % FFN_SUBTILE_ROWS == 0 and FFN_SUBTILE_ROWS % (RADIX * GRID_W) == 0
    chan, tok = _dft_tables(t, gd)
    x, g = _layer0_tail(hf, hb, ox, x, mod, m_norm_g, w_mo[0], post_mix_g, pre_mix_g, pre_ffn_g, post_ffn_g,
                        w_up, ffn_conv_w, ffn_conv_b, w_down, chan, rows, fc)

    x = _fourier_token(tok, g, w_fo[0], f_out_b, post_mix_g, mod, x, float(1.0 / np.sqrt(t * gd)), tm_tok)
    return _ffn(x, mod, 1, pre_ffn_g, post_ffn_g, w_up, ffn_conv_w, ffn_conv_b, w_down, min(1024, t), fc)
```

```python
import functools

import numpy as np
import jax
import jax.numpy as jnp
from jax import lax
from jax.experimental import pallas as pl
from jax.experimental.pallas import tpu as pltpu

F32 = jnp.float32
BF16 = jnp.bfloat16

M_HEADS = 4
M_CHUNK = 128
F_GROUPS = 8
GRID_W = 64
EPS = 1e-6
N_MOD = 6
COND_ROWS = 16
FFN_SUBTILE_ROWS = 256
SCAN_CHUNKS_PER_STEP = 8

VMEM_LIMIT_BYTES = 56 * 1024 * 1024
LANES = 128


def _params(*sem):
    return pltpu.CompilerParams(dimension_semantics=sem, vmem_limit_bytes=VMEM_LIMIT_BYTES)


def _rms(x, g):
    return x * lax.rsqrt(jnp.mean(x * x, axis=-1, keepdims=True) + EPS) * g


def _bdot(a, b):
    return jnp.dot(a, b, preferred_element_type=F32)


def _mod_rows(mod_ref, row):
    m = mod_ref[pl.ds(row, 1), :]
    d = m.shape[1] // N_MOD
    return [m[:, k * d:(k + 1) * d] for k in range(N_MOD)]


def _mod_spec(layer, cond_rows, width):
    return pl.BlockSpec((None, cond_rows, width), lambda b, i: (layer, 0, 0), pipeline_mode=pl.Buffered(1))


def _const_spec(a):
    return pl.BlockSpec(a.shape, lambda b, i: (0,) * a.ndim, pipeline_mode=pl.Buffered(1))


def _ada_kernel(valid_rows, c_ref, cc_ref, w_ref, b_ref, win_ref, o_ref, win_out):
    pad = jnp.zeros((o_ref.shape[1] - c_ref.shape[0] - 1, c_ref.shape[1]), F32)
    c = jnp.concatenate([c_ref[...], cc_ref[...], pad], axis=0)
    s = c * jax.nn.sigmoid(c)
    o_ref[0] = _bdot(s.astype(BF16), w_ref[0].astype(BF16)) + b_ref[pl.ds(pl.program_id(0), 1), :]
    step = pl.program_id(0) * pl.num_programs(1) + pl.program_id(1)
    row = step * win_ref.shape[0] + lax.broadcasted_iota(jnp.int32, win_ref.shape, 0)
    win_out[...] = jnp.where(row < valid_rows, win_ref[...], 0.0).astype(BF16)


def _ada(c, c_ctx, ada_w, ada_b, w_in_t):
    depth, d, n = ada_w.shape
    tn = n // 4
    steps = depth * (n // tn)
    rows, cols = w_in_t.shape
    padded = rows + (-rows) % LANES
    blk = padded // steps
    assert padded % (16 * steps) == 0 and (steps - 1) * blk < rows
    return pl.pallas_call(
        functools.partial(_ada_kernel, rows),
        grid=(depth, n // tn),
        in_specs=[pl.BlockSpec(c.shape, lambda i, j: (0, 0)),
                  pl.BlockSpec(c_ctx.shape, lambda i, j: (0, 0)),
                  pl.BlockSpec((1, d, tn), lambda i, j: (i, 0, j)),
                  pl.BlockSpec((depth, tn), lambda i, j: (0, j)),
                  pl.BlockSpec((blk, cols), lambda i, j: (i * (n // tn) + j, 0))],
        out_specs=[pl.BlockSpec((1, COND_ROWS, tn), lambda i, j: (i, 0, j)),
                   pl.BlockSpec((blk, cols), lambda i, j: (i * (n // tn) + j, 0))],
        out_shape=[jax.ShapeDtypeStruct((depth, COND_ROWS, n), F32),
                   jax.ShapeDtypeStruct((padded, cols), BF16)],
        compiler_params=_params("arbitrary", "arbitrary"),
        name="ada",
    )(c, c_ctx, ada_w, ada_b, w_in_t)


def _inproj_kernel(nconv, with_qo, cond_row, x_ref, mod_ref, g_ref, w_ref, b_ref, *refs):
    nout = 5 if with_qo else 3
    conv_in, outs, conv_out = refs[:nconv], refs[nconv:nconv + nout], refs[nconv + nout:]
    kt_out, v_out, gate_out = outs[-3:]
    hk, hv = kt_out.shape[1], v_out.shape[2]
    dk = hk // M_HEADS
    shift, scale = _mod_rows(mod_ref, pl.program_id(0) if cond_row is None else cond_row)[:2]
    sub = min(x_ref.shape[1], FFN_SUBTILE_ROWS)
    for rs in [slice(s, s + sub) for s in range(0, x_ref.shape[1], sub)]:
        h = _rms(x_ref[0, rs], g_ref[0:1]) * (1.0 + scale) + shift
        hb = h.astype(BF16)

        def proj(lo, hi):
            return lax.dot_general(hb, w_ref[lo:hi, :], (((1,), (1,)), ((), ())),
                                   preferred_element_type=F32) + b_ref[:, lo:hi]

        if with_qo:
            q_out, o_out = outs[:2]
            q_out[0, rs] = proj(0, hk).astype(BF16)
            o_out[0, rs] = jax.nn.sigmoid(proj(2 * hk + hv, 2 * hk + 2 * hv)).astype(BF16)
        kt_out[0, :, rs] = jnp.transpose(proj(hk, 2 * hk) * (dk ** -0.5)).astype(BF16)
        v_out[0, rs] = proj(2 * hk, 2 * hk + hv).astype(BF16)
        gate_out[0, :, rs] = jnp.transpose(proj(2 * hk + 2 * hv, w_ref.shape[0]))[:gate_out.shape[1]]
    for src_ref, dst_ref in zip(conv_in, conv_out):
        dst_ref[...] = src_ref[...].astype(BF16)


def _inproj(x, mod, cond_row, g, w, bias, hk, hv, tm, with_qo, convert=()):
    bsz, t, d = x.shape
    nt = t // tm
    full = lambda a: pl.BlockSpec(a.shape, lambda b, i: (0, 0))
    tok = lambda n: pl.BlockSpec((1, tm, n), lambda b, i: (b, i, 0))
    tok_t = lambda n: pl.BlockSpec((1, n, tm), lambda b, i: (b, 0, i))
    flat = [a.reshape(-1, a.shape[-1]) for a in convert]
    conv_specs = [pl.BlockSpec((a.shape[0] // (bsz * nt), a.shape[1]), lambda b, i: (b * nt + i, 0)) for a in flat]
    assert all(a.shape[0] % (16 * bsz * nt) == 0 for a in flat)
    qo_specs = [tok(hk), tok(hv)] if with_qo else []
    qo_shapes = [jax.ShapeDtypeStruct((bsz, t, hk), BF16), jax.ShapeDtypeStruct((bsz, t, hv), BF16)] if with_qo else []
    outs = pl.pallas_call(
        functools.partial(_inproj_kernel, len(flat), with_qo, cond_row),
        grid=(bsz, nt),
        in_specs=[tok(d), _mod_spec(0, *mod.shape[1:]), full(g), full(w), full(bias)] + conv_specs,
        out_specs=qo_specs + [tok_t(hk), tok(hv), tok_t(4 * M_HEADS)] + conv_specs,
        out_shape=qo_shapes + [jax.ShapeDtypeStruct((bsz, hk, t), BF16), jax.ShapeDtypeStruct((bsz, t, hv), BF16),
                               jax.ShapeDtypeStruct((bsz, 4 * M_HEADS, t), F32)]
                  + [jax.ShapeDtypeStruct(a.shape, BF16) for a in flat],
        compiler_params=_params("parallel", "parallel"),
        name="mlstm_inproj",
    )(x, mod, g, w, bias, *flat)
    nout = len(outs) - len(flat)
    return list(outs[:nout]) + [o.reshape(a.shape) for o, a in zip(outs[nout:], convert)]


def _lane_scan(x, op, fill, reverse, seg):
    n = x.shape[-1]
    lane = lax.broadcasted_iota(jnp.int32, x.shape, x.ndim - 1) % seg
    d = 1
    while d < seg:
        if reverse:
            shifted = jnp.where(lane < seg - d, pltpu.roll(x, n - d, x.ndim - 1), fill)
        else:
            shifted = jnp.where(lane >= d, pltpu.roll(x, d, x.ndim - 1), fill)
        x = op(x, shifted)
        d *= 2
    return x


def _log_sigmoid(x):
    return jnp.minimum(x, 0.0) - jnp.log1p(jnp.exp(-jnp.abs(x)))


def _gate_kernel(orders, gf_ref, gb_ref, of_ref, ob_ref):
    L = M_CHUNK
    for d, (g_ref, o_ref) in enumerate(((gf_ref, of_ref), (gb_ref, ob_ref))):
        edge = L - 1 if d == 0 else 0
        lf = _log_sigmoid(g_ref[:, 1])
        b_all = _lane_scan(lf, jnp.add, 0.0, d == 1, L)
        r1_all = g_ref[:, 0] - b_all
        cm_all = _lane_scan(r1_all, jnp.maximum, -jnp.inf, d == 1, L)
        m = jnp.zeros(lf.shape[:-1] + (1,), F32)
        for c in orders[d]:
            lanes = slice(c * L, (c + 1) * L)
            b, r1, cm = b_all[..., lanes], r1_all[..., lanes], cm_all[..., lanes]
            mx = jnp.maximum(m, cm[..., edge:edge + 1])
            mc = jnp.maximum(m, cm)
            o_ref[0, :, :, lanes] = r1
            o_ref[1, :, :, lanes] = mc
            o_ref[2, :, :, lanes] = jnp.broadcast_to(m, r1.shape)
            o_ref[3, :, :, lanes] = jnp.exp(-(b + mc))
            o_ref[4, :, :, lanes] = jnp.exp(r1 - mx)
            o_ref[5, :, :, lanes] = jnp.broadcast_to(jnp.exp(m - mx), r1.shape)
            m = b[..., edge:edge + 1] + mx


def _gate_prep(gates_c, gates_x):
    bsz = gates_c.shape[0]
    nlt, nct = gates_x.shape[2] // M_CHUNK, gates_c.shape[2] // M_CHUNK
    n = (nlt + nct) * M_CHUNK
    fwd = tuple(range(nlt, nlt + nct)) + tuple(range(nlt))
    bwd = tuple(range(nlt + nct - 1, nlt - 1, -1)) + tuple(range(nlt - 1, -1, -1))
    gf = gb = jnp.concatenate([gates_x, gates_c], axis=2).reshape(bsz, 4, M_HEADS, n)
    out_spec = pl.BlockSpec((6, bsz, M_HEADS, n), lambda j: (0, 0, 0, 0))
    return pl.pallas_call(
        functools.partial(_gate_kernel, (fwd, bwd)),
        grid=(1,),
        in_specs=[pl.BlockSpec((bsz, 2, M_HEADS, n), lambda j: (0, 0, 0, 0)),
                  pl.BlockSpec((bsz, 2, M_HEADS, n), lambda j: (0, 1, 0, 0))],
        out_specs=[out_spec, out_spec],
        out_shape=[jax.ShapeDtypeStruct((6, bsz, M_HEADS, n), F32)] * 2,
        compiler_params=_params("arbitrary"),
        name="mlstm_gates",
    )(gf, gb)


def _scan_kernel(cps, with_outputs, *refs):
    if with_outputs:
        (qf, ktf, vf, rf, qb, ktb, vb, rb, c0, hf_out, hb_out, c_s) = refs
        h_outs = (hf_out, hb_out)
        streams = ((qf, ktf, vf, rf), (qb, ktb, vb, rb))
    else:
        (ktf, vf, rf, ktb, vb, rb, c_s) = refs
        streams = ((None, ktf, vf, rf), (None, ktb, vb, rb))
    s = pl.program_id(1)
    L = M_CHUNK
    dk = ktf.shape[1] // M_HEADS
    dv = vf.shape[2] // M_HEADS

    @pl.when(s == 0)
    def _():
        if with_outputs:
            c_s[...] = c0[...]
        else:
            c_s[...] = jnp.zeros(c_s.shape, F32)

    row = lax.broadcasted_iota(jnp.int32, (L, L), 0)
    col = lax.broadcasted_iota(jnp.int32, (L, L), 1)
    for cc in range(cps):
        sub = (cc, cps - 1 - cc)
        blk = [slice(i * L, (i + 1) * L) for i in sub]
        if with_outputs:
            stack = [r_ref[j, 0, :, blk[d]] for d, r_ref in enumerate((rf, rb)) for j in (1, 3)]
            stack.append(jnp.zeros((L - 4 * M_HEADS, L), F32))
            cols = jnp.transpose(jnp.concatenate(stack, axis=0))
        masks = ((col <= row), (col >= row))
        for h in range(M_HEADS):
            opnds, scores, weighted = {}, {}, {}
            decay = {}
            for d in range(2 if with_outputs else 0):
                row_ref = streams[d][3]
                mc = cols[:, 2 * M_HEADS * d + h:2 * M_HEADS * d + h + 1]
                p = jnp.exp(jnp.where(masks[d], row_ref[0, 0, h:h + 1, blk[d]] - mc, -jnp.inf))
                wi = jnp.exp(row_ref[2, 0, h:h + 1, blk[d]] - mc)
                decay[d] = (p, wi)
            for d in range(2):
                _, kt_ref, v_ref, row_ref = streams[d]
                kt = kt_ref[0, h * dk:(h + 1) * dk, blk[d]]
                v = v_ref[0, blk[d], h * dv:(h + 1) * dv]
                c_old = c_s[0, d, h]
                kw = kt.astype(F32) * row_ref[4, 0, h:h + 1, blk[d]]
                n_upd = jnp.broadcast_to(jnp.sum(kw, axis=1, keepdims=True), (dk, LANES))
                upd = jnp.concatenate([_bdot(kw.astype(BF16), v), n_upd], axis=1)
                c_s[0, d, h] = row_ref[5, 0, h:h + 1, sub[d] * L:sub[d] * L + 1] * c_old + upd
                if with_outputs:
                    opnds[d] = (kt, v, c_old.astype(BF16))
            if not with_outputs:
                continue
            for d in range(2):
                kt, _, cb = opnds[d]
                q = streams[d][0][0, blk[d], h * dk:(h + 1) * dk]
                scores[d] = (q, _bdot(q, jnp.concatenate([kt, cb[:, dv:]], axis=1)))
            for d in range(2):
                q, sq = scores[d]
                p, wi = decay[d]
                sp = sq[:, :L] * p
                lhs = jnp.concatenate([sp, q.astype(F32) * wi], axis=1).astype(BF16)
                den = wi * sq[:, L:] + jnp.sum(sp, axis=1, keepdims=True)
                weighted[d] = (lhs, den)
            for d in range(2):
                _, v, cb = opnds[d]
                lhs, den = weighted[d]
                j = 2 * M_HEADS * d + h + M_HEADS
                num = _bdot(lhs, jnp.concatenate([v, cb[:, :dv]], axis=0))
                inv = 1.0 / jnp.maximum(jnp.abs(den), cols[:, j:j + 1])
                h_outs[d][0, blk[d], h * dv:(h + 1) * dv] = (
                    num * jnp.concatenate([inv] * (dv // LANES), axis=1)).astype(BF16)


def _scan(q, kt, v, gates_f, gates_b, off, init):
    bsz, hk, t = kt.shape
    hv = v.shape[2]
    dk, dv = hk // M_HEADS, hv // M_HEADS
    assert dk == M_CHUNK == LANES and dv % LANES == 0
    nc = t // M_CHUNK
    cps = max(c for c in (SCAN_CHUNKS_PER_STEP, 2, 1) if nc % c == 0 and off % c == 0)
    nsteps = nc // cps
    L = cps * M_CHUNK
    with_outputs = init is not None

    def specs(cidx):
        sp = [pl.BlockSpec((1, hk, L), lambda b, s: (b, 0, cidx(s))),
              pl.BlockSpec((1, L, hv), lambda b, s: (b, cidx(s), 0)),
              pl.BlockSpec((6, 1, M_HEADS, L), lambda b, s: (0, b, 0, off // cps + cidx(s)))]
        if with_outputs:
            sp = [pl.BlockSpec((1, L, hk), lambda b, s: (b, cidx(s), 0))] + sp
        return sp

    lead = [q] if with_outputs else []
    args = lead + [kt, v, gates_f] + lead + [kt, v, gates_b]
    in_specs = specs(lambda s: s) + specs(lambda s: nsteps - 1 - s)
    state_shape = (bsz, 2, M_HEADS, dk, dv + LANES)
    state_spec = pl.BlockSpec((1,) + state_shape[1:], lambda b, s: (b, 0, 0, 0, 0))
    if with_outputs:
        args.append(init)
        in_specs.append(state_spec)
        out_specs = [pl.BlockSpec((1, L, hv), lambda b, s: (b, s, 0)),
                     pl.BlockSpec((1, L, hv), lambda b, s: (b, nsteps - 1 - s, 0))]
        out_shape = [jax.ShapeDtypeStruct((bsz, t, hv), BF16)] * 2
        scratch = [pltpu.VMEM((1,) + state_shape[1:], F32)]
    else:
        out_specs = state_spec
        out_shape = jax.ShapeDtypeStruct(state_shape, F32)
        scratch = []
    return pl.pallas_call(
        functools.partial(_scan_kernel, cps, with_outputs),
        grid=(bsz, nsteps),
        in_specs=in_specs,
        out_specs=out_specs,
        out_shape=out_shape,
        scratch_shapes=scratch,
        compiler_params=_params("parallel", "arbitrary"),
        name="mlstm_scan" if with_outputs else "mlstm_ctx_state",
    )(*args)


def _readout_core(h, sig_o, x, gate, norm_g, w_ref, post_g):
    dv = h.shape[1] // M_HEADS
    parts = []
    for hd in range(M_HEADS):
        hh = h[:, hd * dv:(hd + 1) * dv]
        parts.append(hh * lax.rsqrt(jnp.mean(hh * hh, axis=-1, keepdims=True) + EPS))
    y = (jnp.concatenate(parts, axis=-1) * norm_g * sig_o.astype(F32)).astype(BF16)
    return x + gate * _rms(_bdot(y, w_ref[...]), post_g)


def _ffn_up(fc, x, sh, sc, pre_g, wu_ref, wg_ref, cw_ref, cb, act_ref):
    tm = x.shape[0]
    f = wu_ref.shape[1]
    hb = (_rms(x, pre_g) * (1.0 + sc) + sh).astype(BF16)
    pos = lax.broadcasted_iota(jnp.int32, (tm, fc), 0) % GRID_W
    has_prev = pos != 0
    has_next = pos != GRID_W - 1
    for j in range(f // fc):
        cs = slice(j * fc, (j + 1) * fc)
        u = _bdot(hb, wu_ref[:, cs])
        g = _bdot(hb, wg_ref[:, cs])
        g_prev = jnp.where(has_prev, pltpu.roll(g, 1, 0), 0.0)
        g_next = jnp.where(has_next, pltpu.roll(g, tm - 1, 0), 0.0)
        gc = g_prev * cw_ref[0:1, cs] + g * cw_ref[1:2, cs] + g_next * cw_ref[2:3, cs] + cb[:, cs]
        act_ref[:, cs] = (gc * jax.nn.sigmoid(gc) * u).astype(BF16)


def _ffn_down(x, gate, post_g, wd_ref, act_ref):
    return x + gate * _rms(_bdot(act_ref[...], wd_ref[...]), post_g)


def _ffn_halves(fc, xs, sh, sc, gate, pre_g, post_g, wu_ref, wg_ref, cw_ref, cb, wd_ref, act_ref):
    for s, x in enumerate(xs):
        _ffn_up(fc, x, sh, sc, pre_g, wu_ref, wg_ref, cw_ref, cb, act_ref.at[s])
    return [_ffn_down(x, gate, post_g, wd_ref, act_ref.at[s]) for s, x in enumerate(xs)]


RADIX = 4


def _fourier_channel_core(xq, sh, sc, pre_g, cs_ref, g_ref, rs=slice(None)):
    gd = cs_ref.shape[0]
    hb = [(_rms(x, pre_g) * (1.0 + sc) + sh).astype(BF16) for x in xq]
    for g in range(hb[0].shape[1] // gd):
        lanes = slice(g * gd, (g + 1) * gd)
        r = [_bdot(h[:, lanes], cs_ref[...]) for h in hb]
        a = [v[:, :gd] for v in r]
        b = [v[:, gd:] for v in r]
        a02p, a02m, a13p, a13m = a[0] + a[2], a[0] - a[2], a[1] + a[3], a[1] - a[3]
        b02p, b02m, b13p, b13m = b[0] + b[2], b[0] - b[2], b[1] + b[3], b[1] - b[3]
        re_im = ((a02p + a13p, -(b02p + b13p)),
                 (a02m - b13m, -(b02m + a13m)),
                 (a02p - a13p, b13p - b02p),
                 (a02m + b13m, a13m - b02m))
        for k, (re, im) in enumerate(re_im):
            g_ref[0, k, 0, rs, lanes] = re.astype(BF16)
            g_ref[0, k, 1, rs, lanes] = im.astype(BF16)


def _ffn_params(layer, pre_ref, post_ref, cb_ref):
    return pre_ref[layer:layer + 1], post_ref[layer:layer + 1], cb_ref[layer:layer + 1]


def _ffn_kernel(fc, layer, x_ref, mod_ref, pre_ref, post_ref, wu_ref, wg_ref, cw_ref, cb_ref, wd_ref,
                out_ref, act_ref):
    shift, scale, gate = _mod_rows(mod_ref, pl.program_id(0))[3:]
    pre_g, post_g, cb = _ffn_params(layer, pre_ref, post_ref, cb_ref)
    nsub, sub = act_ref.shape[:2]
    tiles = [slice(s * sub, (s + 1) * sub) for s in range(nsub)]
    ys = _ffn_halves(fc, [x_ref[0, rs] for rs in tiles], shift, scale, gate, pre_g, post_g, wu_ref, wg_ref, cw_ref,
                     cb, wd_ref, act_ref)
    for rs, y in zip(tiles, ys):
        out_ref[0, rs] = y


def _layer0_tail_kernel(fc, hf_ref, hb_ref, o_ref, x_ref, mod0_ref, mod1_ref, ng_ref, wo_ref, pmix_ref, premix_ref,
                        pre_ref, post_ref, wu_ref, wg_ref, cw_ref, cb_ref, wd_ref, cs_ref, out_ref, g_ref, act_ref):
    rows = x_ref.shape[2]
    _, _, gate1, shift2, scale2, gate2 = _mod_rows(mod0_ref, pl.program_id(0))
    next_shift, next_scale = _mod_rows(mod1_ref, pl.program_id(0))[:2]
    pre_g, post_g, cb = _ffn_params(0, pre_ref, post_ref, cb_ref)
    sub = act_ref.shape[1] // RADIX
    slices = [slice(s * sub, (s + 1) * sub) for s in range(act_ref.shape[0])]
    xs = []
    for rs in slices:
        cat = lambda ref: jnp.concatenate([ref[0, q, rs] for q in range(RADIX)], axis=0)
        h = cat(hf_ref).astype(F32) + cat(hb_ref).astype(F32)
        xs.append(_readout_core(h, cat(o_ref), cat(x_ref), gate1, ng_ref[...], wo_ref, pmix_ref[0:1]))
    xs = _ffn_halves(fc, xs, shift2, scale2, gate2, pre_g, post_g, wu_ref, wg_ref, cw_ref, cb, wd_ref, act_ref)
    for s, rs in enumerate(slices):
        xq = [xs[s][q * sub:(q + 1) * sub] for q in range(RADIX)]
        for q in range(RADIX):
            out_ref[0, q, rs] = xq[q]
        _fourier_channel_core(xq, next_shift, next_scale, premix_ref[1:2], cs_ref, g_ref, rs)


def _ffn_specs(layer, pre_g, post_g, d, f, cw, cb):
    def lay(rows, cols, col_block=0):
        return pl.BlockSpec((None, rows, cols), lambda b, i: (layer, 0, col_block),
                            pipeline_mode=pl.Buffered(1))
    return [_const_spec(pre_g), _const_spec(post_g), lay(d, f, 0), lay(d, f, 1), lay(cw.shape[1], f),
            _const_spec(cb), lay(f, d)]


def _ffn(x, mod, layer, pre_g, post_g, w_up, cw, cb, w_down, tm, fc):
    bsz, t, d = x.shape
    f = w_down.shape[1]
    tok = pl.BlockSpec((1, tm, d), lambda b, i: (b, i, 0))
    return pl.pallas_call(
        functools.partial(_ffn_kernel, fc, layer),
        grid=(bsz, t // tm),
        in_specs=[tok, _mod_spec(layer, *mod.shape[1:])] + _ffn_specs(layer, pre_g, post_g, d, f, cw, cb),
        out_specs=tok,
        out_shape=jax.ShapeDtypeStruct((bsz, t, d), F32),
        scratch_shapes=[pltpu.VMEM((tm // FFN_SUBTILE_ROWS, FFN_SUBTILE_ROWS, f), BF16)],
        compiler_params=_params("parallel", "parallel"),
        name="conv_ffn",
    )(x, mod, pre_g, post_g, w_up, w_up, cw, cb, w_down)


def _layer0_tail(hf, hb, sig_o, x, mod, norm_g, w_out, post_mix_g, pre_mix_g, pre_g, post_g, w_up, cw, cb, w_down,
                 cs, rows, fc):
    bsz, t, d = x.shape
    hv = hf.shape[2]
    f = w_down.shape[1]
    tq = t // RADIX
    quarters = lambda a: a.reshape(bsz, RADIX, tq, a.shape[2])
    tok = lambda n: pl.BlockSpec((1, RADIX, rows, n), lambda b, i: (b, 0, i, 0))
    out, g = pl.pallas_call(
        functools.partial(_layer0_tail_kernel, fc),
        grid=(bsz, tq // rows),
        in_specs=([tok(hv), tok(hv), tok(hv), tok(d), _mod_spec(0, *mod.shape[1:]), _mod_spec(1, *mod.shape[1:]),
                   _const_spec(norm_g), _const_spec(w_out), _const_spec(post_mix_g), _const_spec(pre_mix_g)]
                  + _ffn_specs(0, pre_g, post_g, d, f, cw, cb) + [_const_spec(cs)]),
        out_specs=[tok(d), pl.BlockSpec((1, RADIX, 2, rows, d), lambda b, i: (b, 0, 0, i, 0))],
        out_shape=[jax.ShapeDtypeStruct((bsz, RADIX, tq, d), F32),
                   jax.ShapeDtypeStruct((bsz, RADIX, 2, tq, d), BF16)],
        scratch_shapes=[pltpu.VMEM((RADIX * rows // FFN_SUBTILE_ROWS, FFN_SUBTILE_ROWS, f), BF16)],
        compiler_params=_params("parallel", "parallel"),
        name="layer0_tail",
    )(quarters(hf), quarters(hb), quarters(sig_o), quarters(x), mod, mod, norm_g, w_out, post_mix_g, pre_mix_g,
      pre_g, post_g, w_up, w_up, cw, cb, w_down, cs)
    return out.reshape(bsz, t, d), g


def _fourier_token_kernel(scale, dft_ref, g_ref, w_ref, bias_ref, post_ref, mod_ref, x_ref, out_ref,
                          x_s, out_s):
    tm = dft_ref.shape[1]
    nblk = x_s.shape[0]
    gate = _mod_rows(mod_ref, pl.program_id(0))[2]
    for c in range(nblk):
        x_s[c] = x_ref[0, :, c * LANES:(c + 1) * LANES]
    ys = [(_bdot(dft_ref[r], g_ref[0, r]) * scale).astype(BF16) for r in range(RADIX)]
    zs = [_bdot(y, w_ref[...]) + bias_ref[...] for y in ys]
    for r in range(RADIX):
        rows = pl.ds(r, tm, stride=RADIX)
        xr = jnp.concatenate([x_s[c, rows, :] for c in range(nblk)], axis=1)
        res = xr + gate * _rms(zs[r], post_ref[1:2])
        for c in range(nblk):
            out_s[c, rows, :] = res[:, c * LANES:(c + 1) * LANES]
    for c in range(nblk):
        out_ref[0, :, c * LANES:(c + 1) * LANES] = out_s[c]


def _fourier_token(dft, g, w, bias, post_mix_g, mod, x, scale, tm):
    bsz, t, d = x.shape
    tq = t // RADIX
    tok = pl.BlockSpec((1, RADIX * tm, d), lambda b, i: (b, i, 0))
    return pl.pallas_call(
        functools.partial(_fourier_token_kernel, scale),
        grid=(bsz, tq // tm),
        in_specs=[pl.BlockSpec((RADIX, tm, 2 * tq), lambda b, i: (0, i, 0)),
                  pl.BlockSpec((1, RADIX, 2 * tq, d), lambda b, i: (b, 0, 0, 0)),
                  _const_spec(w), _const_spec(bias), _const_spec(post_mix_g), _mod_spec(1, *mod.shape[1:]), tok],
        out_specs=tok,
        out_shape=jax.ShapeDtypeStruct((bsz, t, d), F32),
        scratch_shapes=[pltpu.VMEM((d // LANES, RADIX * tm, LANES), F32)] * 2,
        compiler_params=_params("parallel", "arbitrary"),
        name="fourier_token_dft",
    )(dft, g.reshape(bsz, RADIX, 2 * tq, d), w, bias, post_mix_g, mod, x)


def _dft_tables(t, gd):
    idx = np.arange(gd, dtype=np.int64)
    ang = 2.0 * np.pi * ((idx[:, None] * idx[None, :]) % gd).astype(np.float64) / gd
    chan = np.concatenate([np.cos(ang), np.sin(ang)], axis=1).astype(np.float32)
    tq = t // RADIX
    k = RADIX * np.arange(tq, dtype=np.int64)[None, :, None] + np.arange(RADIX, dtype=np.int64)[:, None, None]
    ang = 2.0 * np.pi * ((k * np.arange(tq, dtype=np.int64)[None, None, :]) % t).astype(np.float64) / t
    tok = np.concatenate([np.cos(ang), np.sin(ang)], axis=2).astype(np.float32)
    return jnp.asarray(chan).astype(BF16), jnp.asarray(tok).astype(BF16)


def kernel(x, c, ctx, c_ctx, ada_w, ada_b, pre_mix_g, post_mix_g, pre_ffn_g, post_ffn_g, ffn_up_w, ffn_conv_w,
           ffn_conv_b, ffn_down_w, m_in_w, m_in_b, m_norm_g, m_out_w, f_out_w, f_out_b):
    bsz, t, d = x.shape
    tc = ctx.shape[1]
    f = ffn_down_w.shape[1]
    assert ada_w.shape[0] == 2 and m_in_w.shape[0] == 1 and f_out_w.shape[0] == 1
    assert bsz + 1 <= COND_ROWS and t % M_CHUNK == 0 and tc % M_CHUNK == 0 and t % GRID_W == 0
    hv = m_out_w.shape[1]
    hk = (m_in_w.shape[2] - 2 * hv - 4 * M_HEADS) // 2
    tm = min(512, t)
    fc = 256
    assert f % fc == 0 and t % tm == 0 and tm % GRID_W == 0

    mod, w_in = _ada(c, c_ctx.reshape(1, d), ada_w, ada_b, jnp.swapaxes(m_in_w[0], 0, 1))

    b_in = jnp.pad(m_in_b, ((0, 0), (0, w_in.shape[0] - m_in_b.shape[1])))
    kc, vc, gc = _inproj(ctx, mod, bsz, pre_mix_g, w_in, b_in, hk, hv, min(tm, tc), False)
    qx, ox, kx, vx, gx, w_up, w_down, w_mo, w_fo = _inproj(
        x, mod, None, pre_mix_g, w_in, b_in, hk, hv, min(1024, t), True,
        convert=(ffn_up_w, ffn_down_w, m_out_w, f_out_w))
    gates_f, gates_b = _gate_prep(gc, gx)
    state = _scan(None, kc, vc, gates_f, gates_b, t // M_CHUNK, None)
    hf, hb = _scan(qx, kx, vx, gates_f, gates_b, 0, state)
    gd = d // F_GROUPS
    tq = t // RADIX
    rows, tm_tok = min(128, tq), min(256, tq)
    assert t % RADIX == 0 and tq % rows == 0 and tq % tm_tok == 0
    assert (RADIX * rows) % FFN_SUBTILE_ROWS == 0 and FFN_SUBTILE_ROWS % (RADIX * GRID_W) == 0
    chan, tok = _dft_tables(t, gd)
    x, g = _layer0_tail(hf, hb, ox, x, mod, m_norm_g, w_mo[0], post_mix_g, pre_mix_g, pre_ffn_g, post_ffn_g,
                        w_up, ffn_conv_w, ffn_conv_b, w_down, chan, rows, fc)

    x = _fourier_token(tok, g, w_fo[0], f_out_b, post_mix_g, mod, x, float(1.0 / np.sqrt(t * gd)), tm_tok)
    return _ffn(x, mod, 1, pre_ffn_g, post_ffn_g, w_up, ffn_conv_w, ffn_conv_b, w_down, min(1024, t), fc)
```

```python
import functools

import numpy as np
import jax
import jax.numpy as jnp
from jax import lax
from jax.experimental import pallas as pl
from jax.experimental.pallas import tpu as pltpu

F32 = jnp.float32
BF16 = jnp.bfloat16

M_HEADS = 4
M_CHUNK = 128
F_GROUPS = 8
GRID_W = 64
EPS = 1e-6
N_MOD = 6
COND_ROWS = 16
FFN_SUBTILE_ROWS = 256
SCAN_CHUNKS_PER_STEP = 8

VMEM_LIMIT_BYTES = 56 * 1024 * 1024
LANES = 128


def _params(*sem):
    return pltpu.CompilerParams(dimension_semantics=sem, vmem_limit_bytes=VMEM_LIMIT_BYTES)


def _rms(x, g):
    return x * lax.rsqrt(jnp.mean(x * x, axis=-1, keepdims=True) + EPS) * g


def _bdot(a, b):
    return jnp.dot(a, b, preferred_element_type=F32)


def _mod_rows(mod_ref, row):
    m = mod_ref[pl.ds(row, 1), :]
    d = m.shape[1] // N_MOD
    return [m[:, k * d:(k + 1) * d] for k in range(N_MOD)]


def _mod_spec(layer, cond_rows, width):
    return pl.BlockSpec((None, cond_rows, width), lambda b, i: (layer, 0, 0), pipeline_mode=pl.Buffered(1))


def _const_spec(a):
    return pl.BlockSpec(a.shape, lambda b, i: (0,) * a.ndim, pipeline_mode=pl.Buffered(1))


def _ada_kernel(valid_rows, c_ref, cc_ref, w_ref, b_ref, win_ref, o_ref, win_out):
    pad = jnp.zeros((o_ref.shape[1] - c_ref.shape[0] - 1, c_ref.shape[1]), F32)
    c = jnp.concatenate([c_ref[...], cc_ref[...], pad], axis=0)
    s = c * jax.nn.sigmoid(c)
    o_ref[0] = _bdot(s.astype(BF16), w_ref[0].astype(BF16)) + b_ref[pl.ds(pl.program_id(0), 1), :]
    step = pl.program_id(0) * pl.num_programs(1) + pl.program_id(1)
    row = step * win_ref.shape[0] + lax.broadcasted_iota(jnp.int32, win_ref.shape, 0)
    win_out[...] = jnp.where(row < valid_rows, win_ref[...], 0.0).astype(BF16)


def _ada(c, c_ctx, ada_w, ada_b, w_in_t):
    depth, d, n = ada_w.shape
    tn = n // 4
    steps = depth * (n // tn)
    rows, cols = w_in_t.shape
    padded = rows + (-rows) % LANES
    blk = padded // steps
    assert padded % (16 * steps) == 0 and (steps - 1) * blk < rows
    return pl.pallas_call(
        functools.partial(_ada_kernel, rows),
        grid=(depth, n // tn),
        in_specs=[pl.BlockSpec(c.shape, lambda i, j: (0, 0)),
                  pl.BlockSpec(c_ctx.shape, lambda i, j: (0, 0)),
                  pl.BlockSpec((1, d, tn), lambda i, j: (i, 0, j)),
                  pl.BlockSpec((depth, tn), lambda i, j: (0, j)),
                  pl.BlockSpec((blk, cols), lambda i, j: (i * (n // tn) + j, 0))],
        out_specs=[pl.BlockSpec((1, COND_ROWS, tn), lambda i, j: (i, 0, j)),
                   pl.BlockSpec((blk, cols), lambda i, j: (i * (n // tn) + j, 0))],
        out_shape=[jax.ShapeDtypeStruct((depth, COND_ROWS, n), F32),
                   jax.ShapeDtypeStruct((padded, cols), BF16)],
        compiler_params=_params("arbitrary", "arbitrary"),
        name="ada",
    )(c, c_ctx, ada_w, ada_b, w_in_t)


def _inproj_kernel(nconv, with_qo, cond_row, x_ref, mod_ref, g_ref, w_ref, b_ref, *refs):
    nout = 5 if with_qo else 3
    conv_in, outs, conv_out = refs[:nconv], refs[nconv:nconv + nout], refs[nconv + nout:]
    kt_out, v_out, gate_out = outs[-3:]
    hk, hv = kt_out.shape[1], v_out.shape[2]
    dk = hk // M_HEADS
    shift, scale = _mod_rows(mod_ref, pl.program_id(0) if cond_row is None else cond_row)[:2]
    sub = min(x_ref.shape[1], FFN_SUBTILE_ROWS)
    for rs in [slice(s, s + sub) for s in range(0, x_ref.shape[1], sub)]:
        h = _rms(x_ref[0, rs], g_ref[0:1]) * (1.0 + scale) + shift
        hb = h.astype(BF16)

        def proj(lo, hi):
            return lax.dot_general(hb, w_ref[lo:hi, :], (((1,), (1,)), ((), ())),
                                   preferred_element_type=F32) + b_ref[:, lo:hi]

        if with_qo:
            q_out, o_out = outs[:2]
            q_out[0, rs] = proj(0, hk).astype(BF16)
            o_out[0, rs] = jax.nn.sigmoid(proj(2 * hk + hv, 2 * hk + 2 * hv)).astype(BF16)
        kt_out[0, :, rs] = jnp.transpose(proj(hk, 2 * hk) * (dk ** -0.5)).astype(BF16)
        v_out[0, rs] = proj(2 * hk, 2 * hk + hv).astype(BF16)
        gate_out[0, :, rs] = jnp.transpose(proj(2 * hk + 2 * hv, w_ref.shape[0]))[:gate_out.shape[1]]
    for src_ref, dst_ref in zip(conv_in, conv_out):
        dst_ref[...] = src_ref[...].astype(BF16)


def _inproj(x, mod, cond_row, g, w, bias, hk, hv, tm, with_qo, convert=()):
    bsz, t, d = x.shape
    nt = t // tm
    tok = lambda n: pl.BlockSpec((1, tm, n), lambda b, i: (b, i, 0))
    tok_t = lambda n: pl.BlockSpec((1, n, tm), lambda b, i: (b, 0, i))
    flat = [a.reshape(-1, a.shape[-1]) for a in convert]
    conv_specs = [pl.BlockSpec((a.shape[0] // (bsz * nt), a.shape[1]), lambda b, i: (b * nt + i, 0)) for a in flat]
    assert all(a.shape[0] % (16 * bsz * nt) == 0 for a in flat)
    qo_specs = [tok(hk), tok(hv)] if with_qo else []
    qo_shapes = [jax.ShapeDtypeStruct((bsz, t, hk), BF16), jax.ShapeDtypeStruct((bsz, t, hv), BF16)] if with_qo else []
    outs = pl.pallas_call(
        functools.partial(_inproj_kernel, len(flat), with_qo, cond_row),
        grid=(bsz, nt),
        in_specs=[tok(d), _mod_spec(0, *mod.shape[1:]), _const_spec(g), _const_spec(w), _const_spec(bias)]
                 + conv_specs,
        out_specs=qo_specs + [tok_t(hk), tok(hv), tok_t(4 * M_HEADS)] + conv_specs,
        out_shape=qo_shapes + [jax.ShapeDtypeStruct((bsz, hk, t), BF16), jax.ShapeDtypeStruct((bsz, t, hv), BF16),
                               jax.ShapeDtypeStruct((bsz, 4 * M_HEADS, t), F32)]
                  + [jax.ShapeDtypeStruct(a.shape, BF16) for a in flat],
        compiler_params=_params("parallel", "parallel"),
        name="mlstm_inproj",
    )(x, mod, g, w, bias, *flat)
    nout = len(outs) - len(flat)
    return list(outs[:nout]) + [o.reshape(a.shape) for o, a in zip(outs[nout:], convert)]


def _lane_scan(x, op, fill, reverse, seg):
    n = x.shape[-1]
    lane = lax.broadcasted_iota(jnp.int32, x.shape, x.ndim - 1) % seg
    d = 1
    while d < seg:
        if reverse:
            shifted = jnp.where(lane < seg - d, pltpu.roll(x, n - d, x.ndim - 1), fill)
        else:
            shifted = jnp.where(lane >= d, pltpu.roll(x, d, x.ndim - 1), fill)
        x = op(x, shifted)
        d *= 2
    return x


def _log_sigmoid(x):
    return jnp.minimum(x, 0.0) - jnp.log1p(jnp.exp(-jnp.abs(x)))


def _gate_kernel(orders, gf_ref, gb_ref, of_ref, ob_ref):
    L = M_CHUNK
    for d, (g_ref, o_ref) in enumerate(((gf_ref, of_ref), (gb_ref, ob_ref))):
        edge = L - 1 if d == 0 else 0
        lf = _log_sigmoid(g_ref[:, 1])
        b_all = _lane_scan(lf, jnp.add, 0.0, d == 1, L)
        r1_all = g_ref[:, 0] - b_all
        cm_all = _lane_scan(r1_all, jnp.maximum, -jnp.inf, d == 1, L)
        m = jnp.zeros(lf.shape[:-1] + (1,), F32)
        for c in orders[d]:
            lanes = slice(c * L, (c + 1) * L)
            b, r1, cm = b_all[..., lanes], r1_all[..., lanes], cm_all[..., lanes]
            mx = jnp.maximum(m, cm[..., edge:edge + 1])
            mc = jnp.maximum(m, cm)
            o_ref[0, :, :, lanes] = r1
            o_ref[1, :, :, lanes] = mc
            o_ref[2, :, :, lanes] = jnp.broadcast_to(m, r1.shape)
            o_ref[3, :, :, lanes] = jnp.exp(-(b + mc))
            o_ref[4, :, :, lanes] = jnp.exp(r1 - mx)
            o_ref[5, :, :, lanes] = jnp.broadcast_to(jnp.exp(m - mx), r1.shape)
            m = b[..., edge:edge + 1] + mx


def _gate_prep(gates_c, gates_x):
    bsz = gates_c.shape[0]
    nlt, nct = gates_x.shape[2] // M_CHUNK, gates_c.shape[2] // M_CHUNK
    n = (nlt + nct) * M_CHUNK
    fwd = tuple(range(nlt, nlt + nct)) + tuple(range(nlt))
    bwd = tuple(range(nlt + nct - 1, nlt - 1, -1)) + tuple(range(nlt - 1, -1, -1))
    gf = gb = jnp.concatenate([gates_x, gates_c], axis=2).reshape(bsz, 4, M_HEADS, n)
    out_spec = pl.BlockSpec((6, bsz, M_HEADS, n), lambda j: (0, 0, 0, 0))
    return pl.pallas_call(
        functools.partial(_gate_kernel, (fwd, bwd)),
        grid=(1,),
        in_specs=[pl.BlockSpec((bsz, 2, M_HEADS, n), lambda j: (0, 0, 0, 0)),
                  pl.BlockSpec((bsz, 2, M_HEADS, n), lambda j: (0, 1, 0, 0))],
        out_specs=[out_spec, out_spec],
        out_shape=[jax.ShapeDtypeStruct((6, bsz, M_HEADS, n), F32)] * 2,
        compiler_params=_params("arbitrary"),
        name="mlstm_gates",
    )(gf, gb)


def _scan_kernel(cps, with_outputs, *refs):
    if with_outputs:
        (qf, ktf, vf, rf, qb, ktb, vb, rb, c0, hf_out, hb_out, c_s) = refs
        h_outs = (hf_out, hb_out)
        streams = ((qf, ktf, vf, rf), (qb, ktb, vb, rb))
    else:
        (ktf, vf, rf, ktb, vb, rb, c_s) = refs
        streams = ((None, ktf, vf, rf), (None, ktb, vb, rb))
    s = pl.program_id(1)
    L = M_CHUNK
    dk = ktf.shape[1] // M_HEADS
    dv = vf.shape[2] // M_HEADS

    @pl.when(s == 0)
    def _():
        if with_outputs:
            c_s[...] = c0[...]
        else:
            c_s[...] = jnp.zeros(c_s.shape, F32)

    row = lax.broadcasted_iota(jnp.int32, (L, L), 0)
    col = lax.broadcasted_iota(jnp.int32, (L, L), 1)
    for cc in range(cps):
        sub = (cc, cps - 1 - cc)
        blk = [slice(i * L, (i + 1) * L) for i in sub]
        if with_outputs:
            stack = [r_ref[j, 0, :, blk[d]] for d, r_ref in enumerate((rf, rb)) for j in (1, 3)]
            stack.append(jnp.zeros((L - 4 * M_HEADS, L), F32))
            cols = jnp.transpose(jnp.concatenate(stack, axis=0))
        masks = ((col <= row), (col >= row))
        for h in range(M_HEADS):
            opnds, scores, weighted = {}, {}, {}
            decay = {}
            for d in range(2 if with_outputs else 0):
                row_ref = streams[d][3]
                mc = cols[:, 2 * M_HEADS * d + h:2 * M_HEADS * d + h + 1]
                p = jnp.exp(jnp.where(masks[d], row_ref[0, 0, h:h + 1, blk[d]] - mc, -jnp.inf))
                wi = jnp.exp(row_ref[2, 0, h:h + 1, blk[d]] - mc)
                decay[d] = (p, wi)
            for d in range(2):
                _, kt_ref, v_ref, row_ref = streams[d]
                kt = kt_ref[0, h * dk:(h + 1) * dk, blk[d]]
                v = v_ref[0, blk[d], h * dv:(h + 1) * dv]
                c_old = c_s[0, d, h]
                kw = kt.astype(F32) * row_ref[4, 0, h:h + 1, blk[d]]
                n_upd = jnp.broadcast_to(jnp.sum(kw, axis=1, keepdims=True), (dk, LANES))
                upd = jnp.concatenate([_bdot(kw.astype(BF16), v), n_upd], axis=1)
                c_s[0, d, h] = row_ref[5, 0, h:h + 1, sub[d] * L:sub[d] * L + 1] * c_old + upd
                if with_outputs:
                    opnds[d] = (kt, v, c_old.astype(BF16))
            if not with_outputs:
                continue
            for d in range(2):
                kt, _, cb = opnds[d]
                q = streams[d][0][0, blk[d], h * dk:(h + 1) * dk]
                scores[d] = (q, _bdot(q, jnp.concatenate([kt, cb[:, dv:]], axis=1)))
            for d in range(2):
                q, sq = scores[d]
                p, wi = decay[d]
                sp = sq[:, :L] * p
                lhs = jnp.concatenate([sp, q.astype(F32) * wi], axis=1).astype(BF16)
                den = wi * sq[:, L:] + jnp.sum(sp, axis=1, keepdims=True)
                weighted[d] = (lhs, den)
            for d in range(2):
                _, v, cb = opnds[d]
                lhs, den = weighted[d]
                j = 2 * M_HEADS * d + h + M_HEADS
                num = _bdot(lhs, jnp.concatenate([v, cb[:, :dv]], axis=0))
                inv = 1.0 / jnp.maximum(jnp.abs(den), cols[:, j:j + 1])
                h_outs[d][0, blk[d], h * dv:(h + 1) * dv] = (
                    num * jnp.concatenate([inv] * (dv // LANES), axis=1)).astype(BF16)


def _scan(q, kt, v, gates_f, gates_b, off, init):
    bsz, hk, t = kt.shape
    hv = v.shape[2]
    dk, dv = hk // M_HEADS, hv // M_HEADS
    assert dk == M_CHUNK == LANES and dv % LANES == 0
    nc = t // M_CHUNK
    cps = max(c for c in (SCAN_CHUNKS_PER_STEP, 2, 1) if nc % c == 0 and off % c == 0)
    nsteps = nc // cps
    L = cps * M_CHUNK
    with_outputs = init is not None

    def specs(cidx):
        sp = [pl.BlockSpec((1, hk, L), lambda b, s: (b, 0, cidx(s))),
              pl.BlockSpec((1, L, hv), lambda b, s: (b, cidx(s), 0)),
              pl.BlockSpec((6, 1, M_HEADS, L), lambda b, s: (0, b, 0, off // cps + cidx(s)))]
        if with_outputs:
            sp = [pl.BlockSpec((1, L, hk), lambda b, s: (b, cidx(s), 0))] + sp
        return sp

    lead = [q] if with_outputs else []
    args = lead + [kt, v, gates_f] + lead + [kt, v, gates_b]
    in_specs = specs(lambda s: s) + specs(lambda s: nsteps - 1 - s)
    state_shape = (bsz, 2, M_HEADS, dk, dv + LANES)
    state_spec = pl.BlockSpec((1,) + state_shape[1:], lambda b, s: (b, 0, 0, 0, 0))
    if with_outputs:
        args.append(init)
        in_specs.append(state_spec)
        out_specs = [pl.BlockSpec((1, L, hv), lambda b, s: (b, s, 0)),
                     pl.BlockSpec((1, L, hv), lambda b, s: (b, nsteps - 1 - s, 0))]
        out_shape = [jax.ShapeDtypeStruct((bsz, t, hv), BF16)] * 2
        scratch = [pltpu.VMEM((1,) + state_shape[1:], F32)]
    else:
        out_specs = state_spec
        out_shape = jax.ShapeDtypeStruct(state_shape, F32)
        scratch = []
    return pl.pallas_call(
        functools.partial(_scan_kernel, cps, with_outputs),
        grid=(bsz, nsteps),
        in_specs=in_specs,
        out_specs=out_specs,
        out_shape=out_shape,
        scratch_shapes=scratch,
        compiler_params=_params("parallel", "arbitrary"),
        name="mlstm_scan" if with_outputs else "mlstm_ctx_state",
    )(*args)


def _readout_core(h, sig_o, x, gate, norm_g, w_ref, post_g):
    dv = h.shape[1] // M_HEADS
    parts = []
    for hd in range(M_HEADS):
        hh = h[:, hd * dv:(hd + 1) * dv]
        parts.append(hh * lax.rsqrt(jnp.mean(hh * hh, axis=-1, keepdims=True) + EPS))
    y = (jnp.concatenate(parts, axis=-1) * norm_g * sig_o.astype(F32)).astype(BF16)
    return x + gate * _rms(_bdot(y, w_ref[...]), post_g)


def _ffn_up(fc, x, sh, sc, pre_g, wu_ref, wg_ref, cw_ref, cb, act_ref):
    tm = x.shape[0]
    f = wu_ref.shape[1]
    hb = (_rms(x, pre_g) * (1.0 + sc) + sh).astype(BF16)
    pos = lax.broadcasted_iota(jnp.int32, (tm, fc), 0) % GRID_W
    has_prev = pos != 0
    has_next = pos != GRID_W - 1
    for j in range(f // fc):
        cs = slice(j * fc, (j + 1) * fc)
        u = _bdot(hb, wu_ref[:, cs])
        g = _bdot(hb, wg_ref[:, cs])
        g_prev = jnp.where(has_prev, pltpu.roll(g, 1, 0), 0.0)
        g_next = jnp.where(has_next, pltpu.roll(g, tm - 1, 0), 0.0)
        gc = g_prev * cw_ref[0:1, cs] + g * cw_ref[1:2, cs] + g_next * cw_ref[2:3, cs] + cb[:, cs]
        act_ref[:, cs] = (gc * jax.nn.sigmoid(gc) * u).astype(BF16)


def _ffn_down(x, gate, post_g, wd_ref, act_ref):
    return x + gate * _rms(_bdot(act_ref[...], wd_ref[...]), post_g)


def _ffn_halves(fc, xs, sh, sc, gate, pre_g, post_g, wu_ref, wg_ref, cw_ref, cb, wd_ref, act_ref):
    for s, x in enumerate(xs):
        _ffn_up(fc, x, sh, sc, pre_g, wu_ref, wg_ref, cw_ref, cb, act_ref.at[s])
    return [_ffn_down(x, gate, post_g, wd_ref, act_ref.at[s]) for s, x in enumerate(xs)]


RADIX = 4


def _fourier_channel_core(xq, sh, sc, pre_g, cs_ref, g_ref, rs=slice(None)):
    gd = cs_ref.shape[0]
    hb = [(_rms(x, pre_g) * (1.0 + sc) + sh).astype(BF16) for x in xq]
    for g in range(hb[0].shape[1] // gd):
        lanes = slice(g * gd, (g + 1) * gd)
        r = [_bdot(h[:, lanes], cs_ref[...]) for h in hb]
        a = [v[:, :gd] for v in r]
        b = [v[:, gd:] for v in r]
        a02p, a02m, a13p, a13m = a[0] + a[2], a[0] - a[2], a[1] + a[3], a[1] - a[3]
        b02p, b02m, b13p, b13m = b[0] + b[2], b[0] - b[2], b[1] + b[3], b[1] - b[3]
        re_im = ((a02p + a13p, -(b02p + b13p)),
                 (a02m - b13m, -(b02m + a13m)),
                 (a02p - a13p, b13p - b02p),
                 (a02m + b13m, a13m - b02m))
        for k, (re, im) in enumerate(re_im):
            g_ref[0, k, 0, rs, lanes] = re.astype(BF16)
            g_ref[0, k, 1, rs, lanes] = im.astype(BF16)


def _ffn_params(layer, pre_ref, post_ref, cb_ref):
    return pre_ref[layer:layer + 1], post_ref[layer:layer + 1], cb_ref[layer:layer + 1]


def _ffn_kernel(fc, layer, x_ref, mod_ref, pre_ref, post_ref, wu_ref, wg_ref, cw_ref, cb_ref, wd_ref,
                out_ref, act_ref):
    shift, scale, gate = _mod_rows(mod_ref, pl.program_id(0))[3:]
    pre_g, post_g, cb = _ffn_params(layer, pre_ref, post_ref, cb_ref)
    nsub, sub = act_ref.shape[:2]
    tiles = [slice(s * sub, (s + 1) * sub) for s in range(nsub)]
    ys = _ffn_halves(fc, [x_ref[0, rs] for rs in tiles], shift, scale, gate, pre_g, post_g, wu_ref, wg_ref, cw_ref,
                     cb, wd_ref, act_ref)
    for rs, y in zip(tiles, ys):
        out_ref[0, rs] = y


def _layer0_tail_kernel(fc, hf_ref, hb_ref, o_ref, x_ref, mod0_ref, mod1_ref, ng_ref, wo_ref, pmix_ref, premix_ref,
                        pre_ref, post_ref, wu_ref, wg_ref, cw_ref, cb_ref, wd_ref, cs_ref, out_ref, g_ref, act_ref):
    rows = x_ref.shape[2]
    _, _, gate1, shift2, scale2, gate2 = _mod_rows(mod0_ref, pl.program_id(0))
    next_shift, next_scale = _mod_rows(mod1_ref, pl.program_id(0))[:2]
    pre_g, post_g, cb = _ffn_params(0, pre_ref, post_ref, cb_ref)
    sub = act_ref.shape[1] // RADIX
    slices = [slice(s * sub, (s + 1) * sub) for s in range(act_ref.shape[0])]
    xs = []
    for rs in slices:
        cat = lambda ref: jnp.concatenate([ref[0, q, rs] for q in range(RADIX)], axis=0)
        h = cat(hf_ref).astype(F32) + cat(hb_ref).astype(F32)
        xs.append(_readout_core(h, cat(o_ref), cat(x_ref), gate1, ng_ref[...], wo_ref, pmix_ref[0:1]))
    xs = _ffn_halves(fc, xs, shift2, scale2, gate2, pre_g, post_g, wu_ref, wg_ref, cw_ref, cb, wd_ref, act_ref)
    for s, rs in enumerate(slices):
        xq = [xs[s][q * sub:(q + 1) * sub] for q in range(RADIX)]
        for q in range(RADIX):
            out_ref[0, q, rs] = xq[q]
        _fourier_channel_core(xq, next_shift, next_scale, premix_ref[1:2], cs_ref, g_ref, rs)


def _ffn_specs(layer, pre_g, post_g, d, f, cw, cb):
    def lay(rows, cols, col_block=0):
        return pl.BlockSpec((None, rows, cols), lambda b, i: (layer, 0, col_block),
                            pipeline_mode=pl.Buffered(1))
    return [_const_spec(pre_g), _const_spec(post_g), lay(d, f, 0), lay(d, f, 1), lay(cw.shape[1], f),
            _const_spec(cb), lay(f, d)]


def _ffn(x, mod, layer, pre_g, post_g, w_up, cw, cb, w_down, tm, fc):
    bsz, t, d = x.shape
    f = w_down.shape[1]
    tok = pl.BlockSpec((1, tm, d), lambda b, i: (b, i, 0))
    return pl.pallas_call(
        functools.partial(_ffn_kernel, fc, layer),
        grid=(bsz, t // tm),
        in_specs=[tok, _mod_spec(layer, *mod.shape[1:])] + _ffn_specs(layer, pre_g, post_g, d, f, cw, cb),
        out_specs=tok,
        out_shape=jax.ShapeDtypeStruct((bsz, t, d), F32),
        scratch_shapes=[pltpu.VMEM((tm // FFN_SUBTILE_ROWS, FFN_SUBTILE_ROWS, f), BF16)],
        compiler_params=_params("parallel", "parallel"),
        name="conv_ffn",
    )(x, mod, pre_g, post_g, w_up, w_up, cw, cb, w_down)


def _layer0_tail(hf, hb, sig_o, x, mod, norm_g, w_out, post_mix_g, pre_mix_g, pre_g, post_g, w_up, cw, cb, w_down,
                 cs, rows, fc):
    bsz, t, d = x.shape
    hv = hf.shape[2]
    f = w_down.shape[1]
    tq = t // RADIX
    quarters = lambda a: a.reshape(bsz, RADIX, tq, a.shape[2])
    tok = lambda n: pl.BlockSpec((1, RADIX, rows, n), lambda b, i: (b, 0, i, 0))
    out, g = pl.pallas_call(
        functools.partial(_layer0_tail_kernel, fc),
        grid=(bsz, tq // rows),
        in_specs=([tok(hv), tok(hv), tok(hv), tok(d), _mod_spec(0, *mod.shape[1:]), _mod_spec(1, *mod.shape[1:]),
                   _const_spec(norm_g), _const_spec(w_out), _const_spec(post_mix_g), _const_spec(pre_mix_g)]
                  + _ffn_specs(0, pre_g, post_g, d, f, cw, cb) + [_const_spec(cs)]),
        out_specs=[tok(d), pl.BlockSpec((1, RADIX, 2, rows, d), lambda b, i: (b, 0, 0, i, 0))],
        out_shape=[jax.ShapeDtypeStruct((bsz, RADIX, tq, d), F32),
                   jax.ShapeDtypeStruct((bsz, RADIX, 2, tq, d), BF16)],
        scratch_shapes=[pltpu.VMEM((RADIX * rows // FFN_SUBTILE_ROWS, FFN_SUBTILE_ROWS, f), BF16)],
        compiler_params=_params("parallel", "parallel"),
        name="layer0_tail",
    )(quarters(hf), quarters(hb), quarters(sig_o), quarters(x), mod, mod, norm_g, w_out, post_mix_g, pre_mix_g,
      pre_g, post_g, w_up, w_up, cw, cb, w_down, cs)
    return out.reshape(bsz, t, d), g


def _fourier_token_kernel(scale, dft_ref, g_ref, w_ref, bias_ref, post_ref, mod_ref, x_ref, out_ref,
                          x_s, out_s):
    tm = dft_ref.shape[1]
    nblk = x_s.shape[0]
    gate = _mod_rows(mod_ref, pl.program_id(0))[2]
    for c in range(nblk):
        x_s[c] = x_ref[0, :, c * LANES:(c + 1) * LANES]
    ys = [(_bdot(dft_ref[r], g_ref[0, r]) * scale).astype(BF16) for r in range(RADIX)]
    zs = [_bdot(y, w_ref[...]) + bias_ref[...] for y in ys]
    for r in range(RADIX):
        rows = pl.ds(r, tm, stride=RADIX)
        xr = jnp.concatenate([x_s[c, rows, :] for c in range(nblk)], axis=1)
        res = xr + gate * _rms(zs[r], post_ref[1:2])
        for c in range(nblk):
            out_s[c, rows, :] = res[:, c * LANES:(c + 1) * LANES]
    for c in range(nblk):
        out_ref[0, :, c * LANES:(c + 1) * LANES] = out_s[c]


def _fourier_token(dft, g, w, bias, post_mix_g, mod, x, scale, tm):
    bsz, t, d = x.shape
    tq = t // RADIX
    tok = pl.BlockSpec((1, RADIX * tm, d), lambda b, i: (b, i, 0))
    return pl.pallas_call(
        functools.partial(_fourier_token_kernel, scale),
        grid=(bsz, tq // tm),
        in_specs=[pl.BlockSpec((RADIX, tm, 2 * tq), lambda b, i: (0, i, 0)),
                  pl.BlockSpec((1, RADIX, 2 * tq, d), lambda b, i: (b, 0, 0, 0)),
                  _const_spec(w), _const_spec(bias), _const_spec(post_mix_g), _mod_spec(1, *mod.shape[1:]), tok],
        out_specs=tok,
        out_shape=jax.ShapeDtypeStruct((bsz, t, d), F32),
        scratch_shapes=[pltpu.VMEM((d // LANES, RADIX * tm, LANES), F32)] * 2,
        compiler_params=_params("parallel", "arbitrary"),
        name="fourier_token_dft",
    )(dft, g.reshape(bsz, RADIX, 2 * tq, d), w, bias, post_mix_g, mod, x)


def _dft_tables(t, gd):
    idx = np.arange(gd, dtype=np.int64)
    ang = 2.0 * np.pi * ((idx[:, None] * idx[None, :]) % gd).astype(np.float64) / gd
    chan = np.concatenate([np.cos(ang), np.sin(ang)], axis=1).astype(np.float32)
    tq = t // RADIX
    k = RADIX * np.arange(tq, dtype=np.int64)[None, :, None] + np.arange(RADIX, dtype=np.int64)[:, None, None]
    ang = 2.0 * np.pi * ((k * np.arange(tq, dtype=np.int64)[None, None, :]) % t).astype(np.float64) / t
    tok = np.concatenate([np.cos(ang), np.sin(ang)], axis=2).astype(np.float32)
    return jnp.asarray(chan).astype(BF16), jnp.asarray(tok).astype(BF16)


def kernel(x, c, ctx, c_ctx, ada_w, ada_b, pre_mix_g, post_mix_g, pre_ffn_g, post_ffn_g, ffn_up_w, ffn_conv_w,
           ffn_conv_b, ffn_down_w, m_in_w, m_in_b, m_norm_g, m_out_w, f_out_w, f_out_b):
    bsz, t, d = x.shape
    tc = ctx.shape[1]
    f = ffn_down_w.shape[1]
    assert ada_w.shape[0] == 2 and m_in_w.shape[0] == 1 and f_out_w.shape[0] == 1
    assert bsz + 1 <= COND_ROWS and t % M_CHUNK == 0 and tc % M_CHUNK == 0 and t % GRID_W == 0
    hv = m_out_w.shape[1]
    hk = (m_in_w.shape[2] - 2 * hv - 4 * M_HEADS) // 2
    tm = min(512, t)
    fc = 256
    assert f % fc == 0 and t % tm == 0 and tm % GRID_W == 0

    mod, w_in = _ada(c, c_ctx.reshape(1, d), ada_w, ada_b, jnp.swapaxes(m_in_w[0], 0, 1))

    b_in = jnp.pad(m_in_b, ((0, 0), (0, w_in.shape[0] - m_in_b.shape[1])))
    kc, vc, gc = _inproj(ctx, mod, bsz, pre_mix_g, w_in, b_in, hk, hv, min(tm, tc), False)
    qx, ox, kx, vx, gx, w_up, w_down, w_mo, w_fo = _inproj(
        x, mod, None, pre_mix_g, w_in, b_in, hk, hv, min(1024, t), True,
        convert=(ffn_up_w, ffn_down_w, m_out_w, f_out_w))
    gates_f, gates_b = _gate_prep(gc, gx)
    state = _scan(None, kc, vc, gates_f, gates_b, t // M_CHUNK, None)
    hf, hb = _scan(qx, kx, vx, gates_f, gates_b, 0, state)
    gd = d // F_GROUPS
    tq = t // RADIX
    rows, tm_tok = min(128, tq), min(256, tq)
    assert t % RADIX == 0 and tq % rows == 0 and tq % tm_tok == 0
    assert (RADIX * rows) % FFN_SUBTILE_ROWS == 0 and FFN_SUBTILE_ROWS % (RADIX * GRID_W) == 0
    chan, tok = _dft_tables(t, gd)
    x, g = _layer0_tail(hf, hb, ox, x, mod, m_norm_g, w_mo[0], post_mix_g, pre_mix_g, pre_ffn_g, post_ffn_g,
                        w_up, ffn_conv_w, ffn_conv_b, w_down, chan, rows, fc)

    x = _fourier_token(tok, g, w_fo[0], f_out_b, post_mix_g, mod, x, float(1.0 / np.sqrt(t * gd)), tm_tok)
    return _ffn(x, mod, 1, pre_ffn_g, post_ffn_g, w_up, ffn_conv_w, ffn_conv_b, w_down, min(1024, t), fc)
```
